```python
import jax, jax.numpy as jnp
from jax import lax
import numpy as np

D_MODEL = 1024
BATCH = 2
SEQ = 16384
DEPTH = 1
DEC_BATCH = 32
DEC_SEQ = 16
PAST_LEN = 4096

CHUNK = 64
POOL_WINDOWS = (2, 4, 8, 16)
N_POOL_GROUPS = 4
POOL_GROUP_WIDTH = 128
POOL_WIDTH = N_POOL_GROUPS * POOL_GROUP_WIDTH
POOL_HIST = 15
POOL_OUT_GROUP = D_MODEL // N_POOL_GROUPS
N_HEADS = 8
N_KV_HEADS = 2
HEAD_DIM = D_MODEL // N_HEADS
GQA_GROUP = N_HEADS // N_KV_HEADS
ROPE_THETA = 500000.0
N_IDX_HEADS = 8
IDX_DIM = 64
TOPK_MAX = 256
Q_BLOCK = 128
D_FF = ((8 * D_MODEL + 3 * 256 - 1) // (3 * 256)) * 256
ATTN_SCALE = HEAD_DIM ** -0.5
IDX_SCALE = (N_IDX_HEADS ** -0.5) * (IDX_DIM ** -0.5)
RMS_EPS = 1e-6
IN_WIDTHS = (POOL_WIDTH, N_HEADS * HEAD_DIM, N_KV_HEADS * HEAD_DIM, N_KV_HEADS * HEAD_DIM,
             N_IDX_HEADS * IDX_DIM, IDX_DIM, N_IDX_HEADS, D_MODEL, D_MODEL)
IN_WIDTH = sum(IN_WIDTHS)

kernel_name = "chunk_causal_pool_dsa_hybrid_step"


def rmsnorm(x, g):
    xf = x.astype(jnp.float32)
    xf = xf * lax.rsqrt(jnp.mean(xf * xf, axis=-1, keepdims=True) + RMS_EPS)
    return (xf * g.astype(jnp.float32)).astype(x.dtype)


def partial_rope(x, pos):
    d = x.shape[-1]
    rot = d // 4
    half = rot // 2
    inv = ROPE_THETA ** (-jnp.arange(half, dtype=jnp.float32) / half)
    ang = pos.astype(jnp.float32)[:, None] * inv[None, :]
    cos = jnp.cos(ang)[:, None, :]
    sin = jnp.sin(ang)[:, None, :]
    x1 = x[..., :half].astype(jnp.float32)
    x2 = x[..., half:rot].astype(jnp.float32)
    out = jnp.concatenate([(x1 * cos - x2 * sin).astype(x.dtype),
                           (x2 * cos + x1 * sin).astype(x.dtype), x[..., rot:]], axis=-1)
    return out


def pool_mix(u_hist, u_new, pos_new, w_pool, pool_scale):
    B, T = u_new.shape[:2]
    ext = jnp.concatenate([u_hist, u_new], axis=1)
    extf = ext.astype(jnp.float32)
    cs = jnp.pad(jnp.cumsum(extf, axis=1), ((0, 0), (1, 0), (0, 0)))
    outs = []
    for g, w in enumerate(POOL_WINDOWS):
        lo, hi = g * POOL_GROUP_WIDTH, (g + 1) * POOL_GROUP_WIDTH
        win = cs[:, POOL_HIST + 1:POOL_HIST + 1 + T, lo:hi] - cs[:, POOL_HIST + 1 - w:POOL_HIST + 1 - w + T, lo:hi]
        cnt = jnp.minimum(pos_new + 1, w).astype(jnp.float32)[None, :, None]
        outs.append(win / cnt - extf[:, POOL_HIST:, lo:hi])
    pooled = jnp.stack(outs, axis=2)
    a = jnp.einsum('btgc,gcd->btgd', pooled, w_pool.astype(jnp.float32)).reshape(B, T, D_MODEL)
    a = (a * pool_scale.astype(jnp.float32)).astype(u_new.dtype)
    return a, ext[:, T:]


def sparse_attention(q, k, v, q_idx, k_idx, w_idx, q_pos, k_pos):
    B, T = q.shape[:2]
    L = k.shape[1]
    n_sel = min(TOPK_MAX, L // 4)
    qb = Q_BLOCK if T % Q_BLOCK == 0 else T
    nb = T // qb
    k_chunk = k_pos // CHUNK
    k_idx_f = k_idx.astype(jnp.float32)
    gather = jax.vmap(lambda arr, ind: arr[ind])

    def block(args):
        q_b, qi_b, wi_b, qp_b = args
        q_chunk = qp_b // CHUNK
        adm = k_chunk[None, :] <= q_chunk[:, None]
        rel = jax.nn.relu(jnp.einsum('bqhd,bsd->bqhs', qi_b.astype(jnp.float32), k_idx_f))
        isc = jnp.einsum('bqhs,bqh->bqs', rel, wi_b.astype(jnp.float32) * IDX_SCALE)
        isc = jnp.where(adm[None], isc, -jnp.inf)
        _, idx = lax.top_k(isc, n_sel)
        k_sel = gather(k, idx).astype(jnp.float32)
        v_sel = gather(v, idx).astype(jnp.float32)
        valid = k_chunk[idx] <= q_chunk[None, :, None]
        qg = q_b.reshape(B, qb, N_KV_HEADS, GQA_GROUP, HEAD_DIM).astype(jnp.float32)
        logits = jnp.einsum('bqkgd,bqnkd->bqkgn', qg, k_sel) * ATTN_SCALE
        logits = jnp.where(valid[:, :, None, None, :], logits, -jnp.inf)
        p = jax.nn.softmax(logits, axis=-1)
        o = jnp.einsum('bqkgn,bqnkd->bqkgd', p, v_sel)
        return o.reshape(B, qb, N_HEADS * HEAD_DIM).astype(q.dtype)

    xs = (q.reshape(B, nb, qb, N_HEADS, HEAD_DIM).swapaxes(0, 1),
          q_idx.reshape(B, nb, qb, N_IDX_HEADS, IDX_DIM).swapaxes(0, 1),
          w_idx.reshape(B, nb, qb, N_IDX_HEADS).swapaxes(0, 1),
          q_pos.reshape(nb, qb))
    out = lax.map(block, xs)
    return out.swapaxes(0, 1).reshape(B, T, N_HEADS * HEAD_DIM)


def layer(x, pos, u_hist, k_cache, v_cache, kidx_cache, w_in, w_pool, pool_scale, w_out,
          w_gate_up, w_down, g_mix_pre, g_mix_post, g_ffn_pre, g_ffn_post):
    B, T = x.shape[:2]
    h = rmsnorm(x, g_mix_pre)
    proj = h @ w_in
    offs = np.cumsum(IN_WIDTHS)[:-1].tolist()
    u, q, k, v, qi, ki, wi, ga, gb = jnp.split(proj, offs, axis=-1)
    q = partial_rope(q.reshape(B, T, N_HEADS, HEAD_DIM), pos)
    k = partial_rope(k.reshape(B, T, N_KV_HEADS, HEAD_DIM), pos)
    v = v.reshape(B, T, N_KV_HEADS, HEAD_DIM)
    qi = partial_rope(qi.reshape(B, T, N_IDX_HEADS, IDX_DIM), pos)
    ki = partial_rope(ki[:, :, None, :], pos)[:, :, 0, :]
    a, new_pool = pool_mix(u_hist, u, pos, w_pool, pool_scale)
    if k_cache is None:
        k_all, v_all, ki_all = k, v, ki
    else:
        k_all = jnp.concatenate([k_cache, k], axis=1)
        v_all = jnp.concatenate([v_cache, v], axis=1)
        ki_all = jnp.concatenate([kidx_cache, ki], axis=1)
    k_pos = jnp.arange(k_all.shape[1], dtype=jnp.int32)
    b = sparse_attention(q, k_all, v_all, qi, ki_all, wi, pos, k_pos)
    merged = jax.nn.sigmoid(ga) * a + jax.nn.sigmoid(gb) * b
    x = x + rmsnorm(merged @ w_out, g_mix_post)
    h2 = rmsnorm(x, g_ffn_pre)
    gate, up = jnp.split(h2 @ w_gate_up, 2, axis=-1)
    f = (jax.nn.silu(gate) * up) @ w_down
    x = x + rmsnorm(f, g_ffn_post)
    return x, k, v, ki, new_pool


def setup_inputs(seed: int = 0) -> dict:
    key = jax.random.key(seed)
    ks = jax.random.split(key, 16)
    f32 = jnp.float32
    nrm = lambda kk, shape, scale: jax.random.normal(kk, shape, f32) * scale
    return {
        "x_prompt": nrm(ks[0], (BATCH, SEQ, D_MODEL), 1.0),
        "x_sample": nrm(ks[1], (DEC_BATCH, DEC_SEQ, D_MODEL), 1.0),
        "cache_k": nrm(ks[2], (DEPTH, DEC_BATCH, PAST_LEN, N_KV_HEADS, HEAD_DIM), 1.0),
        "cache_v": nrm(ks[3], (DEPTH, DEC_BATCH, PAST_LEN, N_KV_HEADS, HEAD_DIM), 1.0),
        "cache_k_idx": nrm(ks[4], (DEPTH, DEC_BATCH, PAST_LEN, IDX_DIM), 1.0),
        "state_pool": nrm(ks[5], (DEPTH, DEC_BATCH, POOL_HIST, POOL_WIDTH), 1.0),
        "w_in": nrm(ks[6], (DEPTH, D_MODEL, IN_WIDTH), D_MODEL ** -0.5),
        "w_pool": nrm(ks[7], (DEPTH, N_POOL_GROUPS, POOL_GROUP_WIDTH, POOL_OUT_GROUP), POOL_GROUP_WIDTH ** -0.5),
        "pool_scale": 1.0 + nrm(ks[8], (DEPTH, D_MODEL), 0.1),
        "w_out": nrm(ks[9], (DEPTH, D_MODEL, D_MODEL), D_MODEL ** -0.5),
        "w_gate_up": nrm(ks[10], (DEPTH, D_MODEL, 2 * D_FF), D_MODEL ** -0.5),
        "w_down": nrm(ks[11], (DEPTH, D_FF, D_MODEL), D_FF ** -0.5),
        "norm_mix_pre": 1.0 + nrm(ks[12], (DEPTH, D_MODEL), 0.05),
        "norm_mix_post": 1.0 + nrm(ks[13], (DEPTH, D_MODEL), 0.05),
        "norm_ffn_pre": 1.0 + nrm(ks[14], (DEPTH, D_MODEL), 0.05),
        "norm_ffn_post": 1.0 + nrm(ks[15], (DEPTH, D_MODEL), 0.05),
    }


def reference(x_prompt, x_sample, cache_k, cache_v, cache_k_idx, state_pool, w_in, w_pool, pool_scale,
              w_out, w_gate_up, w_down, norm_mix_pre, norm_mix_post, norm_ffn_pre, norm_ffn_post):
    T_p = x_prompt.shape[1]
    T_s = x_sample.shape[1]
    past = cache_k.shape[2]
    pos_p = jnp.arange(T_p, dtype=jnp.int32)
    pos_s = past + jnp.arange(T_s, dtype=jnp.int32)
    hist_p = jnp.zeros((x_prompt.shape[0], POOL_HIST, POOL_WIDTH), x_prompt.dtype)
    xp, xs = x_prompt, x_sample
    kp, vp, kip, pp, ksn, vsn, kisn, psn = [], [], [], [], [], [], [], []
    for l in range(DEPTH):
        w = (w_in[l], w_pool[l], pool_scale[l], w_out[l], w_gate_up[l], w_down[l],
             norm_mix_pre[l], norm_mix_post[l], norm_ffn_pre[l], norm_ffn_post[l])
        xp, k1, v1, ki1, p1 = layer(xp, pos_p, hist_p, None, None, None, *w)
        xs, k2, v2, ki2, p2 = layer(xs, pos_s, state_pool[l], cache_k[l], cache_v[l], cache_k_idx[l], *w)
        kp.append(k1); vp.append(v1); kip.append(ki1); pp.append(p1)
        ksn.append(k2); vsn.append(v2); kisn.append(ki2); psn.append(p2)
    return (xp, xs, jnp.stack(kp), jnp.stack(vp), jnp.stack(kip), jnp.stack(pp),
            jnp.stack(ksn), jnp.stack(vsn), jnp.stack(kisn), jnp.stack(psn))
```

```python
import functools

import jax
import jax.numpy as jnp
from jax import lax
from jax.experimental import pallas as pl
from jax.experimental.pallas import tpu as pltpu

F32 = jnp.float32
BF16 = jnp.bfloat16
I32 = jnp.int32

LANES = 128
CHUNK = 64
POOL_WINDOWS = (2, 4, 8, 16)
N_POOL_GROUPS = 4
POOL_GROUP_WIDTH = 128
POOL_WIDTH = N_POOL_GROUPS * POOL_GROUP_WIDTH
POOL_HIST = 15
N_HEADS = 8
N_KV_HEADS = 2
HEAD_DIM = 128
GQA_GROUP = N_HEADS // N_KV_HEADS
ROPE_THETA = 500000.0
N_IDX_HEADS = 8
IDX_DIM = 64
TOPK_MAX = 256
RMS_EPS = 1e-6
ATTN_SCALE = HEAD_DIM ** -0.5
IDX_SCALE = (N_IDX_HEADS ** -0.5) * (IDX_DIM ** -0.5)

HALO = 32
VMEM_LIMIT = 56 * 1024 * 1024

_NEG_INF = float("-inf")
_POS_INF = float("inf")
_INADMISSIBLE = -(2 ** 31)


def _resident(block_shape, index_map):
    return pl.BlockSpec(block_shape, index_map, pipeline_mode=pl.Buffered(1))


def _rope(xs, cos, sin, half, period):
    lane = lax.broadcasted_iota(I32, xs.shape, 1)
    ahead = pltpu.roll(xs, LANES - half, 1)
    behind = pltpu.roll(xs, half, 1)
    partner = jnp.where((lane & (period - 1)) < half, ahead, behind)
    return xs * cos + partner * sin


def _sigmoid(x):
    return 1.0 / (1.0 + jnp.exp(-x))


def _proj_kernel(x_ref, g_ref, w_ref, hist_ref, cq_ref, sq_ref, ci_ref, si_ref, wpool_ref, pscale_ref,
                 u_ref, qatt_ref, k_ref, v_ref, qs_ref, kiw_ref, pa_ref, sgb_ref,
                 e_ref, s2_ref, s4_ref, s8_ref, *, tm, pos0, d_model):
    i = pl.program_id(1)
    x = x_ref[0]
    h = x * lax.rsqrt(jnp.mean(x * x, axis=-1, keepdims=True) + RMS_EPS) * g_ref[...]
    proj = jnp.dot(h.astype(BF16), w_ref[...], preferred_element_type=F32)

    o_q = POOL_WIDTH
    o_k = o_q + N_HEADS * HEAD_DIM
    o_v = o_k + N_KV_HEADS * HEAD_DIM
    o_qi = o_v + N_KV_HEADS * HEAD_DIM
    o_kiw = o_qi + N_IDX_HEADS * IDX_DIM
    o_ga = o_kiw + LANES
    o_gb = o_ga + d_model

    cq, sq, ci, si = cq_ref[...], sq_ref[...], ci_ref[...], si_ref[...]
    half_q = HEAD_DIM // 8
    half_i = IDX_DIM // 8

    for hd in range(N_HEADS):
        qh = _rope(proj[:, o_q + hd * HEAD_DIM:o_q + (hd + 1) * HEAD_DIM], cq, sq, half_q, HEAD_DIM)
        qatt_ref[0, hd] = (qh * ATTN_SCALE).astype(BF16)
    for kh in range(N_KV_HEADS):
        k_ref[0, :, kh * HEAD_DIM:(kh + 1) * HEAD_DIM] = _rope(
            proj[:, o_k + kh * HEAD_DIM:o_k + (kh + 1) * HEAD_DIM], cq, sq, half_q, HEAD_DIM)
    v_ref[0] = proj[:, o_v:o_qi]

    for pr in range(N_IDX_HEADS // 2):
        qi2 = _rope(proj[:, o_qi + pr * LANES:o_qi + (pr + 1) * LANES], ci, si, half_i, IDX_DIM)
        hi = qi2.astype(BF16).astype(F32)
        lo = qi2 - hi
        lane2 = lax.broadcasted_iota(I32, qi2.shape, 1)
        first = jnp.where(lane2 < IDX_DIM, hi, pltpu.roll(lo, IDX_DIM, 1)).astype(BF16)
        second = jnp.where(lane2 < IDX_DIM, pltpu.roll(hi, IDX_DIM, 1), lo).astype(BF16)
        for sub, slab in enumerate((first, second)):
            qs_ref[0, 2 * pr + sub, :, 0:LANES] = slab
            qs_ref[0, 2 * pr + sub, :, LANES:2 * LANES] = slab

    kiw = _rope(proj[:, o_kiw:o_kiw + LANES], ci, si, half_i, IDX_DIM)
    lane = lax.broadcasted_iota(I32, kiw.shape, 1)
    kiw_ref[0] = jnp.where(lane < IDX_DIM, kiw, proj[:, o_kiw:o_kiw + LANES] * IDX_SCALE)

    u = proj[:, 0:POOL_WIDTH]
    u_ref[0] = u

    @pl.when(i == 0)
    def _():
        e_ref[0:HALO // 2, :] = jnp.zeros((HALO // 2, POOL_WIDTH), F32)
        e_ref[HALO // 2:HALO, :] = hist_ref[0]

    e_ref[HALO:HALO + tm, :] = u
    n2, n4, n8 = tm + 24, tm + 16, tm + 8
    s2_ref[8:8 + n2, :] = e_ref[8:8 + n2, :] + e_ref[7:7 + n2, :]
    s4_ref[16:16 + n4, :] = s2_ref[16:16 + n4, :] + s2_ref[14:14 + n4, :]
    s8_ref[24:24 + n8, :] = s4_ref[24:24 + n8, :] + s4_ref[20:20 + n8, :]
    s16 = s8_ref[HALO:HALO + tm, :] + s8_ref[HALO - 8:HALO - 8 + tm, :]
    wins = (s2_ref[HALO:HALO + tm, :], s4_ref[HALO:HALO + tm, :], s8_ref[HALO:HALO + tm, :], s16)
    e_ref[HALO // 2:HALO, :] = e_ref[HALO // 2 + tm:HALO + tm, :]

    pos = pos0 + i * tm + lax.broadcasted_iota(I32, (tm, POOL_GROUP_WIDTH), 0)
    a_parts = []
    for g, w in enumerate(POOL_WINDOWS):
        lo_l, hi_l = g * POOL_GROUP_WIDTH, (g + 1) * POOL_GROUP_WIDTH
        cnt = jnp.minimum(pos + 1, w).astype(F32)
        pooled = wins[g][:, lo_l:hi_l] / cnt - u[:, lo_l:hi_l]
        a_parts.append(jnp.dot(pooled.astype(BF16), wpool_ref[g], preferred_element_type=F32))
    a = jnp.concatenate(a_parts, axis=-1) * pscale_ref[...]
    pa_ref[0] = _sigmoid(proj[:, o_ga:o_gb]) * a
    sgb_ref[0] = _sigmoid(proj[:, o_gb:o_gb + d_model])


def _project(x, hist16, pos0, w_in_r, g_pre, w_pool, pool_scale, tm):
    B, T, D = x.shape
    W = w_in_r.shape[1]
    pos = pos0 + jnp.arange(T, dtype=I32)
    cq, sq = _rope_tables(pos, HEAD_DIM)
    ci, si = _rope_tables(pos, IDX_DIM)
    row = lambda b, i: (b, i, 0)
    tab = lambda b, i: (i, 0)
    const2 = lambda b, i: (0, 0)
    out_shape = (
        jax.ShapeDtypeStruct((B, T, POOL_WIDTH), F32),
        jax.ShapeDtypeStruct((B, N_HEADS, T, HEAD_DIM), BF16),
        jax.ShapeDtypeStruct((B, T, N_KV_HEADS * HEAD_DIM), F32),
        jax.ShapeDtypeStruct((B, T, N_KV_HEADS * HEAD_DIM), F32),
        jax.ShapeDtypeStruct((B, N_IDX_HEADS, T, 4 * IDX_DIM), BF16),
        jax.ShapeDtypeStruct((B, T, LANES), F32),
        jax.ShapeDtypeStruct((B, T, D), F32),
        jax.ShapeDtypeStruct((B, T, D), F32),
    )
    out_specs = (
        pl.BlockSpec((1, tm, POOL_WIDTH), row),
        pl.BlockSpec((1, N_HEADS, tm, HEAD_DIM), lambda b, i: (b, 0, i, 0)),
        pl.BlockSpec((1, tm, N_KV_HEADS * HEAD_DIM), row),
        pl.BlockSpec((1, tm, N_KV_HEADS * HEAD_DIM), row),
        pl.BlockSpec((1, N_IDX_HEADS, tm, 4 * IDX_DIM), lambda b, i: (b, 0, i, 0)),
        pl.BlockSpec((1, tm, LANES), row),
        pl.BlockSpec((1, tm, D), row),
        pl.BlockSpec((1, tm, D), row),
    )
    in_specs = [
        pl.BlockSpec((1, tm, D), row),
        _resident((1, D), const2),
        _resident((D, W), const2),
        pl.BlockSpec((1, HALO // 2, POOL_WIDTH), lambda b, i: (b, 0, 0)),
        pl.BlockSpec((tm, LANES), tab), pl.BlockSpec((tm, LANES), tab),
        pl.BlockSpec((tm, LANES), tab), pl.BlockSpec((tm, LANES), tab),
        _resident((N_POOL_GROUPS, POOL_GROUP_WIDTH, D // N_POOL_GROUPS), lambda b, i: (0, 0, 0)),
        _resident((1, D), const2),
    ]
    scratch = [pltpu.VMEM((HALO + tm, POOL_WIDTH), F32) for _ in range(4)]
    return pl.pallas_call(
        functools.partial(_proj_kernel, tm=tm, pos0=pos0, d_model=D),
        grid=(B, T // tm), in_specs=in_specs, out_specs=out_specs, out_shape=out_shape,
        scratch_shapes=scratch, name="proj",
        compiler_params=pltpu.CompilerParams(dimension_semantics=("arbitrary", "arbitrary"),
                                             vmem_limit_bytes=VMEM_LIMIT),
    )(x, g_pre, w_in_r, hist16, cq, sq, ci, si, w_pool, pool_scale)


def _rope_tables(pos, dim):
    rot = dim // 4
    half = rot // 2
    inv = ROPE_THETA ** (-jnp.arange(half, dtype=F32) / half)
    ang = pos.astype(F32)[:, None] * inv[None, :]
    cos, sin = jnp.cos(ang), jnp.sin(ang)
    rest = dim - rot
    n = pos.shape[0]
    c = jnp.concatenate([cos, cos, jnp.ones((n, rest), F32)], axis=-1)
    s = jnp.concatenate([-sin, sin, jnp.zeros((n, rest), F32)], axis=-1)
    return jnp.tile(c, (1, LANES // dim)), jnp.tile(s, (1, LANES // dim))


def _split_kernel(kt_ref, out_ref):
    k = kt_ref[0]
    hi = k.astype(BF16)
    lo = (k - hi.astype(F32)).astype(BF16)
    out_ref[0, 0, 0:IDX_DIM, :] = hi
    out_ref[0, 0, IDX_DIM:2 * IDX_DIM, :] = hi
    out_ref[0, 0, 2 * IDX_DIM:3 * IDX_DIM, :] = lo
    out_ref[0, 0, 3 * IDX_DIM:4 * IDX_DIM, :] = lo


def _split_keys(ki_t, ts):
    B, _, Lp = ki_t.shape
    return pl.pallas_call(
        _split_kernel, grid=(B, Lp // ts),
        in_specs=[pl.BlockSpec((1, IDX_DIM, ts), lambda b, t: (b, 0, t))],
        out_specs=pl.BlockSpec((1, 1, 4 * IDX_DIM, ts), lambda b, t: (b, t, 0, 0)),
        out_shape=jax.ShapeDtypeStruct((B, Lp // ts, 4 * IDX_DIM, ts), BF16), name="split_keys",
        compiler_params=pltpu.CompilerParams(dimension_semantics=("arbitrary", "arbitrary")),
    )(ki_t)


def _key_of(x):
    b = pltpu.bitcast(x, I32)
    return b ^ ((b >> 31) & 0x7FFFFFFF)


def _float_of(k):
    return pltpu.bitcast(k ^ ((k >> 31) & 0x7FFFFFFF), F32)


def _row_sum(x):
    return jnp.broadcast_to(jnp.sum(x, axis=-1, keepdims=True), x.shape)


def _attn_kernel(qs_ref, kiw_ref, q_ref, ki_ref, kt_ref, v_ref, o_ref,
                 sc_ref, wb_ref, m_ref, l_ref, acc_ref, *, tq, ts, q0, n_keys, k_sel):
    qi = pl.program_id(1)
    n_tiles_total = ki_ref.shape[1]
    nc = ts // LANES
    row = lax.broadcasted_iota(I32, (tq, LANES), 0)
    lane = lax.broadcasted_iota(I32, (tq, LANES), 1)
    qpos = q0 + qi * tq + row
    n_adm = jnp.minimum((qpos // CHUNK + 1) * CHUNK, n_keys)
    last_adm = jnp.minimum(((q0 + (qi + 1) * tq - 1) // CHUNK + 1) * CHUNK, n_keys)
    n_t = jnp.minimum((last_adm + ts - 1) // ts, n_tiles_total)

    kiw = kiw_ref[0]
    for hd in range(N_IDX_HEADS):
        wb_ref[hd] = jnp.broadcast_to(kiw[:, IDX_DIM + hd:IDX_DIM + hd + 1], (tq, LANES))
    qs2 = qs_ref[0].reshape(N_IDX_HEADS * tq, 4 * IDX_DIM)

    def score_tile(t, carry):
        rmin, rmax = carry
        s_all = jnp.dot(qs2, ki_ref[0, t], preferred_element_type=F32)
        for c in range(nc):
            acc = None
            for hd in range(N_IDX_HEADS):
                r = jnp.maximum(s_all[hd * tq:(hd + 1) * tq, c * LANES:(c + 1) * LANES], 0.0) * wb_ref[hd]
                acc = r if acc is None else acc + r
            adm = (t * ts + c * LANES + lane) < n_adm
            sc_ref[t, :, c * LANES:(c + 1) * LANES] = jnp.where(adm, _key_of(acc), _INADMISSIBLE)
            rmax = jnp.maximum(rmax, jnp.where(adm, acc, _NEG_INF))
            rmin = jnp.minimum(rmin, jnp.where(adm, acc, _POS_INF))
        return rmin, rmax

    rmin, rmax = lax.fori_loop(0, n_t, score_tile,
                               (jnp.full((tq, LANES), _POS_INF, F32), jnp.full((tq, LANES), _NEG_INF, F32)))
    rmin = jnp.broadcast_to(jnp.min(rmin, axis=-1, keepdims=True), (tq, LANES))
    rmax = jnp.broadcast_to(jnp.max(rmax, axis=-1, keepdims=True), (tq, LANES))

    def count_ge(thr):
        def body(t, cnt):
            for c in range(nc):
                cnt = cnt + jnp.where(sc_ref[t, :, c * LANES:(c + 1) * LANES] >= thr, 1.0, 0.0)
            return cnt
        return _row_sum(lax.fori_loop(0, n_t, body, jnp.zeros((tq, LANES), F32)))

    kf = float(k_sel)
    take_all = n_adm <= k_sel
    lo0 = _key_of(rmin) - 1
    hi0 = _key_of(rmax) + 2
    act0 = jnp.where(take_all, 0, 1)

    def bis_cond(st):
        return st[3] > 0

    def bis_body(st):
        lo, hi, act, _ = st
        mid = (lo & hi) + ((lo ^ hi) >> 1)
        cnt = count_ge(mid)
        on = act > 0
        ge = cnt >= kf
        hit = cnt == kf
        lo = jnp.where(on, jnp.where(ge, mid, lo), lo)
        hi = jnp.where(on, jnp.where(hit, mid + 1, jnp.where(ge, hi, mid)), hi)
        act = jnp.where(on, jnp.where(hi > lo + 1, 1, 0), 0)
        return lo, hi, act, jnp.max(act)

    lo, hi, _, _ = lax.while_loop(bis_cond, bis_body, (lo0, hi0, act0, jnp.max(act0)))
    thr = jnp.where(take_all, _INADMISSIBLE + 1, lo)
    thr_up = jnp.where(take_all, _INADMISSIBLE + 1, lo + 1)

    c_ge = count_ge(thr)
    c_gt = count_ge(thr_up)
    need = kf - c_gt
    tied = jnp.where(take_all, 0, jnp.where(c_ge > kf, 1, 0))

    def count_tied_below(cut):
        def body(t, cnt):
            for c in range(nc):
                s = sc_ref[t, :, c * LANES:(c + 1) * LANES]
                col = t * ts + c * LANES + lane
                cnt = cnt + jnp.where(s == thr, jnp.where(col < cut, 1.0, 0.0), 0.0)
            return cnt
        return _row_sum(lax.fori_loop(0, n_t, body, jnp.zeros((tq, LANES), F32)))

    def cut_body(st):
        lo_c, hi_c, act, _ = st
        mid = (lo_c + hi_c) >> 1
        cnt = count_tied_below(mid)
        on = act > 0
        le = cnt <= need
        lo_c = jnp.where(on, jnp.where(le, mid, lo_c), lo_c)
        hi_c = jnp.where(on, jnp.where(le, hi_c, mid), hi_c)
        act = jnp.where(on, jnp.where(hi_c - lo_c > 1, 1, 0), 0)
        return lo_c, hi_c, act, jnp.max(act)

    cut_lo, _, _, _ = lax.while_loop(
        bis_cond, cut_body,
        (jnp.zeros((tq, LANES), I32), jnp.full((tq, LANES), n_tiles_total * ts + 1, I32), tied, jnp.max(tied)))
    cut = jnp.where(tied > 0, cut_lo, n_tiles_total * ts + 1)

    m_ref[...] = jnp.full(m_ref.shape, _NEG_INF, F32)
    l_ref[...] = jnp.zeros(l_ref.shape, F32)
    acc_ref[...] = jnp.zeros(acc_ref.shape, F32)

    def attn_tile(t, carry):
        sel = []
        for c in range(nc):
            col = t * ts + c * LANES + lane
            bound = jnp.where(col < cut, thr, thr_up)
            sel.append(sc_ref[t, :, c * LANES:(c + 1) * LANES] >= bound)
        for g in range(N_KV_HEADS):
            qg = q_ref[0, g * GQA_GROUP:(g + 1) * GQA_GROUP].reshape(GQA_GROUP * tq, HEAD_DIM)
            s_all = jnp.dot(qg, kt_ref[0, g, t], preferred_element_type=F32)
            v_t = v_ref[0, g, t]
            for j in range(GQA_GROUP):
                hd = g * GQA_GROUP + j
                s = [jnp.where(sel[c], s_all[j * tq:(j + 1) * tq, c * LANES:(c + 1) * LANES], _NEG_INF)
                     for c in range(nc)]
                mx = s[0]
                for c in range(1, nc):
                    mx = jnp.maximum(mx, s[c])
                m_old = m_ref[hd]
                m_new = jnp.maximum(m_old, jnp.broadcast_to(jnp.max(mx, axis=-1, keepdims=True), (tq, LANES)))
                m_safe = jnp.where(m_new == _NEG_INF, 0.0, m_new)
                alpha = jnp.exp(m_old - m_safe)
                p = [jnp.exp(s[c] - m_safe) for c in range(nc)]
                ps = p[0]
                for c in range(1, nc):
                    ps = ps + p[c]
                l_ref[hd] = alpha * l_ref[hd] + _row_sum(ps)
                pv = jnp.dot(jnp.concatenate(p, axis=-1).astype(BF16), v_t, preferred_element_type=F32)
                acc_ref[hd] = alpha * acc_ref[hd] + pv
                m_ref[hd] = m_new
        return carry

    lax.fori_loop(0, n_t, attn_tile, 0)
    for hd in range(N_HEADS):
        o_ref[0, :, hd * HEAD_DIM:(hd + 1) * HEAD_DIM] = acc_ref[hd] / l_ref[hd]


def _attend(qs, kiw, q_att, ki4, kt, v, q0, n_keys, tq, ts):
    B, _, T, _ = qs.shape
    n_tiles = ki4.shape[1]
    k_sel = min(TOPK_MAX, n_keys // 4)
    kern = functools.partial(_attn_kernel, tq=tq, ts=ts, q0=q0, n_keys=n_keys, k_sel=k_sel)
    return pl.pallas_call(
        kern, grid=(B, T // tq),
        in_specs=[
            pl.BlockSpec((1, N_IDX_HEADS, tq, 4 * IDX_DIM), lambda b, i: (b, 0, i, 0)),
            pl.BlockSpec((1, tq, LANES), lambda b, i: (b, i, 0)),
            pl.BlockSpec((1, N_HEADS, tq, HEAD_DIM), lambda b, i: (b, 0, i, 0)),
            _resident((1, n_tiles, 4 * IDX_DIM, ts), lambda b, i: (b, 0, 0, 0)),
            _resident((1, N_KV_HEADS, n_tiles, HEAD_DIM, ts), lambda b, i: (b, 0, 0, 0, 0)),
            _resident((1, N_KV_HEADS, n_tiles, ts, HEAD_DIM), lambda b, i: (b, 0, 0, 0, 0)),
        ],
        out_specs=pl.BlockSpec((1, tq, N_HEADS * HEAD_DIM), lambda b, i: (b, i, 0)),
        out_shape=jax.ShapeDtypeStruct((B, T, N_HEADS * HEAD_DIM), F32),
        scratch_shapes=[
            pltpu.VMEM((n_tiles, tq, ts), I32),
            pltpu.VMEM((N_IDX_HEADS, tq, LANES), F32),
            pltpu.VMEM((N_HEADS, tq, LANES), F32),
            pltpu.VMEM((N_HEADS, tq, LANES), F32),
            pltpu.VMEM((N_HEADS, tq, HEAD_DIM), F32),
        ],
        name="attend",
        compiler_params=pltpu.CompilerParams(dimension_semantics=("arbitrary", "arbitrary"),
                                             vmem_limit_bytes=VMEM_LIMIT),
    )(qs, kiw, q_att, ki4, kt, v)


def _rms(x, g):
    return x * lax.rsqrt(jnp.mean(x * x, axis=-1, keepdims=True) + RMS_EPS) * g


def _out_kernel(x_ref, pa_ref, sgb_ref, b_ref, wo_ref, g1_ref, g2_ref, wgu_ref, wd_ref, g3_ref, y_ref, *, d_ff):
    merged = pa_ref[...] + sgb_ref[...] * b_ref[...]
    mix = jnp.dot(merged.astype(BF16), wo_ref[...], preferred_element_type=F32)
    x1 = x_ref[...] + _rms(mix, g1_ref[...])
    h2 = _rms(x1, g2_ref[...]).astype(BF16)
    gu = jnp.dot(h2, wgu_ref[...], preferred_element_type=F32)
    gate, up = gu[:, :d_ff], gu[:, d_ff:]
    act = (gate * _sigmoid(gate) * up).astype(BF16)
    f = jnp.dot(act, wd_ref[...], preferred_element_type=F32)
    y_ref[...] = x1 + _rms(f, g3_ref[...])


def _finish(x, pa, sgb, b, w_out, g_post, g_ffn_pre, w_gate_up, w_down, g_ffn_post, tm):
    N, D = x.shape
    d_ff = w_down.shape[0]
    row = pl.BlockSpec((tm, D), lambda i: (i, 0))
    const = lambda i: (0, 0)
    return pl.pallas_call(
        functools.partial(_out_kernel, d_ff=d_ff), grid=(N // tm,),
        in_specs=[row, row, row, row,
                  _resident((D, D), const), _resident((1, D), const), _resident((1, D), const),
                  _resident((D, 2 * d_ff), const), _resident((d_ff, D), const), _resident((1, D), const)],
        out_specs=row, out_shape=jax.ShapeDtypeStruct((N, D), F32), name="finish",
        compiler_params=pltpu.CompilerParams(dimension_semantics=("arbitrary",), vmem_limit_bytes=VMEM_LIMIT),
    )(x, pa, sgb, b, w_out, g_post, g_ffn_pre, w_gate_up, w_down, g_ffn_post)


def _tile_keys(n_keys, ts):
    return -(-n_keys // ts) * ts


def _layer(x, pos0, hist, k_cache, v_cache, ki_cache, wts, tm, tq, ts, tm_out):
    (w_in_r, w_pool, pool_scale, w_out, w_gate_up, w_down, g_pre, g_post, g_ffn_pre, g_ffn_post) = wts
    B, T, D = x.shape
    hist16 = jnp.concatenate([jnp.zeros((B, HALO // 2 - POOL_HIST, POOL_WIDTH), F32), hist], axis=1)
    u, q_att, k, v, qs, kiw, pa, sgb = _project(x, hist16, pos0, w_in_r, g_pre, w_pool, pool_scale, tm)
    ki = kiw[:, :, :IDX_DIM]
    if k_cache is None:
        k_all, v_all, ki_all = k, v, ki
    else:
        k_all = jnp.concatenate([k_cache.reshape(B, -1, N_KV_HEADS * HEAD_DIM), k], axis=1)
        v_all = jnp.concatenate([v_cache.reshape(B, -1, N_KV_HEADS * HEAD_DIM), v], axis=1)
        ki_all = jnp.concatenate([ki_cache, ki], axis=1)
    n_keys = k_all.shape[1]
    lp = _tile_keys(n_keys, ts)
    pad = ((0, 0), (0, lp - n_keys), (0, 0))
    nt = lp // ts
    kt = jnp.pad(k_all.astype(BF16), pad).reshape(B, nt, ts, N_KV_HEADS, HEAD_DIM).transpose(0, 3, 1, 4, 2)
    vt = jnp.pad(v_all.astype(BF16), pad).reshape(B, nt, ts, N_KV_HEADS, HEAD_DIM).transpose(0, 3, 1, 2, 4)
    ki4 = _split_keys(jnp.swapaxes(jnp.pad(ki_all, pad), 1, 2), ts)
    b = _attend(qs, kiw, q_att, ki4, kt, vt, pos0, n_keys, tq, ts)
    y = _finish(x.reshape(B * T, D), pa.reshape(B * T, D), sgb.reshape(B * T, D), b.reshape(B * T, D),
                w_out, g_post, g_ffn_pre, w_gate_up, w_down, g_ffn_post, tm_out).reshape(B, T, D)
    new_pool = jnp.concatenate([hist, u], axis=1)[:, T:]
    return (y, k.reshape(B, T, N_KV_HEADS, HEAD_DIM), v.reshape(B, T, N_KV_HEADS, HEAD_DIM), ki, new_pool)


def _relayout_w_in(w_in):
    d = w_in.shape[0]
    o_kiw = POOL_WIDTH + N_HEADS * HEAD_DIM + 2 * N_KV_HEADS * HEAD_DIM + N_IDX_HEADS * IDX_DIM
    narrow = IDX_DIM + N_IDX_HEADS
    padded = jnp.concatenate([w_in[:, :o_kiw + narrow], jnp.zeros((d, LANES - narrow), w_in.dtype),
                              w_in[:, o_kiw + narrow:]], axis=1)
    return padded.astype(BF16)


def kernel(x_prompt, x_sample, cache_k, cache_v, cache_k_idx, state_pool, w_in, w_pool, pool_scale, w_out,
           w_gate_up, w_down, norm_mix_pre, norm_mix_post, norm_ffn_pre, norm_ffn_post):
    depth = w_in.shape[0]
    past = cache_k.shape[2]
    t_p, t_s = x_prompt.shape[1], x_sample.shape[1]
    hist_p = jnp.zeros((x_prompt.shape[0], POOL_HIST, POOL_WIDTH), x_prompt.dtype)
    xp, xs = x_prompt, x_sample
    outs = [[] for _ in range(8)]
    for l in range(depth):
        wts = (_relayout_w_in(w_in[l]), w_pool[l].astype(BF16), pool_scale[l][None, :], w_out[l].astype(BF16),
               w_gate_up[l].astype(BF16), w_down[l].astype(BF16), norm_mix_pre[l][None, :],
               norm_mix_post[l][None, :], norm_ffn_pre[l][None, :], norm_ffn_post[l][None, :])
        tm_p = min(256, t_p)
        tq_p = min(128, t_p)
        xp, k1, v1, ki1, p1 = _layer(xp, 0, hist_p, None, None, None, wts, tm_p, tq_p, 512, tm_p)
        n_s = xs.shape[0] * t_s
        xs, k2, v2, ki2, p2 = _layer(xs, past, state_pool[l], cache_k[l], cache_v[l], cache_k_idx[l], wts,
                                     t_s, t_s, 512, min(256, n_s))
        for lst, val in zip(outs, (k1, v1, ki1, p1, k2, v2, ki2, p2)):
            lst.append(val)
    return (xp, xs) + tuple(jnp.stack(o) for o in outs)
```

```python
import functools

import jax
import jax.numpy as jnp
from jax import lax
from jax.experimental import pallas as pl
from jax.experimental.pallas import tpu as pltpu

F32 = jnp.float32
BF16 = jnp.bfloat16
I32 = jnp.int32

LANES = 128
CHUNK = 64
POOL_WINDOWS = (2, 4, 8, 16)
N_POOL_GROUPS = 4
POOL_GROUP_WIDTH = 128
POOL_WIDTH = N_POOL_GROUPS * POOL_GROUP_WIDTH
POOL_HIST = 15
N_HEADS = 8
N_KV_HEADS = 2
HEAD_DIM = 128
GQA_GROUP = N_HEADS // N_KV_HEADS
ROPE_THETA = 500000.0
N_IDX_HEADS = 8
IDX_DIM = 64
TOPK_MAX = 256
RMS_EPS = 1e-6
ATTN_SCALE = HEAD_DIM ** -0.5
IDX_SCALE = (N_IDX_HEADS ** -0.5) * (IDX_DIM ** -0.5)

LOG2_E = 1.4426950408889634
VALUE_PIVOT_STEPS = 8
ROW_CHUNK = 32
HALO = 32
VMEM_LIMIT = 56 * 1024 * 1024

_NEG_INF = float("-inf")
_POS_INF = float("inf")
_INADMISSIBLE = -(2 ** 31)
_NO_KEY = 2 ** 31 - 1
_M_INIT = -1e30


def _resident(block_shape, index_map):
    return pl.BlockSpec(block_shape, index_map, pipeline_mode=pl.Buffered(1))


def _rope(xs, cos, sin, half, period):
    lane = lax.broadcasted_iota(I32, xs.shape, 1)
    ahead = pltpu.roll(xs, LANES - half, 1)
    behind = pltpu.roll(xs, half, 1)
    partner = jnp.where((lane & (period - 1)) < half, ahead, behind)
    return xs * cos + partner * sin


def _sigmoid(x):
    return 1.0 / (1.0 + jnp.exp(-x))


def _proj_kernel(x_ref, g_ref, w_ref, hist_ref, cq_ref, sq_ref, ci_ref, si_ref, wpool_ref, pscale_ref,
                 u_ref, qatt_ref, k_ref, v_ref, qs_ref, kiw_ref, pa_ref, sgb_ref,
                 e_ref, s2_ref, s4_ref, s8_ref, *, tm, pos0, d_model):
    i = pl.program_id(1)
    x = x_ref[0]
    h = x * lax.rsqrt(jnp.mean(x * x, axis=-1, keepdims=True) + RMS_EPS) * g_ref[...]
    proj = jnp.dot(h.astype(BF16), w_ref[...], preferred_element_type=F32)

    o_q = POOL_WIDTH
    o_k = o_q + N_HEADS * HEAD_DIM
    o_v = o_k + N_KV_HEADS * HEAD_DIM
    o_qi = o_v + N_KV_HEADS * HEAD_DIM
    o_kiw = o_qi + N_IDX_HEADS * IDX_DIM
    o_ga = o_kiw + LANES
    o_gb = o_ga + d_model

    cq, sq, ci, si = cq_ref[...], sq_ref[...], ci_ref[...], si_ref[...]
    half_q = HEAD_DIM // 8
    half_i = IDX_DIM // 8

    for hd in range(N_HEADS):
        qh = _rope(proj[:, o_q + hd * HEAD_DIM:o_q + (hd + 1) * HEAD_DIM], cq, sq, half_q, HEAD_DIM)
        qatt_ref[0, hd] = (qh * (ATTN_SCALE * LOG2_E)).astype(BF16)
    for kh in range(N_KV_HEADS):
        k_ref[0, :, kh * HEAD_DIM:(kh + 1) * HEAD_DIM] = _rope(
            proj[:, o_k + kh * HEAD_DIM:o_k + (kh + 1) * HEAD_DIM], cq, sq, half_q, HEAD_DIM)
    v_ref[0] = proj[:, o_v:o_qi]

    for pr in range(N_IDX_HEADS // 2):
        qi2 = _rope(proj[:, o_qi + pr * LANES:o_qi + (pr + 1) * LANES], ci, si, half_i, IDX_DIM)
        hi = qi2.astype(BF16).astype(F32)
        lo = qi2 - hi
        lane2 = lax.broadcasted_iota(I32, qi2.shape, 1)
        first = jnp.where(lane2 < IDX_DIM, hi, pltpu.roll(lo, IDX_DIM, 1)).astype(BF16)
        second = jnp.where(lane2 < IDX_DIM, pltpu.roll(hi, IDX_DIM, 1), lo).astype(BF16)
        for sub, slab in enumerate((first, second)):
            qs_ref[0, 2 * pr + sub, :, 0:LANES] = slab
            qs_ref[0, 2 * pr + sub, :, LANES:2 * LANES] = slab

    kiw = _rope(proj[:, o_kiw:o_kiw + LANES], ci, si, half_i, IDX_DIM)
    lane = lax.broadcasted_iota(I32, kiw.shape, 1)
    kiw_ref[0] = jnp.where(lane < IDX_DIM, kiw, proj[:, o_kiw:o_kiw + LANES] * IDX_SCALE)

    u = proj[:, 0:POOL_WIDTH]
    u_ref[0] = u

    @pl.when(i == 0)
    def _():
        e_ref[0:HALO // 2, :] = jnp.zeros((HALO // 2, POOL_WIDTH), F32)
        e_ref[HALO // 2:HALO, :] = hist_ref[0]

    e_ref[HALO:HALO + tm, :] = u
    n2, n4, n8 = tm + 24, tm + 16, tm + 8
    s2_ref[8:8 + n2, :] = e_ref[8:8 + n2, :] + e_ref[7:7 + n2, :]
    s4_ref[16:16 + n4, :] = s2_ref[16:16 + n4, :] + s2_ref[14:14 + n4, :]
    s8_ref[24:24 + n8, :] = s4_ref[24:24 + n8, :] + s4_ref[20:20 + n8, :]
    s16 = s8_ref[HALO:HALO + tm, :] + s8_ref[HALO - 8:HALO - 8 + tm, :]
    wins = (s2_ref[HALO:HALO + tm, :], s4_ref[HALO:HALO + tm, :], s8_ref[HALO:HALO + tm, :], s16)
    e_ref[HALO // 2:HALO, :] = e_ref[HALO // 2 + tm:HALO + tm, :]

    pos = pos0 + i * tm + lax.broadcasted_iota(I32, (tm, POOL_GROUP_WIDTH), 0)
    a_parts = []
    for g, w in enumerate(POOL_WINDOWS):
        lo_l, hi_l = g * POOL_GROUP_WIDTH, (g + 1) * POOL_GROUP_WIDTH
        cnt = jnp.minimum(pos + 1, w).astype(F32)
        pooled = wins[g][:, lo_l:hi_l] / cnt - u[:, lo_l:hi_l]
        a_parts.append(jnp.dot(pooled.astype(BF16), wpool_ref[g], preferred_element_type=F32))
    a = jnp.concatenate(a_parts, axis=-1) * pscale_ref[...]
    pa_ref[0] = _sigmoid(proj[:, o_ga:o_gb]) * a
    sgb_ref[0] = _sigmoid(proj[:, o_gb:o_gb + d_model])


def _project(x, hist16, pos0, w_in_r, g_pre, w_pool, pool_scale, tm):
    B, T, D = x.shape
    W = w_in_r.shape[1]
    pos = pos0 + jnp.arange(T, dtype=I32)
    cq, sq = _rope_tables(pos, HEAD_DIM)
    ci, si = _rope_tables(pos, IDX_DIM)
    row = lambda b, i: (b, i, 0)
    tab = lambda b, i: (i, 0)
    const2 = lambda b, i: (0, 0)
    out_shape = (
        jax.ShapeDtypeStruct((B, T, POOL_WIDTH), F32),
        jax.ShapeDtypeStruct((B, N_HEADS, T, HEAD_DIM), BF16),
        jax.ShapeDtypeStruct((B, T, N_KV_HEADS * HEAD_DIM), F32),
        jax.ShapeDtypeStruct((B, T, N_KV_HEADS * HEAD_DIM), F32),
        jax.ShapeDtypeStruct((B, N_IDX_HEADS, T, 4 * IDX_DIM), BF16),
        jax.ShapeDtypeStruct((B, T, LANES), F32),
        jax.ShapeDtypeStruct((B, T, D), F32),
        jax.ShapeDtypeStruct((B, T, D), F32),
    )
    out_specs = (
        pl.BlockSpec((1, tm, POOL_WIDTH), row),
        pl.BlockSpec((1, N_HEADS, tm, HEAD_DIM), lambda b, i: (b, 0, i, 0)),
        pl.BlockSpec((1, tm, N_KV_HEADS * HEAD_DIM), row),
        pl.BlockSpec((1, tm, N_KV_HEADS * HEAD_DIM), row),
        pl.BlockSpec((1, N_IDX_HEADS, tm, 4 * IDX_DIM), lambda b, i: (b, 0, i, 0)),
        pl.BlockSpec((1, tm, LANES), row),
        pl.BlockSpec((1, tm, D), row),
        pl.BlockSpec((1, tm, D), row),
    )
    in_specs = [
        pl.BlockSpec((1, tm, D), row),
        _resident((1, D), const2),
        _resident((D, W), const2),
        pl.BlockSpec((1, HALO // 2, POOL_WIDTH), lambda b, i: (b, 0, 0)),
        pl.BlockSpec((tm, LANES), tab), pl.BlockSpec((tm, LANES), tab),
        pl.BlockSpec((tm, LANES), tab), pl.BlockSpec((tm, LANES), tab),
        _resident((N_POOL_GROUPS, POOL_GROUP_WIDTH, D // N_POOL_GROUPS), lambda b, i: (0, 0, 0)),
        _resident((1, D), const2),
    ]
    scratch = [pltpu.VMEM((HALO + tm, POOL_WIDTH), F32) for _ in range(4)]
    return pl.pallas_call(
        functools.partial(_proj_kernel, tm=tm, pos0=pos0, d_model=D),
        grid=(B, T // tm), in_specs=in_specs, out_specs=out_specs, out_shape=out_shape,
        scratch_shapes=scratch, name="proj",
        compiler_params=pltpu.CompilerParams(dimension_semantics=("arbitrary", "arbitrary"),
                                             vmem_limit_bytes=VMEM_LIMIT),
    )(x, g_pre, w_in_r, hist16, cq, sq, ci, si, w_pool, pool_scale)


def _rope_tables(pos, dim):
    rot = dim // 4
    half = rot // 2
    inv = ROPE_THETA ** (-jnp.arange(half, dtype=F32) / half)
    ang = pos.astype(F32)[:, None] * inv[None, :]
    cos, sin = jnp.cos(ang), jnp.sin(ang)
    rest = dim - rot
    n = pos.shape[0]
    c = jnp.concatenate([cos, cos, jnp.ones((n, rest), F32)], axis=-1)
    s = jnp.concatenate([-sin, sin, jnp.zeros((n, rest), F32)], axis=-1)
    return jnp.tile(c, (1, LANES // dim)), jnp.tile(s, (1, LANES // dim))


def _split_kernel(kt_ref, out_ref):
    k = kt_ref[0]
    hi = k.astype(BF16)
    lo = (k - hi.astype(F32)).astype(BF16)
    out_ref[0, 0, 0:IDX_DIM, :] = hi
    out_ref[0, 0, IDX_DIM:2 * IDX_DIM, :] = hi
    out_ref[0, 0, 2 * IDX_DIM:3 * IDX_DIM, :] = lo
    out_ref[0, 0, 3 * IDX_DIM:4 * IDX_DIM, :] = lo


def _split_keys(ki_t, ts):
    B, _, Lp = ki_t.shape
    return pl.pallas_call(
        _split_kernel, grid=(B, Lp // ts),
        in_specs=[pl.BlockSpec((1, IDX_DIM, ts), lambda b, t: (b, 0, t))],
        out_specs=pl.BlockSpec((1, 1, 4 * IDX_DIM, ts), lambda b, t: (b, t, 0, 0)),
        out_shape=jax.ShapeDtypeStruct((B, Lp // ts, 4 * IDX_DIM, ts), BF16), name="split_keys",
        compiler_params=pltpu.CompilerParams(dimension_semantics=("arbitrary", "arbitrary")),
    )(ki_t)


def _key_of(x):
    b = pltpu.bitcast(x, I32)
    return b ^ ((b >> 31) & 0x7FFFFFFF)


def _float_of(k):
    return pltpu.bitcast(k ^ ((k >> 31) & 0x7FFFFFFF), F32)


def _row_sum(x):
    return jnp.broadcast_to(jnp.sum(x, axis=-1, keepdims=True), x.shape)


def _attn_kernel(qs_ref, kiw_ref, q_ref, ki_ref, kt_ref, v_ref, o_ref,
                 sc_ref, wb_ref, si_ref, m_ref, acc_ref, bias_ref, s_ref, p_ref, alpha_ref,
                 *, tq, ts, q0, n_keys, k_sel):
    qi = pl.program_id(1)
    n_tiles_total = ki_ref.shape[1]
    nc = ts // LANES
    row = lax.broadcasted_iota(I32, (tq, LANES), 0)
    lane = lax.broadcasted_iota(I32, (tq, LANES), 1)
    qpos = q0 + qi * tq + row
    n_adm = jnp.minimum((qpos // CHUNK + 1) * CHUNK, n_keys)
    last_adm = jnp.minimum(((q0 + (qi + 1) * tq - 1) // CHUNK + 1) * CHUNK, n_keys)
    n_t = jnp.minimum((last_adm + ts - 1) // ts, n_tiles_total)

    kiw = kiw_ref[0]
    for hd in range(N_IDX_HEADS):
        wb_ref[hd] = jnp.broadcast_to(kiw[:, IDX_DIM + hd:IDX_DIM + hd + 1], (tq, LANES))
    qs2 = qs_ref[0].reshape(N_IDX_HEADS * tq, 4 * IDX_DIM)

    def idx_logits(t, slot):
        si_ref[slot] = jnp.dot(qs2, ki_ref[0, t], preferred_element_type=F32)

    def score_tile(t, slot, carry):
        rmin, rmax = carry
        for c in range(nc):
            acc = None
            for hd in range(N_IDX_HEADS):
                r = jnp.maximum(si_ref[slot, hd * tq:(hd + 1) * tq, c * LANES:(c + 1) * LANES], 0.0) * wb_ref[hd]
                acc = r if acc is None else acc + r
            adm = (t * ts + c * LANES + lane) < n_adm
            sc_ref[t, :, c * LANES:(c + 1) * LANES] = jnp.where(adm, _key_of(acc), _INADMISSIBLE)
            rmax = jnp.maximum(rmax, jnp.where(adm, acc, _NEG_INF))
            rmin = jnp.minimum(rmin, jnp.where(adm, acc, _POS_INF))
        return rmin, rmax

    idx_logits(0, 0)

    def score_pair(u, carry):
        t0 = 2 * u
        t1 = jnp.minimum(t0 + 1, n_t - 1)
        idx_logits(t1, 1)
        carry = score_tile(t0, 0, carry)
        idx_logits(jnp.minimum(t0 + 2, n_t - 1), 0)
        return score_tile(t1, 1, carry)

    rmin, rmax = lax.fori_loop(0, (n_t + 1) // 2, score_pair,
                               (jnp.full((tq, LANES), _POS_INF, F32), jnp.full((tq, LANES), _NEG_INF, F32)))
    rmin = jnp.broadcast_to(jnp.min(rmin, axis=-1, keepdims=True), (tq, LANES))
    rmax = jnp.broadcast_to(jnp.max(rmax, axis=-1, keepdims=True), (tq, LANES))

    def count_ge(thr):
        def body(t, cnt):
            for c in range(nc):
                cnt = cnt + jnp.where(sc_ref[t, :, c * LANES:(c + 1) * LANES] >= thr, 1.0, 0.0)
            return cnt
        return _row_sum(lax.fori_loop(0, n_t, body, jnp.zeros((tq, LANES), F32)))

    kf = float(k_sel)
    take_all = n_adm <= k_sel
    lo0 = _key_of(rmin) - 1
    hi0 = _key_of(rmax) + 2
    act0 = jnp.where(take_all, 0, 1)

    def bis_cond(st):
        return st[3] > 0

    def bis_body(st):
        lo, hi, act, _, it = st
        mid = (lo & hi) + ((lo ^ hi) >> 1)
        vmid = _key_of((_float_of(lo) + _float_of(hi)) * 0.5)
        by_value = jnp.where(it < VALUE_PIVOT_STEPS, jnp.where(vmid > lo, jnp.where(vmid < hi, 1, 0), 0), 0)
        mid = jnp.where(by_value > 0, vmid, mid)
        cnt = count_ge(mid)
        on = act > 0
        ge = cnt >= kf
        hit = cnt == kf
        lo = jnp.where(on, jnp.where(ge, mid, lo), lo)
        hi = jnp.where(on, jnp.where(hit, mid + 1, jnp.where(ge, hi, mid)), hi)
        act = jnp.where(on, jnp.where(hi > lo + 1, 1, 0), 0)
        return lo, hi, act, jnp.max(act), it + 1

    lo, hi, _, _, _ = lax.while_loop(bis_cond, bis_body, (lo0, hi0, act0, jnp.max(act0), jnp.int32(0)))
    thr = jnp.where(take_all, _INADMISSIBLE + 1, lo)
    thr_up = jnp.where(take_all, _INADMISSIBLE + 1, lo + 1)

    c_ge = count_ge(thr)
    c_gt = count_ge(thr_up)
    need = kf - c_gt
    tied = jnp.where(take_all, 0, jnp.where(c_ge > kf, 1, 0))

    def count_tied_below(cut):
        def body(t, cnt):
            for c in range(nc):
                s = sc_ref[t, :, c * LANES:(c + 1) * LANES]
                col = t * ts + c * LANES + lane
                cnt = cnt + jnp.where(s == thr, jnp.where(col < cut, 1.0, 0.0), 0.0)
            return cnt
        return _row_sum(lax.fori_loop(0, n_t, body, jnp.zeros((tq, LANES), F32)))

    def cut_body(st):
        lo_c, hi_c, act, _ = st
        mid = (lo_c + hi_c) >> 1
        cnt = count_tied_below(mid)
        on = act > 0
        le = cnt <= need
        lo_c = jnp.where(on, jnp.where(le, mid, lo_c), lo_c)
        hi_c = jnp.where(on, jnp.where(le, hi_c, mid), hi_c)
        act = jnp.where(on, jnp.where(hi_c - lo_c > 1, 1, 0), 0)
        return lo_c, hi_c, act, jnp.max(act)

    cut_lo, _, _, _ = lax.while_loop(
        bis_cond, cut_body,
        (jnp.zeros((tq, LANES), I32), jnp.full((tq, LANES), n_tiles_total * ts + 1, I32), tied, jnp.max(tied)))
    cut = jnp.where(tied > 0, cut_lo, n_tiles_total * ts + 1)

    m_ref[...] = jnp.full(m_ref.shape, _M_INIT, F32)
    acc_ref[...] = jnp.zeros(acc_ref.shape, F32)
    rc = min(tq, ROW_CHUNK)
    ones = jnp.ones((ts, LANES), BF16)
    last = n_t - 1

    def logits(t, slot):
        for g in range(N_KV_HEADS):
            qg = q_ref[0, g * GQA_GROUP:(g + 1) * GQA_GROUP].reshape(GQA_GROUP * tq, HEAD_DIM)
            s_ref[slot, g] = jnp.dot(qg, kt_ref[0, g, t], preferred_element_type=F32)

    def softmax_pv(t, slot, live):
        for c in range(nc):
            col = t * ts + c * LANES + lane
            bound = jnp.where(live, jnp.where(col < cut, thr, thr_up), _NO_KEY)
            bias_ref[slot, :, c * LANES:(c + 1) * LANES] = jnp.where(
                sc_ref[t, :, c * LANES:(c + 1) * LANES] >= bound, 0.0, _NEG_INF)
        for g in range(N_KV_HEADS):
            for j in range(GQA_GROUP):
                hd = g * GQA_GROUP + j
                for r0 in range(0, tq, rc):
                    r1 = j * tq + r0
                    s = [s_ref[slot, g, r1:r1 + rc, c * LANES:(c + 1) * LANES]
                         + bias_ref[slot, r0:r0 + rc, c * LANES:(c + 1) * LANES] for c in range(nc)]
                    mx = s[0]
                    for c in range(1, nc):
                        mx = jnp.maximum(mx, s[c])
                    m_old = m_ref[hd, r0:r0 + rc]
                    m_new = jnp.maximum(m_old, jnp.broadcast_to(jnp.max(mx, axis=-1, keepdims=True), (rc, LANES)))
                    alpha_ref[slot, g, r1:r1 + rc] = jnp.exp2(m_old - m_new)
                    for c in range(nc):
                        p_ref[slot, g, r1:r1 + rc, c * LANES:(c + 1) * LANES] = jnp.exp2(s[c] - m_new).astype(BF16)
                    m_ref[hd, r0:r0 + rc] = m_new
            v_aug = jnp.concatenate([v_ref[0, g, t], ones], axis=-1)
            pv = jnp.dot(p_ref[slot, g], v_aug, preferred_element_type=F32)
            alpha = alpha_ref[slot, g]
            for half in range(2):
                cols = slice(half * LANES, (half + 1) * LANES)
                acc_ref[g, :, cols] = alpha * acc_ref[g, :, cols] + pv[:, cols]

    logits(0, 0)

    def attn_pair(u, carry):
        t0 = 2 * u
        t1 = jnp.minimum(t0 + 1, last)
        logits(t1, 1)
        softmax_pv(t0, 0, True)
        logits(jnp.minimum(t0 + 2, last), 0)
        softmax_pv(t1, 1, t0 + 1 < n_t)
        return carry

    lax.fori_loop(0, (n_t + 1) // 2, attn_pair, 0)
    for hd in range(N_HEADS):
        g, j = divmod(hd, GQA_GROUP)
        o_ref[0, :, hd * HEAD_DIM:(hd + 1) * HEAD_DIM] = (
            acc_ref[g, j * tq:(j + 1) * tq, 0:HEAD_DIM] / acc_ref[g, j * tq:(j + 1) * tq, HEAD_DIM:2 * HEAD_DIM])


def _attend(qs, kiw, q_att, ki4, kt, v, q0, n_keys, tq, ts):
    B, _, T, _ = qs.shape
    n_tiles = ki4.shape[1]
    k_sel = min(TOPK_MAX, n_keys // 4)
    kern = functools.partial(_attn_kernel, tq=tq, ts=ts, q0=q0, n_keys=n_keys, k_sel=k_sel)
    return pl.pallas_call(
        kern, grid=(B, T // tq),
        in_specs=[
            pl.BlockSpec((1, N_IDX_HEADS, tq, 4 * IDX_DIM), lambda b, i: (b, 0, i, 0)),
            pl.BlockSpec((1, tq, LANES), lambda b, i: (b, i, 0)),
            pl.BlockSpec((1, N_HEADS, tq, HEAD_DIM), lambda b, i: (b, 0, i, 0)),
            _resident((1, n_tiles, 4 * IDX_DIM, ts), lambda b, i: (b, 0, 0, 0)),
            _resident((1, N_KV_HEADS, n_tiles, HEAD_DIM, ts), lambda b, i: (b, 0, 0, 0, 0)),
            _resident((1, N_KV_HEADS, n_tiles, ts, HEAD_DIM), lambda b, i: (b, 0, 0, 0, 0)),
        ],
        out_specs=pl.BlockSpec((1, tq, N_HEADS * HEAD_DIM), lambda b, i: (b, i, 0)),
        out_shape=jax.ShapeDtypeStruct((B, T, N_HEADS * HEAD_DIM), F32),
        scratch_shapes=[
            pltpu.VMEM((n_tiles, tq, ts), I32),
            pltpu.VMEM((N_IDX_HEADS, tq, LANES), F32),
            pltpu.VMEM((2, N_IDX_HEADS * tq, ts), F32),
            pltpu.VMEM((N_HEADS, tq, LANES), F32),
            pltpu.VMEM((N_KV_HEADS, GQA_GROUP * tq, 2 * HEAD_DIM), F32),
            pltpu.VMEM((2, tq, ts), F32),
            pltpu.VMEM((2, N_KV_HEADS, GQA_GROUP * tq, ts), F32),
            pltpu.VMEM((2, N_KV_HEADS, GQA_GROUP * tq, ts), BF16),
            pltpu.VMEM((2, N_KV_HEADS, GQA_GROUP * tq, LANES), F32),
        ],
        name="attend",
        compiler_params=pltpu.CompilerParams(dimension_semantics=("arbitrary", "arbitrary"),
                                             vmem_limit_bytes=VMEM_LIMIT),
    )(qs, kiw, q_att, ki4, kt, v)


def _rms(x, g):
    return x * lax.rsqrt(jnp.mean(x * x, axis=-1, keepdims=True) + RMS_EPS) * g


def _out_kernel(x_ref, pa_ref, sgb_ref, b_ref, wo_ref, g1_ref, g2_ref, wgu_ref, wd_ref, g3_ref, y_ref, *, d_ff):
    merged = pa_ref[...] + sgb_ref[...] * b_ref[...]
    mix = jnp.dot(merged.astype(BF16), wo_ref[...], preferred_element_type=F32)
    x1 = x_ref[...] + _rms(mix, g1_ref[...])
    h2 = _rms(x1, g2_ref[...]).astype(BF16)
    gu = jnp.dot(h2, wgu_ref[...], preferred_element_type=F32)
    gate, up = gu[:, :d_ff], gu[:, d_ff:]
    act = (gate * _sigmoid(gate) * up).astype(BF16)
    f = jnp.dot(act, wd_ref[...], preferred_element_type=F32)
    y_ref[...] = x1 + _rms(f, g3_ref[...])


def _finish(x, pa, sgb, b, w_out, g_post, g_ffn_pre, w_gate_up, w_down, g_ffn_post, tm):
    N, D = x.shape
    d_ff = w_down.shape[0]
    row = pl.BlockSpec((tm, D), lambda i: (i, 0))
    const = lambda i: (0, 0)
    return pl.pallas_call(
        functools.partial(_out_kernel, d_ff=d_ff), grid=(N // tm,),
        in_specs=[row, row, row, row,
                  _resident((D, D), const), _resident((1, D), const), _resident((1, D), const),
                  _resident((D, 2 * d_ff), const), _resident((d_ff, D), const), _resident((1, D), const)],
        out_specs=row, out_shape=jax.ShapeDtypeStruct((N, D), F32), name="finish",
        compiler_params=pltpu.CompilerParams(dimension_semantics=("arbitrary",), vmem_limit_bytes=VMEM_LIMIT),
    )(x, pa, sgb, b, w_out, g_post, g_ffn_pre, w_gate_up, w_down, g_ffn_post)


def _tile_keys(n_keys, ts):
    return -(-n_keys // ts) * ts


def _layer(x, pos0, hist, k_cache, v_cache, ki_cache, wts, tm, tq, ts, tm_out):
    (w_in_r, w_pool, pool_scale, w_out, w_gate_up, w_down, g_pre, g_post, g_ffn_pre, g_ffn_post) = wts
    B, T, D = x.shape
    hist16 = jnp.concatenate([jnp.zeros((B, HALO // 2 - POOL_HIST, POOL_WIDTH), F32), hist], axis=1)
    u, q_att, k, v, qs, kiw, pa, sgb = _project(x, hist16, pos0, w_in_r, g_pre, w_pool, pool_scale, tm)
    ki = kiw[:, :, :IDX_DIM]
    if k_cache is None:
        k_all, v_all, ki_all = k, v, ki
    else:
        k_all = jnp.concatenate([k_cache.reshape(B, -1, N_KV_HEADS * HEAD_DIM), k], axis=1)
        v_all = jnp.concatenate([v_cache.reshape(B, -1, N_KV_HEADS * HEAD_DIM), v], axis=1)
        ki_all = jnp.concatenate([ki_cache, ki], axis=1)
    n_keys = k_all.shape[1]
    lp = _tile_keys(n_keys, ts)
    pad = ((0, 0), (0, lp - n_keys), (0, 0))
    nt = lp // ts
    kt = jnp.pad(k_all.astype(BF16), pad).reshape(B, nt, ts, N_KV_HEADS, HEAD_DIM).transpose(0, 3, 1, 4, 2)
    vt = jnp.pad(v_all.astype(BF16), pad).reshape(B, nt, ts, N_KV_HEADS, HEAD_DIM).transpose(0, 3, 1, 2, 4)
    ki4 = _split_keys(jnp.swapaxes(jnp.pad(ki_all, pad), 1, 2), ts)
    b = _attend(qs, kiw, q_att, ki4, kt, vt, pos0, n_keys, tq, ts)
    y = _finish(x.reshape(B * T, D), pa.reshape(B * T, D), sgb.reshape(B * T, D), b.reshape(B * T, D),
                w_out, g_post, g_ffn_pre, w_gate_up, w_down, g_ffn_post, tm_out).reshape(B, T, D)
    new_pool = jnp.concatenate([hist, u], axis=1)[:, T:]
    return (y, k.reshape(B, T, N_KV_HEADS, HEAD_DIM), v.reshape(B, T, N_KV_HEADS, HEAD_DIM), ki, new_pool)


def _relayout_w_in(w_in):
    d = w_in.shape[0]
    o_kiw = POOL_WIDTH + N_HEADS * HEAD_DIM + 2 * N_KV_HEADS * HEAD_DIM + N_IDX_HEADS * IDX_DIM
    narrow = IDX_DIM + N_IDX_HEADS
    padded = jnp.concatenate([w_in[:, :o_kiw + narrow], jnp.zeros((d, LANES - narrow), w_in.dtype),
                              w_in[:, o_kiw + narrow:]], axis=1)
    return padded.astype(BF16)


def kernel(x_prompt, x_sample, cache_k, cache_v, cache_k_idx, state_pool, w_in, w_pool, pool_scale, w_out,
           w_gate_up, w_down, norm_mix_pre, norm_mix_post, norm_ffn_pre, norm_ffn_post):
    depth = w_in.shape[0]
    past = cache_k.shape[2]
    t_p, t_s = x_prompt.shape[1], x_sample.shape[1]
    hist_p = jnp.zeros((x_prompt.shape[0], POOL_HIST, POOL_WIDTH), x_prompt.dtype)
    xp, xs = x_prompt, x_sample
    outs = [[] for _ in range(8)]
    for l in range(depth):
        wts = (_relayout_w_in(w_in[l]), w_pool[l].astype(BF16), pool_scale[l][None, :], w_out[l].astype(BF16),
               w_gate_up[l].astype(BF16), w_down[l].astype(BF16), norm_mix_pre[l][None, :],
               norm_mix_post[l][None, :], norm_ffn_pre[l][None, :], norm_ffn_post[l][None, :])
        tm_p = min(256, t_p)
        tq_p = min(128, t_p)
        xp, k1, v1, ki1, p1 = _layer(xp, 0, hist_p, None, None, None, wts, tm_p, tq_p, 512, tm_p)
        n_s = xs.shape[0] * t_s
        xs, k2, v2, ki2, p2 = _layer(xs, past, state_pool[l], cache_k[l], cache_v[l], cache_k_idx[l], wts,
                                     t_s, t_s, 512, min(256, n_s))
        for lst, val in zip(outs, (k1, v1, ki1, p1, k2, v2, ki2, p2)):
            lst.append(val)
    return (xp, xs) + tuple(jnp.stack(o) for o in outs)
```

```python
import functools

import jax
import jax.numpy as jnp
from jax import lax
from jax.experimental import pallas as pl
from jax.experimental.pallas import tpu as pltpu

F32 = jnp.float32
BF16 = jnp.bfloat16
I32 = jnp.int32
I16 = jnp.int16

LANES = 128
CHUNK = 64
POOL_WINDOWS = (2, 4, 8, 16)
N_POOL_GROUPS = 4
POOL_GROUP_WIDTH = 128
POOL_WIDTH = N_POOL_GROUPS * POOL_GROUP_WIDTH
POOL_HIST = 15
N_HEADS = 8
N_KV_HEADS = 2
HEAD_DIM = 128
GQA_GROUP = N_HEADS // N_KV_HEADS
ROPE_THETA = 500000.0
N_IDX_HEADS = 8
IDX_DIM = 64
TOPK_MAX = 256
RMS_EPS = 1e-6
ATTN_SCALE = HEAD_DIM ** -0.5
IDX_SCALE = (N_IDX_HEADS ** -0.5) * (IDX_DIM ** -0.5)

LOG2_E = 1.4426950408889634
VALUE_PIVOT_STEPS = 6
ROW_CHUNK = 32
HALO = 32
VMEM_LIMIT = 56 * 1024 * 1024

_NEG_INF = float("-inf")
_POS_INF = float("inf")
_INADMISSIBLE = -(2 ** 31)
_NO_KEY = 2 ** 31 - 1
_M_INIT = -1e30


def _resident(block_shape, index_map):
    return pl.BlockSpec(block_shape, index_map, pipeline_mode=pl.Buffered(1))


def _rope(xs, cos, sin, half, period):
    lane = lax.broadcasted_iota(I32, xs.shape, 1)
    ahead = pltpu.roll(xs, LANES - half, 1)
    behind = pltpu.roll(xs, half, 1)
    partner = jnp.where((lane & (period - 1)) < half, ahead, behind)
    return xs * cos + partner * sin


def _sigmoid(x):
    return 1.0 / (1.0 + jnp.exp(-x))


def _proj_kernel(x_ref, g_ref, w_ref, hist_ref, cq_ref, sq_ref, ci_ref, si_ref, wpool_ref, pscale_ref,
                 u_ref, qatt_ref, k_ref, v_ref, qs_ref, kiw_ref, pa_ref, sgb_ref,
                 e_ref, s2_ref, s4_ref, s8_ref, *, tm, pos0, d_model):
    i = pl.program_id(1)
    x = x_ref[0]
    h = x * lax.rsqrt(jnp.mean(x * x, axis=-1, keepdims=True) + RMS_EPS) * g_ref[...]
    proj = jnp.dot(h.astype(BF16), w_ref[...], preferred_element_type=F32)

    o_q = POOL_WIDTH
    o_k = o_q + N_HEADS * HEAD_DIM
    o_v = o_k + N_KV_HEADS * HEAD_DIM
    o_qi = o_v + N_KV_HEADS * HEAD_DIM
    o_kiw = o_qi + N_IDX_HEADS * IDX_DIM
    o_ga = o_kiw + LANES
    o_gb = o_ga + d_model

    cq, sq, ci, si = cq_ref[...], sq_ref[...], ci_ref[...], si_ref[...]
    half_q = HEAD_DIM // 8
    half_i = IDX_DIM // 8

    for hd in range(N_HEADS):
        qh = _rope(proj[:, o_q + hd * HEAD_DIM:o_q + (hd + 1) * HEAD_DIM], cq, sq, half_q, HEAD_DIM)
        qatt_ref[0, hd] = (qh * (ATTN_SCALE * LOG2_E)).astype(BF16)
    for kh in range(N_KV_HEADS):
        k_ref[0, :, kh * HEAD_DIM:(kh + 1) * HEAD_DIM] = _rope(
            proj[:, o_k + kh * HEAD_DIM:o_k + (kh + 1) * HEAD_DIM], cq, sq, half_q, HEAD_DIM)
    v_ref[0] = proj[:, o_v:o_qi]

    for pr in range(N_IDX_HEADS // 2):
        qi2 = _rope(proj[:, o_qi + pr * LANES:o_qi + (pr + 1) * LANES], ci, si, half_i, IDX_DIM)
        hi = qi2.astype(BF16).astype(F32)
        lo = qi2 - hi
        lane2 = lax.broadcasted_iota(I32, qi2.shape, 1)
        first = jnp.where(lane2 < IDX_DIM, hi, pltpu.roll(lo, IDX_DIM, 1)).astype(BF16)
        second = jnp.where(lane2 < IDX_DIM, pltpu.roll(hi, IDX_DIM, 1), lo).astype(BF16)
        for sub, slab in enumerate((first, second)):
            qs_ref[0, 2 * pr + sub, :, 0:LANES] = slab
            qs_ref[0, 2 * pr + sub, :, LANES:2 * LANES] = slab

    kiw = _rope(proj[:, o_kiw:o_kiw + LANES], ci, si, half_i, IDX_DIM)
    lane = lax.broadcasted_iota(I32, kiw.shape, 1)
    kiw_ref[0] = jnp.where(lane < IDX_DIM, kiw, proj[:, o_kiw:o_kiw + LANES] * IDX_SCALE)

    u = proj[:, 0:POOL_WIDTH]
    u_ref[0] = u

    @pl.when(i == 0)
    def _():
        e_ref[0:HALO // 2, :] = jnp.zeros((HALO // 2, POOL_WIDTH), F32)
        e_ref[HALO // 2:HALO, :] = hist_ref[0]

    e_ref[HALO:HALO + tm, :] = u
    n2, n4, n8 = tm + 24, tm + 16, tm + 8
    s2_ref[8:8 + n2, :] = e_ref[8:8 + n2, :] + e_ref[7:7 + n2, :]
    s4_ref[16:16 + n4, :] = s2_ref[16:16 + n4, :] + s2_ref[14:14 + n4, :]
    s8_ref[24:24 + n8, :] = s4_ref[24:24 + n8, :] + s4_ref[20:20 + n8, :]
    s16 = s8_ref[HALO:HALO + tm, :] + s8_ref[HALO - 8:HALO - 8 + tm, :]
    wins = (s2_ref[HALO:HALO + tm, :], s4_ref[HALO:HALO + tm, :], s8_ref[HALO:HALO + tm, :], s16)
    e_ref[HALO // 2:HALO, :] = e_ref[HALO // 2 + tm:HALO + tm, :]

    pos = pos0 + i * tm + lax.broadcasted_iota(I32, (tm, POOL_GROUP_WIDTH), 0)
    a_parts = []
    for g, w in enumerate(POOL_WINDOWS):
        lo_l, hi_l = g * POOL_GROUP_WIDTH, (g + 1) * POOL_GROUP_WIDTH
        cnt = jnp.minimum(pos + 1, w).astype(F32)
        pooled = wins[g][:, lo_l:hi_l] / cnt - u[:, lo_l:hi_l]
        a_parts.append(jnp.dot(pooled.astype(BF16), wpool_ref[g], preferred_element_type=F32))
    a = jnp.concatenate(a_parts, axis=-1) * pscale_ref[...]
    pa_ref[0] = _sigmoid(proj[:, o_ga:o_gb]) * a
    sgb_ref[0] = _sigmoid(proj[:, o_gb:o_gb + d_model])


def _project(x, hist16, pos0, w_in_r, g_pre, w_pool, pool_scale, tm):
    B, T, D = x.shape
    W = w_in_r.shape[1]
    pos = pos0 + jnp.arange(T, dtype=I32)
    cq, sq = _rope_tables(pos, HEAD_DIM)
    ci, si = _rope_tables(pos, IDX_DIM)
    row = lambda b, i: (b, i, 0)
    tab = lambda b, i: (i, 0)
    const2 = lambda b, i: (0, 0)
    out_shape = (
        jax.ShapeDtypeStruct((B, T, POOL_WIDTH), F32),
        jax.ShapeDtypeStruct((B, N_HEADS, T, HEAD_DIM), BF16),
        jax.ShapeDtypeStruct((B, T, N_KV_HEADS * HEAD_DIM), F32),
        jax.ShapeDtypeStruct((B, T, N_KV_HEADS * HEAD_DIM), F32),
        jax.ShapeDtypeStruct((B, N_IDX_HEADS, T, 4 * IDX_DIM), BF16),
        jax.ShapeDtypeStruct((B, T, LANES), F32),
        jax.ShapeDtypeStruct((B, T, D), F32),
        jax.ShapeDtypeStruct((B, T, D), F32),
    )
    out_specs = (
        pl.BlockSpec((1, tm, POOL_WIDTH), row),
        pl.BlockSpec((1, N_HEADS, tm, HEAD_DIM), lambda b, i: (b, 0, i, 0)),
        pl.BlockSpec((1, tm, N_KV_HEADS * HEAD_DIM), row),
        pl.BlockSpec((1, tm, N_KV_HEADS * HEAD_DIM), row),
        pl.BlockSpec((1, N_IDX_HEADS, tm, 4 * IDX_DIM), lambda b, i: (b, 0, i, 0)),
        pl.BlockSpec((1, tm, LANES), row),
        pl.BlockSpec((1, tm, D), row),
        pl.BlockSpec((1, tm, D), row),
    )
    in_specs = [
        pl.BlockSpec((1, tm, D), row),
        _resident((1, D), const2),
        _resident((D, W), const2),
        pl.BlockSpec((1, HALO // 2, POOL_WIDTH), lambda b, i: (b, 0, 0)),
        pl.BlockSpec((tm, LANES), tab), pl.BlockSpec((tm, LANES), tab),
        pl.BlockSpec((tm, LANES), tab), pl.BlockSpec((tm, LANES), tab),
        _resident((N_POOL_GROUPS, POOL_GROUP_WIDTH, D // N_POOL_GROUPS), lambda b, i: (0, 0, 0)),
        _resident((1, D), const2),
    ]
    scratch = [pltpu.VMEM((HALO + tm, POOL_WIDTH), F32) for _ in range(4)]
    return pl.pallas_call(
        functools.partial(_proj_kernel, tm=tm, pos0=pos0, d_model=D),
        grid=(B, T // tm), in_specs=in_specs, out_specs=out_specs, out_shape=out_shape,
        scratch_shapes=scratch, name="proj",
        compiler_params=pltpu.CompilerParams(dimension_semantics=("arbitrary", "arbitrary"),
                                             vmem_limit_bytes=VMEM_LIMIT),
    )(x, g_pre, w_in_r, hist16, cq, sq, ci, si, w_pool, pool_scale)


def _rope_tables(pos, dim):
    rot = dim // 4
    half = rot // 2
    inv = ROPE_THETA ** (-jnp.arange(half, dtype=F32) / half)
    ang = pos.astype(F32)[:, None] * inv[None, :]
    cos, sin = jnp.cos(ang), jnp.sin(ang)
    rest = dim - rot
    n = pos.shape[0]
    c = jnp.concatenate([cos, cos, jnp.ones((n, rest), F32)], axis=-1)
    s = jnp.concatenate([-sin, sin, jnp.zeros((n, rest), F32)], axis=-1)
    return jnp.tile(c, (1, LANES // dim)), jnp.tile(s, (1, LANES // dim))


def _split_kernel(kt_ref, out_ref):
    k = kt_ref[0]
    hi = k.astype(BF16)
    lo = (k - hi.astype(F32)).astype(BF16)
    out_ref[0, 0, 0:IDX_DIM, :] = hi
    out_ref[0, 0, IDX_DIM:2 * IDX_DIM, :] = hi
    out_ref[0, 0, 2 * IDX_DIM:3 * IDX_DIM, :] = lo
    out_ref[0, 0, 3 * IDX_DIM:4 * IDX_DIM, :] = lo


def _split_keys(ki_t, ts):
    B, _, Lp = ki_t.shape
    return pl.pallas_call(
        _split_kernel, grid=(B, Lp // ts),
        in_specs=[pl.BlockSpec((1, IDX_DIM, ts), lambda b, t: (b, 0, t))],
        out_specs=pl.BlockSpec((1, 1, 4 * IDX_DIM, ts), lambda b, t: (b, t, 0, 0)),
        out_shape=jax.ShapeDtypeStruct((B, Lp // ts, 4 * IDX_DIM, ts), BF16), name="split_keys",
        compiler_params=pltpu.CompilerParams(dimension_semantics=("arbitrary", "arbitrary")),
    )(ki_t)


def _key_of(x):
    b = pltpu.bitcast(x, I32)
    return b ^ ((b >> 31) & 0x7FFFFFFF)


def _float_of(k):
    return pltpu.bitcast(k ^ ((k >> 31) & 0x7FFFFFFF), F32)


def _row_sum(x):
    return jnp.broadcast_to(jnp.sum(x, axis=-1, keepdims=True), x.shape)


def _attn_kernel(qs_ref, kiw_ref, q_ref, ki_ref, kt_ref, v_ref, o_ref,
                 sc_ref, h16_ref, l16_ref, wb_ref, s_ref, m_ref, acc_ref, bias_ref, p_ref, alpha_ref,
                 *, tq, ts, q0, n_keys, k_sel):
    qi = pl.program_id(1)
    n_tiles_total = ki_ref.shape[1]
    nc = ts // LANES
    row = lax.broadcasted_iota(I32, (tq, LANES), 0)
    lane = lax.broadcasted_iota(I32, (tq, LANES), 1)
    qpos = q0 + qi * tq + row
    n_adm = jnp.minimum((qpos // CHUNK + 1) * CHUNK, n_keys)
    last_adm = jnp.minimum(((q0 + (qi + 1) * tq - 1) // CHUNK + 1) * CHUNK, n_keys)
    n_t = jnp.minimum((last_adm + ts - 1) // ts, n_tiles_total)

    kiw = kiw_ref[0]
    for hd in range(N_IDX_HEADS):
        wb_ref[hd] = jnp.broadcast_to(kiw[:, IDX_DIM + hd:IDX_DIM + hd + 1], (tq, LANES))
    qs2 = qs_ref[0].reshape(N_IDX_HEADS * tq, 4 * IDX_DIM)

    def idx_logits(t, slot):
        s_ref[slot] = jnp.dot(qs2, ki_ref[0, t], preferred_element_type=F32)

    def score_tile(t, slot, carry):
        rmin, rmax = carry
        for c in range(nc):
            acc = None
            for hd in range(N_IDX_HEADS):
                r = jnp.maximum(s_ref[slot, hd * tq:(hd + 1) * tq, c * LANES:(c + 1) * LANES], 0.0) * wb_ref[hd]
                acc = r if acc is None else acc + r
            adm = (t * ts + c * LANES + lane) < n_adm
            key = jnp.where(adm, _key_of(acc), _INADMISSIBLE)
            sc_ref[t, :, c * LANES:(c + 1) * LANES] = key
            h16_ref[t, :, c * LANES:(c + 1) * LANES] = (key >> 16).astype(I16)
            rmax = jnp.maximum(rmax, jnp.where(adm, acc, _NEG_INF))
            rmin = jnp.minimum(rmin, jnp.where(adm, acc, _POS_INF))
        return rmin, rmax

    idx_logits(0, 0)

    def score_pair(u, carry):
        t0 = 2 * u
        t1 = jnp.minimum(t0 + 1, n_t - 1)
        idx_logits(t1, 1)
        carry = score_tile(t0, 0, carry)
        idx_logits(jnp.minimum(t0 + 2, n_t - 1), 0)
        return score_tile(t1, 1, carry)

    rmin, rmax = lax.fori_loop(0, (n_t + 1) // 2, score_pair,
                               (jnp.full((tq, LANES), _POS_INF, F32), jnp.full((tq, LANES), _NEG_INF, F32)))
    rmin = jnp.broadcast_to(jnp.min(rmin, axis=-1, keepdims=True), (tq, LANES))
    rmax = jnp.broadcast_to(jnp.max(rmax, axis=-1, keepdims=True), (tq, LANES))

    def count16(ref, thr):
        thr16 = thr.astype(I16)
        one, zero = jnp.ones((tq, LANES), I16), jnp.zeros((tq, LANES), I16)

        def body(t, cnt):
            for c in range(nc):
                cnt = cnt + jnp.where(ref[t, :, c * LANES:(c + 1) * LANES] >= thr16, one, zero)
            return cnt
        return _row_sum(lax.fori_loop(0, n_t, body, zero).astype(F32))

    def bis_cond(st):
        return st[0] > 0

    kf = float(k_sel)
    take_all = n_adm <= k_sel
    lo0 = (_key_of(rmin) - 1) >> 16
    hi0 = ((_key_of(rmax) + 2) >> 16) + 1
    act0 = jnp.where(take_all, 0, jnp.where(hi0 > lo0 + 1, 1, 0))
    zf = jnp.zeros((tq, LANES), F32)
    zi = jnp.zeros((tq, LANES), I32)

    def stage1(st):
        _, it, lo, hi, clo, chi, found, act = st
        mid = (lo + hi) >> 1
        vmid = _key_of((_float_of(lo << 16) + _float_of(hi << 16)) * 0.5) >> 16
        by_value = jnp.where(it < VALUE_PIVOT_STEPS, jnp.where(vmid > lo, jnp.where(vmid < hi, 1, 0), 0), 0)
        mid = jnp.where(by_value > 0, vmid, mid)
        cnt = count16(h16_ref, mid)
        up = jnp.where(act > 0, jnp.where(cnt >= kf, 1, 0), 0)
        dn = act - up
        lo = jnp.where(up > 0, mid, lo)
        clo = jnp.where(up > 0, cnt, clo)
        hi = jnp.where(dn > 0, mid, hi)
        chi = jnp.where(dn > 0, cnt, chi)
        hit = jnp.where(up > 0, jnp.where(cnt == kf, 1, 0), 0)
        found = found + hit
        act = jnp.where(act > 0, jnp.where(hit > 0, 0, jnp.where(hi > lo + 1, 1, 0)), 0)
        return jnp.max(act), it + 1, lo, hi, clo, chi, found, act

    _, _, bkt, _, clo, chi, found1, _ = lax.while_loop(
        bis_cond, stage1, (jnp.max(act0), jnp.int32(0), lo0, hi0, n_adm.astype(F32), zf, zi, act0))

    b16 = bkt.astype(I16)
    lowest = jnp.full((tq, LANES), -32768, I16)

    def low_halves(t, carry):
        for c in range(nc):
            cs = slice(c * LANES, (c + 1) * LANES)
            low = ((sc_ref[t, :, cs] & 0xFFFF) - 32768).astype(I16)
            l16_ref[t, :, cs] = jnp.where(h16_ref[t, :, cs] == b16, low, lowest)
        return carry

    lax.fori_loop(0, n_t, low_halves, 0)
    need2 = kf - chi
    act2_0 = jnp.where(take_all, 0, jnp.where(found1 > 0, 0, 1))

    def stage2(st):
        _, lo, hi, clo2, chi2, found, act = st
        mid = (lo + hi) >> 1
        cnt = count16(l16_ref, mid)
        up = jnp.where(act > 0, jnp.where(cnt >= need2, 1, 0), 0)
        dn = act - up
        lo = jnp.where(up > 0, mid, lo)
        clo2 = jnp.where(up > 0, cnt, clo2)
        hi = jnp.where(dn > 0, mid, hi)
        chi2 = jnp.where(dn > 0, cnt, chi2)
        hit = jnp.where(up > 0, jnp.where(cnt == need2, 1, 0), 0)
        found = found + hit
        act = jnp.where(act > 0, jnp.where(hit > 0, 0, jnp.where(hi > lo + 1, 1, 0)), 0)
        return jnp.max(act), lo, hi, clo2, chi2, found, act

    _, low_t, _, clo2, chi2, found2, _ = lax.while_loop(
        bis_cond, stage2,
        (jnp.max(act2_0), jnp.full((tq, LANES), -32768, I32), jnp.full((tq, LANES), 32768, I32),
         clo - chi, zf, zi, act2_0))

    thr = (bkt << 16) + jnp.where(found1 > 0, 0, low_t + 32768)
    thr = jnp.where(take_all, _INADMISSIBLE + 1, thr)
    thr_up = jnp.where(take_all, _INADMISSIBLE + 1, thr + 1)

    exact = found1 + found2
    c_ge = chi + clo2
    need = kf - (chi + chi2)
    tied = jnp.where(take_all, 0, jnp.where(exact > 0, 0, jnp.where(c_ge > kf, 1, 0)))

    def count_tied_below(cut):
        def body(t, cnt):
            for c in range(nc):
                s = sc_ref[t, :, c * LANES:(c + 1) * LANES]
                col = t * ts + c * LANES + lane
                cnt = cnt + jnp.where(s == thr, jnp.where(col < cut, 1.0, 0.0), 0.0)
            return cnt
        return _row_sum(lax.fori_loop(0, n_t, body, jnp.zeros((tq, LANES), F32)))

    def cut_body(st):
        _, lo_c, hi_c, act = st
        mid = (lo_c + hi_c) >> 1
        cnt = count_tied_below(mid)
        on = act > 0
        le = cnt <= need
        lo_c = jnp.where(on, jnp.where(le, mid, lo_c), lo_c)
        hi_c = jnp.where(on, jnp.where(le, hi_c, mid), hi_c)
        act = jnp.where(on, jnp.where(hi_c - lo_c > 1, 1, 0), 0)
        return jnp.max(act), lo_c, hi_c, act

    _, cut_lo, _, _ = lax.while_loop(
        bis_cond, cut_body,
        (jnp.max(tied), jnp.zeros((tq, LANES), I32), jnp.full((tq, LANES), n_tiles_total * ts + 1, I32), tied))
    cut = jnp.where(tied > 0, cut_lo, n_tiles_total * ts + 1)

    m_ref[...] = jnp.full(m_ref.shape, _M_INIT, F32)
    acc_ref[...] = jnp.zeros(acc_ref.shape, F32)
    rc = min(tq, ROW_CHUNK)
    ones = jnp.ones((ts, LANES), BF16)
    last = n_t - 1

    def logits(t, slot):
        for g in range(N_KV_HEADS):
            qg = q_ref[0, g * GQA_GROUP:(g + 1) * GQA_GROUP].reshape(GQA_GROUP * tq, HEAD_DIM)
            s_ref[slot, g * GQA_GROUP * tq:(g + 1) * GQA_GROUP * tq] = jnp.dot(
                qg, kt_ref[0, g, t], preferred_element_type=F32)

    def softmax_pv(t, slot, live):
        for c in range(nc):
            col = t * ts + c * LANES + lane
            bound = jnp.where(live, jnp.where(col < cut, thr, thr_up), _NO_KEY)
            bias_ref[slot, :, c * LANES:(c + 1) * LANES] = jnp.where(
                sc_ref[t, :, c * LANES:(c + 1) * LANES] >= bound, 0.0, _NEG_INF)
        for g in range(N_KV_HEADS):
            for j in range(GQA_GROUP):
                hd = g * GQA_GROUP + j
                for r0 in range(0, tq, rc):
                    r1 = j * tq + r0
                    r2 = hd * tq + r0
                    s = [s_ref[slot, r2:r2 + rc, c * LANES:(c + 1) * LANES]
                         + bias_ref[slot, r0:r0 + rc, c * LANES:(c + 1) * LANES] for c in range(nc)]
                    mx = s[0]
                    for c in range(1, nc):
                        mx = jnp.maximum(mx, s[c])
                    m_old = m_ref[hd, r0:r0 + rc]
                    m_new = jnp.maximum(m_old, jnp.broadcast_to(jnp.max(mx, axis=-1, keepdims=True), (rc, LANES)))
                    alpha_ref[slot, g, r1:r1 + rc] = jnp.exp2(m_old - m_new)
                    for c in range(nc):
                        p_ref[slot, g, r1:r1 + rc, c * LANES:(c + 1) * LANES] = jnp.exp2(s[c] - m_new).astype(BF16)
                    m_ref[hd, r0:r0 + rc] = m_new
            v_aug = jnp.concatenate([v_ref[0, g, t], ones], axis=-1)
            pv = jnp.dot(p_ref[slot, g], v_aug, preferred_element_type=F32)
            alpha = alpha_ref[slot, g]
            for half in range(2):
                cols = slice(half * LANES, (half + 1) * LANES)
                acc_ref[g, :, cols] = alpha * acc_ref[g, :, cols] + pv[:, cols]

    logits(0, 0)

    def attn_pair(u, carry):
        t0 = 2 * u
        t1 = jnp.minimum(t0 + 1, last)
        logits(t1, 1)
        softmax_pv(t0, 0, True)
        logits(jnp.minimum(t0 + 2, last), 0)
        softmax_pv(t1, 1, t0 + 1 < n_t)
        return carry

    lax.fori_loop(0, (n_t + 1) // 2, attn_pair, 0)
    for hd in range(N_HEADS):
        g, j = divmod(hd, GQA_GROUP)
        o_ref[0, :, hd * HEAD_DIM:(hd + 1) * HEAD_DIM] = (
            acc_ref[g, j * tq:(j + 1) * tq, 0:HEAD_DIM] / acc_ref[g, j * tq:(j + 1) * tq, HEAD_DIM:2 * HEAD_DIM])


def _attend(qs, kiw, q_att, ki4, kt, v, q0, n_keys, tq, ts):
    B, _, T, _ = qs.shape
    n_tiles = ki4.shape[1]
    k_sel = min(TOPK_MAX, n_keys // 4)
    kern = functools.partial(_attn_kernel, tq=tq, ts=ts, q0=q0, n_keys=n_keys, k_sel=k_sel)
    return pl.pallas_call(
        kern, grid=(B, T // tq),
        in_specs=[
            pl.BlockSpec((1, N_IDX_HEADS, tq, 4 * IDX_DIM), lambda b, i: (b, 0, i, 0)),
            pl.BlockSpec((1, tq, LANES), lambda b, i: (b, i, 0)),
            pl.BlockSpec((1, N_HEADS, tq, HEAD_DIM), lambda b, i: (b, 0, i, 0)),
            _resident((1, n_tiles, 4 * IDX_DIM, ts), lambda b, i: (b, 0, 0, 0)),
            _resident((1, N_KV_HEADS, n_tiles, HEAD_DIM, ts), lambda b, i: (b, 0, 0, 0, 0)),
            _resident((1, N_KV_HEADS, n_tiles, ts, HEAD_DIM), lambda b, i: (b, 0, 0, 0, 0)),
        ],
        out_specs=pl.BlockSpec((1, tq, N_HEADS * HEAD_DIM), lambda b, i: (b, i, 0)),
        out_shape=jax.ShapeDtypeStruct((B, T, N_HEADS * HEAD_DIM), F32),
        scratch_shapes=[
            pltpu.VMEM((n_tiles, tq, ts), I32),
            pltpu.VMEM((n_tiles, tq, ts), I16),
            pltpu.VMEM((n_tiles, tq, ts), I16),
            pltpu.VMEM((N_IDX_HEADS, tq, LANES), F32),
            pltpu.VMEM((2, N_HEADS * tq, ts), F32),
            pltpu.VMEM((N_HEADS, tq, LANES), F32),
            pltpu.VMEM((N_KV_HEADS, GQA_GROUP * tq, 2 * HEAD_DIM), F32),
            pltpu.VMEM((2, tq, ts), F32),
            pltpu.VMEM((2, N_KV_HEADS, GQA_GROUP * tq, ts), BF16),
            pltpu.VMEM((2, N_KV_HEADS, GQA_GROUP * tq, LANES), F32),
        ],
        name="attend",
        compiler_params=pltpu.CompilerParams(dimension_semantics=("arbitrary", "arbitrary"),
                                             vmem_limit_bytes=VMEM_LIMIT),
    )(qs, kiw, q_att, ki4, kt, v)


def _rms(x, g):
    return x * lax.rsqrt(jnp.mean(x * x, axis=-1, keepdims=True) + RMS_EPS) * g


def _out_kernel(x_ref, pa_ref, sgb_ref, b_ref, wo_ref, g1_ref, g2_ref, wgu_ref, wd_ref, g3_ref, y_ref, *, d_ff):
    merged = pa_ref[...] + sgb_ref[...] * b_ref[...]
    mix = jnp.dot(merged.astype(BF16), wo_ref[...], preferred_element_type=F32)
    x1 = x_ref[...] + _rms(mix, g1_ref[...])
    h2 = _rms(x1, g2_ref[...]).astype(BF16)
    gu = jnp.dot(h2, wgu_ref[...], preferred_element_type=F32)
    gate, up = gu[:, :d_ff], gu[:, d_ff:]
    act = (gate * _sigmoid(gate) * up).astype(BF16)
    f = jnp.dot(act, wd_ref[...], preferred_element_type=F32)
    y_ref[...] = x1 + _rms(f, g3_ref[...])


def _finish(x, pa, sgb, b, w_out, g_post, g_ffn_pre, w_gate_up, w_down, g_ffn_post, tm):
    N, D = x.shape
    d_ff = w_down.shape[0]
    row = pl.BlockSpec((tm, D), lambda i: (i, 0))
    const = lambda i: (0, 0)
    return pl.pallas_call(
        functools.partial(_out_kernel, d_ff=d_ff), grid=(N // tm,),
        in_specs=[row, row, row, row,
                  _resident((D, D), const), _resident((1, D), const), _resident((1, D), const),
                  _resident((D, 2 * d_ff), const), _resident((d_ff, D), const), _resident((1, D), const)],
        out_specs=row, out_shape=jax.ShapeDtypeStruct((N, D), F32), name="finish",
        compiler_params=pltpu.CompilerParams(dimension_semantics=("arbitrary",), vmem_limit_bytes=VMEM_LIMIT),
    )(x, pa, sgb, b, w_out, g_post, g_ffn_pre, w_gate_up, w_down, g_ffn_post)


def _tile_keys(n_keys, ts):
    return -(-n_keys // ts) * ts


def _layer(x, pos0, hist, k_cache, v_cache, ki_cache, wts, tm, tq, ts, tm_out):
    (w_in_r, w_pool, pool_scale, w_out, w_gate_up, w_down, g_pre, g_post, g_ffn_pre, g_ffn_post) = wts
    B, T, D = x.shape
    hist16 = jnp.concatenate([jnp.zeros((B, HALO // 2 - POOL_HIST, POOL_WIDTH), F32), hist], axis=1)
    u, q_att, k, v, qs, kiw, pa, sgb = _project(x, hist16, pos0, w_in_r, g_pre, w_pool, pool_scale, tm)
    ki = kiw[:, :, :IDX_DIM]
    if k_cache is None:
        k_all, v_all, ki_all = k, v, ki
    else:
        k_all = jnp.concatenate([k_cache.reshape(B, -1, N_KV_HEADS * HEAD_DIM), k], axis=1)
        v_all = jnp.concatenate([v_cache.reshape(B, -1, N_KV_HEADS * HEAD_DIM), v], axis=1)
        ki_all = jnp.concatenate([ki_cache, ki], axis=1)
    n_keys = k_all.shape[1]
    lp = _tile_keys(n_keys, ts)
    pad = ((0, 0), (0, lp - n_keys), (0, 0))
    nt = lp // ts
    kt = jnp.pad(k_all.astype(BF16), pad).reshape(B, nt, ts, N_KV_HEADS, HEAD_DIM).transpose(0, 3, 1, 4, 2)
    vt = jnp.pad(v_all.astype(BF16), pad).reshape(B, nt, ts, N_KV_HEADS, HEAD_DIM).transpose(0, 3, 1, 2, 4)
    ki4 = _split_keys(jnp.swapaxes(jnp.pad(ki_all, pad), 1, 2), ts)
    b = _attend(qs, kiw, q_att, ki4, kt, vt, pos0, n_keys, tq, ts)
    y = _finish(x.reshape(B * T, D), pa.reshape(B * T, D), sgb.reshape(B * T, D), b.reshape(B * T, D),
                w_out, g_post, g_ffn_pre, w_gate_up, w_down, g_ffn_post, tm_out).reshape(B, T, D)
    new_pool = jnp.concatenate([hist, u], axis=1)[:, T:]
    return (y, k.reshape(B, T, N_KV_HEADS, HEAD_DIM), v.reshape(B, T, N_KV_HEADS, HEAD_DIM), ki, new_pool)


def _relayout_w_in(w_in):
    d = w_in.shape[0]
    o_kiw = POOL_WIDTH + N_HEADS * HEAD_DIM + 2 * N_KV_HEADS * HEAD_DIM + N_IDX_HEADS * IDX_DIM
    narrow = IDX_DIM + N_IDX_HEADS
    padded = jnp.concatenate([w_in[:, :o_kiw + narrow], jnp.zeros((d, LANES - narrow), w_in.dtype),
                              w_in[:, o_kiw + narrow:]], axis=1)
    return padded.astype(BF16)


def kernel(x_prompt, x_sample, cache_k, cache_v, cache_k_idx, state_pool, w_in, w_pool, pool_scale, w_out,
           w_gate_up, w_down, norm_mix_pre, norm_mix_post, norm_ffn_pre, norm_ffn_post):
    depth = w_in.shape[0]
    past = cache_k.shape[2]
    t_p, t_s = x_prompt.shape[1], x_sample.shape[1]
    hist_p = jnp.zeros((x_prompt.shape[0], POOL_HIST, POOL_WIDTH), x_prompt.dtype)
    xp, xs = x_prompt, x_sample
    outs = [[] for _ in range(8)]
    for l in range(depth):
        wts = (_relayout_w_in(w_in[l]), w_pool[l].astype(BF16), pool_scale[l][None, :], w_out[l].astype(BF16),
               w_gate_up[l].astype(BF16), w_down[l].astype(BF16), norm_mix_pre[l][None, :],
               norm_mix_post[l][None, :], norm_ffn_pre[l][None, :], norm_ffn_post[l][None, :])
        tm_p = min(256, t_p)
        tq_p = min(128, t_p)
        xp, k1, v1, ki1, p1 = _layer(xp, 0, hist_p, None, None, None, wts, tm_p, tq_p, 512, tm_p)
        n_s = xs.shape[0] * t_s
        xs, k2, v2, ki2, p2 = _layer(xs, past, state_pool[l], cache_k[l], cache_v[l], cache_k_idx[l], wts,
                                     t_s, t_s, 512, min(256, n_s))
        for lst, val in zip(outs, (k1, v1, ki1, p1, k2, v2, ki2, p2)):
            lst.append(val)
    return (xp, xs) + tuple(jnp.stack(o) for o in outs)
```

```python
import functools

import jax
import jax.numpy as jnp
from jax import lax
from jax.experimental import pallas as pl
from jax.experimental.pallas import tpu as pltpu

F32 = jnp.float32
BF16 = jnp.bfloat16
I32 = jnp.int32
I16 = jnp.int16

LANES = 128
CHUNK = 64
POOL_WINDOWS = (2, 4, 8, 16)
N_POOL_GROUPS = 4
POOL_GROUP_WIDTH = 128
POOL_WIDTH = N_POOL_GROUPS * POOL_GROUP_WIDTH
POOL_HIST = 15
N_HEADS = 8
N_KV_HEADS = 2
HEAD_DIM = 128
GQA_GROUP = N_HEADS // N_KV_HEADS
ROPE_THETA = 500000.0
N_IDX_HEADS = 8
IDX_DIM = 64
TOPK_MAX = 256
RMS_EPS = 1e-6
ATTN_SCALE = HEAD_DIM ** -0.5
IDX_SCALE = (N_IDX_HEADS ** -0.5) * (IDX_DIM ** -0.5)

LOG2_E = 1.4426950408889634
VALUE_PIVOT_STEPS = 28
ROW_CHUNK = 32
HALO = 32
VMEM_LIMIT = 56 * 1024 * 1024

_NEG_INF = float("-inf")
_POS_INF = float("inf")
_INADMISSIBLE = -(2 ** 31)
_NO_KEY = 2 ** 31 - 1
_KEY_MIN_NORMAL = 0x00800000
_KEY_NEG_ZERO = -1
_M_INIT = -1e30


def _resident(block_shape, index_map):
    return pl.BlockSpec(block_shape, index_map, pipeline_mode=pl.Buffered(1))


def _rope(xs, cos, sin, half, period):
    lane = lax.broadcasted_iota(I32, xs.shape, 1)
    ahead = pltpu.roll(xs, LANES - half, 1)
    behind = pltpu.roll(xs, half, 1)
    partner = jnp.where((lane & (period - 1)) < half, ahead, behind)
    return xs * cos + partner * sin


def _sigmoid(x):
    return 1.0 / (1.0 + jnp.exp(-x))


def _proj_kernel(x_ref, g_ref, w_ref, hist_ref, cq_ref, sq_ref, ci_ref, si_ref, wpool_ref, pscale_ref,
                 u_ref, qatt_ref, k_ref, v_ref, qs_ref, kiw_ref, pa_ref, sgb_ref, *rest, tm, pos0, d_model, key_tiles):
    if key_tiles:
        kt_ref, vt_ref, ki4_ref, e_ref, s2_ref, s4_ref, s8_ref = rest
    else:
        e_ref, s2_ref, s4_ref, s8_ref = rest
    i = pl.program_id(1)
    x = x_ref[0]
    h = x * lax.rsqrt(jnp.mean(x * x, axis=-1, keepdims=True) + RMS_EPS) * g_ref[...]
    proj = jnp.dot(h.astype(BF16), w_ref[...], preferred_element_type=F32)

    o_q = POOL_WIDTH
    o_k = o_q + N_HEADS * HEAD_DIM
    o_v = o_k + N_KV_HEADS * HEAD_DIM
    o_qi = o_v + N_KV_HEADS * HEAD_DIM
    o_kiw = o_qi + N_IDX_HEADS * IDX_DIM
    o_ga = o_kiw + LANES
    o_gb = o_ga + d_model

    cq, sq, ci, si = cq_ref[...], sq_ref[...], ci_ref[...], si_ref[...]
    half_q = HEAD_DIM // 8
    half_i = IDX_DIM // 8

    for hd in range(N_HEADS):
        qh = _rope(proj[:, o_q + hd * HEAD_DIM:o_q + (hd + 1) * HEAD_DIM], cq, sq, half_q, HEAD_DIM)
        qatt_ref[0, hd] = (qh * (ATTN_SCALE * LOG2_E)).astype(BF16)
    for kh in range(N_KV_HEADS):
        k_h = _rope(proj[:, o_k + kh * HEAD_DIM:o_k + (kh + 1) * HEAD_DIM], cq, sq, half_q, HEAD_DIM)
        k_ref[0, :, kh * HEAD_DIM:(kh + 1) * HEAD_DIM] = k_h
        if key_tiles:
            kt_ref[0, kh, 0] = k_h.T.astype(BF16)
            vt_ref[0, kh, 0] = proj[:, o_v + kh * HEAD_DIM:o_v + (kh + 1) * HEAD_DIM].astype(BF16)
    v_ref[0] = proj[:, o_v:o_qi]

    for pr in range(N_IDX_HEADS // 2):
        qi2 = _rope(proj[:, o_qi + pr * LANES:o_qi + (pr + 1) * LANES], ci, si, half_i, IDX_DIM)
        hi = qi2.astype(BF16).astype(F32)
        lo = qi2 - hi
        lane2 = lax.broadcasted_iota(I32, qi2.shape, 1)
        first = jnp.where(lane2 < IDX_DIM, hi, pltpu.roll(lo, IDX_DIM, 1)).astype(BF16)
        second = jnp.where(lane2 < IDX_DIM, pltpu.roll(hi, IDX_DIM, 1), lo).astype(BF16)
        for sub, slab in enumerate((first, second)):
            qs_ref[0, 2 * pr + sub, :, 0:LANES] = slab
            qs_ref[0, 2 * pr + sub, :, LANES:2 * LANES] = slab

    kiw = _rope(proj[:, o_kiw:o_kiw + LANES], ci, si, half_i, IDX_DIM)
    lane = lax.broadcasted_iota(I32, kiw.shape, 1)
    kiw_ref[0] = jnp.where(lane < IDX_DIM, kiw, proj[:, o_kiw:o_kiw + LANES] * IDX_SCALE)
    if key_tiles:
        ki_t = kiw.T[0:IDX_DIM]
        ki_hi = ki_t.astype(BF16)
        ki_lo = (ki_t - ki_hi.astype(F32)).astype(BF16)
        ki4_ref[0, 0, 0:IDX_DIM] = ki_hi
        ki4_ref[0, 0, IDX_DIM:2 * IDX_DIM] = ki_hi
        ki4_ref[0, 0, 2 * IDX_DIM:3 * IDX_DIM] = ki_lo
        ki4_ref[0, 0, 3 * IDX_DIM:4 * IDX_DIM] = ki_lo

    u = proj[:, 0:POOL_WIDTH]
    u_ref[0] = u

    @pl.when(i == 0)
    def _():
        e_ref[0:HALO // 2, :] = jnp.zeros((HALO // 2, POOL_WIDTH), F32)
        e_ref[HALO // 2:HALO, :] = hist_ref[0]

    e_ref[HALO:HALO + tm, :] = u
    n2, n4, n8 = tm + 24, tm + 16, tm + 8
    s2_ref[8:8 + n2, :] = e_ref[8:8 + n2, :] + e_ref[7:7 + n2, :]
    s4_ref[16:16 + n4, :] = s2_ref[16:16 + n4, :] + s2_ref[14:14 + n4, :]
    s8_ref[24:24 + n8, :] = s4_ref[24:24 + n8, :] + s4_ref[20:20 + n8, :]
    s16 = s8_ref[HALO:HALO + tm, :] + s8_ref[HALO - 8:HALO - 8 + tm, :]
    wins = (s2_ref[HALO:HALO + tm, :], s4_ref[HALO:HALO + tm, :], s8_ref[HALO:HALO + tm, :], s16)
    e_ref[HALO // 2:HALO, :] = e_ref[HALO // 2 + tm:HALO + tm, :]

    pos = pos0 + i * tm + lax.broadcasted_iota(I32, (tm, POOL_GROUP_WIDTH), 0)
    a_parts = []
    for g, w in enumerate(POOL_WINDOWS):
        lo_l, hi_l = g * POOL_GROUP_WIDTH, (g + 1) * POOL_GROUP_WIDTH
        cnt = jnp.minimum(pos + 1, w).astype(F32)
        pooled = wins[g][:, lo_l:hi_l] / cnt - u[:, lo_l:hi_l]
        a_parts.append(jnp.dot(pooled.astype(BF16), wpool_ref[g], preferred_element_type=F32))
    a = jnp.concatenate(a_parts, axis=-1) * pscale_ref[...]
    pa_ref[0] = _sigmoid(proj[:, o_ga:o_gb]) * a
    sgb_ref[0] = _sigmoid(proj[:, o_gb:o_gb + d_model])


def _project(x, hist16, pos0, w_in_r, g_pre, w_pool, pool_scale, tm, key_tile=None):
    B, T, D = x.shape
    W = w_in_r.shape[1]
    pos = pos0 + jnp.arange(T, dtype=I32)
    cq, sq = _rope_tables(pos, HEAD_DIM)
    ci, si = _rope_tables(pos, IDX_DIM)
    row = lambda b, i: (b, i, 0)
    tab = lambda b, i: (i, 0)
    const2 = lambda b, i: (0, 0)
    out_shape = (
        jax.ShapeDtypeStruct((B, T, POOL_WIDTH), F32),
        jax.ShapeDtypeStruct((B, N_HEADS, T, HEAD_DIM), BF16),
        jax.ShapeDtypeStruct((B, T, N_KV_HEADS * HEAD_DIM), F32),
        jax.ShapeDtypeStruct((B, T, N_KV_HEADS * HEAD_DIM), F32),
        jax.ShapeDtypeStruct((B, N_IDX_HEADS, T, 4 * IDX_DIM), BF16),
        jax.ShapeDtypeStruct((B, T, LANES), F32),
        jax.ShapeDtypeStruct((B, T, D), F32),
        jax.ShapeDtypeStruct((B, T, D), F32),
    )
    out_specs = (
        pl.BlockSpec((1, tm, POOL_WIDTH), row),
        pl.BlockSpec((1, N_HEADS, tm, HEAD_DIM), lambda b, i: (b, 0, i, 0)),
        pl.BlockSpec((1, tm, N_KV_HEADS * HEAD_DIM), row),
        pl.BlockSpec((1, tm, N_KV_HEADS * HEAD_DIM), row),
        pl.BlockSpec((1, N_IDX_HEADS, tm, 4 * IDX_DIM), lambda b, i: (b, 0, i, 0)),
        pl.BlockSpec((1, tm, LANES), row),
        pl.BlockSpec((1, tm, D), row),
        pl.BlockSpec((1, tm, D), row),
    )
    in_specs = [
        pl.BlockSpec((1, tm, D), row),
        _resident((1, D), const2),
        _resident((D, W), const2),
        pl.BlockSpec((1, HALO // 2, POOL_WIDTH), lambda b, i: (b, 0, 0)),
        pl.BlockSpec((tm, LANES), tab), pl.BlockSpec((tm, LANES), tab),
        pl.BlockSpec((tm, LANES), tab), pl.BlockSpec((tm, LANES), tab),
        _resident((N_POOL_GROUPS, POOL_GROUP_WIDTH, D // N_POOL_GROUPS), lambda b, i: (0, 0, 0)),
        _resident((1, D), const2),
    ]
    if key_tile is not None:
        assert key_tile % tm == 0 and T % key_tile == 0
        per = key_tile // tm
        n_tiles = T // key_tile
        out_shape += (
            jax.ShapeDtypeStruct((B, N_KV_HEADS, n_tiles, HEAD_DIM, key_tile), BF16),
            jax.ShapeDtypeStruct((B, N_KV_HEADS, n_tiles, key_tile, HEAD_DIM), BF16),
            jax.ShapeDtypeStruct((B, n_tiles, 4 * IDX_DIM, key_tile), BF16),
        )
        out_specs += (
            pl.BlockSpec((1, N_KV_HEADS, 1, HEAD_DIM, tm), lambda b, i: (b, 0, i // per, 0, i % per)),
            pl.BlockSpec((1, N_KV_HEADS, 1, tm, HEAD_DIM), lambda b, i: (b, 0, i // per, i % per, 0)),
            pl.BlockSpec((1, 1, 4 * IDX_DIM, tm), lambda b, i: (b, i // per, 0, i % per)),
        )
    scratch = [pltpu.VMEM((HALO + tm, POOL_WIDTH), F32) for _ in range(4)]
    return pl.pallas_call(
        functools.partial(_proj_kernel, tm=tm, pos0=pos0, d_model=D, key_tiles=key_tile is not None),
        grid=(B, T // tm), in_specs=in_specs, out_specs=out_specs, out_shape=out_shape,
        scratch_shapes=scratch, name="proj",
        compiler_params=pltpu.CompilerParams(dimension_semantics=("arbitrary", "arbitrary"),
                                             vmem_limit_bytes=VMEM_LIMIT),
    )(x, g_pre, w_in_r, hist16, cq, sq, ci, si, w_pool, pool_scale)


def _rope_tables(pos, dim):
    rot = dim // 4
    half = rot // 2
    inv = ROPE_THETA ** (-jnp.arange(half, dtype=F32) / half)
    ang = pos.astype(F32)[:, None] * inv[None, :]
    cos, sin = jnp.cos(ang), jnp.sin(ang)
    rest = dim - rot
    n = pos.shape[0]
    c = jnp.concatenate([cos, cos, jnp.ones((n, rest), F32)], axis=-1)
    s = jnp.concatenate([-sin, sin, jnp.zeros((n, rest), F32)], axis=-1)
    return jnp.tile(c, (1, LANES // dim)), jnp.tile(s, (1, LANES // dim))


def _split_kernel(kt_ref, out_ref):
    k = kt_ref[0]
    hi = k.astype(BF16)
    lo = (k - hi.astype(F32)).astype(BF16)
    out_ref[0, 0, 0:IDX_DIM, :] = hi
    out_ref[0, 0, IDX_DIM:2 * IDX_DIM, :] = hi
    out_ref[0, 0, 2 * IDX_DIM:3 * IDX_DIM, :] = lo
    out_ref[0, 0, 3 * IDX_DIM:4 * IDX_DIM, :] = lo


def _split_keys(ki_t, ts):
    B, _, Lp = ki_t.shape
    return pl.pallas_call(
        _split_kernel, grid=(B, Lp // ts),
        in_specs=[pl.BlockSpec((1, IDX_DIM, ts), lambda b, t: (b, 0, t))],
        out_specs=pl.BlockSpec((1, 1, 4 * IDX_DIM, ts), lambda b, t: (b, t, 0, 0)),
        out_shape=jax.ShapeDtypeStruct((B, Lp // ts, 4 * IDX_DIM, ts), BF16), name="split_keys",
        compiler_params=pltpu.CompilerParams(dimension_semantics=("arbitrary", "arbitrary")),
    )(ki_t)


def _key_of(x):
    b = pltpu.bitcast(x, I32)
    return b ^ ((b >> 31) & 0x7FFFFFFF)


def _float_of(k):
    return pltpu.bitcast(k ^ ((k >> 31) & 0x7FFFFFFF), F32)


def _row_sum(x):
    return jnp.broadcast_to(jnp.sum(x, axis=-1, keepdims=True), x.shape)


def _attn_kernel(qs_ref, kiw_ref, q_ref, ki_ref, kt_ref, v_ref, o_ref,
                 sc_ref, mm_ref, wb_ref, s_ref, m_ref, acc_ref, bias_ref, p_ref, alpha_ref,
                 *, tq, ts, q0, n_keys, k_sel):
    qi = pl.program_id(1)
    n_tiles_total = ki_ref.shape[1]
    nc = ts // LANES
    row = lax.broadcasted_iota(I32, (tq, LANES), 0)
    lane = lax.broadcasted_iota(I32, (tq, LANES), 1)
    qpos = q0 + qi * tq + row
    n_adm = jnp.minimum((qpos // CHUNK + 1) * CHUNK, n_keys)
    last_adm = jnp.minimum(((q0 + (qi + 1) * tq - 1) // CHUNK + 1) * CHUNK, n_keys)
    n_t = jnp.minimum((last_adm + ts - 1) // ts, n_tiles_total)

    kiw = kiw_ref[0]
    for hd in range(N_IDX_HEADS):
        wb_ref[hd] = jnp.broadcast_to(kiw[:, IDX_DIM + hd:IDX_DIM + hd + 1], (tq, LANES))
    qs2 = qs_ref[0].reshape(N_IDX_HEADS * tq, 4 * IDX_DIM)

    def idx_logits(t, slot):
        s_ref[slot] = jnp.dot(qs2, ki_ref[0, t], preferred_element_type=F32)

    rc = min(tq, ROW_CHUNK)
    mm_ref[0] = jnp.full((tq, LANES), _POS_INF, F32)
    mm_ref[1] = jnp.full((tq, LANES), _NEG_INF, F32)

    def score_tile(t, slot, masked):
        for r0 in range(0, tq, rc):
            rows = slice(r0, r0 + rc)
            mn, mx = mm_ref[0, rows], mm_ref[1, rows]
            for c in range(nc):
                cs = slice(c * LANES, (c + 1) * LANES)
                acc = None
                for hd in range(N_IDX_HEADS):
                    r = jnp.maximum(s_ref[slot, hd * tq + r0:hd * tq + r0 + rc, cs], 0.0) * wb_ref[hd, rows]
                    acc = r if acc is None else acc + r
                key = _key_of(acc)
                if masked:
                    qpos_c = q0 + qi * tq + r0 + lax.broadcasted_iota(I32, (rc, LANES), 0)
                    n_adm_c = jnp.minimum((qpos_c // CHUNK + 1) * CHUNK, n_keys)
                    adm = (t * ts + c * LANES + lax.broadcasted_iota(I32, (rc, LANES), 1)) < n_adm_c
                    key = jnp.where(adm, key, _INADMISSIBLE)
                    mx = jnp.maximum(mx, jnp.where(adm, acc, _NEG_INF))
                    mn = jnp.minimum(mn, jnp.where(adm, acc, _POS_INF))
                else:
                    mx = jnp.maximum(mx, acc)
                    mn = jnp.minimum(mn, acc)
                sc_ref[t, rows, cs] = key
            mm_ref[0, rows] = mn
            mm_ref[1, rows] = mx

    n_full = jnp.minimum(jnp.minimum(((q0 + qi * tq) // CHUNK + 1) * CHUNK, n_keys) // ts, n_t)
    last_full = jnp.maximum(n_full - 1, 0)
    idx_logits(0, 0)

    def score_pair(u, carry):
        t0 = 2 * u
        t1 = jnp.minimum(t0 + 1, last_full)
        idx_logits(t1, 1)
        score_tile(t0, 0, False)
        idx_logits(jnp.minimum(t0 + 2, last_full), 0)
        score_tile(t1, 1, False)
        return carry

    lax.fori_loop(0, (n_full + 1) // 2, score_pair, 0)

    def score_tail(t, carry):
        idx_logits(t, 0)
        score_tile(t, 0, True)
        return carry

    lax.fori_loop(n_full, n_t, score_tail, 0)
    rmin = jnp.broadcast_to(jnp.min(mm_ref[0], axis=-1, keepdims=True), (tq, LANES))
    rmax = jnp.broadcast_to(jnp.max(mm_ref[1], axis=-1, keepdims=True), (tq, LANES))

    def count_ge(thr):
        def body(t, cnt):
            for c in range(nc):
                cnt = cnt + jnp.where(sc_ref[t, :, c * LANES:(c + 1) * LANES] >= thr, 1.0, 0.0)
            return cnt
        return _row_sum(lax.fori_loop(0, n_t, body, jnp.zeros((tq, LANES), F32)))

    def bis_cond(st):
        return st[0] > 0

    kf = float(k_sel)
    take_all = n_adm <= k_sel
    lo0 = _key_of(rmin) - 1
    hi0 = _key_of(rmax) + 2
    c_pos = count_ge(jnp.full((tq, LANES), _KEY_MIN_NORMAL, I32))
    c_nn = count_ge(jnp.full((tq, LANES), _KEY_NEG_ZERO, I32))
    pos = c_pos >= kf
    neg = c_nn < kf
    lo = jnp.where(pos, _KEY_MIN_NORMAL, jnp.where(neg, lo0, _KEY_NEG_ZERO))
    hi = jnp.where(pos, hi0, jnp.where(neg, _KEY_NEG_ZERO, _KEY_MIN_NORMAL))
    clo = jnp.where(pos, c_pos, jnp.where(neg, n_adm.astype(F32), c_nn))
    chi = jnp.where(pos, 0.0, jnp.where(neg, c_nn, c_pos))
    exact0 = jnp.where(clo == kf, 1, 0)
    act0 = jnp.where(take_all, 0, jnp.where(exact0 > 0, 0, jnp.where(hi > lo + 1, 1, 0)))

    def bis_body(st):
        _, it, lo, hi, clo, chi, exact, act = st
        kmid = (lo & hi) + ((lo ^ hi) >> 1)
        vmid = _key_of((_float_of(lo) + _float_of(hi)) * 0.5)
        by_value = jnp.where(it < VALUE_PIVOT_STEPS, jnp.where(vmid > lo, jnp.where(vmid < hi, 1, 0), 0), 0)
        mid = jnp.where(by_value > 0, vmid, kmid)
        cnt = count_ge(mid)
        up = jnp.where(act > 0, jnp.where(cnt >= kf, 1, 0), 0)
        dn = act - up
        lo = jnp.where(up > 0, mid, lo)
        clo = jnp.where(up > 0, cnt, clo)
        hi = jnp.where(dn > 0, mid, hi)
        chi = jnp.where(dn > 0, cnt, chi)
        hit = jnp.where(up > 0, jnp.where(cnt == kf, 1, 0), 0)
        exact = exact + hit
        act = jnp.where(act > 0, jnp.where(hit > 0, 0, jnp.where(hi > lo + 1, 1, 0)), 0)
        return jnp.max(act), it + 1, lo, hi, clo, chi, exact, act

    _, _, lo, _, clo, chi, exact, _ = lax.while_loop(
        bis_cond, bis_body, (jnp.max(act0), jnp.int32(0), lo, hi, clo, chi, exact0, act0))
    thr = jnp.where(take_all, _INADMISSIBLE + 1, lo)
    thr_up = jnp.where(take_all, _INADMISSIBLE + 1, lo + 1)

    need = kf - chi
    tied = jnp.where(take_all, 0, jnp.where(exact > 0, 0, jnp.where(clo > kf, 1, 0)))

    def count_tied_below(cut):
        def body(t, cnt):
            for c in range(nc):
                s = sc_ref[t, :, c * LANES:(c + 1) * LANES]
                col = t * ts + c * LANES + lane
                cnt = cnt + jnp.where(s == thr, jnp.where(col < cut, 1.0, 0.0), 0.0)
            return cnt
        return _row_sum(lax.fori_loop(0, n_t, body, jnp.zeros((tq, LANES), F32)))

    def cut_body(st):
        _, lo_c, hi_c, act = st
        mid = (lo_c + hi_c) >> 1
        cnt = count_tied_below(mid)
        on = act > 0
        le = cnt <= need
        lo_c = jnp.where(on, jnp.where(le, mid, lo_c), lo_c)
        hi_c = jnp.where(on, jnp.where(le, hi_c, mid), hi_c)
        act = jnp.where(on, jnp.where(hi_c - lo_c > 1, 1, 0), 0)
        return jnp.max(act), lo_c, hi_c, act

    _, cut_lo, _, _ = lax.while_loop(
        bis_cond, cut_body,
        (jnp.max(tied), jnp.zeros((tq, LANES), I32), jnp.full((tq, LANES), n_tiles_total * ts + 1, I32), tied))
    cut = jnp.where(tied > 0, cut_lo, n_tiles_total * ts + 1)

    m_ref[...] = jnp.full(m_ref.shape, _M_INIT, F32)
    acc_ref[...] = jnp.zeros(acc_ref.shape, F32)
    rc = min(tq, ROW_CHUNK)
    ones = jnp.ones((ts, LANES), BF16)
    last = n_t - 1

    def logits(t, slot):
        for g in range(N_KV_HEADS):
            qg = q_ref[0, g * GQA_GROUP:(g + 1) * GQA_GROUP].reshape(GQA_GROUP * tq, HEAD_DIM)
            s_ref[slot, g * GQA_GROUP * tq:(g + 1) * GQA_GROUP * tq] = jnp.dot(
                qg, kt_ref[0, g, t], preferred_element_type=F32)

    def softmax_pv(t, slot, live):
        for c in range(nc):
            col = t * ts + c * LANES + lane
            bound = jnp.where(live, jnp.where(col < cut, thr, thr_up), _NO_KEY)
            bias_ref[slot, :, c * LANES:(c + 1) * LANES] = jnp.where(
                sc_ref[t, :, c * LANES:(c + 1) * LANES] >= bound, 0.0, _NEG_INF)
        for g in range(N_KV_HEADS):
            for j in range(GQA_GROUP):
                hd = g * GQA_GROUP + j
                for r0 in range(0, tq, rc):
                    r1 = j * tq + r0
                    r2 = hd * tq + r0
                    s = [s_ref[slot, r2:r2 + rc, c * LANES:(c + 1) * LANES]
                         + bias_ref[slot, r0:r0 + rc, c * LANES:(c + 1) * LANES] for c in range(nc)]
                    mx = s[0]
                    for c in range(1, nc):
                        mx = jnp.maximum(mx, s[c])
                    m_old = m_ref[hd, r0:r0 + rc]
                    m_new = jnp.maximum(m_old, jnp.broadcast_to(jnp.max(mx, axis=-1, keepdims=True), (rc, LANES)))
                    alpha_ref[slot, g, r1:r1 + rc] = jnp.exp2(m_old - m_new)
                    for c in range(nc):
                        p_ref[slot, g, r1:r1 + rc, c * LANES:(c + 1) * LANES] = jnp.exp2(s[c] - m_new).astype(BF16)
                    m_ref[hd, r0:r0 + rc] = m_new
            v_aug = jnp.concatenate([v_ref[0, g, t], ones], axis=-1)
            pv = jnp.dot(p_ref[slot, g], v_aug, preferred_element_type=F32)
            alpha = alpha_ref[slot, g]
            for half in range(2):
                cols = slice(half * LANES, (half + 1) * LANES)
                acc_ref[g, :, cols] = alpha * acc_ref[g, :, cols] + pv[:, cols]

    logits(0, 0)

    def attn_pair(u, carry):
        t0 = 2 * u
        t1 = jnp.minimum(t0 + 1, last)
        logits(t1, 1)
        softmax_pv(t0, 0, True)
        logits(jnp.minimum(t0 + 2, last), 0)
        softmax_pv(t1, 1, t0 + 1 < n_t)
        return carry

    lax.fori_loop(0, (n_t + 1) // 2, attn_pair, 0)
    for hd in range(N_HEADS):
        g, j = divmod(hd, GQA_GROUP)
        o_ref[0, :, hd * HEAD_DIM:(hd + 1) * HEAD_DIM] = (
            acc_ref[g, j * tq:(j + 1) * tq, 0:HEAD_DIM] / acc_ref[g, j * tq:(j + 1) * tq, HEAD_DIM:2 * HEAD_DIM])


def _attend(qs, kiw, q_att, ki4, kt, v, q0, n_keys, tq, ts):
    B, _, T, _ = qs.shape
    n_tiles = ki4.shape[1]
    k_sel = min(TOPK_MAX, n_keys // 4)
    kern = functools.partial(_attn_kernel, tq=tq, ts=ts, q0=q0, n_keys=n_keys, k_sel=k_sel)
    return pl.pallas_call(
        kern, grid=(B, T // tq),
        in_specs=[
            pl.BlockSpec((1, N_IDX_HEADS, tq, 4 * IDX_DIM), lambda b, i: (b, 0, i, 0)),
            pl.BlockSpec((1, tq, LANES), lambda b, i: (b, i, 0)),
            pl.BlockSpec((1, N_HEADS, tq, HEAD_DIM), lambda b, i: (b, 0, i, 0)),
            _resident((1, n_tiles, 4 * IDX_DIM, ts), lambda b, i: (b, 0, 0, 0)),
            _resident((1, N_KV_HEADS, n_tiles, HEAD_DIM, ts), lambda b, i: (b, 0, 0, 0, 0)),
            _resident((1, N_KV_HEADS, n_tiles, ts, HEAD_DIM), lambda b, i: (b, 0, 0, 0, 0)),
        ],
        out_specs=pl.BlockSpec((1, tq, N_HEADS * HEAD_DIM), lambda b, i: (b, i, 0)),
        out_shape=jax.ShapeDtypeStruct((B, T, N_HEADS * HEAD_DIM), F32),
        scratch_shapes=[
            pltpu.VMEM((n_tiles, tq, ts), I32),
            pltpu.VMEM((2, tq, LANES), F32),
            pltpu.VMEM((N_IDX_HEADS, tq, LANES), F32),
            pltpu.VMEM((2, N_HEADS * tq, ts), F32),
            pltpu.VMEM((N_HEADS, tq, LANES), F32),
            pltpu.VMEM((N_KV_HEADS, GQA_GROUP * tq, 2 * HEAD_DIM), F32),
            pltpu.VMEM((2, tq, ts), F32),
            pltpu.VMEM((2, N_KV_HEADS, GQA_GROUP * tq, ts), BF16),
            pltpu.VMEM((2, N_KV_HEADS, GQA_GROUP * tq, LANES), F32),
        ],
        name="attend",
        compiler_params=pltpu.CompilerParams(dimension_semantics=("arbitrary", "arbitrary"),
                                             vmem_limit_bytes=VMEM_LIMIT),
    )(qs, kiw, q_att, ki4, kt, v)


def _rms(x, g):
    return x * lax.rsqrt(jnp.mean(x * x, axis=-1, keepdims=True) + RMS_EPS) * g


def _out_kernel(x_ref, pa_ref, sgb_ref, b_ref, wo_ref, g1_ref, g2_ref, wgu_ref, wd_ref, g3_ref, y_ref, *, d_ff):
    merged = pa_ref[...] + sgb_ref[...] * b_ref[...]
    mix = jnp.dot(merged.astype(BF16), wo_ref[...], preferred_element_type=F32)
    x1 = x_ref[...] + _rms(mix, g1_ref[...])
    h2 = _rms(x1, g2_ref[...]).astype(BF16)
    gu = jnp.dot(h2, wgu_ref[...], preferred_element_type=F32)
    gate, up = gu[:, :d_ff], gu[:, d_ff:]
    act = (gate * _sigmoid(gate) * up).astype(BF16)
    f = jnp.dot(act, wd_ref[...], preferred_element_type=F32)
    y_ref[...] = x1 + _rms(f, g3_ref[...])


def _finish(x, pa, sgb, b, w_out, g_post, g_ffn_pre, w_gate_up, w_down, g_ffn_post, tm):
    N, D = x.shape
    d_ff = w_down.shape[0]
    row = pl.BlockSpec((tm, D), lambda i: (i, 0))
    const = lambda i: (0, 0)
    return pl.pallas_call(
        functools.partial(_out_kernel, d_ff=d_ff), grid=(N // tm,),
        in_specs=[row, row, row, row,
                  _resident((D, D), const), _resident((1, D), const), _resident((1, D), const),
                  _resident((D, 2 * d_ff), const), _resident((d_ff, D), const), _resident((1, D), const)],
        out_specs=row, out_shape=jax.ShapeDtypeStruct((N, D), F32), name="finish",
        compiler_params=pltpu.CompilerParams(dimension_semantics=("arbitrary",), vmem_limit_bytes=VMEM_LIMIT),
    )(x, pa, sgb, b, w_out, g_post, g_ffn_pre, w_gate_up, w_down, g_ffn_post)


def _tile_keys(n_keys, ts):
    return -(-n_keys // ts) * ts


def _layer(x, pos0, hist, k_cache, v_cache, ki_cache, wts, tm, tq, ts, tm_out):
    (w_in_r, w_pool, pool_scale, w_out, w_gate_up, w_down, g_pre, g_post, g_ffn_pre, g_ffn_post) = wts
    B, T, D = x.shape
    hist16 = jnp.concatenate([jnp.zeros((B, HALO // 2 - POOL_HIST, POOL_WIDTH), F32), hist], axis=1)
    own_keys_only = k_cache is None and T % ts == 0 and ts % tm == 0
    outs = _project(x, hist16, pos0, w_in_r, g_pre, w_pool, pool_scale, tm, ts if own_keys_only else None)
    u, q_att, k, v, qs, kiw, pa, sgb = outs[:8]
    ki = kiw[:, :, :IDX_DIM]
    if own_keys_only:
        kt, vt, ki4 = outs[8:]
        n_keys = T
    else:
        if k_cache is None:
            k_all, v_all, ki_all = k, v, ki
        else:
            k_all = jnp.concatenate([k_cache.reshape(B, -1, N_KV_HEADS * HEAD_DIM), k], axis=1)
            v_all = jnp.concatenate([v_cache.reshape(B, -1, N_KV_HEADS * HEAD_DIM), v], axis=1)
            ki_all = jnp.concatenate([ki_cache, ki], axis=1)
        n_keys = k_all.shape[1]
        lp = _tile_keys(n_keys, ts)
        pad = ((0, 0), (0, lp - n_keys), (0, 0))
        nt = lp // ts
        kt = jnp.pad(k_all.astype(BF16), pad).reshape(B, nt, ts, N_KV_HEADS, HEAD_DIM).transpose(0, 3, 1, 4, 2)
        vt = jnp.pad(v_all.astype(BF16), pad).reshape(B, nt, ts, N_KV_HEADS, HEAD_DIM).transpose(0, 3, 1, 2, 4)
        ki4 = _split_keys(jnp.swapaxes(jnp.pad(ki_all, pad), 1, 2), ts)
    b = _attend(qs, kiw, q_att, ki4, kt, vt, pos0, n_keys, tq, ts)
    y = _finish(x.reshape(B * T, D), pa.reshape(B * T, D), sgb.reshape(B * T, D), b.reshape(B * T, D),
                w_out, g_post, g_ffn_pre, w_gate_up, w_down, g_ffn_post, tm_out).reshape(B, T, D)
    new_pool = jnp.concatenate([hist, u], axis=1)[:, T:]
    return (y, k.reshape(B, T, N_KV_HEADS, HEAD_DIM), v.reshape(B, T, N_KV_HEADS, HEAD_DIM), ki, new_pool)


def _relayout_w_in(w_in):
    d = w_in.shape[0]
    o_kiw = POOL_WIDTH + N_HEADS * HEAD_DIM + 2 * N_KV_HEADS * HEAD_DIM + N_IDX_HEADS * IDX_DIM
    narrow = IDX_DIM + N_IDX_HEADS
    padded = jnp.concatenate([w_in[:, :o_kiw + narrow], jnp.zeros((d, LANES - narrow), w_in.dtype),
                              w_in[:, o_kiw + narrow:]], axis=1)
    return padded.astype(BF16)


def kernel(x_prompt, x_sample, cache_k, cache_v, cache_k_idx, state_pool, w_in, w_pool, pool_scale, w_out,
           w_gate_up, w_down, norm_mix_pre, norm_mix_post, norm_ffn_pre, norm_ffn_post):
    depth = w_in.shape[0]
    past = cache_k.shape[2]
    t_p, t_s = x_prompt.shape[1], x_sample.shape[1]
    hist_p = jnp.zeros((x_prompt.shape[0], POOL_HIST, POOL_WIDTH), x_prompt.dtype)
    xp, xs = x_prompt, x_sample
    outs = [[] for _ in range(8)]
    for l in range(depth):
        wts = (_relayout_w_in(w_in[l]), w_pool[l].astype(BF16), pool_scale[l][None, :], w_out[l].astype(BF16),
               w_gate_up[l].astype(BF16), w_down[l].astype(BF16), norm_mix_pre[l][None, :],
               norm_mix_post[l][None, :], norm_ffn_pre[l][None, :], norm_ffn_post[l][None, :])
        tm_p = min(256, t_p)
        tq_p = min(128, t_p)
        xp, k1, v1, ki1, p1 = _layer(xp, 0, hist_p, None, None, None, wts, tm_p, tq_p, 512, tm_p)
        n_s = xs.shape[0] * t_s
        xs, k2, v2, ki2, p2 = _layer(xs, past, state_pool[l], cache_k[l], cache_v[l], cache_k_idx[l], wts,
                                     t_s, t_s, 512, min(256, n_s))
        for lst, val in zip(outs, (k1, v1, ki1, p1, k2, v2, ki2, p2)):
            lst.append(val)
    return (xp, xs) + tuple(jnp.stack(o) for o in outs)
```

```python
import functools

import jax
import jax.numpy as jnp
from jax import lax
from jax.experimental import pallas as pl
from jax.experimental.pallas import tpu as pltpu

F32 = jnp.float32
BF16 = jnp.bfloat16
I32 = jnp.int32
I16 = jnp.int16

LANES = 128
CHUNK = 64
POOL_WINDOWS = (2, 4, 8, 16)
N_POOL_GROUPS = 4
POOL_GROUP_WIDTH = 128
POOL_WIDTH = N_POOL_GROUPS * POOL_GROUP_WIDTH
POOL_HIST = 15
N_HEADS = 8
N_KV_HEADS = 2
HEAD_DIM = 128
GQA_GROUP = N_HEADS // N_KV_HEADS
ROPE_THETA = 500000.0
N_IDX_HEADS = 8
IDX_DIM = 64
TOPK_MAX = 256
RMS_EPS = 1e-6
ATTN_SCALE = HEAD_DIM ** -0.5
IDX_SCALE = (N_IDX_HEADS ** -0.5) * (IDX_DIM ** -0.5)

LOG2_E = 1.4426950408889634
VALUE_PIVOT_STEPS = 28
KEY_ROWS = 64
ROW_CHUNK = 32
HALO = 32
VMEM_LIMIT = 56 * 1024 * 1024

_NEG_INF = float("-inf")
_POS_INF = float("inf")
_INADMISSIBLE = -(2 ** 31)
_NO_KEY = 2 ** 31 - 1
_KEY_MIN_NORMAL = 0x00800000
_KEY_NEG_ZERO = -1
_M_INIT = -1e30


def _resident(block_shape, index_map):
    return pl.BlockSpec(block_shape, index_map, pipeline_mode=pl.Buffered(1))


def _rope(xs, cos, sin, half, period):
    lane = lax.broadcasted_iota(I32, xs.shape, 1)
    ahead = pltpu.roll(xs, LANES - half, 1)
    behind = pltpu.roll(xs, half, 1)
    partner = jnp.where((lane & (period - 1)) < half, ahead, behind)
    return xs * cos + partner * sin


def _sigmoid(x):
    return 1.0 / (1.0 + jnp.exp(-x))


def _proj_kernel(x_ref, g_ref, w_ref, hist_ref, cq_ref, sq_ref, ci_ref, si_ref, wpool_ref, pscale_ref,
                 u_ref, qatt_ref, k_ref, v_ref, qs_ref, kiw_ref, pa_ref, sgb_ref, *rest, tm, pos0, d_model, key_tiles):
    if key_tiles:
        kt_ref, vt_ref, ki4_ref, wrow_ref, e_ref, s2_ref, s4_ref, s8_ref = rest
    else:
        e_ref, s2_ref, s4_ref, s8_ref = rest
    i = pl.program_id(1)
    x = x_ref[0]
    h = x * lax.rsqrt(jnp.mean(x * x, axis=-1, keepdims=True) + RMS_EPS) * g_ref[...]
    proj = jnp.dot(h.astype(BF16), w_ref[...], preferred_element_type=F32)

    o_q = POOL_WIDTH
    o_k = o_q + N_HEADS * HEAD_DIM
    o_v = o_k + N_KV_HEADS * HEAD_DIM
    o_qi = o_v + N_KV_HEADS * HEAD_DIM
    o_kiw = o_qi + N_IDX_HEADS * IDX_DIM
    o_ga = o_kiw + LANES
    o_gb = o_ga + d_model

    cq, sq, ci, si = cq_ref[...], sq_ref[...], ci_ref[...], si_ref[...]
    half_q = HEAD_DIM // 8
    half_i = IDX_DIM // 8

    for hd in range(N_HEADS):
        qh = _rope(proj[:, o_q + hd * HEAD_DIM:o_q + (hd + 1) * HEAD_DIM], cq, sq, half_q, HEAD_DIM)
        qatt_ref[0, hd] = (qh * (ATTN_SCALE * LOG2_E)).astype(BF16)
    for kh in range(N_KV_HEADS):
        k_h = _rope(proj[:, o_k + kh * HEAD_DIM:o_k + (kh + 1) * HEAD_DIM], cq, sq, half_q, HEAD_DIM)
        k_ref[0, :, kh * HEAD_DIM:(kh + 1) * HEAD_DIM] = k_h
        if key_tiles:
            kt_ref[0, kh, 0] = k_h.T.astype(BF16)
            vt_ref[0, kh, 0] = proj[:, o_v + kh * HEAD_DIM:o_v + (kh + 1) * HEAD_DIM].astype(BF16)
    v_ref[0] = proj[:, o_v:o_qi]

    for pr in range(N_IDX_HEADS // 2):
        qi2 = _rope(proj[:, o_qi + pr * LANES:o_qi + (pr + 1) * LANES], ci, si, half_i, IDX_DIM)
        hi = qi2.astype(BF16).astype(F32)
        lo = qi2 - hi
        lane2 = lax.broadcasted_iota(I32, qi2.shape, 1)
        first = jnp.where(lane2 < IDX_DIM, hi, pltpu.roll(lo, IDX_DIM, 1))
        second = jnp.where(lane2 < IDX_DIM, pltpu.roll(hi, IDX_DIM, 1), lo)
        for sub, slab in enumerate((first, second)):
            hd = 2 * pr + sub
            if key_tiles:
                for sb in range(tm // LANES):
                    slab_t = slab[sb * LANES:(sb + 1) * LANES].T.astype(BF16)
                    qs_ref[0, sb, 0:LANES, hd * LANES:(hd + 1) * LANES] = slab_t
                    qs_ref[0, sb, LANES:2 * LANES, hd * LANES:(hd + 1) * LANES] = slab_t
            else:
                qs_ref[0, hd, :, 0:LANES] = slab.astype(BF16)
                qs_ref[0, hd, :, LANES:2 * LANES] = slab.astype(BF16)

    kiw = _rope(proj[:, o_kiw:o_kiw + LANES], ci, si, half_i, IDX_DIM)
    lane = lax.broadcasted_iota(I32, kiw.shape, 1)
    kiw_full = jnp.where(lane < IDX_DIM, kiw, proj[:, o_kiw:o_kiw + LANES] * IDX_SCALE)
    kiw_ref[0] = kiw_full
    if key_tiles:
        ki_hi = kiw.astype(BF16).astype(F32)
        ki_lo = kiw - ki_hi
        ki4_ref[0, 0, :, 0:LANES] = jnp.where(lane < IDX_DIM, ki_hi, pltpu.roll(ki_hi, IDX_DIM, 1)).astype(BF16)
        ki4_ref[0, 0, :, LANES:2 * LANES] = jnp.where(lane < IDX_DIM, ki_lo, pltpu.roll(ki_lo, IDX_DIM, 1)).astype(BF16)
        for sb in range(tm // LANES):
            wrow_ref[0, sb] = kiw_full[sb * LANES:(sb + 1) * LANES].T[IDX_DIM:IDX_DIM + N_IDX_HEADS]

    u = proj[:, 0:POOL_WIDTH]
    u_ref[0] = u

    @pl.when(i == 0)
    def _():
        e_ref[0:HALO // 2, :] = jnp.zeros((HALO // 2, POOL_WIDTH), F32)
        e_ref[HALO // 2:HALO, :] = hist_ref[0]

    e_ref[HALO:HALO + tm, :] = u
    n2, n4, n8 = tm + 24, tm + 16, tm + 8
    s2_ref[8:8 + n2, :] = e_ref[8:8 + n2, :] + e_ref[7:7 + n2, :]
    s4_ref[16:16 + n4, :] = s2_ref[16:16 + n4, :] + s2_ref[14:14 + n4, :]
    s8_ref[24:24 + n8, :] = s4_ref[24:24 + n8, :] + s4_ref[20:20 + n8, :]
    s16 = s8_ref[HALO:HALO + tm, :] + s8_ref[HALO - 8:HALO - 8 + tm, :]
    wins = (s2_ref[HALO:HALO + tm, :], s4_ref[HALO:HALO + tm, :], s8_ref[HALO:HALO + tm, :], s16)
    e_ref[HALO // 2:HALO, :] = e_ref[HALO // 2 + tm:HALO + tm, :]

    pos = pos0 + i * tm + lax.broadcasted_iota(I32, (tm, POOL_GROUP_WIDTH), 0)
    a_parts = []
    for g, w in enumerate(POOL_WINDOWS):
        lo_l, hi_l = g * POOL_GROUP_WIDTH, (g + 1) * POOL_GROUP_WIDTH
        cnt = jnp.minimum(pos + 1, w).astype(F32)
        pooled = wins[g][:, lo_l:hi_l] / cnt - u[:, lo_l:hi_l]
        a_parts.append(jnp.dot(pooled.astype(BF16), wpool_ref[g], preferred_element_type=F32))
    a = jnp.concatenate(a_parts, axis=-1) * pscale_ref[...]
    pa_ref[0] = _sigmoid(proj[:, o_ga:o_gb]) * a
    sgb_ref[0] = _sigmoid(proj[:, o_gb:o_gb + d_model])


def _project(x, hist16, pos0, w_in_r, g_pre, w_pool, pool_scale, tm, key_tile=None):
    B, T, D = x.shape
    W = w_in_r.shape[1]
    pos = pos0 + jnp.arange(T, dtype=I32)
    cq, sq = _rope_tables(pos, HEAD_DIM)
    ci, si = _rope_tables(pos, IDX_DIM)
    row = lambda b, i: (b, i, 0)
    tab = lambda b, i: (i, 0)
    const2 = lambda b, i: (0, 0)
    out_shape = (
        jax.ShapeDtypeStruct((B, T, POOL_WIDTH), F32),
        jax.ShapeDtypeStruct((B, N_HEADS, T, HEAD_DIM), BF16),
        jax.ShapeDtypeStruct((B, T, N_KV_HEADS * HEAD_DIM), F32),
        jax.ShapeDtypeStruct((B, T, N_KV_HEADS * HEAD_DIM), F32),
        jax.ShapeDtypeStruct((B, N_IDX_HEADS, T, 4 * IDX_DIM), BF16),
        jax.ShapeDtypeStruct((B, T, LANES), F32),
        jax.ShapeDtypeStruct((B, T, D), F32),
        jax.ShapeDtypeStruct((B, T, D), F32),
    )
    out_specs = (
        pl.BlockSpec((1, tm, POOL_WIDTH), row),
        pl.BlockSpec((1, N_HEADS, tm, HEAD_DIM), lambda b, i: (b, 0, i, 0)),
        pl.BlockSpec((1, tm, N_KV_HEADS * HEAD_DIM), row),
        pl.BlockSpec((1, tm, N_KV_HEADS * HEAD_DIM), row),
        pl.BlockSpec((1, N_IDX_HEADS, tm, 4 * IDX_DIM), lambda b, i: (b, 0, i, 0)),
        pl.BlockSpec((1, tm, LANES), row),
        pl.BlockSpec((1, tm, D), row),
        pl.BlockSpec((1, tm, D), row),
    )
    in_specs = [
        pl.BlockSpec((1, tm, D), row),
        _resident((1, D), const2),
        _resident((D, W), const2),
        pl.BlockSpec((1, HALO // 2, POOL_WIDTH), lambda b, i: (b, 0, 0)),
        pl.BlockSpec((tm, LANES), tab), pl.BlockSpec((tm, LANES), tab),
        pl.BlockSpec((tm, LANES), tab), pl.BlockSpec((tm, LANES), tab),
        _resident((N_POOL_GROUPS, POOL_GROUP_WIDTH, D // N_POOL_GROUPS), lambda b, i: (0, 0, 0)),
        _resident((1, D), const2),
    ]
    if key_tile is not None:
        assert key_tile % tm == 0 and T % key_tile == 0 and tm % LANES == 0
        per = key_tile // tm
        n_tiles = T // key_tile
        sub = tm // LANES
        out_shape = out_shape[:4] + (
            jax.ShapeDtypeStruct((B, T // LANES, 4 * IDX_DIM, N_IDX_HEADS * LANES), BF16),
        ) + out_shape[5:] + (
            jax.ShapeDtypeStruct((B, N_KV_HEADS, n_tiles, HEAD_DIM, key_tile), BF16),
            jax.ShapeDtypeStruct((B, N_KV_HEADS, n_tiles, key_tile, HEAD_DIM), BF16),
            jax.ShapeDtypeStruct((B, n_tiles, key_tile, 4 * IDX_DIM), BF16),
            jax.ShapeDtypeStruct((B, T // LANES, N_IDX_HEADS, LANES), F32),
        )
        out_specs = out_specs[:4] + (
            pl.BlockSpec((1, sub, 4 * IDX_DIM, N_IDX_HEADS * LANES), lambda b, i: (b, i, 0, 0)),
        ) + out_specs[5:] + (
            pl.BlockSpec((1, N_KV_HEADS, 1, HEAD_DIM, tm), lambda b, i: (b, 0, i // per, 0, i % per)),
            pl.BlockSpec((1, N_KV_HEADS, 1, tm, HEAD_DIM), lambda b, i: (b, 0, i // per, i % per, 0)),
            pl.BlockSpec((1, 1, tm, 4 * IDX_DIM), lambda b, i: (b, i // per, i % per, 0)),
            pl.BlockSpec((1, sub, N_IDX_HEADS, LANES), lambda b, i: (b, i, 0, 0)),
        )
    scratch = [pltpu.VMEM((HALO + tm, POOL_WIDTH), F32) for _ in range(4)]
    return pl.pallas_call(
        functools.partial(_proj_kernel, tm=tm, pos0=pos0, d_model=D, key_tiles=key_tile is not None),
        grid=(B, T // tm), in_specs=in_specs, out_specs=out_specs, out_shape=out_shape,
        scratch_shapes=scratch, name="proj",
        compiler_params=pltpu.CompilerParams(dimension_semantics=("arbitrary", "arbitrary"),
                                             vmem_limit_bytes=VMEM_LIMIT),
    )(x, g_pre, w_in_r, hist16, cq, sq, ci, si, w_pool, pool_scale)


def _rope_tables(pos, dim):
    rot = dim // 4
    half = rot // 2
    inv = ROPE_THETA ** (-jnp.arange(half, dtype=F32) / half)
    ang = pos.astype(F32)[:, None] * inv[None, :]
    cos, sin = jnp.cos(ang), jnp.sin(ang)
    rest = dim - rot
    n = pos.shape[0]
    c = jnp.concatenate([cos, cos, jnp.ones((n, rest), F32)], axis=-1)
    s = jnp.concatenate([-sin, sin, jnp.zeros((n, rest), F32)], axis=-1)
    return jnp.tile(c, (1, LANES // dim)), jnp.tile(s, (1, LANES // dim))


def _split_kernel(kt_ref, out_ref):
    k = kt_ref[0]
    hi = k.astype(BF16)
    lo = (k - hi.astype(F32)).astype(BF16)
    out_ref[0, 0, 0:IDX_DIM, :] = hi
    out_ref[0, 0, IDX_DIM:2 * IDX_DIM, :] = hi
    out_ref[0, 0, 2 * IDX_DIM:3 * IDX_DIM, :] = lo
    out_ref[0, 0, 3 * IDX_DIM:4 * IDX_DIM, :] = lo


def _split_keys(ki_t, ts):
    B, _, Lp = ki_t.shape
    return pl.pallas_call(
        _split_kernel, grid=(B, Lp // ts),
        in_specs=[pl.BlockSpec((1, IDX_DIM, ts), lambda b, t: (b, 0, t))],
        out_specs=pl.BlockSpec((1, 1, 4 * IDX_DIM, ts), lambda b, t: (b, t, 0, 0)),
        out_shape=jax.ShapeDtypeStruct((B, Lp // ts, 4 * IDX_DIM, ts), BF16), name="split_keys",
        compiler_params=pltpu.CompilerParams(dimension_semantics=("arbitrary", "arbitrary")),
    )(ki_t)


def _key_of(x):
    b = pltpu.bitcast(x, I32)
    return b ^ ((b >> 31) & 0x7FFFFFFF)


def _float_of(k):
    return pltpu.bitcast(k ^ ((k >> 31) & 0x7FFFFFFF), F32)


def _row_sum(x):
    return jnp.broadcast_to(jnp.sum(x, axis=-1, keepdims=True), x.shape)


def _attn_kernel(qs_ref, kiw_ref, q_ref, ki_ref, kt_ref, v_ref, o_ref,
                 sc_ref, mm_ref, wb_ref, s_ref, m_ref, acc_ref, bias_ref, p_ref, alpha_ref,
                 *, tq, ts, q0, n_keys, k_sel):
    qi = pl.program_id(1)
    n_tiles_total = ki_ref.shape[1]
    nc = ts // LANES
    row = lax.broadcasted_iota(I32, (tq, LANES), 0)
    lane = lax.broadcasted_iota(I32, (tq, LANES), 1)
    qpos = q0 + qi * tq + row
    n_adm = jnp.minimum((qpos // CHUNK + 1) * CHUNK, n_keys)
    last_adm = jnp.minimum(((q0 + (qi + 1) * tq - 1) // CHUNK + 1) * CHUNK, n_keys)
    n_t = jnp.minimum((last_adm + ts - 1) // ts, n_tiles_total)

    kiw = kiw_ref[0]
    for hd in range(N_IDX_HEADS):
        wb_ref[hd] = jnp.broadcast_to(kiw[:, IDX_DIM + hd:IDX_DIM + hd + 1], (tq, LANES))
    qs2 = qs_ref[0].reshape(N_IDX_HEADS * tq, 4 * IDX_DIM)

    def idx_logits(t, slot):
        s_ref[slot] = jnp.dot(qs2, ki_ref[0, t], preferred_element_type=F32)

    rc = min(tq, ROW_CHUNK)
    mm_ref[0] = jnp.full((tq, LANES), _POS_INF, F32)
    mm_ref[1] = jnp.full((tq, LANES), _NEG_INF, F32)

    def score_tile(t, slot, masked):
        for r0 in range(0, tq, rc):
            rows = slice(r0, r0 + rc)
            mn, mx = mm_ref[0, rows], mm_ref[1, rows]
            for c in range(nc):
                cs = slice(c * LANES, (c + 1) * LANES)
                acc = None
                for hd in range(N_IDX_HEADS):
                    r = jnp.maximum(s_ref[slot, hd * tq + r0:hd * tq + r0 + rc, cs], 0.0) * wb_ref[hd, rows]
                    acc = r if acc is None else acc + r
                key = _key_of(acc)
                if masked:
                    qpos_c = q0 + qi * tq + r0 + lax.broadcasted_iota(I32, (rc, LANES), 0)
                    n_adm_c = jnp.minimum((qpos_c // CHUNK + 1) * CHUNK, n_keys)
                    adm = (t * ts + c * LANES + lax.broadcasted_iota(I32, (rc, LANES), 1)) < n_adm_c
                    key = jnp.where(adm, key, _INADMISSIBLE)
                    mx = jnp.maximum(mx, jnp.where(adm, acc, _NEG_INF))
                    mn = jnp.minimum(mn, jnp.where(adm, acc, _POS_INF))
                else:
                    mx = jnp.maximum(mx, acc)
                    mn = jnp.minimum(mn, acc)
                sc_ref[t, rows, cs] = key
            mm_ref[0, rows] = mn
            mm_ref[1, rows] = mx

    n_full = jnp.minimum(jnp.minimum(((q0 + qi * tq) // CHUNK + 1) * CHUNK, n_keys) // ts, n_t)
    last_full = jnp.maximum(n_full - 1, 0)
    idx_logits(0, 0)

    def score_pair(u, carry):
        t0 = 2 * u
        t1 = jnp.minimum(t0 + 1, last_full)
        idx_logits(t1, 1)
        score_tile(t0, 0, False)
        idx_logits(jnp.minimum(t0 + 2, last_full), 0)
        score_tile(t1, 1, False)
        return carry

    lax.fori_loop(0, (n_full + 1) // 2, score_pair, 0)

    def score_tail(t, carry):
        idx_logits(t, 0)
        score_tile(t, 0, True)
        return carry

    lax.fori_loop(n_full, n_t, score_tail, 0)
    rmin = jnp.broadcast_to(jnp.min(mm_ref[0], axis=-1, keepdims=True), (tq, LANES))
    rmax = jnp.broadcast_to(jnp.max(mm_ref[1], axis=-1, keepdims=True), (tq, LANES))

    def count_ge(thr):
        def body(t, cnt):
            for c in range(nc):
                cnt = cnt + jnp.where(sc_ref[t, :, c * LANES:(c + 1) * LANES] >= thr, 1.0, 0.0)
            return cnt
        return _row_sum(lax.fori_loop(0, n_t, body, jnp.zeros((tq, LANES), F32)))

    def bis_cond(st):
        return st[0] > 0

    kf = float(k_sel)
    take_all = n_adm <= k_sel
    lo0 = _key_of(rmin) - 1
    hi0 = _key_of(rmax) + 2
    c_pos = count_ge(jnp.full((tq, LANES), _KEY_MIN_NORMAL, I32))
    c_nn = count_ge(jnp.full((tq, LANES), _KEY_NEG_ZERO, I32))
    pos = c_pos >= kf
    neg = c_nn < kf
    lo = jnp.where(pos, _KEY_MIN_NORMAL, jnp.where(neg, lo0, _KEY_NEG_ZERO))
    hi = jnp.where(pos, hi0, jnp.where(neg, _KEY_NEG_ZERO, _KEY_MIN_NORMAL))
    clo = jnp.where(pos, c_pos, jnp.where(neg, n_adm.astype(F32), c_nn))
    chi = jnp.where(pos, 0.0, jnp.where(neg, c_nn, c_pos))
    exact0 = jnp.where(clo == kf, 1, 0)
    act0 = jnp.where(take_all, 0, jnp.where(exact0 > 0, 0, jnp.where(hi > lo + 1, 1, 0)))

    def bis_body(st):
        _, it, lo, hi, clo, chi, exact, act = st
        kmid = (lo & hi) + ((lo ^ hi) >> 1)
        vmid = _key_of((_float_of(lo) + _float_of(hi)) * 0.5)
        by_value = jnp.where(it < VALUE_PIVOT_STEPS, jnp.where(vmid > lo, jnp.where(vmid < hi, 1, 0), 0), 0)
        mid = jnp.where(by_value > 0, vmid, kmid)
        cnt = count_ge(mid)
        up = jnp.where(act > 0, jnp.where(cnt >= kf, 1, 0), 0)
        dn = act - up
        lo = jnp.where(up > 0, mid, lo)
        clo = jnp.where(up > 0, cnt, clo)
        hi = jnp.where(dn > 0, mid, hi)
        chi = jnp.where(dn > 0, cnt, chi)
        hit = jnp.where(up > 0, jnp.where(cnt == kf, 1, 0), 0)
        exact = exact + hit
        act = jnp.where(act > 0, jnp.where(hit > 0, 0, jnp.where(hi > lo + 1, 1, 0)), 0)
        return jnp.max(act), it + 1, lo, hi, clo, chi, exact, act

    _, _, lo, _, clo, chi, exact, _ = lax.while_loop(
        bis_cond, bis_body, (jnp.max(act0), jnp.int32(0), lo, hi, clo, chi, exact0, act0))
    thr = jnp.where(take_all, _INADMISSIBLE + 1, lo)
    thr_up = jnp.where(take_all, _INADMISSIBLE + 1, lo + 1)

    need = kf - chi
    tied = jnp.where(take_all, 0, jnp.where(exact > 0, 0, jnp.where(clo > kf, 1, 0)))

    def count_tied_below(cut):
        def body(t, cnt):
            for c in range(nc):
                s = sc_ref[t, :, c * LANES:(c + 1) * LANES]
                col = t * ts + c * LANES + lane
                cnt = cnt + jnp.where(s == thr, jnp.where(col < cut, 1.0, 0.0), 0.0)
            return cnt
        return _row_sum(lax.fori_loop(0, n_t, body, jnp.zeros((tq, LANES), F32)))

    def cut_body(st):
        _, lo_c, hi_c, act = st
        mid = (lo_c + hi_c) >> 1
        cnt = count_tied_below(mid)
        on = act > 0
        le = cnt <= need
        lo_c = jnp.where(on, jnp.where(le, mid, lo_c), lo_c)
        hi_c = jnp.where(on, jnp.where(le, hi_c, mid), hi_c)
        act = jnp.where(on, jnp.where(hi_c - lo_c > 1, 1, 0), 0)
        return jnp.max(act), lo_c, hi_c, act

    _, cut_lo, _, _ = lax.while_loop(
        bis_cond, cut_body,
        (jnp.max(tied), jnp.zeros((tq, LANES), I32), jnp.full((tq, LANES), n_tiles_total * ts + 1, I32), tied))
    cut = jnp.where(tied > 0, cut_lo, n_tiles_total * ts + 1)

    m_ref[...] = jnp.full(m_ref.shape, _M_INIT, F32)
    acc_ref[...] = jnp.zeros(acc_ref.shape, F32)
    rc = min(tq, ROW_CHUNK)
    ones = jnp.ones((ts, LANES), BF16)
    last = n_t - 1

    def logits(t, slot):
        for g in range(N_KV_HEADS):
            qg = q_ref[0, g * GQA_GROUP:(g + 1) * GQA_GROUP].reshape(GQA_GROUP * tq, HEAD_DIM)
            s_ref[slot, g * GQA_GROUP * tq:(g + 1) * GQA_GROUP * tq] = jnp.dot(
                qg, kt_ref[0, g, t], preferred_element_type=F32)

    def softmax_pv(t, slot, live):
        for c in range(nc):
            col = t * ts + c * LANES + lane
            bound = jnp.where(live, jnp.where(col < cut, thr, thr_up), _NO_KEY)
            bias_ref[slot, :, c * LANES:(c + 1) * LANES] = jnp.where(
                sc_ref[t, :, c * LANES:(c + 1) * LANES] >= bound, 0.0, _NEG_INF)
        for g in range(N_KV_HEADS):
            for j in range(GQA_GROUP):
                hd = g * GQA_GROUP + j
                for r0 in range(0, tq, rc):
                    r1 = j * tq + r0
                    r2 = hd * tq + r0
                    s = [s_ref[slot, r2:r2 + rc, c * LANES:(c + 1) * LANES]
                         + bias_ref[slot, r0:r0 + rc, c * LANES:(c + 1) * LANES] for c in range(nc)]
                    mx = s[0]
                    for c in range(1, nc):
                        mx = jnp.maximum(mx, s[c])
                    m_old = m_ref[hd, r0:r0 + rc]
                    m_new = jnp.maximum(m_old, jnp.broadcast_to(jnp.max(mx, axis=-1, keepdims=True), (rc, LANES)))
                    alpha_ref[slot, g, r1:r1 + rc] = jnp.exp2(m_old - m_new)
                    for c in range(nc):
                        p_ref[slot, g, r1:r1 + rc, c * LANES:(c + 1) * LANES] = jnp.exp2(s[c] - m_new).astype(BF16)
                    m_ref[hd, r0:r0 + rc] = m_new
            v_aug = jnp.concatenate([v_ref[0, g, t], ones], axis=-1)
            pv = jnp.dot(p_ref[slot, g], v_aug, preferred_element_type=F32)
            alpha = alpha_ref[slot, g]
            for half in range(2):
                cols = slice(half * LANES, (half + 1) * LANES)
                acc_ref[g, :, cols] = alpha * acc_ref[g, :, cols] + pv[:, cols]

    logits(0, 0)

    def attn_pair(u, carry):
        t0 = 2 * u
        t1 = jnp.minimum(t0 + 1, last)
        logits(t1, 1)
        softmax_pv(t0, 0, True)
        logits(jnp.minimum(t0 + 2, last), 0)
        softmax_pv(t1, 1, t0 + 1 < n_t)
        return carry

    lax.fori_loop(0, (n_t + 1) // 2, attn_pair, 0)
    for hd in range(N_HEADS):
        g, j = divmod(hd, GQA_GROUP)
        o_ref[0, :, hd * HEAD_DIM:(hd + 1) * HEAD_DIM] = (
            acc_ref[g, j * tq:(j + 1) * tq, 0:HEAD_DIM] / acc_ref[g, j * tq:(j + 1) * tq, HEAD_DIM:2 * HEAD_DIM])


def _attn_t_kernel(qst_ref, wrow_ref, q_ref, ki_ref, kt_ref, v_ref, o_ref,
                   sc_ref, mm_ref, st_ref, s_ref, m_ref, acc_ref, bias_ref, p_ref, alpha_ref,
                   *, tq, ts, q0, n_keys, k_sel):
    qi = pl.program_id(1)
    n_tiles_total = ki_ref.shape[1]
    nc = ts // LANES
    rk = KEY_ROWS
    qpos = q0 + qi * tq + lax.broadcasted_iota(I32, (1, tq), 1)
    n_adm = jnp.minimum((qpos // CHUNK + 1) * CHUNK, n_keys)
    last_adm = jnp.minimum(((q0 + (qi + 1) * tq - 1) // CHUNK + 1) * CHUNK, n_keys)
    n_t = jnp.minimum((last_adm + ts - 1) // ts, n_tiles_total)
    n_full = jnp.minimum(jnp.minimum(((q0 + qi * tq) // CHUNK + 1) * CHUNK, n_keys) // ts, n_t)
    last_full = jnp.maximum(n_full - 1, 0)
    qst = qst_ref[0, 0]
    wr = wrow_ref[0, 0]

    mm_ref[0] = jnp.full((rk, tq), _POS_INF, F32)
    mm_ref[1] = jnp.full((rk, tq), _NEG_INF, F32)

    def idx_logits(t, slot):
        st_ref[slot] = jnp.dot(ki_ref[0, t], qst, preferred_element_type=F32)

    def score_tile(t, slot, masked):
        mn, mx = mm_ref[0], mm_ref[1]
        for r0 in range(0, ts, rk):
            acc = None
            for hd in range(N_IDX_HEADS):
                r = jnp.maximum(st_ref[slot, r0:r0 + rk, hd * tq:(hd + 1) * tq], 0.0) * wr[hd:hd + 1, :]
                acc = r if acc is None else acc + r
            key = _key_of(acc)
            if masked:
                adm = (t * ts + r0 + lax.broadcasted_iota(I32, (rk, tq), 0)) < n_adm
                key = jnp.where(adm, key, _INADMISSIBLE)
                mx = jnp.maximum(mx, jnp.where(adm, acc, _NEG_INF))
                mn = jnp.minimum(mn, jnp.where(adm, acc, _POS_INF))
            else:
                mx = jnp.maximum(mx, acc)
                mn = jnp.minimum(mn, acc)
            sc_ref[t, r0:r0 + rk, :] = key
        mm_ref[0] = mn
        mm_ref[1] = mx

    idx_logits(0, 0)

    def score_pair(u, carry):
        t0 = 2 * u
        t1 = jnp.minimum(t0 + 1, last_full)
        idx_logits(t1, 1)
        score_tile(t0, 0, False)
        idx_logits(jnp.minimum(t0 + 2, last_full), 0)
        score_tile(t1, 1, False)
        return carry

    lax.fori_loop(0, (n_full + 1) // 2, score_pair, 0)

    def score_tail(t, carry):
        idx_logits(t, 0)
        score_tile(t, 0, True)
        return carry

    lax.fori_loop(n_full, n_t, score_tail, 0)
    rmin = jnp.min(mm_ref[0], axis=0, keepdims=True)
    rmax = jnp.max(mm_ref[1], axis=0, keepdims=True)

    def count_ge(thr):
        def body(t, cnt):
            for r0 in range(0, ts, rk):
                cnt = cnt + jnp.where(sc_ref[t, r0:r0 + rk, :] >= thr, 1.0, 0.0)
            return cnt
        return jnp.sum(lax.fori_loop(0, n_t, body, jnp.zeros((rk, tq), F32)), axis=0, keepdims=True)

    def bis_cond(st):
        return st[0] > 0

    kf = float(k_sel)
    take_all = n_adm <= k_sel
    lo0 = _key_of(rmin) - 1
    hi0 = _key_of(rmax) + 2
    c_pos = count_ge(jnp.full((1, tq), _KEY_MIN_NORMAL, I32))
    c_nn = count_ge(jnp.full((1, tq), _KEY_NEG_ZERO, I32))
    pos = c_pos >= kf
    neg = c_nn < kf
    lo = jnp.where(pos, _KEY_MIN_NORMAL, jnp.where(neg, lo0, _KEY_NEG_ZERO))
    hi = jnp.where(pos, hi0, jnp.where(neg, _KEY_NEG_ZERO, _KEY_MIN_NORMAL))
    clo = jnp.where(pos, c_pos, jnp.where(neg, n_adm.astype(F32), c_nn))
    chi = jnp.where(pos, 0.0, jnp.where(neg, c_nn, c_pos))
    exact0 = jnp.where(clo == kf, 1, 0)
    act0 = jnp.where(take_all, 0, jnp.where(exact0 > 0, 0, jnp.where(hi > lo + 1, 1, 0)))

    def bis_step(it, lo, hi, clo, chi, exact, act):
        kmid = (lo & hi) + ((lo ^ hi) >> 1)
        vmid = _key_of((_float_of(lo) + _float_of(hi)) * 0.5)
        by_value = jnp.where(it < VALUE_PIVOT_STEPS, jnp.where(vmid > lo, jnp.where(vmid < hi, 1, 0), 0), 0)
        mid = jnp.where(by_value > 0, vmid, kmid)
        cnt = count_ge(mid)
        up = jnp.where(act > 0, jnp.where(cnt >= kf, 1, 0), 0)
        dn = act - up
        lo = jnp.where(up > 0, mid, lo)
        clo = jnp.where(up > 0, cnt, clo)
        hi = jnp.where(dn > 0, mid, hi)
        chi = jnp.where(dn > 0, cnt, chi)
        hit = jnp.where(up > 0, jnp.where(cnt == kf, 1, 0), 0)
        exact = exact + hit
        act = jnp.where(act > 0, jnp.where(hit > 0, 0, jnp.where(hi > lo + 1, 1, 0)), 0)
        return it + 1, lo, hi, clo, chi, exact, act

    def any_active(act):
        return jnp.max(act.astype(F32))

    def bis_body(st):
        st = bis_step(*bis_step(*st[1:]))
        return (any_active(st[-1]),) + st

    _, _, lo, _, clo, chi, exact, _ = lax.while_loop(
        bis_cond, bis_body, (any_active(act0), jnp.int32(0), lo, hi, clo, chi, exact0, act0))
    thr = jnp.where(take_all, _INADMISSIBLE + 1, lo)
    thr_up = jnp.where(take_all, _INADMISSIBLE + 1, lo + 1)

    need = kf - chi
    tied = jnp.where(take_all, 0, jnp.where(exact > 0, 0, jnp.where(clo > kf, 1, 0)))

    def count_tied_below(cut):
        def body(t, cnt):
            for r0 in range(0, ts, rk):
                kidx = t * ts + r0 + lax.broadcasted_iota(I32, (rk, tq), 0)
                cnt = cnt + jnp.where(sc_ref[t, r0:r0 + rk, :] == thr, jnp.where(kidx < cut, 1.0, 0.0), 0.0)
            return cnt
        return jnp.sum(lax.fori_loop(0, n_t, body, jnp.zeros((rk, tq), F32)), axis=0, keepdims=True)

    def cut_body(st):
        _, lo_c, hi_c, act = st
        mid = (lo_c + hi_c) >> 1
        cnt = count_tied_below(mid)
        on = act > 0
        le = cnt <= need
        lo_c = jnp.where(on, jnp.where(le, mid, lo_c), lo_c)
        hi_c = jnp.where(on, jnp.where(le, hi_c, mid), hi_c)
        act = jnp.where(on, jnp.where(hi_c - lo_c > 1, 1, 0), 0)
        return jnp.max(act), lo_c, hi_c, act

    _, cut_lo, _, _ = lax.while_loop(
        bis_cond, cut_body,
        (jnp.max(tied), jnp.zeros((1, tq), I32), jnp.full((1, tq), n_tiles_total * ts + 1, I32), tied))
    cut = jnp.where(tied > 0, cut_lo, n_tiles_total * ts + 1)

    m_ref[...] = jnp.full(m_ref.shape, _M_INIT, F32)
    acc_ref[...] = jnp.zeros(acc_ref.shape, F32)
    rc = min(tq, ROW_CHUNK)
    ones = jnp.ones((ts, LANES), BF16)
    last = n_t - 1

    def logits(t, slot):
        for g in range(N_KV_HEADS):
            qg = q_ref[0, g * GQA_GROUP:(g + 1) * GQA_GROUP].reshape(GQA_GROUP * tq, HEAD_DIM)
            s_ref[slot, g * GQA_GROUP * tq:(g + 1) * GQA_GROUP * tq] = jnp.dot(
                qg, kt_ref[0, g, t], preferred_element_type=F32)

    def softmax_pv(t, slot, live):
        for c in range(nc):
            kidx = t * ts + c * LANES + lax.broadcasted_iota(I32, (LANES, tq), 0)
            bound = jnp.where(live, jnp.where(kidx < cut, thr, thr_up), _NO_KEY)
            bias_t = jnp.where(sc_ref[t, c * LANES:(c + 1) * LANES, :] >= bound, 0.0, _NEG_INF)
            bias_ref[slot, :, c * LANES:(c + 1) * LANES] = bias_t.T
        for g in range(N_KV_HEADS):
            for j in range(GQA_GROUP):
                hd = g * GQA_GROUP + j
                for r0 in range(0, tq, rc):
                    r1 = j * tq + r0
                    r2 = hd * tq + r0
                    s = [s_ref[slot, r2:r2 + rc, c * LANES:(c + 1) * LANES]
                         + bias_ref[slot, r0:r0 + rc, c * LANES:(c + 1) * LANES] for c in range(nc)]
                    mx = s[0]
                    for c in range(1, nc):
                        mx = jnp.maximum(mx, s[c])
                    m_old = m_ref[hd, r0:r0 + rc]
                    m_new = jnp.maximum(m_old, jnp.broadcast_to(jnp.max(mx, axis=-1, keepdims=True), (rc, LANES)))
                    alpha_ref[slot, g, r1:r1 + rc] = jnp.exp2(m_old - m_new)
                    for c in range(nc):
                        p_ref[slot, g, r1:r1 + rc, c * LANES:(c + 1) * LANES] = jnp.exp2(s[c] - m_new).astype(BF16)
                    m_ref[hd, r0:r0 + rc] = m_new
            v_aug = jnp.concatenate([v_ref[0, g, t], ones], axis=-1)
            pv = jnp.dot(p_ref[slot, g], v_aug, preferred_element_type=F32)
            alpha = alpha_ref[slot, g]
            for half in range(2):
                cols = slice(half * LANES, (half + 1) * LANES)
                acc_ref[g, :, cols] = alpha * acc_ref[g, :, cols] + pv[:, cols]

    logits(0, 0)

    def attn_pair(u, carry):
        t0 = 2 * u
        t1 = jnp.minimum(t0 + 1, last)
        logits(t1, 1)
        softmax_pv(t0, 0, True)
        logits(jnp.minimum(t0 + 2, last), 0)
        softmax_pv(t1, 1, t0 + 1 < n_t)
        return carry

    lax.fori_loop(0, (n_t + 1) // 2, attn_pair, 0)
    for hd in range(N_HEADS):
        g, j = divmod(hd, GQA_GROUP)
        o_ref[0, :, hd * HEAD_DIM:(hd + 1) * HEAD_DIM] = (
            acc_ref[g, j * tq:(j + 1) * tq, 0:HEAD_DIM] / acc_ref[g, j * tq:(j + 1) * tq, HEAD_DIM:2 * HEAD_DIM])


def _attend_t(qst, wrow, q_att, ki4, kt, v, q0, n_keys, tq, ts):
    B, _, T, _ = q_att.shape
    n_tiles = ki4.shape[1]
    k_sel = min(TOPK_MAX, n_keys // 4)
    kern = functools.partial(_attn_t_kernel, tq=tq, ts=ts, q0=q0, n_keys=n_keys, k_sel=k_sel)
    return pl.pallas_call(
        kern, grid=(B, T // tq),
        in_specs=[
            pl.BlockSpec((1, 1, 4 * IDX_DIM, N_IDX_HEADS * tq), lambda b, i: (b, i, 0, 0)),
            pl.BlockSpec((1, 1, N_IDX_HEADS, tq), lambda b, i: (b, i, 0, 0)),
            pl.BlockSpec((1, N_HEADS, tq, HEAD_DIM), lambda b, i: (b, 0, i, 0)),
            _resident((1, n_tiles, ts, 4 * IDX_DIM), lambda b, i: (b, 0, 0, 0)),
            _resident((1, N_KV_HEADS, n_tiles, HEAD_DIM, ts), lambda b, i: (b, 0, 0, 0, 0)),
            _resident((1, N_KV_HEADS, n_tiles, ts, HEAD_DIM), lambda b, i: (b, 0, 0, 0, 0)),
        ],
        out_specs=pl.BlockSpec((1, tq, N_HEADS * HEAD_DIM), lambda b, i: (b, i, 0)),
        out_shape=jax.ShapeDtypeStruct((B, T, N_HEADS * HEAD_DIM), F32),
        scratch_shapes=[
            pltpu.VMEM((n_tiles, ts, tq), I32),
            pltpu.VMEM((2, KEY_ROWS, tq), F32),
            pltpu.VMEM((2, ts, N_IDX_HEADS * tq), F32),
            pltpu.VMEM((2, N_HEADS * tq, ts), F32),
            pltpu.VMEM((N_HEADS, tq, LANES), F32),
            pltpu.VMEM((N_KV_HEADS, GQA_GROUP * tq, 2 * HEAD_DIM), F32),
            pltpu.VMEM((2, tq, ts), F32),
            pltpu.VMEM((2, N_KV_HEADS, GQA_GROUP * tq, ts), BF16),
            pltpu.VMEM((2, N_KV_HEADS, GQA_GROUP * tq, LANES), F32),
        ],
        name="attend_t",
        compiler_params=pltpu.CompilerParams(dimension_semantics=("arbitrary", "arbitrary"),
                                             vmem_limit_bytes=VMEM_LIMIT),
    )(qst, wrow, q_att, ki4, kt, v)


def _attend(qs, kiw, q_att, ki4, kt, v, q0, n_keys, tq, ts):
    B, _, T, _ = qs.shape
    n_tiles = ki4.shape[1]
    k_sel = min(TOPK_MAX, n_keys // 4)
    kern = functools.partial(_attn_kernel, tq=tq, ts=ts, q0=q0, n_keys=n_keys, k_sel=k_sel)
    return pl.pallas_call(
        kern, grid=(B, T // tq),
        in_specs=[
            pl.BlockSpec((1, N_IDX_HEADS, tq, 4 * IDX_DIM), lambda b, i: (b, 0, i, 0)),
            pl.BlockSpec((1, tq, LANES), lambda b, i: (b, i, 0)),
            pl.BlockSpec((1, N_HEADS, tq, HEAD_DIM), lambda b, i: (b, 0, i, 0)),
            _resident((1, n_tiles, 4 * IDX_DIM, ts), lambda b, i: (b, 0, 0, 0)),
            _resident((1, N_KV_HEADS, n_tiles, HEAD_DIM, ts), lambda b, i: (b, 0, 0, 0, 0)),
            _resident((1, N_KV_HEADS, n_tiles, ts, HEAD_DIM), lambda b, i: (b, 0, 0, 0, 0)),
        ],
        out_specs=pl.BlockSpec((1, tq, N_HEADS * HEAD_DIM), lambda b, i: (b, i, 0)),
        out_shape=jax.ShapeDtypeStruct((B, T, N_HEADS * HEAD_DIM), F32),
        scratch_shapes=[
            pltpu.VMEM((n_tiles, tq, ts), I32),
            pltpu.VMEM((2, tq, LANES), F32),
            pltpu.VMEM((N_IDX_HEADS, tq, LANES), F32),
            pltpu.VMEM((2, N_HEADS * tq, ts), F32),
            pltpu.VMEM((N_HEADS, tq, LANES), F32),
            pltpu.VMEM((N_KV_HEADS, GQA_GROUP * tq, 2 * HEAD_DIM), F32),
            pltpu.VMEM((2, tq, ts), F32),
            pltpu.VMEM((2, N_KV_HEADS, GQA_GROUP * tq, ts), BF16),
            pltpu.VMEM((2, N_KV_HEADS, GQA_GROUP * tq, LANES), F32),
        ],
        name="attend",
        compiler_params=pltpu.CompilerParams(dimension_semantics=("arbitrary", "arbitrary"),
                                             vmem_limit_bytes=VMEM_LIMIT),
    )(qs, kiw, q_att, ki4, kt, v)


def _rms(x, g):
    return x * lax.rsqrt(jnp.mean(x * x, axis=-1, keepdims=True) + RMS_EPS) * g


def _out_kernel(x_ref, pa_ref, sgb_ref, b_ref, wo_ref, g1_ref, g2_ref, wgu_ref, wd_ref, g3_ref, y_ref, *, d_ff):
    merged = pa_ref[...] + sgb_ref[...] * b_ref[...]
    mix = jnp.dot(merged.astype(BF16), wo_ref[...], preferred_element_type=F32)
    x1 = x_ref[...] + _rms(mix, g1_ref[...])
    h2 = _rms(x1, g2_ref[...]).astype(BF16)
    gu = jnp.dot(h2, wgu_ref[...], preferred_element_type=F32)
    gate, up = gu[:, :d_ff], gu[:, d_ff:]
    act = (gate * _sigmoid(gate) * up).astype(BF16)
    f = jnp.dot(act, wd_ref[...], preferred_element_type=F32)
    y_ref[...] = x1 + _rms(f, g3_ref[...])


def _finish(x, pa, sgb, b, w_out, g_post, g_ffn_pre, w_gate_up, w_down, g_ffn_post, tm):
    N, D = x.shape
    d_ff = w_down.shape[0]
    row = pl.BlockSpec((tm, D), lambda i: (i, 0))
    const = lambda i: (0, 0)
    return pl.pallas_call(
        functools.partial(_out_kernel, d_ff=d_ff), grid=(N // tm,),
        in_specs=[row, row, row, row,
                  _resident((D, D), const), _resident((1, D), const), _resident((1, D), const),
                  _resident((D, 2 * d_ff), const), _resident((d_ff, D), const), _resident((1, D), const)],
        out_specs=row, out_shape=jax.ShapeDtypeStruct((N, D), F32), name="finish",
        compiler_params=pltpu.CompilerParams(dimension_semantics=("arbitrary",), vmem_limit_bytes=VMEM_LIMIT),
    )(x, pa, sgb, b, w_out, g_post, g_ffn_pre, w_gate_up, w_down, g_ffn_post)


def _tile_keys(n_keys, ts):
    return -(-n_keys // ts) * ts


def _layer(x, pos0, hist, k_cache, v_cache, ki_cache, wts, tm, tq, ts, tm_out):
    (w_in_r, w_pool, pool_scale, w_out, w_gate_up, w_down, g_pre, g_post, g_ffn_pre, g_ffn_post) = wts
    B, T, D = x.shape
    hist16 = jnp.concatenate([jnp.zeros((B, HALO // 2 - POOL_HIST, POOL_WIDTH), F32), hist], axis=1)
    own_keys_only = k_cache is None and T % ts == 0 and ts % tm == 0 and tq == LANES and tm % LANES == 0
    outs = _project(x, hist16, pos0, w_in_r, g_pre, w_pool, pool_scale, tm, ts if own_keys_only else None)
    u, q_att, k, v, qs, kiw, pa, sgb = outs[:8]
    ki = kiw[:, :, :IDX_DIM]
    if own_keys_only:
        kt, vt, ki4, wrow = outs[8:]
        b = _attend_t(qs, wrow, q_att, ki4, kt, vt, pos0, T, tq, ts)
    else:
        if k_cache is None:
            k_all, v_all, ki_all = k, v, ki
        else:
            k_all = jnp.concatenate([k_cache.reshape(B, -1, N_KV_HEADS * HEAD_DIM), k], axis=1)
            v_all = jnp.concatenate([v_cache.reshape(B, -1, N_KV_HEADS * HEAD_DIM), v], axis=1)
            ki_all = jnp.concatenate([ki_cache, ki], axis=1)
        n_keys = k_all.shape[1]
        lp = _tile_keys(n_keys, ts)
        pad = ((0, 0), (0, lp - n_keys), (0, 0))
        nt = lp // ts
        kt = jnp.pad(k_all.astype(BF16), pad).reshape(B, nt, ts, N_KV_HEADS, HEAD_DIM).transpose(0, 3, 1, 4, 2)
        vt = jnp.pad(v_all.astype(BF16), pad).reshape(B, nt, ts, N_KV_HEADS, HEAD_DIM).transpose(0, 3, 1, 2, 4)
        ki4 = _split_keys(jnp.swapaxes(jnp.pad(ki_all, pad), 1, 2), ts)
        b = _attend(qs, kiw, q_att, ki4, kt, vt, pos0, n_keys, tq, ts)
    y = _finish(x.reshape(B * T, D), pa.reshape(B * T, D), sgb.reshape(B * T, D), b.reshape(B * T, D),
                w_out, g_post, g_ffn_pre, w_gate_up, w_down, g_ffn_post, tm_out).reshape(B, T, D)
    new_pool = jnp.concatenate([hist, u], axis=1)[:, T:]
    return (y, k.reshape(B, T, N_KV_HEADS, HEAD_DIM), v.reshape(B, T, N_KV_HEADS, HEAD_DIM), ki, new_pool)


def _relayout_w_in(w_in):
    d = w_in.shape[0]
    o_kiw = POOL_WIDTH + N_HEADS * HEAD_DIM + 2 * N_KV_HEADS * HEAD_DIM + N_IDX_HEADS * IDX_DIM
    narrow = IDX_DIM + N_IDX_HEADS
    padded = jnp.concatenate([w_in[:, :o_kiw + narrow], jnp.zeros((d, LANES - narrow), w_in.dtype),
                              w_in[:, o_kiw + narrow:]], axis=1)
    return padded.astype(BF16)


def kernel(x_prompt, x_sample, cache_k, cache_v, cache_k_idx, state_pool, w_in, w_pool, pool_scale, w_out,
           w_gate_up, w_down, norm_mix_pre, norm_mix_post, norm_ffn_pre, norm_ffn_post):
    depth = w_in.shape[0]
    past = cache_k.shape[2]
    t_p, t_s = x_prompt.shape[1], x_sample.shape[1]
    hist_p = jnp.zeros((x_prompt.shape[0], POOL_HIST, POOL_WIDTH), x_prompt.dtype)
    xp, xs = x_prompt, x_sample
    outs = [[] for _ in range(8)]
    for l in range(depth):
        wts = (_relayout_w_in(w_in[l]), w_pool[l].astype(BF16), pool_scale[l][None, :], w_out[l].astype(BF16),
               w_gate_up[l].astype(BF16), w_down[l].astype(BF16), norm_mix_pre[l][None, :],
               norm_mix_post[l][None, :], norm_ffn_pre[l][None, :], norm_ffn_post[l][None, :])
        tm_p = min(256, t_p)
        tq_p = min(128, t_p)
        xp, k1, v1, ki1, p1 = _layer(xp, 0, hist_p, None, None, None, wts, tm_p, tq_p, 512, tm_p)
        n_s = xs.shape[0] * t_s
        xs, k2, v2, ki2, p2 = _layer(xs, past, state_pool[l], cache_k[l], cache_v[l], cache_k_idx[l], wts,
                                     t_s, t_s, 512, min(256, n_s))
        for lst, val in zip(outs, (k1, v1, ki1, p1, k2, v2, ki2, p2)):
            lst.append(val)
    return (xp, xs) + tuple(jnp.stack(o) for o in outs)
```

```python
import functools

import jax
import jax.numpy as jnp
from jax import lax
from jax.experimental import pallas as pl
from jax.experimental.pallas import tpu as pltpu

F32 = jnp.float32
BF16 = jnp.bfloat16
I32 = jnp.int32
I16 = jnp.int16

LANES = 128
CHUNK = 64
POOL_WINDOWS = (2, 4, 8, 16)
N_POOL_GROUPS = 4
POOL_GROUP_WIDTH = 128
POOL_WIDTH = N_POOL_GROUPS * POOL_GROUP_WIDTH
POOL_HIST = 15
N_HEADS = 8
N_KV_HEADS = 2
HEAD_DIM = 128
GQA_GROUP = N_HEADS // N_KV_HEADS
ROPE_THETA = 500000.0
N_IDX_HEADS = 8
IDX_DIM = 64
TOPK_MAX = 256
RMS_EPS = 1e-6
ATTN_SCALE = HEAD_DIM ** -0.5
IDX_SCALE = (N_IDX_HEADS ** -0.5) * (IDX_DIM ** -0.5)

LOG2_E = 1.4426950408889634
VALUE_PIVOT_STEPS = 28
MAX_FIXED_REFERENCE = 60.0
KEY_ROWS = 64
ROW_CHUNK = 32
HALO = 32
VMEM_LIMIT = 56 * 1024 * 1024

_NEG_INF = float("-inf")
_POS_INF = float("inf")
_INADMISSIBLE = -(2 ** 31)
_NO_KEY = 2 ** 31 - 1
_KEY_MIN_NORMAL = 0x00800000
_KEY_NEG_ZERO = -1
_M_INIT = -1e30


def _resident(block_shape, index_map):
    return pl.BlockSpec(block_shape, index_map, pipeline_mode=pl.Buffered(1))


def _rope(xs, cos, sin, half, period):
    lane = lax.broadcasted_iota(I32, xs.shape, 1)
    ahead = pltpu.roll(xs, LANES - half, 1)
    behind = pltpu.roll(xs, half, 1)
    partner = jnp.where((lane & (period - 1)) < half, ahead, behind)
    return xs * cos + partner * sin


def _sigmoid(x):
    return 1.0 / (1.0 + jnp.exp(-x))


def _proj_kernel(x_ref, g_ref, w_ref, hist_ref, cq_ref, sq_ref, ci_ref, si_ref, wpool_ref, pscale_ref,
                 u_ref, qatt_ref, k_ref, v_ref, qs_ref, kiw_ref, pa_ref, sgb_ref, *rest, tm, pos0, d_model, key_tiles):
    if key_tiles:
        kt_ref, vt_ref, ki4_ref, wrow_ref, e_ref, s2_ref, s4_ref, s8_ref = rest
    else:
        e_ref, s2_ref, s4_ref, s8_ref = rest
    i = pl.program_id(1)
    x = x_ref[0]
    h = x * lax.rsqrt(jnp.mean(x * x, axis=-1, keepdims=True) + RMS_EPS) * g_ref[...]
    proj = jnp.dot(h.astype(BF16), w_ref[...], preferred_element_type=F32)

    o_q = POOL_WIDTH
    o_k = o_q + N_HEADS * HEAD_DIM
    o_v = o_k + N_KV_HEADS * HEAD_DIM
    o_qi = o_v + N_KV_HEADS * HEAD_DIM
    o_kiw = o_qi + N_IDX_HEADS * IDX_DIM
    o_ga = o_kiw + LANES
    o_gb = o_ga + d_model

    cq, sq, ci, si = cq_ref[...], sq_ref[...], ci_ref[...], si_ref[...]
    half_q = HEAD_DIM // 8
    half_i = IDX_DIM // 8

    for hd in range(N_HEADS):
        qh = _rope(proj[:, o_q + hd * HEAD_DIM:o_q + (hd + 1) * HEAD_DIM], cq, sq, half_q, HEAD_DIM)
        qatt_ref[0, hd] = (qh * (ATTN_SCALE * LOG2_E)).astype(BF16)
    for kh in range(N_KV_HEADS):
        k_h = _rope(proj[:, o_k + kh * HEAD_DIM:o_k + (kh + 1) * HEAD_DIM], cq, sq, half_q, HEAD_DIM)
        k_ref[0, :, kh * HEAD_DIM:(kh + 1) * HEAD_DIM] = k_h
        if key_tiles:
            kt_ref[0, kh, 0] = k_h.T.astype(BF16)
            vt_ref[0, kh, 0] = proj[:, o_v + kh * HEAD_DIM:o_v + (kh + 1) * HEAD_DIM].astype(BF16)
    v_ref[0] = proj[:, o_v:o_qi]

    for pr in range(N_IDX_HEADS // 2):
        qi2 = _rope(proj[:, o_qi + pr * LANES:o_qi + (pr + 1) * LANES], ci, si, half_i, IDX_DIM)
        hi = qi2.astype(BF16).astype(F32)
        lo = qi2 - hi
        lane2 = lax.broadcasted_iota(I32, qi2.shape, 1)
        first = jnp.where(lane2 < IDX_DIM, hi, pltpu.roll(lo, IDX_DIM, 1))
        second = jnp.where(lane2 < IDX_DIM, pltpu.roll(hi, IDX_DIM, 1), lo)
        for sub, slab in enumerate((first, second)):
            hd = 2 * pr + sub
            if key_tiles:
                for sb in range(tm // LANES):
                    slab_t = slab[sb * LANES:(sb + 1) * LANES].T.astype(BF16)
                    qs_ref[0, sb, 0:LANES, hd * LANES:(hd + 1) * LANES] = slab_t
                    qs_ref[0, sb, LANES:2 * LANES, hd * LANES:(hd + 1) * LANES] = slab_t
            else:
                qs_ref[0, hd, :, 0:LANES] = slab.astype(BF16)
                qs_ref[0, hd, :, LANES:2 * LANES] = slab.astype(BF16)

    kiw = _rope(proj[:, o_kiw:o_kiw + LANES], ci, si, half_i, IDX_DIM)
    lane = lax.broadcasted_iota(I32, kiw.shape, 1)
    kiw_full = jnp.where(lane < IDX_DIM, kiw, proj[:, o_kiw:o_kiw + LANES] * IDX_SCALE)
    kiw_ref[0] = kiw_full
    if key_tiles:
        ki_hi = kiw.astype(BF16).astype(F32)
        ki_lo = kiw - ki_hi
        ki4_ref[0, 0, :, 0:LANES] = jnp.where(lane < IDX_DIM, ki_hi, pltpu.roll(ki_hi, IDX_DIM, 1)).astype(BF16)
        ki4_ref[0, 0, :, LANES:2 * LANES] = jnp.where(lane < IDX_DIM, ki_lo, pltpu.roll(ki_lo, IDX_DIM, 1)).astype(BF16)
        for sb in range(tm // LANES):
            wrow_ref[0, sb] = kiw_full[sb * LANES:(sb + 1) * LANES].T[IDX_DIM:IDX_DIM + N_IDX_HEADS]

    u = proj[:, 0:POOL_WIDTH]
    u_ref[0] = u

    @pl.when(i == 0)
    def _():
        e_ref[0:HALO // 2, :] = jnp.zeros((HALO // 2, POOL_WIDTH), F32)
        e_ref[HALO // 2:HALO, :] = hist_ref[0]

    e_ref[HALO:HALO + tm, :] = u
    n2, n4, n8 = tm + 24, tm + 16, tm + 8
    s2_ref[8:8 + n2, :] = e_ref[8:8 + n2, :] + e_ref[7:7 + n2, :]
    s4_ref[16:16 + n4, :] = s2_ref[16:16 + n4, :] + s2_ref[14:14 + n4, :]
    s8_ref[24:24 + n8, :] = s4_ref[24:24 + n8, :] + s4_ref[20:20 + n8, :]
    s16 = s8_ref[HALO:HALO + tm, :] + s8_ref[HALO - 8:HALO - 8 + tm, :]
    wins = (s2_ref[HALO:HALO + tm, :], s4_ref[HALO:HALO + tm, :], s8_ref[HALO:HALO + tm, :], s16)
    e_ref[HALO // 2:HALO, :] = e_ref[HALO // 2 + tm:HALO + tm, :]

    pos = pos0 + i * tm + lax.broadcasted_iota(I32, (tm, POOL_GROUP_WIDTH), 0)
    a_parts = []
    for g, w in enumerate(POOL_WINDOWS):
        lo_l, hi_l = g * POOL_GROUP_WIDTH, (g + 1) * POOL_GROUP_WIDTH
        cnt = jnp.minimum(pos + 1, w).astype(F32)
        pooled = wins[g][:, lo_l:hi_l] / cnt - u[:, lo_l:hi_l]
        a_parts.append(jnp.dot(pooled.astype(BF16), wpool_ref[g], preferred_element_type=F32))
    a = jnp.concatenate(a_parts, axis=-1) * pscale_ref[...]
    pa_ref[0] = _sigmoid(proj[:, o_ga:o_gb]) * a
    sgb_ref[0] = _sigmoid(proj[:, o_gb:o_gb + d_model])


def _project(x, hist16, pos0, w_in_r, g_pre, w_pool, pool_scale, tm, key_tile=None):
    B, T, D = x.shape
    W = w_in_r.shape[1]
    pos = pos0 + jnp.arange(T, dtype=I32)
    cq, sq = _rope_tables(pos, HEAD_DIM)
    ci, si = _rope_tables(pos, IDX_DIM)
    row = lambda b, i: (b, i, 0)
    tab = lambda b, i: (i, 0)
    const2 = lambda b, i: (0, 0)
    out_shape = (
        jax.ShapeDtypeStruct((B, T, POOL_WIDTH), F32),
        jax.ShapeDtypeStruct((B, N_HEADS, T, HEAD_DIM), BF16),
        jax.ShapeDtypeStruct((B, T, N_KV_HEADS * HEAD_DIM), F32),
        jax.ShapeDtypeStruct((B, T, N_KV_HEADS * HEAD_DIM), F32),
        jax.ShapeDtypeStruct((B, N_IDX_HEADS, T, 4 * IDX_DIM), BF16),
        jax.ShapeDtypeStruct((B, T, LANES), F32),
        jax.ShapeDtypeStruct((B, T, D), F32),
        jax.ShapeDtypeStruct((B, T, D), F32),
    )
    out_specs = (
        pl.BlockSpec((1, tm, POOL_WIDTH), row),
        pl.BlockSpec((1, N_HEADS, tm, HEAD_DIM), lambda b, i: (b, 0, i, 0)),
        pl.BlockSpec((1, tm, N_KV_HEADS * HEAD_DIM), row),
        pl.BlockSpec((1, tm, N_KV_HEADS * HEAD_DIM), row),
        pl.BlockSpec((1, N_IDX_HEADS, tm, 4 * IDX_DIM), lambda b, i: (b, 0, i, 0)),
        pl.BlockSpec((1, tm, LANES), row),
        pl.BlockSpec((1, tm, D), row),
        pl.BlockSpec((1, tm, D), row),
    )
    in_specs = [
        pl.BlockSpec((1, tm, D), row),
        _resident((1, D), const2),
        _resident((D, W), const2),
        pl.BlockSpec((1, HALO // 2, POOL_WIDTH), lambda b, i: (b, 0, 0)),
        pl.BlockSpec((tm, LANES), tab), pl.BlockSpec((tm, LANES), tab),
        pl.BlockSpec((tm, LANES), tab), pl.BlockSpec((tm, LANES), tab),
        _resident((N_POOL_GROUPS, POOL_GROUP_WIDTH, D // N_POOL_GROUPS), lambda b, i: (0, 0, 0)),
        _resident((1, D), const2),
    ]
    if key_tile is not None:
        assert key_tile % tm == 0 and T % key_tile == 0 and tm % LANES == 0
        per = key_tile // tm
        n_tiles = T // key_tile
        sub = tm // LANES
        out_shape = out_shape[:4] + (
            jax.ShapeDtypeStruct((B, T // LANES, 4 * IDX_DIM, N_IDX_HEADS * LANES), BF16),
        ) + out_shape[5:] + (
            jax.ShapeDtypeStruct((B, N_KV_HEADS, n_tiles, HEAD_DIM, key_tile), BF16),
            jax.ShapeDtypeStruct((B, N_KV_HEADS, n_tiles, key_tile, HEAD_DIM), BF16),
            jax.ShapeDtypeStruct((B, n_tiles, key_tile, 4 * IDX_DIM), BF16),
            jax.ShapeDtypeStruct((B, T // LANES, N_IDX_HEADS, LANES), F32),
        )
        out_specs = out_specs[:4] + (
            pl.BlockSpec((1, sub, 4 * IDX_DIM, N_IDX_HEADS * LANES), lambda b, i: (b, i, 0, 0)),
        ) + out_specs[5:] + (
            pl.BlockSpec((1, N_KV_HEADS, 1, HEAD_DIM, tm), lambda b, i: (b, 0, i // per, 0, i % per)),
            pl.BlockSpec((1, N_KV_HEADS, 1, tm, HEAD_DIM), lambda b, i: (b, 0, i // per, i % per, 0)),
            pl.BlockSpec((1, 1, tm, 4 * IDX_DIM), lambda b, i: (b, i // per, i % per, 0)),
            pl.BlockSpec((1, sub, N_IDX_HEADS, LANES), lambda b, i: (b, i, 0, 0)),
        )
    scratch = [pltpu.VMEM((HALO + tm, POOL_WIDTH), F32) for _ in range(4)]
    return pl.pallas_call(
        functools.partial(_proj_kernel, tm=tm, pos0=pos0, d_model=D, key_tiles=key_tile is not None),
        grid=(B, T // tm), in_specs=in_specs, out_specs=out_specs, out_shape=out_shape,
        scratch_shapes=scratch, name="proj",
        compiler_params=pltpu.CompilerParams(dimension_semantics=("arbitrary", "arbitrary"),
                                             vmem_limit_bytes=VMEM_LIMIT),
    )(x, g_pre, w_in_r, hist16, cq, sq, ci, si, w_pool, pool_scale)


def _rope_tables(pos, dim):
    rot = dim // 4
    half = rot // 2
    inv = ROPE_THETA ** (-jnp.arange(half, dtype=F32) / half)
    ang = pos.astype(F32)[:, None] * inv[None, :]
    cos, sin = jnp.cos(ang), jnp.sin(ang)
    rest = dim - rot
    n = pos.shape[0]
    c = jnp.concatenate([cos, cos, jnp.ones((n, rest), F32)], axis=-1)
    s = jnp.concatenate([-sin, sin, jnp.zeros((n, rest), F32)], axis=-1)
    return jnp.tile(c, (1, LANES // dim)), jnp.tile(s, (1, LANES // dim))


def _split_kernel(k2_ref, out_ref):
    k2 = k2_ref[0]
    hi = k2.astype(BF16)
    out_ref[0, :, 0:LANES] = hi
    out_ref[0, :, LANES:2 * LANES] = (k2 - hi.astype(F32)).astype(BF16)


def _split_keys(ki2, ts):
    B, Lp, _ = ki2.shape
    return pl.pallas_call(
        _split_kernel, grid=(B, Lp // ts),
        in_specs=[pl.BlockSpec((1, ts, LANES), lambda b, t: (b, t, 0))],
        out_specs=pl.BlockSpec((1, ts, 2 * LANES), lambda b, t: (b, t, 0)),
        out_shape=jax.ShapeDtypeStruct((B, Lp, 2 * LANES), BF16), name="split_keys",
        compiler_params=pltpu.CompilerParams(dimension_semantics=("arbitrary", "arbitrary")),
    )(ki2)


def _key_of(x):
    b = pltpu.bitcast(x, I32)
    return b ^ ((b >> 31) & 0x7FFFFFFF)


def _float_of(k):
    return pltpu.bitcast(k ^ ((k >> 31) & 0x7FFFFFFF), F32)


def _row_sum(x):
    return jnp.broadcast_to(jnp.sum(x, axis=-1, keepdims=True), x.shape)


def _attn_kernel(qs_ref, kiw_ref, q_ref, ki_ref, kt_ref, v_ref, o_ref,
                 sc_ref, mm_ref, wb_ref, s_ref, m_ref, acc_ref, bias_ref, p_ref, alpha_ref,
                 *, tq, ts, q0, n_keys, k_sel):
    qi = pl.program_id(1)
    n_tiles_total = ki_ref.shape[1] // ts
    nc = ts // LANES
    nt_dims = (((1,), (1,)), ((), ()))

    def key_rows(t):
        return pl.ds(pl.multiple_of(t * ts, ts), ts)

    row = lax.broadcasted_iota(I32, (tq, LANES), 0)
    lane = lax.broadcasted_iota(I32, (tq, LANES), 1)
    qpos = q0 + qi * tq + row
    n_adm = jnp.minimum((qpos // CHUNK + 1) * CHUNK, n_keys)
    last_adm = jnp.minimum(((q0 + (qi + 1) * tq - 1) // CHUNK + 1) * CHUNK, n_keys)
    n_t = jnp.minimum((last_adm + ts - 1) // ts, n_tiles_total)

    kiw = kiw_ref[0]
    for hd in range(N_IDX_HEADS):
        wb_ref[hd] = jnp.broadcast_to(kiw[:, IDX_DIM + hd:IDX_DIM + hd + 1], (tq, LANES))
    qs2 = qs_ref[0].reshape(N_IDX_HEADS * tq, 4 * IDX_DIM)

    def idx_logits(t, slot):
        s_ref[slot] = lax.dot_general(qs2, ki_ref[0, key_rows(t), :], nt_dims,
                                      preferred_element_type=F32)

    rc = min(tq, ROW_CHUNK)
    mm_ref[0] = jnp.full((tq, LANES), _POS_INF, F32)
    mm_ref[1] = jnp.full((tq, LANES), _NEG_INF, F32)

    def score_tile(t, slot, masked):
        for r0 in range(0, tq, rc):
            rows = slice(r0, r0 + rc)
            mn, mx = mm_ref[0, rows], mm_ref[1, rows]
            for c in range(nc):
                cs = slice(c * LANES, (c + 1) * LANES)
                acc = None
                for hd in range(N_IDX_HEADS):
                    r = jnp.maximum(s_ref[slot, hd * tq + r0:hd * tq + r0 + rc, cs], 0.0) * wb_ref[hd, rows]
                    acc = r if acc is None else acc + r
                key = _key_of(acc)
                if masked:
                    qpos_c = q0 + qi * tq + r0 + lax.broadcasted_iota(I32, (rc, LANES), 0)
                    n_adm_c = jnp.minimum((qpos_c // CHUNK + 1) * CHUNK, n_keys)
                    adm = (t * ts + c * LANES + lax.broadcasted_iota(I32, (rc, LANES), 1)) < n_adm_c
                    key = jnp.where(adm, key, _INADMISSIBLE)
                    mx = jnp.maximum(mx, jnp.where(adm, acc, _NEG_INF))
                    mn = jnp.minimum(mn, jnp.where(adm, acc, _POS_INF))
                else:
                    mx = jnp.maximum(mx, acc)
                    mn = jnp.minimum(mn, acc)
                sc_ref[t, rows, cs] = key
            mm_ref[0, rows] = mn
            mm_ref[1, rows] = mx

    n_full = jnp.minimum(jnp.minimum(((q0 + qi * tq) // CHUNK + 1) * CHUNK, n_keys) // ts, n_t)
    last_full = jnp.maximum(n_full - 1, 0)
    idx_logits(0, 0)

    def score_pair(u, carry):
        t0 = 2 * u
        t1 = jnp.minimum(t0 + 1, last_full)
        idx_logits(t1, 1)
        score_tile(t0, 0, False)
        idx_logits(jnp.minimum(t0 + 2, last_full), 0)
        score_tile(t1, 1, False)
        return carry

    lax.fori_loop(0, (n_full + 1) // 2, score_pair, 0)

    def score_tail(t, carry):
        idx_logits(t, 0)
        score_tile(t, 0, True)
        return carry

    lax.fori_loop(n_full, n_t, score_tail, 0)
    rmin = jnp.broadcast_to(jnp.min(mm_ref[0], axis=-1, keepdims=True), (tq, LANES))
    rmax = jnp.broadcast_to(jnp.max(mm_ref[1], axis=-1, keepdims=True), (tq, LANES))

    def count_ge(thr):
        def body(t, cnt):
            for c in range(nc):
                cnt = cnt + jnp.where(sc_ref[t, :, c * LANES:(c + 1) * LANES] >= thr, 1.0, 0.0)
            return cnt
        return _row_sum(lax.fori_loop(0, n_t, body, jnp.zeros((tq, LANES), F32)))

    def bis_cond(st):
        return st[0] > 0

    kf = float(k_sel)
    take_all = n_adm <= k_sel
    lo0 = _key_of(rmin) - 1
    hi0 = _key_of(rmax) + 2
    c_pos = count_ge(jnp.full((tq, LANES), _KEY_MIN_NORMAL, I32))
    c_nn = count_ge(jnp.full((tq, LANES), _KEY_NEG_ZERO, I32))
    pos = c_pos >= kf
    neg = c_nn < kf
    lo = jnp.where(pos, _KEY_MIN_NORMAL, jnp.where(neg, lo0, _KEY_NEG_ZERO))
    hi = jnp.where(pos, hi0, jnp.where(neg, _KEY_NEG_ZERO, _KEY_MIN_NORMAL))
    clo = jnp.where(pos, c_pos, jnp.where(neg, n_adm.astype(F32), c_nn))
    chi = jnp.where(pos, 0.0, jnp.where(neg, c_nn, c_pos))
    exact0 = jnp.where(clo == kf, 1, 0)
    act0 = jnp.where(take_all, 0, jnp.where(exact0 > 0, 0, jnp.where(hi > lo + 1, 1, 0)))

    def bis_body(st):
        _, it, lo, hi, clo, chi, exact, act = st
        kmid = (lo & hi) + ((lo ^ hi) >> 1)
        vmid = _key_of((_float_of(lo) + _float_of(hi)) * 0.5)
        by_value = jnp.where(it < VALUE_PIVOT_STEPS, jnp.where(vmid > lo, jnp.where(vmid < hi, 1, 0), 0), 0)
        mid = jnp.where(by_value > 0, vmid, kmid)
        cnt = count_ge(mid)
        up = jnp.where(act > 0, jnp.where(cnt >= kf, 1, 0), 0)
        dn = act - up
        lo = jnp.where(up > 0, mid, lo)
        clo = jnp.where(up > 0, cnt, clo)
        hi = jnp.where(dn > 0, mid, hi)
        chi = jnp.where(dn > 0, cnt, chi)
        hit = jnp.where(up > 0, jnp.where(cnt == kf, 1, 0), 0)
        exact = exact + hit
        act = jnp.where(act > 0, jnp.where(hit > 0, 0, jnp.where(hi > lo + 1, 1, 0)), 0)
        return jnp.max(act), it + 1, lo, hi, clo, chi, exact, act

    _, _, lo, _, clo, chi, exact, _ = lax.while_loop(
        bis_cond, bis_body, (jnp.max(act0), jnp.int32(0), lo, hi, clo, chi, exact0, act0))
    thr = jnp.where(take_all, _INADMISSIBLE + 1, lo)
    thr_up = jnp.where(take_all, _INADMISSIBLE + 1, lo + 1)

    need = kf - chi
    tied = jnp.where(take_all, 0, jnp.where(exact > 0, 0, jnp.where(clo > kf, 1, 0)))

    def count_tied_below(cut):
        def body(t, cnt):
            for c in range(nc):
                s = sc_ref[t, :, c * LANES:(c + 1) * LANES]
                col = t * ts + c * LANES + lane
                cnt = cnt + jnp.where(s == thr, jnp.where(col < cut, 1.0, 0.0), 0.0)
            return cnt
        return _row_sum(lax.fori_loop(0, n_t, body, jnp.zeros((tq, LANES), F32)))

    def cut_body(st):
        _, lo_c, hi_c, act = st
        mid = (lo_c + hi_c) >> 1
        cnt = count_tied_below(mid)
        on = act > 0
        le = cnt <= need
        lo_c = jnp.where(on, jnp.where(le, mid, lo_c), lo_c)
        hi_c = jnp.where(on, jnp.where(le, hi_c, mid), hi_c)
        act = jnp.where(on, jnp.where(hi_c - lo_c > 1, 1, 0), 0)
        return jnp.max(act), lo_c, hi_c, act

    _, cut_lo, _, _ = lax.while_loop(
        bis_cond, cut_body,
        (jnp.max(tied), jnp.zeros((tq, LANES), I32), jnp.full((tq, LANES), n_tiles_total * ts + 1, I32), tied))
    cut = jnp.where(tied > 0, cut_lo, n_tiles_total * ts + 1)

    m_ref[...] = jnp.full(m_ref.shape, _M_INIT, F32)
    acc_ref[...] = jnp.zeros(acc_ref.shape, F32)
    rc = min(tq, ROW_CHUNK)
    ones = jnp.ones((ts, LANES), BF16)
    last = n_t - 1

    def logits(t, slot):
        for g in range(N_KV_HEADS):
            qg = q_ref[0, g * GQA_GROUP:(g + 1) * GQA_GROUP].reshape(GQA_GROUP * tq, HEAD_DIM)
            s_ref[slot, g * GQA_GROUP * tq:(g + 1) * GQA_GROUP * tq] = lax.dot_general(
                qg, kt_ref[0, key_rows(t), g * HEAD_DIM:(g + 1) * HEAD_DIM], nt_dims,
                preferred_element_type=F32)

    def softmax_pv(t, slot, live):
        for c in range(nc):
            col = t * ts + c * LANES + lane
            bound = jnp.where(live, jnp.where(col < cut, thr, thr_up), _NO_KEY)
            bias_ref[slot, :, c * LANES:(c + 1) * LANES] = jnp.where(
                sc_ref[t, :, c * LANES:(c + 1) * LANES] >= bound, 0.0, _NEG_INF)
        for g in range(N_KV_HEADS):
            for j in range(GQA_GROUP):
                hd = g * GQA_GROUP + j
                for r0 in range(0, tq, rc):
                    r1 = j * tq + r0
                    r2 = hd * tq + r0
                    s = [s_ref[slot, r2:r2 + rc, c * LANES:(c + 1) * LANES]
                         + bias_ref[slot, r0:r0 + rc, c * LANES:(c + 1) * LANES] for c in range(nc)]
                    mx = s[0]
                    for c in range(1, nc):
                        mx = jnp.maximum(mx, s[c])
                    m_old = m_ref[hd, r0:r0 + rc]
                    m_new = jnp.maximum(m_old, jnp.broadcast_to(jnp.max(mx, axis=-1, keepdims=True), (rc, LANES)))
                    alpha_ref[slot, g, r1:r1 + rc] = jnp.exp2(m_old - m_new)
                    for c in range(nc):
                        p_ref[slot, g, r1:r1 + rc, c * LANES:(c + 1) * LANES] = jnp.exp2(s[c] - m_new).astype(BF16)
                    m_ref[hd, r0:r0 + rc] = m_new
            v_aug = jnp.concatenate([v_ref[0, key_rows(t), g * HEAD_DIM:(g + 1) * HEAD_DIM], ones], axis=-1)
            pv = jnp.dot(p_ref[slot, g], v_aug, preferred_element_type=F32)
            alpha = alpha_ref[slot, g]
            for half in range(2):
                cols = slice(half * LANES, (half + 1) * LANES)
                acc_ref[g, :, cols] = alpha * acc_ref[g, :, cols] + pv[:, cols]

    logits(0, 0)

    def attn_pair(u, carry):
        t0 = 2 * u
        t1 = jnp.minimum(t0 + 1, last)
        logits(t1, 1)
        softmax_pv(t0, 0, True)
        logits(jnp.minimum(t0 + 2, last), 0)
        softmax_pv(t1, 1, t0 + 1 < n_t)
        return carry

    lax.fori_loop(0, (n_t + 1) // 2, attn_pair, 0)
    for hd in range(N_HEADS):
        g, j = divmod(hd, GQA_GROUP)
        o_ref[0, :, hd * HEAD_DIM:(hd + 1) * HEAD_DIM] = (
            acc_ref[g, j * tq:(j + 1) * tq, 0:HEAD_DIM] / acc_ref[g, j * tq:(j + 1) * tq, HEAD_DIM:2 * HEAD_DIM])


def _attn_t_kernel(qst_ref, wrow_ref, q_ref, ki_ref, kt_ref, v_ref, o_ref,
                   sc_ref, mm_ref, st_ref, s_ref, m_ref, acc_ref, bias_ref, p_ref, alpha_ref, kmax_ref,
                   *, tq, ts, q0, n_keys, k_sel):
    qi = pl.program_id(1)
    n_tiles_total = ki_ref.shape[1]
    nc = ts // LANES
    rk = KEY_ROWS
    qpos = q0 + qi * tq + lax.broadcasted_iota(I32, (1, tq), 1)
    n_adm = jnp.minimum((qpos // CHUNK + 1) * CHUNK, n_keys)
    last_adm = jnp.minimum(((q0 + (qi + 1) * tq - 1) // CHUNK + 1) * CHUNK, n_keys)
    n_t = jnp.minimum((last_adm + ts - 1) // ts, n_tiles_total)
    n_full = jnp.minimum(jnp.minimum(((q0 + qi * tq) // CHUNK + 1) * CHUNK, n_keys) // ts, n_t)
    last_full = jnp.maximum(n_full - 1, 0)
    qst = qst_ref[0, 0]
    wr = wrow_ref[0, 0]

    mm_ref[0] = jnp.full((rk, tq), _POS_INF, F32)
    mm_ref[1] = jnp.full((rk, tq), _NEG_INF, F32)

    def idx_logits(t, slot):
        st_ref[slot] = jnp.dot(ki_ref[0, t], qst, preferred_element_type=F32)

    def score_tile(t, slot, masked):
        mn, mx = mm_ref[0], mm_ref[1]
        for r0 in range(0, ts, rk):
            acc = None
            for hd in range(N_IDX_HEADS):
                r = jnp.maximum(st_ref[slot, r0:r0 + rk, hd * tq:(hd + 1) * tq], 0.0) * wr[hd:hd + 1, :]
                acc = r if acc is None else acc + r
            key = _key_of(acc)
            if masked:
                adm = (t * ts + r0 + lax.broadcasted_iota(I32, (rk, tq), 0)) < n_adm
                key = jnp.where(adm, key, _INADMISSIBLE)
                mx = jnp.maximum(mx, jnp.where(adm, acc, _NEG_INF))
                mn = jnp.minimum(mn, jnp.where(adm, acc, _POS_INF))
            else:
                mx = jnp.maximum(mx, acc)
                mn = jnp.minimum(mn, acc)
            sc_ref[t, r0:r0 + rk, :] = key
        mm_ref[0] = mn
        mm_ref[1] = mx

    idx_logits(0, 0)

    def score_pair(u, carry):
        t0 = 2 * u
        t1 = jnp.minimum(t0 + 1, last_full)
        idx_logits(t1, 1)
        score_tile(t0, 0, False)
        idx_logits(jnp.minimum(t0 + 2, last_full), 0)
        score_tile(t1, 1, False)
        return carry

    lax.fori_loop(0, (n_full + 1) // 2, score_pair, 0)

    def score_tail(t, carry):
        idx_logits(t, 0)
        score_tile(t, 0, True)
        return carry

    lax.fori_loop(n_full, n_t, score_tail, 0)
    rmin = jnp.min(mm_ref[0], axis=0, keepdims=True)
    rmax = jnp.max(mm_ref[1], axis=0, keepdims=True)

    def count_ge(thr):
        def body(t, cnt):
            for r0 in range(0, ts, rk):
                cnt = cnt + jnp.where(sc_ref[t, r0:r0 + rk, :] >= thr, 1.0, 0.0)
            return cnt
        return jnp.sum(lax.fori_loop(0, n_t, body, jnp.zeros((rk, tq), F32)), axis=0, keepdims=True)

    def bis_cond(st):
        return st[0] > 0

    kf = float(k_sel)
    take_all = n_adm <= k_sel
    lo0 = _key_of(rmin) - 1
    hi0 = _key_of(rmax) + 2
    c_pos = count_ge(jnp.full((1, tq), _KEY_MIN_NORMAL, I32))
    c_nn = count_ge(jnp.full((1, tq), _KEY_NEG_ZERO, I32))
    pos = c_pos >= kf
    neg = c_nn < kf
    lo = jnp.where(pos, _KEY_MIN_NORMAL, jnp.where(neg, lo0, _KEY_NEG_ZERO))
    hi = jnp.where(pos, hi0, jnp.where(neg, _KEY_NEG_ZERO, _KEY_MIN_NORMAL))
    clo = jnp.where(pos, c_pos, jnp.where(neg, n_adm.astype(F32), c_nn))
    chi = jnp.where(pos, 0.0, jnp.where(neg, c_nn, c_pos))
    exact0 = jnp.where(clo == kf, 1, 0)
    act0 = jnp.where(take_all, 0, jnp.where(exact0 > 0, 0, jnp.where(hi > lo + 1, 1, 0)))

    def bis_step(it, lo, hi, clo, chi, exact, act):
        kmid = (lo & hi) + ((lo ^ hi) >> 1)
        vmid = _key_of((_float_of(lo) + _float_of(hi)) * 0.5)
        by_value = jnp.where(it < VALUE_PIVOT_STEPS, jnp.where(vmid > lo, jnp.where(vmid < hi, 1, 0), 0), 0)
        mid = jnp.where(by_value > 0, vmid, kmid)
        cnt = count_ge(mid)
        up = jnp.where(act > 0, jnp.where(cnt >= kf, 1, 0), 0)
        dn = act - up
        lo = jnp.where(up > 0, mid, lo)
        clo = jnp.where(up > 0, cnt, clo)
        hi = jnp.where(dn > 0, mid, hi)
        chi = jnp.where(dn > 0, cnt, chi)
        hit = jnp.where(up > 0, jnp.where(cnt == kf, 1, 0), 0)
        exact = exact + hit
        act = jnp.where(act > 0, jnp.where(hit > 0, 0, jnp.where(hi > lo + 1, 1, 0)), 0)
        return it + 1, lo, hi, clo, chi, exact, act

    def any_active(act):
        return jnp.max(act.astype(F32))

    def bis_body(st):
        st = bis_step(*bis_step(*st[1:]))
        return (any_active(st[-1]),) + st

    _, _, lo, _, clo, chi, exact, _ = lax.while_loop(
        bis_cond, bis_body, (any_active(act0), jnp.int32(0), lo, hi, clo, chi, exact0, act0))
    thr = jnp.where(take_all, _INADMISSIBLE + 1, lo)
    thr_up = jnp.where(take_all, _INADMISSIBLE + 1, lo + 1)

    need = kf - chi
    tied = jnp.where(take_all, 0, jnp.where(exact > 0, 0, jnp.where(clo > kf, 1, 0)))

    def count_tied_below(cut):
        def body(t, cnt):
            for r0 in range(0, ts, rk):
                kidx = t * ts + r0 + lax.broadcasted_iota(I32, (rk, tq), 0)
                cnt = cnt + jnp.where(sc_ref[t, r0:r0 + rk, :] == thr, jnp.where(kidx < cut, 1.0, 0.0), 0.0)
            return cnt
        return jnp.sum(lax.fori_loop(0, n_t, body, jnp.zeros((rk, tq), F32)), axis=0, keepdims=True)

    def cut_body(st):
        _, lo_c, hi_c, act = st
        mid = (lo_c + hi_c) >> 1
        cnt = count_tied_below(mid)
        on = act > 0
        le = cnt <= need
        lo_c = jnp.where(on, jnp.where(le, mid, lo_c), lo_c)
        hi_c = jnp.where(on, jnp.where(le, hi_c, mid), hi_c)
        act = jnp.where(on, jnp.where(hi_c - lo_c > 1, 1, 0), 0)
        return jnp.max(act), lo_c, hi_c, act

    _, cut_lo, _, _ = lax.while_loop(
        bis_cond, cut_body,
        (jnp.max(tied), jnp.zeros((1, tq), I32), jnp.full((1, tq), n_tiles_total * ts + 1, I32), tied))
    cut = jnp.where(tied > 0, cut_lo, n_tiles_total * ts + 1)

    acc_ref[...] = jnp.zeros(acc_ref.shape, F32)
    rc = min(tq, ROW_CHUNK)
    ones = jnp.ones((ts, LANES), BF16)
    last = n_t - 1

    def logits(t, slot):
        for g in range(N_KV_HEADS):
            qg = q_ref[0, g * GQA_GROUP:(g + 1) * GQA_GROUP].reshape(GQA_GROUP * tq, HEAD_DIM)
            s_ref[slot, g * GQA_GROUP * tq:(g + 1) * GQA_GROUP * tq] = jnp.dot(
                qg, kt_ref[0, g, t], preferred_element_type=F32)

    def softmax_pv(t, slot, live, fixed):
        for c in range(nc):
            kidx = t * ts + c * LANES + lax.broadcasted_iota(I32, (LANES, tq), 0)
            bound = jnp.where(live, jnp.where(kidx < cut, thr, thr_up), _NO_KEY)
            bias_t = jnp.where(sc_ref[t, c * LANES:(c + 1) * LANES, :] >= bound, 0.0, _NEG_INF)
            bias_ref[slot, :, c * LANES:(c + 1) * LANES] = bias_t.T
        for g in range(N_KV_HEADS):
            for j in range(GQA_GROUP):
                hd = g * GQA_GROUP + j
                for r0 in range(0, tq, rc):
                    r1 = j * tq + r0
                    r2 = hd * tq + r0
                    s = [s_ref[slot, r2:r2 + rc, c * LANES:(c + 1) * LANES]
                         + bias_ref[slot, r0:r0 + rc, c * LANES:(c + 1) * LANES] for c in range(nc)]
                    if fixed:
                        m_new = m_ref[hd, r0:r0 + rc]
                    else:
                        mx = s[0]
                        for c in range(1, nc):
                            mx = jnp.maximum(mx, s[c])
                        m_old = m_ref[hd, r0:r0 + rc]
                        m_new = jnp.maximum(
                            m_old, jnp.broadcast_to(jnp.max(mx, axis=-1, keepdims=True), (rc, LANES)))
                        alpha_ref[slot, g, r1:r1 + rc] = jnp.exp2(m_old - m_new)
                        m_ref[hd, r0:r0 + rc] = m_new
                    for c in range(nc):
                        p_ref[slot, g, r1:r1 + rc, c * LANES:(c + 1) * LANES] = jnp.exp2(s[c] - m_new).astype(BF16)
            v_aug = jnp.concatenate([v_ref[0, g, t], ones], axis=-1)
            pv = jnp.dot(p_ref[slot, g], v_aug, preferred_element_type=F32)
            for half in range(2):
                cols = slice(half * LANES, (half + 1) * LANES)
                if fixed:
                    acc_ref[g, :, cols] = acc_ref[g, :, cols] + pv[:, cols]
                else:
                    acc_ref[g, :, cols] = alpha_ref[slot, g] * acc_ref[g, :, cols] + pv[:, cols]

    def attend(fixed):
        logits(0, 0)

        def attn_pair(u, carry):
            t0 = 2 * u
            t1 = jnp.minimum(t0 + 1, last)
            logits(t1, 1)
            softmax_pv(t0, 0, True, fixed)
            logits(jnp.minimum(t0 + 2, last), 0)
            softmax_pv(t1, 1, t0 + 1 < n_t, fixed)
            return carry

        lax.fori_loop(0, (n_t + 1) // 2, attn_pair, 0)

    @pl.when(qi == 0)
    def _():
        for g in range(N_KV_HEADS):
            def sq_norm_max(t, best):
                kk = kt_ref[0, g, t].astype(F32)
                return jnp.maximum(best, jnp.sum(kk * kk, axis=0, keepdims=True))
            best = lax.fori_loop(0, n_tiles_total, sq_norm_max, jnp.zeros((1, ts), F32))
            kmax_ref[g] = jnp.broadcast_to(jnp.max(best, axis=-1, keepdims=True), (8, LANES))

    top = jnp.zeros((tq, LANES), F32)
    for hd in range(N_HEADS):
        qf = q_ref[0, hd].astype(F32)
        q_sq = _row_sum(qf * qf)
        ref_pt = jnp.sqrt(q_sq * kmax_ref[hd // GQA_GROUP, 0:1, :]) * 1.001 + 1e-3
        m_ref[hd] = ref_pt
        top = jnp.maximum(top, ref_pt)
    fits = jnp.max(top) <= MAX_FIXED_REFERENCE

    @pl.when(fits)
    def _():
        attend(True)

    @pl.when(jnp.logical_not(fits))
    def _():
        m_ref[...] = jnp.full(m_ref.shape, _M_INIT, F32)
        attend(False)

    for hd in range(N_HEADS):
        g, j = divmod(hd, GQA_GROUP)
        o_ref[0, :, hd * HEAD_DIM:(hd + 1) * HEAD_DIM] = (
            acc_ref[g, j * tq:(j + 1) * tq, 0:HEAD_DIM] / acc_ref[g, j * tq:(j + 1) * tq, HEAD_DIM:2 * HEAD_DIM])


def _attend_t(qst, wrow, q_att, ki4, kt, v, q0, n_keys, tq, ts):
    B, _, T, _ = q_att.shape
    n_tiles = ki4.shape[1]
    k_sel = min(TOPK_MAX, n_keys // 4)
    kern = functools.partial(_attn_t_kernel, tq=tq, ts=ts, q0=q0, n_keys=n_keys, k_sel=k_sel)
    return pl.pallas_call(
        kern, grid=(B, T // tq),
        in_specs=[
            pl.BlockSpec((1, 1, 4 * IDX_DIM, N_IDX_HEADS * tq), lambda b, i: (b, i, 0, 0)),
            pl.BlockSpec((1, 1, N_IDX_HEADS, tq), lambda b, i: (b, i, 0, 0)),
            pl.BlockSpec((1, N_HEADS, tq, HEAD_DIM), lambda b, i: (b, 0, i, 0)),
            _resident((1, n_tiles, ts, 4 * IDX_DIM), lambda b, i: (b, 0, 0, 0)),
            _resident((1, N_KV_HEADS, n_tiles, HEAD_DIM, ts), lambda b, i: (b, 0, 0, 0, 0)),
            _resident((1, N_KV_HEADS, n_tiles, ts, HEAD_DIM), lambda b, i: (b, 0, 0, 0, 0)),
        ],
        out_specs=pl.BlockSpec((1, tq, N_HEADS * HEAD_DIM), lambda b, i: (b, i, 0)),
        out_shape=jax.ShapeDtypeStruct((B, T, N_HEADS * HEAD_DIM), F32),
        scratch_shapes=[
            pltpu.VMEM((n_tiles, ts, tq), I32),
            pltpu.VMEM((2, KEY_ROWS, tq), F32),
            pltpu.VMEM((2, ts, N_IDX_HEADS * tq), F32),
            pltpu.VMEM((2, N_HEADS * tq, ts), F32),
            pltpu.VMEM((N_HEADS, tq, LANES), F32),
            pltpu.VMEM((N_KV_HEADS, GQA_GROUP * tq, 2 * HEAD_DIM), F32),
            pltpu.VMEM((2, tq, ts), F32),
            pltpu.VMEM((2, N_KV_HEADS, GQA_GROUP * tq, ts), BF16),
            pltpu.VMEM((2, N_KV_HEADS, GQA_GROUP * tq, LANES), F32),
            pltpu.VMEM((N_KV_HEADS, 8, LANES), F32),
        ],
        name="attend_t",
        compiler_params=pltpu.CompilerParams(dimension_semantics=("arbitrary", "arbitrary"),
                                             vmem_limit_bytes=VMEM_LIMIT),
    )(qst, wrow, q_att, ki4, kt, v)


def _attend(qs, kiw, q_att, ki4, kt, v, q0, n_keys, tq, ts):
    B, _, T, _ = qs.shape
    lp = ki4.shape[1]
    n_tiles = lp // ts
    k_sel = min(TOPK_MAX, n_keys // 4)
    kern = functools.partial(_attn_kernel, tq=tq, ts=ts, q0=q0, n_keys=n_keys, k_sel=k_sel)
    keys = pl.BlockSpec((1, lp, 2 * LANES), lambda b, i: (b, 0, 0))
    return pl.pallas_call(
        kern, grid=(B, T // tq),
        in_specs=[
            pl.BlockSpec((1, N_IDX_HEADS, tq, 4 * IDX_DIM), lambda b, i: (b, 0, i, 0)),
            pl.BlockSpec((1, tq, LANES), lambda b, i: (b, i, 0)),
            pl.BlockSpec((1, N_HEADS, tq, HEAD_DIM), lambda b, i: (b, 0, i, 0)),
            keys, keys, keys,
        ],
        out_specs=pl.BlockSpec((1, tq, N_HEADS * HEAD_DIM), lambda b, i: (b, i, 0)),
        out_shape=jax.ShapeDtypeStruct((B, T, N_HEADS * HEAD_DIM), F32),
        scratch_shapes=[
            pltpu.VMEM((n_tiles, tq, ts), I32),
            pltpu.VMEM((2, tq, LANES), F32),
            pltpu.VMEM((N_IDX_HEADS, tq, LANES), F32),
            pltpu.VMEM((2, N_HEADS * tq, ts), F32),
            pltpu.VMEM((N_HEADS, tq, LANES), F32),
            pltpu.VMEM((N_KV_HEADS, GQA_GROUP * tq, 2 * HEAD_DIM), F32),
            pltpu.VMEM((2, tq, ts), F32),
            pltpu.VMEM((2, N_KV_HEADS, GQA_GROUP * tq, ts), BF16),
            pltpu.VMEM((2, N_KV_HEADS, GQA_GROUP * tq, LANES), F32),
        ],
        name="attend",
        compiler_params=pltpu.CompilerParams(dimension_semantics=("arbitrary", "arbitrary"),
                                             vmem_limit_bytes=VMEM_LIMIT),
    )(qs, kiw, q_att, ki4, kt, v)


def _rms(x, g):
    return x * lax.rsqrt(jnp.mean(x * x, axis=-1, keepdims=True) + RMS_EPS) * g


def _out_kernel(x_ref, pa_ref, sgb_ref, b_ref, wo_ref, g1_ref, g2_ref, wgu_ref, wd_ref, g3_ref, y_ref, *, d_ff):
    merged = pa_ref[...] + sgb_ref[...] * b_ref[...]
    mix = jnp.dot(merged.astype(BF16), wo_ref[...], preferred_element_type=F32)
    x1 = x_ref[...] + _rms(mix, g1_ref[...])
    h2 = _rms(x1, g2_ref[...]).astype(BF16)
    gu = jnp.dot(h2, wgu_ref[...], preferred_element_type=F32)
    gate, up = gu[:, :d_ff], gu[:, d_ff:]
    act = (gate * _sigmoid(gate) * up).astype(BF16)
    f = jnp.dot(act, wd_ref[...], preferred_element_type=F32)
    y_ref[...] = x1 + _rms(f, g3_ref[...])


def _finish(x, pa, sgb, b, w_out, g_post, g_ffn_pre, w_gate_up, w_down, g_ffn_post, tm):
    N, D = x.shape
    d_ff = w_down.shape[0]
    row = pl.BlockSpec((tm, D), lambda i: (i, 0))
    const = lambda i: (0, 0)
    return pl.pallas_call(
        functools.partial(_out_kernel, d_ff=d_ff), grid=(N // tm,),
        in_specs=[row, row, row, row,
                  _resident((D, D), const), _resident((1, D), const), _resident((1, D), const),
                  _resident((D, 2 * d_ff), const), _resident((d_ff, D), const), _resident((1, D), const)],
        out_specs=row, out_shape=jax.ShapeDtypeStruct((N, D), F32), name="finish",
        compiler_params=pltpu.CompilerParams(dimension_semantics=("arbitrary",), vmem_limit_bytes=VMEM_LIMIT),
    )(x, pa, sgb, b, w_out, g_post, g_ffn_pre, w_gate_up, w_down, g_ffn_post)


def _tile_keys(n_keys, ts):
    return -(-n_keys // ts) * ts


def _layer(x, pos0, hist, k_cache, v_cache, ki_cache, wts, tm, tq, ts, tm_out):
    (w_in_r, w_pool, pool_scale, w_out, w_gate_up, w_down, g_pre, g_post, g_ffn_pre, g_ffn_post) = wts
    B, T, D = x.shape
    hist16 = jnp.concatenate([jnp.zeros((B, HALO // 2 - POOL_HIST, POOL_WIDTH), F32), hist], axis=1)
    own_keys_only = k_cache is None and T % ts == 0 and ts % tm == 0 and tq == LANES and tm % LANES == 0
    outs = _project(x, hist16, pos0, w_in_r, g_pre, w_pool, pool_scale, tm, ts if own_keys_only else None)
    u, q_att, k, v, qs, kiw, pa, sgb = outs[:8]
    ki = kiw[:, :, :IDX_DIM]
    if own_keys_only:
        kt, vt, ki4, wrow = outs[8:]
        b = _attend_t(qs, wrow, q_att, ki4, kt, vt, pos0, T, tq, ts)
    else:
        if k_cache is None:
            k_all, v_all, ki_all = k, v, ki
        else:
            k_all = jnp.concatenate([k_cache.reshape(B, -1, N_KV_HEADS * HEAD_DIM), k], axis=1)
            v_all = jnp.concatenate([v_cache.reshape(B, -1, N_KV_HEADS * HEAD_DIM), v], axis=1)
            ki_all = jnp.concatenate([ki_cache, ki], axis=1)
        n_keys = k_all.shape[1]
        lp = _tile_keys(n_keys, ts)
        pad = ((0, 0), (0, lp - n_keys), (0, 0))
        kt = jnp.pad(k_all.astype(BF16), pad)
        vt = jnp.pad(v_all.astype(BF16), pad)
        ki4 = _split_keys(jnp.pad(jnp.concatenate([ki_all, ki_all], axis=-1), pad), ts)
        b = _attend(qs, kiw, q_att, ki4, kt, vt, pos0, n_keys, tq, ts)
    y = _finish(x.reshape(B * T, D), pa.reshape(B * T, D), sgb.reshape(B * T, D), b.reshape(B * T, D),
                w_out, g_post, g_ffn_pre, w_gate_up, w_down, g_ffn_post, tm_out).reshape(B, T, D)
    new_pool = jnp.concatenate([hist, u], axis=1)[:, T:]
    return (y, k.reshape(B, T, N_KV_HEADS, HEAD_DIM), v.reshape(B, T, N_KV_HEADS, HEAD_DIM), ki, new_pool)


def _relayout_w_in(w_in):
    d = w_in.shape[0]
    o_kiw = POOL_WIDTH + N_HEADS * HEAD_DIM + 2 * N_KV_HEADS * HEAD_DIM + N_IDX_HEADS * IDX_DIM
    narrow = IDX_DIM + N_IDX_HEADS
    padded = jnp.concatenate([w_in[:, :o_kiw + narrow], jnp.zeros((d, LANES - narrow), w_in.dtype),
                              w_in[:, o_kiw + narrow:]], axis=1)
    return padded.astype(BF16)


def kernel(x_prompt, x_sample, cache_k, cache_v, cache_k_idx, state_pool, w_in, w_pool, pool_scale, w_out,
           w_gate_up, w_down, norm_mix_pre, norm_mix_post, norm_ffn_pre, norm_ffn_post):
    depth = w_in.shape[0]
    past = cache_k.shape[2]
    t_p, t_s = x_prompt.shape[1], x_sample.shape[1]
    hist_p = jnp.zeros((x_prompt.shape[0], POOL_HIST, POOL_WIDTH), x_prompt.dtype)
    xp, xs = x_prompt, x_sample
    outs = [[] for _ in range(8)]
    for l in range(depth):
        wts = (_relayout_w_in(w_in[l]), w_pool[l].astype(BF16), pool_scale[l][None, :], w_out[l].astype(BF16),
               w_gate_up[l].astype(BF16), w_down[l].astype(BF16), norm_mix_pre[l][None, :],
               norm_mix_post[l][None, :], norm_ffn_pre[l][None, :], norm_ffn_post[l][None, :])
        tm_p = min(256, t_p)
        tq_p = min(128, t_p)
        xp, k1, v1, ki1, p1 = _layer(xp, 0, hist_p, None, None, None, wts, tm_p, tq_p, 512, tm_p)
        n_s = xs.shape[0] * t_s
        xs, k2, v2, ki2, p2 = _layer(xs, past, state_pool[l], cache_k[l], cache_v[l], cache_k_idx[l], wts,
                                     t_s, t_s, 512, min(256, n_s))
        for lst, val in zip(outs, (k1, v1, ki1, p1, k2, v2, ki2, p2)):
            lst.append(val)
    return (xp, xs) + tuple(jnp.stack(o) for o in outs)
```

```python
import functools

import jax
import jax.numpy as jnp
from jax import lax
from jax.experimental import pallas as pl
from jax.experimental.pallas import tpu as pltpu

F32 = jnp.float32
BF16 = jnp.bfloat16
I32 = jnp.int32
I16 = jnp.int16

LANES = 128
CHUNK = 64
POOL_WINDOWS = (2, 4, 8, 16)
N_POOL_GROUPS = 4
POOL_GROUP_WIDTH = 128
POOL_WIDTH = N_POOL_GROUPS * POOL_GROUP_WIDTH
POOL_HIST = 15
N_HEADS = 8
N_KV_HEADS = 2
HEAD_DIM = 128
GQA_GROUP = N_HEADS // N_KV_HEADS
ROPE_THETA = 500000.0
N_IDX_HEADS = 8
IDX_DIM = 64
TOPK_MAX = 256
RMS_EPS = 1e-6
ATTN_SCALE = HEAD_DIM ** -0.5
IDX_SCALE = (N_IDX_HEADS ** -0.5) * (IDX_DIM ** -0.5)

LOG2_E = 1.4426950408889634
VALUE_PIVOT_STEPS = 28
MAX_FIXED_REFERENCE = 60.0
SPLIT_ROWS = 1536
KEY_ROWS = 64
ROW_CHUNK = 32
HALO = 32
VMEM_LIMIT = 60 * 1024 * 1024

_NEG_INF = float("-inf")
_POS_INF = float("inf")
_INADMISSIBLE = -(2 ** 31)
_NO_KEY = 2 ** 31 - 1
_KEY_MIN_NORMAL = 0x00800000
_KEY_NEG_ZERO = -1
_HIGH_MIN_NORMAL = _KEY_MIN_NORMAL >> 16
_M_INIT = -1e30


def _resident(block_shape, index_map):
    return pl.BlockSpec(block_shape, index_map, pipeline_mode=pl.Buffered(1))


def _rope(xs, cos, sin, half, period):
    lane = lax.broadcasted_iota(I32, xs.shape, 1)
    ahead = pltpu.roll(xs, LANES - half, 1)
    behind = pltpu.roll(xs, half, 1)
    partner = jnp.where((lane & (period - 1)) < half, ahead, behind)
    return xs * cos + partner * sin


def _sigmoid(x):
    return 1.0 / (1.0 + jnp.exp(-x))


def _proj_kernel(x_ref, g_ref, w_ref, hist_ref, cq_ref, sq_ref, ci_ref, si_ref, wpool_ref, pscale_ref,
                 u_ref, qatt_ref, k_ref, v_ref, qs_ref, kiw_ref, pa_ref, sgb_ref, *rest, tm, pos0, d_model, key_tiles):
    if key_tiles:
        kt_ref, vt_ref, ki4_ref, wrow_ref, e_ref, s2_ref, s4_ref, s8_ref = rest
    else:
        e_ref, s2_ref, s4_ref, s8_ref = rest
    i = pl.program_id(1)
    x = x_ref[0]
    h = x * lax.rsqrt(jnp.mean(x * x, axis=-1, keepdims=True) + RMS_EPS) * g_ref[...]
    proj = jnp.dot(h.astype(BF16), w_ref[...], preferred_element_type=F32)

    o_q = POOL_WIDTH
    o_k = o_q + N_HEADS * HEAD_DIM
    o_v = o_k + N_KV_HEADS * HEAD_DIM
    o_qi = o_v + N_KV_HEADS * HEAD_DIM
    o_kiw = o_qi + N_IDX_HEADS * IDX_DIM
    o_ga = o_kiw + LANES
    o_gb = o_ga + d_model

    cq, sq, ci, si = cq_ref[...], sq_ref[...], ci_ref[...], si_ref[...]
    half_q = HEAD_DIM // 8
    half_i = IDX_DIM // 8

    for hd in range(N_HEADS):
        qh = _rope(proj[:, o_q + hd * HEAD_DIM:o_q + (hd + 1) * HEAD_DIM], cq, sq, half_q, HEAD_DIM)
        qatt_ref[0, hd] = (qh * (ATTN_SCALE * LOG2_E)).astype(BF16)
    for kh in range(N_KV_HEADS):
        k_h = _rope(proj[:, o_k + kh * HEAD_DIM:o_k + (kh + 1) * HEAD_DIM], cq, sq, half_q, HEAD_DIM)
        k_ref[0, :, kh * HEAD_DIM:(kh + 1) * HEAD_DIM] = k_h
        if key_tiles:
            kt_ref[0, kh, 0] = k_h.T.astype(BF16)
            vt_ref[0, kh, 0] = proj[:, o_v + kh * HEAD_DIM:o_v + (kh + 1) * HEAD_DIM].astype(BF16)
    v_ref[0] = proj[:, o_v:o_qi]

    for pr in range(N_IDX_HEADS // 2):
        qi2 = _rope(proj[:, o_qi + pr * LANES:o_qi + (pr + 1) * LANES], ci, si, half_i, IDX_DIM)
        hi = qi2.astype(BF16).astype(F32)
        lo = qi2 - hi
        lane2 = lax.broadcasted_iota(I32, qi2.shape, 1)
        first = jnp.where(lane2 < IDX_DIM, hi, pltpu.roll(lo, IDX_DIM, 1))
        second = jnp.where(lane2 < IDX_DIM, pltpu.roll(hi, IDX_DIM, 1), lo)
        for sub, slab in enumerate((first, second)):
            hd = 2 * pr + sub
            if key_tiles:
                for sb in range(tm // LANES):
                    slab_t = slab[sb * LANES:(sb + 1) * LANES].T.astype(BF16)
                    qs_ref[0, sb, 0:LANES, hd * LANES:(hd + 1) * LANES] = slab_t
                    qs_ref[0, sb, LANES:2 * LANES, hd * LANES:(hd + 1) * LANES] = slab_t
            else:
                qs_ref[0, hd, :, 0:LANES] = slab.astype(BF16)
                qs_ref[0, hd, :, LANES:2 * LANES] = slab.astype(BF16)

    kiw = _rope(proj[:, o_kiw:o_kiw + LANES], ci, si, half_i, IDX_DIM)
    lane = lax.broadcasted_iota(I32, kiw.shape, 1)
    kiw_full = jnp.where(lane < IDX_DIM, kiw, proj[:, o_kiw:o_kiw + LANES] * IDX_SCALE)
    kiw_ref[0] = kiw_full
    if key_tiles:
        ki_hi = kiw.astype(BF16).astype(F32)
        ki_lo = kiw - ki_hi
        ki4_ref[0, 0, :, 0:LANES] = jnp.where(lane < IDX_DIM, ki_hi, pltpu.roll(ki_hi, IDX_DIM, 1)).astype(BF16)
        ki4_ref[0, 0, :, LANES:2 * LANES] = jnp.where(lane < IDX_DIM, ki_lo, pltpu.roll(ki_lo, IDX_DIM, 1)).astype(BF16)
        for sb in range(tm // LANES):
            wrow_ref[0, sb] = kiw_full[sb * LANES:(sb + 1) * LANES].T[IDX_DIM:IDX_DIM + N_IDX_HEADS]

    u = proj[:, 0:POOL_WIDTH]
    u_ref[0] = u

    @pl.when(i == 0)
    def _():
        e_ref[0:HALO // 2, :] = jnp.zeros((HALO // 2, POOL_WIDTH), F32)
        e_ref[HALO // 2:HALO, :] = hist_ref[0]

    e_ref[HALO:HALO + tm, :] = u
    n2, n4, n8 = tm + 24, tm + 16, tm + 8
    s2_ref[8:8 + n2, :] = e_ref[8:8 + n2, :] + e_ref[7:7 + n2, :]
    s4_ref[16:16 + n4, :] = s2_ref[16:16 + n4, :] + s2_ref[14:14 + n4, :]
    s8_ref[24:24 + n8, :] = s4_ref[24:24 + n8, :] + s4_ref[20:20 + n8, :]
    s16 = s8_ref[HALO:HALO + tm, :] + s8_ref[HALO - 8:HALO - 8 + tm, :]
    wins = (s2_ref[HALO:HALO + tm, :], s4_ref[HALO:HALO + tm, :], s8_ref[HALO:HALO + tm, :], s16)
    e_ref[HALO // 2:HALO, :] = e_ref[HALO // 2 + tm:HALO + tm, :]

    pos = pos0 + i * tm + lax.broadcasted_iota(I32, (tm, POOL_GROUP_WIDTH), 0)
    a_parts = []
    for g, w in enumerate(POOL_WINDOWS):
        lo_l, hi_l = g * POOL_GROUP_WIDTH, (g + 1) * POOL_GROUP_WIDTH
        cnt = jnp.minimum(pos + 1, w).astype(F32)
        pooled = wins[g][:, lo_l:hi_l] / cnt - u[:, lo_l:hi_l]
        a_parts.append(jnp.dot(pooled.astype(BF16), wpool_ref[g], preferred_element_type=F32))
    a = jnp.concatenate(a_parts, axis=-1) * pscale_ref[...]
    pa_ref[0] = _sigmoid(proj[:, o_ga:o_gb]) * a
    sgb_ref[0] = _sigmoid(proj[:, o_gb:o_gb + d_model])


def _project(x, hist16, pos0, w_in_r, g_pre, w_pool, pool_scale, tm, key_tile=None):
    B, T, D = x.shape
    W = w_in_r.shape[1]
    pos = pos0 + jnp.arange(T, dtype=I32)
    cq, sq = _rope_tables(pos, HEAD_DIM)
    ci, si = _rope_tables(pos, IDX_DIM)
    row = lambda b, i: (b, i, 0)
    tab = lambda b, i: (i, 0)
    const2 = lambda b, i: (0, 0)
    out_shape = (
        jax.ShapeDtypeStruct((B, T, POOL_WIDTH), F32),
        jax.ShapeDtypeStruct((B, N_HEADS, T, HEAD_DIM), BF16),
        jax.ShapeDtypeStruct((B, T, N_KV_HEADS * HEAD_DIM), F32),
        jax.ShapeDtypeStruct((B, T, N_KV_HEADS * HEAD_DIM), F32),
        jax.ShapeDtypeStruct((B, N_IDX_HEADS, T, 4 * IDX_DIM), BF16),
        jax.ShapeDtypeStruct((B, T, LANES), F32),
        jax.ShapeDtypeStruct((B, T, D), F32),
        jax.ShapeDtypeStruct((B, T, D), F32),
    )
    out_specs = (
        pl.BlockSpec((1, tm, POOL_WIDTH), row),
        pl.BlockSpec((1, N_HEADS, tm, HEAD_DIM), lambda b, i: (b, 0, i, 0)),
        pl.BlockSpec((1, tm, N_KV_HEADS * HEAD_DIM), row),
        pl.BlockSpec((1, tm, N_KV_HEADS * HEAD_DIM), row),
        pl.BlockSpec((1, N_IDX_HEADS, tm, 4 * IDX_DIM), lambda b, i: (b, 0, i, 0)),
        pl.BlockSpec((1, tm, LANES), row),
        pl.BlockSpec((1, tm, D), row),
        pl.BlockSpec((1, tm, D), row),
    )
    in_specs = [
        pl.BlockSpec((1, tm, D), row),
        _resident((1, D), const2),
        _resident((D, W), const2),
        pl.BlockSpec((1, HALO // 2, POOL_WIDTH), lambda b, i: (b, 0, 0)),
        pl.BlockSpec((tm, LANES), tab), pl.BlockSpec((tm, LANES), tab),
        pl.BlockSpec((tm, LANES), tab), pl.BlockSpec((tm, LANES), tab),
        _resident((N_POOL_GROUPS, POOL_GROUP_WIDTH, D // N_POOL_GROUPS), lambda b, i: (0, 0, 0)),
        _resident((1, D), const2),
    ]
    if key_tile is not None:
        assert key_tile % tm == 0 and T % key_tile == 0 and tm % LANES == 0
        per = key_tile // tm
        n_tiles = T // key_tile
        sub = tm // LANES
        out_shape = out_shape[:4] + (
            jax.ShapeDtypeStruct((B, T // LANES, 4 * IDX_DIM, N_IDX_HEADS * LANES), BF16),
        ) + out_shape[5:] + (
            jax.ShapeDtypeStruct((B, N_KV_HEADS, n_tiles, HEAD_DIM, key_tile), BF16),
            jax.ShapeDtypeStruct((B, N_KV_HEADS, n_tiles, key_tile, HEAD_DIM), BF16),
            jax.ShapeDtypeStruct((B, n_tiles, key_tile, 4 * IDX_DIM), BF16),
            jax.ShapeDtypeStruct((B, T // LANES, N_IDX_HEADS, LANES), F32),
        )
        out_specs = out_specs[:4] + (
            pl.BlockSpec((1, sub, 4 * IDX_DIM, N_IDX_HEADS * LANES), lambda b, i: (b, i, 0, 0)),
        ) + out_specs[5:] + (
            pl.BlockSpec((1, N_KV_HEADS, 1, HEAD_DIM, tm), lambda b, i: (b, 0, i // per, 0, i % per)),
            pl.BlockSpec((1, N_KV_HEADS, 1, tm, HEAD_DIM), lambda b, i: (b, 0, i // per, i % per, 0)),
            pl.BlockSpec((1, 1, tm, 4 * IDX_DIM), lambda b, i: (b, i // per, i % per, 0)),
            pl.BlockSpec((1, sub, N_IDX_HEADS, LANES), lambda b, i: (b, i, 0, 0)),
        )
    scratch = [pltpu.VMEM((HALO + tm, POOL_WIDTH), F32) for _ in range(4)]
    return pl.pallas_call(
        functools.partial(_proj_kernel, tm=tm, pos0=pos0, d_model=D, key_tiles=key_tile is not None),
        grid=(B, T // tm), in_specs=in_specs, out_specs=out_specs, out_shape=out_shape,
        scratch_shapes=scratch, name="proj",
        compiler_params=pltpu.CompilerParams(dimension_semantics=("arbitrary", "arbitrary"),
                                             vmem_limit_bytes=VMEM_LIMIT),
    )(x, g_pre, w_in_r, hist16, cq, sq, ci, si, w_pool, pool_scale)


def _rope_tables(pos, dim):
    rot = dim // 4
    half = rot // 2
    inv = ROPE_THETA ** (-jnp.arange(half, dtype=F32) / half)
    ang = pos.astype(F32)[:, None] * inv[None, :]
    cos, sin = jnp.cos(ang), jnp.sin(ang)
    rest = dim - rot
    n = pos.shape[0]
    c = jnp.concatenate([cos, cos, jnp.ones((n, rest), F32)], axis=-1)
    s = jnp.concatenate([-sin, sin, jnp.zeros((n, rest), F32)], axis=-1)
    return jnp.tile(c, (1, LANES // dim)), jnp.tile(s, (1, LANES // dim))


def _split_kernel(k2_ref, out_ref):
    k2 = k2_ref[0]
    hi = k2.astype(BF16)
    out_ref[0, :, 0:LANES] = hi
    out_ref[0, :, LANES:2 * LANES] = (k2 - hi.astype(F32)).astype(BF16)


def _split_keys(ki2, ts):
    B, Lp, _ = ki2.shape
    rows = ts * max(1, SPLIT_ROWS // ts)
    rows = rows if Lp % rows == 0 else ts
    return pl.pallas_call(
        _split_kernel, grid=(B, Lp // rows),
        in_specs=[pl.BlockSpec((1, rows, LANES), lambda b, t: (b, t, 0))],
        out_specs=pl.BlockSpec((1, rows, 2 * LANES), lambda b, t: (b, t, 0)),
        out_shape=jax.ShapeDtypeStruct((B, Lp, 2 * LANES), BF16), name="split_keys",
        compiler_params=pltpu.CompilerParams(dimension_semantics=("arbitrary", "arbitrary")),
    )(ki2)


def _key_of(x):
    b = pltpu.bitcast(x, I32)
    return b ^ ((b >> 31) & 0x7FFFFFFF)


def _float_of(k):
    return pltpu.bitcast(k ^ ((k >> 31) & 0x7FFFFFFF), F32)


def _row_sum(x):
    return jnp.broadcast_to(jnp.sum(x, axis=-1, keepdims=True), x.shape)


def _attn_kernel(qs_ref, kiw_ref, q_ref, ki_ref, kt_ref, v_ref, o_ref,
                 sc_ref, mm_ref, wb_ref, s_ref, m_ref, acc_ref, bias_ref, p_ref, alpha_ref,
                 *, tq, ts, q0, n_keys, k_sel):
    qi = pl.program_id(1)
    n_tiles_total = ki_ref.shape[1] // ts
    nc = ts // LANES
    nt_dims = (((1,), (1,)), ((), ()))

    def key_rows(t):
        return pl.ds(pl.multiple_of(t * ts, ts), ts)

    row = lax.broadcasted_iota(I32, (tq, LANES), 0)
    lane = lax.broadcasted_iota(I32, (tq, LANES), 1)
    qpos = q0 + qi * tq + row
    n_adm = jnp.minimum((qpos // CHUNK + 1) * CHUNK, n_keys)
    last_adm = jnp.minimum(((q0 + (qi + 1) * tq - 1) // CHUNK + 1) * CHUNK, n_keys)
    n_t = jnp.minimum((last_adm + ts - 1) // ts, n_tiles_total)

    kiw = kiw_ref[0]
    for hd in range(N_IDX_HEADS):
        wb_ref[hd] = jnp.broadcast_to(kiw[:, IDX_DIM + hd:IDX_DIM + hd + 1], (tq, LANES))
    qs2 = qs_ref[0].reshape(N_IDX_HEADS * tq, 4 * IDX_DIM)

    def idx_logits(t, slot):
        s_ref[slot] = lax.dot_general(qs2, ki_ref[0, key_rows(t), :], nt_dims,
                                      preferred_element_type=F32)

    rc = min(tq, ROW_CHUNK)
    mm_ref[0] = jnp.full((tq, LANES), _POS_INF, F32)
    mm_ref[1] = jnp.full((tq, LANES), _NEG_INF, F32)

    def score_tile(t, slot, masked):
        for r0 in range(0, tq, rc):
            rows = slice(r0, r0 + rc)
            mn, mx = mm_ref[0, rows], mm_ref[1, rows]
            for c in range(nc):
                cs = slice(c * LANES, (c + 1) * LANES)
                acc = None
                for hd in range(N_IDX_HEADS):
                    r = jnp.maximum(s_ref[slot, hd * tq + r0:hd * tq + r0 + rc, cs], 0.0) * wb_ref[hd, rows]
                    acc = r if acc is None else acc + r
                key = _key_of(acc)
                if masked:
                    qpos_c = q0 + qi * tq + r0 + lax.broadcasted_iota(I32, (rc, LANES), 0)
                    n_adm_c = jnp.minimum((qpos_c // CHUNK + 1) * CHUNK, n_keys)
                    adm = (t * ts + c * LANES + lax.broadcasted_iota(I32, (rc, LANES), 1)) < n_adm_c
                    key = jnp.where(adm, key, _INADMISSIBLE)
                    mx = jnp.maximum(mx, jnp.where(adm, acc, _NEG_INF))
                    mn = jnp.minimum(mn, jnp.where(adm, acc, _POS_INF))
                else:
                    mx = jnp.maximum(mx, acc)
                    mn = jnp.minimum(mn, acc)
                sc_ref[t, rows, cs] = key
            mm_ref[0, rows] = mn
            mm_ref[1, rows] = mx

    n_full = jnp.minimum(jnp.minimum(((q0 + qi * tq) // CHUNK + 1) * CHUNK, n_keys) // ts, n_t)
    last_full = jnp.maximum(n_full - 1, 0)
    idx_logits(0, 0)

    def score_pair(u, carry):
        t0 = 2 * u
        t1 = jnp.minimum(t0 + 1, last_full)
        idx_logits(t1, 1)
        score_tile(t0, 0, False)
        idx_logits(jnp.minimum(t0 + 2, last_full), 0)
        score_tile(t1, 1, False)
        return carry

    lax.fori_loop(0, (n_full + 1) // 2, score_pair, 0)

    def score_tail(t, carry):
        idx_logits(t, 0)
        score_tile(t, 0, True)
        return carry

    lax.fori_loop(n_full, n_t, score_tail, 0)
    rmin = jnp.broadcast_to(jnp.min(mm_ref[0], axis=-1, keepdims=True), (tq, LANES))
    rmax = jnp.broadcast_to(jnp.max(mm_ref[1], axis=-1, keepdims=True), (tq, LANES))

    def count_ge(thr):
        def body(t, cnt):
            for c in range(nc):
                cnt = cnt + jnp.where(sc_ref[t, :, c * LANES:(c + 1) * LANES] >= thr, 1.0, 0.0)
            return cnt
        return _row_sum(lax.fori_loop(0, n_t, body, jnp.zeros((tq, LANES), F32)))

    def bis_cond(st):
        return st[0] > 0

    kf = float(k_sel)
    take_all = n_adm <= k_sel
    lo0 = _key_of(rmin) - 1
    hi0 = _key_of(rmax) + 2
    c_pos = count_ge(jnp.full((tq, LANES), _KEY_MIN_NORMAL, I32))
    c_nn = count_ge(jnp.full((tq, LANES), _KEY_NEG_ZERO, I32))
    pos = c_pos >= kf
    neg = c_nn < kf
    lo = jnp.where(pos, _KEY_MIN_NORMAL, jnp.where(neg, lo0, _KEY_NEG_ZERO))
    hi = jnp.where(pos, hi0, jnp.where(neg, _KEY_NEG_ZERO, _KEY_MIN_NORMAL))
    clo = jnp.where(pos, c_pos, jnp.where(neg, n_adm.astype(F32), c_nn))
    chi = jnp.where(pos, 0.0, jnp.where(neg, c_nn, c_pos))
    exact0 = jnp.where(clo == kf, 1, 0)
    act0 = jnp.where(take_all, 0, jnp.where(exact0 > 0, 0, jnp.where(hi > lo + 1, 1, 0)))

    def bis_body(st):
        _, it, lo, hi, clo, chi, exact, act = st
        kmid = (lo & hi) + ((lo ^ hi) >> 1)
        vmid = _key_of((_float_of(lo) + _float_of(hi)) * 0.5)
        by_value = jnp.where(it < VALUE_PIVOT_STEPS, jnp.where(vmid > lo, jnp.where(vmid < hi, 1, 0), 0), 0)
        mid = jnp.where(by_value > 0, vmid, kmid)
        cnt = count_ge(mid)
        up = jnp.where(act > 0, jnp.where(cnt >= kf, 1, 0), 0)
        dn = act - up
        lo = jnp.where(up > 0, mid, lo)
        clo = jnp.where(up > 0, cnt, clo)
        hi = jnp.where(dn > 0, mid, hi)
        chi = jnp.where(dn > 0, cnt, chi)
        hit = jnp.where(up > 0, jnp.where(cnt == kf, 1, 0), 0)
        exact = exact + hit
        act = jnp.where(act > 0, jnp.where(hit > 0, 0, jnp.where(hi > lo + 1, 1, 0)), 0)
        return jnp.max(act), it + 1, lo, hi, clo, chi, exact, act

    _, _, lo, _, clo, chi, exact, _ = lax.while_loop(
        bis_cond, bis_body, (jnp.max(act0), jnp.int32(0), lo, hi, clo, chi, exact0, act0))
    thr = jnp.where(take_all, _INADMISSIBLE + 1, lo)
    thr_up = jnp.where(take_all, _INADMISSIBLE + 1, lo + 1)

    need = kf - chi
    tied = jnp.where(take_all, 0, jnp.where(exact > 0, 0, jnp.where(clo > kf, 1, 0)))

    def count_tied_below(cut):
        def body(t, cnt):
            for c in range(nc):
                s = sc_ref[t, :, c * LANES:(c + 1) * LANES]
                col = t * ts + c * LANES + lane
                cnt = cnt + jnp.where(s == thr, jnp.where(col < cut, 1.0, 0.0), 0.0)
            return cnt
        return _row_sum(lax.fori_loop(0, n_t, body, jnp.zeros((tq, LANES), F32)))

    def cut_body(st):
        _, lo_c, hi_c, act = st
        mid = (lo_c + hi_c) >> 1
        cnt = count_tied_below(mid)
        on = act > 0
        le = cnt <= need
        lo_c = jnp.where(on, jnp.where(le, mid, lo_c), lo_c)
        hi_c = jnp.where(on, jnp.where(le, hi_c, mid), hi_c)
        act = jnp.where(on, jnp.where(hi_c - lo_c > 1, 1, 0), 0)
        return jnp.max(act), lo_c, hi_c, act

    _, cut_lo, _, _ = lax.while_loop(
        bis_cond, cut_body,
        (jnp.max(tied), jnp.zeros((tq, LANES), I32), jnp.full((tq, LANES), n_tiles_total * ts + 1, I32), tied))
    cut = jnp.where(tied > 0, cut_lo, n_tiles_total * ts + 1)

    m_ref[...] = jnp.full(m_ref.shape, _M_INIT, F32)
    acc_ref[...] = jnp.zeros(acc_ref.shape, F32)
    rc = min(tq, ROW_CHUNK)
    ones = jnp.ones((ts, LANES), BF16)
    last = n_t - 1

    def logits(t, slot):
        for g in range(N_KV_HEADS):
            qg = q_ref[0, g * GQA_GROUP:(g + 1) * GQA_GROUP].reshape(GQA_GROUP * tq, HEAD_DIM)
            s_ref[slot, g * GQA_GROUP * tq:(g + 1) * GQA_GROUP * tq] = lax.dot_general(
                qg, kt_ref[0, key_rows(t), g * HEAD_DIM:(g + 1) * HEAD_DIM], nt_dims,
                preferred_element_type=F32)

    def softmax_pv(t, slot, live):
        for c in range(nc):
            col = t * ts + c * LANES + lane
            bound = jnp.where(live, jnp.where(col < cut, thr, thr_up), _NO_KEY)
            bias_ref[slot, :, c * LANES:(c + 1) * LANES] = jnp.where(
                sc_ref[t, :, c * LANES:(c + 1) * LANES] >= bound, 0.0, _NEG_INF)
        for g in range(N_KV_HEADS):
            for j in range(GQA_GROUP):
                hd = g * GQA_GROUP + j
                for r0 in range(0, tq, rc):
                    r1 = j * tq + r0
                    r2 = hd * tq + r0
                    s = [s_ref[slot, r2:r2 + rc, c * LANES:(c + 1) * LANES]
                         + bias_ref[slot, r0:r0 + rc, c * LANES:(c + 1) * LANES] for c in range(nc)]
                    mx = s[0]
                    for c in range(1, nc):
                        mx = jnp.maximum(mx, s[c])
                    m_old = m_ref[hd, r0:r0 + rc]
                    m_new = jnp.maximum(m_old, jnp.broadcast_to(jnp.max(mx, axis=-1, keepdims=True), (rc, LANES)))
                    alpha_ref[slot, g, r1:r1 + rc] = jnp.exp2(m_old - m_new)
                    for c in range(nc):
                        p_ref[slot, g, r1:r1 + rc, c * LANES:(c + 1) * LANES] = jnp.exp2(s[c] - m_new).astype(BF16)
                    m_ref[hd, r0:r0 + rc] = m_new
            v_aug = jnp.concatenate([v_ref[0, key_rows(t), g * HEAD_DIM:(g + 1) * HEAD_DIM], ones], axis=-1)
            pv = jnp.dot(p_ref[slot, g], v_aug, preferred_element_type=F32)
            alpha = alpha_ref[slot, g]
            for half in range(2):
                cols = slice(half * LANES, (half + 1) * LANES)
                acc_ref[g, :, cols] = alpha * acc_ref[g, :, cols] + pv[:, cols]

    logits(0, 0)

    def attn_pair(u, carry):
        t0 = 2 * u
        t1 = jnp.minimum(t0 + 1, last)
        logits(t1, 1)
        softmax_pv(t0, 0, True)
        logits(jnp.minimum(t0 + 2, last), 0)
        softmax_pv(t1, 1, t0 + 1 < n_t)
        return carry

    lax.fori_loop(0, (n_t + 1) // 2, attn_pair, 0)
    for hd in range(N_HEADS):
        g, j = divmod(hd, GQA_GROUP)
        o_ref[0, :, hd * HEAD_DIM:(hd + 1) * HEAD_DIM] = (
            acc_ref[g, j * tq:(j + 1) * tq, 0:HEAD_DIM] / acc_ref[g, j * tq:(j + 1) * tq, HEAD_DIM:2 * HEAD_DIM])


def _attn_t_kernel(qst_ref, wrow_ref, q_ref, ki_ref, kt_ref, v_ref, o_ref,
                   sc_ref, h16_ref, l16_ref, mm_ref, st_ref, s_ref, m_ref, acc_ref, bias_ref, p_ref, alpha_ref, kmax_ref,
                   *, tq, ts, q0, n_keys, k_sel):
    qi = pl.program_id(1)
    n_tiles_total = ki_ref.shape[1]
    nc = ts // LANES
    rk = KEY_ROWS
    qpos = q0 + qi * tq + lax.broadcasted_iota(I32, (1, tq), 1)
    n_adm = jnp.minimum((qpos // CHUNK + 1) * CHUNK, n_keys)
    last_adm = jnp.minimum(((q0 + (qi + 1) * tq - 1) // CHUNK + 1) * CHUNK, n_keys)
    n_t = jnp.minimum((last_adm + ts - 1) // ts, n_tiles_total)
    n_full = jnp.minimum(jnp.minimum(((q0 + qi * tq) // CHUNK + 1) * CHUNK, n_keys) // ts, n_t)
    last_full = jnp.maximum(n_full - 1, 0)
    qst = qst_ref[0, 0]
    wr = wrow_ref[0, 0]

    mm_ref[0] = jnp.full((rk, tq), _POS_INF, F32)
    mm_ref[1] = jnp.full((rk, tq), _NEG_INF, F32)

    def idx_logits(t, slot):
        st_ref[slot] = jnp.dot(ki_ref[0, t], qst, preferred_element_type=F32)

    def score_tile(t, slot, masked):
        mn, mx = mm_ref[0], mm_ref[1]
        for r0 in range(0, ts, rk):
            acc = None
            for hd in range(N_IDX_HEADS):
                r = jnp.maximum(st_ref[slot, r0:r0 + rk, hd * tq:(hd + 1) * tq], 0.0) * wr[hd:hd + 1, :]
                acc = r if acc is None else acc + r
            key = _key_of(acc)
            if masked:
                adm = (t * ts + r0 + lax.broadcasted_iota(I32, (rk, tq), 0)) < n_adm
                key = jnp.where(adm, key, _INADMISSIBLE)
                mx = jnp.maximum(mx, jnp.where(adm, acc, _NEG_INF))
                mn = jnp.minimum(mn, jnp.where(adm, acc, _POS_INF))
            else:
                mx = jnp.maximum(mx, acc)
                mn = jnp.minimum(mn, acc)
            sc_ref[t, r0:r0 + rk, :] = key
            h16_ref[t, r0:r0 + rk, :] = (key >> 16).astype(I16)
        mm_ref[0] = mn
        mm_ref[1] = mx

    idx_logits(0, 0)

    def score_pair(u, carry):
        t0 = 2 * u
        t1 = jnp.minimum(t0 + 1, last_full)
        idx_logits(t1, 1)
        score_tile(t0, 0, False)
        idx_logits(jnp.minimum(t0 + 2, last_full), 0)
        score_tile(t1, 1, False)
        return carry

    lax.fori_loop(0, (n_full + 1) // 2, score_pair, 0)

    def score_tail(t, carry):
        idx_logits(t, 0)
        score_tile(t, 0, True)
        return carry

    lax.fori_loop(n_full, n_t, score_tail, 0)
    rmin = jnp.min(mm_ref[0], axis=0, keepdims=True)
    rmax = jnp.max(mm_ref[1], axis=0, keepdims=True)

    one16, zero16 = jnp.ones((rk, tq), I16), jnp.zeros((rk, tq), I16)

    def count16(ref, thr):
        thr16 = jnp.broadcast_to(thr, (rk, tq)).astype(I16)

        def body(t, cnt):
            for r0 in range(0, ts, rk):
                cnt = cnt + jnp.where(ref[t, r0:r0 + rk, :] >= thr16, one16, zero16)
            return cnt
        return jnp.sum(lax.fori_loop(0, n_t, body, zero16).astype(F32), axis=0, keepdims=True)

    def bis_cond(st):
        return st[0] > 0

    def any_active(act):
        return jnp.max(act.astype(F32))

    def bisect16(ref, target, state, by_value):
        def step(it, lo, hi, clo, chi, found, act):
            mid = (lo + hi) >> 1
            if by_value:
                vmid = _key_of((_float_of(lo << 16) + _float_of(hi << 16)) * 0.5) >> 16
                ok = jnp.where(it < VALUE_PIVOT_STEPS, jnp.where(vmid > lo, jnp.where(vmid < hi, 1, 0), 0), 0)
                mid = jnp.where(ok > 0, vmid, mid)
            cnt = count16(ref, mid)
            up = jnp.where(act > 0, jnp.where(cnt >= target, 1, 0), 0)
            dn = act - up
            lo = jnp.where(up > 0, mid, lo)
            clo = jnp.where(up > 0, cnt, clo)
            hi = jnp.where(dn > 0, mid, hi)
            chi = jnp.where(dn > 0, cnt, chi)
            hit = jnp.where(up > 0, jnp.where(cnt == target, 1, 0), 0)
            found = found + hit
            act = jnp.where(act > 0, jnp.where(hit > 0, 0, jnp.where(hi > lo + 1, 1, 0)), 0)
            return it + 1, lo, hi, clo, chi, found, act

        def body(st):
            st = step(*step(*st[1:]))
            return (any_active(st[-1]),) + st

        return lax.while_loop(bis_cond, body, (any_active(state[-1]),) + tuple(state))[1:]

    kf = float(k_sel)
    take_all = n_adm <= k_sel
    lo0 = (_key_of(rmin) - 1) >> 16
    hi0 = ((_key_of(rmax) + 2) >> 16) + 1
    c_pos = count16(h16_ref, jnp.full((1, tq), _HIGH_MIN_NORMAL, I32))
    c_mid = count16(h16_ref, jnp.full((1, tq), -_HIGH_MIN_NORMAL, I32))
    pos = c_pos >= kf
    neg = c_mid < kf
    lo = jnp.where(pos, _HIGH_MIN_NORMAL, jnp.where(neg, lo0, -_HIGH_MIN_NORMAL))
    hi = jnp.where(pos, hi0, jnp.where(neg, -_HIGH_MIN_NORMAL, _HIGH_MIN_NORMAL))
    clo = jnp.where(pos, c_pos, jnp.where(neg, n_adm.astype(F32), c_mid))
    chi = jnp.where(pos, 0.0, jnp.where(neg, c_mid, c_pos))
    found0 = jnp.where(clo == kf, 1, 0)
    act0 = jnp.where(take_all, 0, jnp.where(found0 > 0, 0, jnp.where(hi > lo + 1, 1, 0)))
    _, bkt, _, clo, chi, found1, _ = bisect16(h16_ref, kf, (jnp.int32(0), lo, hi, clo, chi, found0, act0), True)

    b16 = jnp.broadcast_to(bkt, (rk, tq)).astype(I16)
    lowest = jnp.full((rk, tq), -32768, I16)

    def low_halves(t, carry):
        for r0 in range(0, ts, rk):
            low = ((sc_ref[t, r0:r0 + rk, :] & 0xFFFF) - 32768).astype(I16)
            l16_ref[t, r0:r0 + rk, :] = jnp.where(h16_ref[t, r0:r0 + rk, :] == b16, low, lowest)
        return carry

    lax.fori_loop(0, n_t, low_halves, 0)
    need2 = kf - chi
    act2 = jnp.where(take_all, 0, jnp.where(found1 > 0, 0, 1))
    zi = jnp.zeros((1, tq), I32)
    _, low_t, _, clo2, chi2, found2, _ = bisect16(
        l16_ref, need2,
        (jnp.int32(0), jnp.full((1, tq), -32768, I32), jnp.full((1, tq), 32768, I32), clo - chi,
         jnp.zeros((1, tq), F32), zi, act2), False)

    thr = (bkt << 16) + jnp.where(found1 > 0, 0, low_t + 32768)
    thr = jnp.where(take_all, _INADMISSIBLE + 1, thr)
    thr_up = jnp.where(take_all, _INADMISSIBLE + 1, thr + 1)
    exact = found1 + found2
    clo = chi + clo2
    chi = chi + chi2

    need = kf - chi
    tied = jnp.where(take_all, 0, jnp.where(exact > 0, 0, jnp.where(clo > kf, 1, 0)))

    def count_tied_below(cut):
        def body(t, cnt):
            for r0 in range(0, ts, rk):
                kidx = t * ts + r0 + lax.broadcasted_iota(I32, (rk, tq), 0)
                cnt = cnt + jnp.where(sc_ref[t, r0:r0 + rk, :] == thr, jnp.where(kidx < cut, 1.0, 0.0), 0.0)
            return cnt
        return jnp.sum(lax.fori_loop(0, n_t, body, jnp.zeros((rk, tq), F32)), axis=0, keepdims=True)

    def cut_body(st):
        _, lo_c, hi_c, act = st
        mid = (lo_c + hi_c) >> 1
        cnt = count_tied_below(mid)
        on = act > 0
        le = cnt <= need
        lo_c = jnp.where(on, jnp.where(le, mid, lo_c), lo_c)
        hi_c = jnp.where(on, jnp.where(le, hi_c, mid), hi_c)
        act = jnp.where(on, jnp.where(hi_c - lo_c > 1, 1, 0), 0)
        return jnp.max(act), lo_c, hi_c, act

    _, cut_lo, _, _ = lax.while_loop(
        bis_cond, cut_body,
        (jnp.max(tied), jnp.zeros((1, tq), I32), jnp.full((1, tq), n_tiles_total * ts + 1, I32), tied))
    cut = jnp.where(tied > 0, cut_lo, n_tiles_total * ts + 1)

    acc_ref[...] = jnp.zeros(acc_ref.shape, F32)
    rc = min(tq, ROW_CHUNK)
    ones = jnp.ones((ts, LANES), BF16)
    last = n_t - 1

    def logits(t, slot):
        for g in range(N_KV_HEADS):
            qg = q_ref[0, g * GQA_GROUP:(g + 1) * GQA_GROUP].reshape(GQA_GROUP * tq, HEAD_DIM)
            s_ref[slot, g * GQA_GROUP * tq:(g + 1) * GQA_GROUP * tq] = jnp.dot(
                qg, kt_ref[0, g, t], preferred_element_type=F32)

    def softmax_pv(t, slot, live, fixed):
        for c in range(nc):
            kidx = t * ts + c * LANES + lax.broadcasted_iota(I32, (LANES, tq), 0)
            bound = jnp.where(live, jnp.where(kidx < cut, thr, thr_up), _NO_KEY)
            bias_t = jnp.where(sc_ref[t, c * LANES:(c + 1) * LANES, :] >= bound, 0.0, _NEG_INF)
            bias_ref[slot, :, c * LANES:(c + 1) * LANES] = bias_t.T
        for g in range(N_KV_HEADS):
            for j in range(GQA_GROUP):
                hd = g * GQA_GROUP + j
                for r0 in range(0, tq, rc):
                    r1 = j * tq + r0
                    r2 = hd * tq + r0
                    s = [s_ref[slot, r2:r2 + rc, c * LANES:(c + 1) * LANES]
                         + bias_ref[slot, r0:r0 + rc, c * LANES:(c + 1) * LANES] for c in range(nc)]
                    if fixed:
                        m_new = m_ref[hd, r0:r0 + rc]
                    else:
                        mx = s[0]
                        for c in range(1, nc):
                            mx = jnp.maximum(mx, s[c])
                        m_old = m_ref[hd, r0:r0 + rc]
                        m_new = jnp.maximum(
                            m_old, jnp.broadcast_to(jnp.max(mx, axis=-1, keepdims=True), (rc, LANES)))
                        alpha_ref[slot, g, r1:r1 + rc] = jnp.exp2(m_old - m_new)
                        m_ref[hd, r0:r0 + rc] = m_new
                    for c in range(nc):
                        p_ref[slot, g, r1:r1 + rc, c * LANES:(c + 1) * LANES] = jnp.exp2(s[c] - m_new).astype(BF16)
            v_aug = jnp.concatenate([v_ref[0, g, t], ones], axis=-1)
            pv = jnp.dot(p_ref[slot, g], v_aug, preferred_element_type=F32)
            for half in range(2):
                cols = slice(half * LANES, (half + 1) * LANES)
                if fixed:
                    acc_ref[g, :, cols] = acc_ref[g, :, cols] + pv[:, cols]
                else:
                    acc_ref[g, :, cols] = alpha_ref[slot, g] * acc_ref[g, :, cols] + pv[:, cols]

    def attend(fixed):
        def attn_pair(u, carry):
            t0 = 2 * u
            t1 = jnp.minimum(t0 + 1, last)
            logits(t1, 1)
            softmax_pv(t0, 0, True, fixed)
            logits(jnp.minimum(t0 + 2, last), 0)
            softmax_pv(t1, 1, t0 + 1 < n_t, fixed)
            return carry

        lax.fori_loop(0, (n_t + 1) // 2, attn_pair, 0)

    @pl.when(qi == 0)
    def _():
        for g in range(N_KV_HEADS):
            def sq_norm_max(t, best):
                kk = kt_ref[0, g, t].astype(F32)
                return jnp.maximum(best, jnp.sum(kk * kk, axis=0, keepdims=True))
            best = lax.fori_loop(0, n_tiles_total, sq_norm_max, jnp.zeros((1, ts), F32))
            kmax_ref[g] = jnp.broadcast_to(jnp.max(best, axis=-1, keepdims=True), (8, LANES))

    logits(0, 0)
    top = jnp.zeros((tq, LANES), F32)
    for hd in range(N_HEADS):
        qf = q_ref[0, hd].astype(F32)
        q_sq = _row_sum(qf * qf)
        ref_pt = jnp.sqrt(q_sq * kmax_ref[hd // GQA_GROUP, 0:1, :]) * 1.001 + 1e-3
        m_ref[hd] = ref_pt
        top = jnp.maximum(top, ref_pt)
    fits = jnp.max(top) <= MAX_FIXED_REFERENCE

    @pl.when(fits)
    def _():
        attend(True)

    @pl.when(jnp.logical_not(fits))
    def _():
        m_ref[...] = jnp.full(m_ref.shape, _M_INIT, F32)
        attend(False)

    for hd in range(N_HEADS):
        g, j = divmod(hd, GQA_GROUP)
        o_ref[0, :, hd * HEAD_DIM:(hd + 1) * HEAD_DIM] = (
            acc_ref[g, j * tq:(j + 1) * tq, 0:HEAD_DIM] / acc_ref[g, j * tq:(j + 1) * tq, HEAD_DIM:2 * HEAD_DIM])


def _attend_t(qst, wrow, q_att, ki4, kt, v, q0, n_keys, tq, ts):
    B, _, T, _ = q_att.shape
    n_tiles = ki4.shape[1]
    k_sel = min(TOPK_MAX, n_keys // 4)
    kern = functools.partial(_attn_t_kernel, tq=tq, ts=ts, q0=q0, n_keys=n_keys, k_sel=k_sel)
    return pl.pallas_call(
        kern, grid=(B, T // tq),
        in_specs=[
            pl.BlockSpec((1, 1, 4 * IDX_DIM, N_IDX_HEADS * tq), lambda b, i: (b, i, 0, 0)),
            pl.BlockSpec((1, 1, N_IDX_HEADS, tq), lambda b, i: (b, i, 0, 0)),
            pl.BlockSpec((1, N_HEADS, tq, HEAD_DIM), lambda b, i: (b, 0, i, 0)),
            _resident((1, n_tiles, ts, 4 * IDX_DIM), lambda b, i: (b, 0, 0, 0)),
            _resident((1, N_KV_HEADS, n_tiles, HEAD_DIM, ts), lambda b, i: (b, 0, 0, 0, 0)),
            _resident((1, N_KV_HEADS, n_tiles, ts, HEAD_DIM), lambda b, i: (b, 0, 0, 0, 0)),
        ],
        out_specs=pl.BlockSpec((1, tq, N_HEADS * HEAD_DIM), lambda b, i: (b, i, 0)),
        out_shape=jax.ShapeDtypeStruct((B, T, N_HEADS * HEAD_DIM), F32),
        scratch_shapes=[
            pltpu.VMEM((n_tiles, ts, tq), I32),
            pltpu.VMEM((n_tiles, ts, tq), I16),
            pltpu.VMEM((n_tiles, ts, tq), I16),
            pltpu.VMEM((2, KEY_ROWS, tq), F32),
            pltpu.VMEM((2, ts, N_IDX_HEADS * tq), F32),
            pltpu.VMEM((2, N_HEADS * tq, ts), F32),
            pltpu.VMEM((N_HEADS, tq, LANES), F32),
            pltpu.VMEM((N_KV_HEADS, GQA_GROUP * tq, 2 * HEAD_DIM), F32),
            pltpu.VMEM((2, tq, ts), F32),
            pltpu.VMEM((2, N_KV_HEADS, GQA_GROUP * tq, ts), BF16),
            pltpu.VMEM((2, N_KV_HEADS, GQA_GROUP * tq, LANES), F32),
            pltpu.VMEM((N_KV_HEADS, 8, LANES), F32),
        ],
        name="attend_t",
        compiler_params=pltpu.CompilerParams(dimension_semantics=("arbitrary", "arbitrary"),
                                             vmem_limit_bytes=VMEM_LIMIT),
    )(qst, wrow, q_att, ki4, kt, v)


def _attend(qs, kiw, q_att, ki4, kt, v, q0, n_keys, tq, ts):
    B, _, T, _ = qs.shape
    lp = ki4.shape[1]
    n_tiles = lp // ts
    k_sel = min(TOPK_MAX, n_keys // 4)
    kern = functools.partial(_attn_kernel, tq=tq, ts=ts, q0=q0, n_keys=n_keys, k_sel=k_sel)
    keys = pl.BlockSpec((1, lp, 2 * LANES), lambda b, i: (b, 0, 0))
    return pl.pallas_call(
        kern, grid=(B, T // tq),
        in_specs=[
            pl.BlockSpec((1, N_IDX_HEADS, tq, 4 * IDX_DIM), lambda b, i: (b, 0, i, 0)),
            pl.BlockSpec((1, tq, LANES), lambda b, i: (b, i, 0)),
            pl.BlockSpec((1, N_HEADS, tq, HEAD_DIM), lambda b, i: (b, 0, i, 0)),
            keys, keys, keys,
        ],
        out_specs=pl.BlockSpec((1, tq, N_HEADS * HEAD_DIM), lambda b, i: (b, i, 0)),
        out_shape=jax.ShapeDtypeStruct((B, T, N_HEADS * HEAD_DIM), F32),
        scratch_shapes=[
            pltpu.VMEM((n_tiles, tq, ts), I32),
            pltpu.VMEM((2, tq, LANES), F32),
            pltpu.VMEM((N_IDX_HEADS, tq, LANES), F32),
            pltpu.VMEM((2, N_HEADS * tq, ts), F32),
            pltpu.VMEM((N_HEADS, tq, LANES), F32),
            pltpu.VMEM((N_KV_HEADS, GQA_GROUP * tq, 2 * HEAD_DIM), F32),
            pltpu.VMEM((2, tq, ts), F32),
            pltpu.VMEM((2, N_KV_HEADS, GQA_GROUP * tq, ts), BF16),
            pltpu.VMEM((2, N_KV_HEADS, GQA_GROUP * tq, LANES), F32),
        ],
        name="attend",
        compiler_params=pltpu.CompilerParams(dimension_semantics=("arbitrary", "arbitrary"),
                                             vmem_limit_bytes=VMEM_LIMIT),
    )(qs, kiw, q_att, ki4, kt, v)


def _rms(x, g):
    return x * lax.rsqrt(jnp.mean(x * x, axis=-1, keepdims=True) + RMS_EPS) * g


def _out_kernel(x_ref, pa_ref, sgb_ref, b_ref, wo_ref, g1_ref, g2_ref, wgu_ref, wd_ref, g3_ref, y_ref, *, d_ff):
    merged = pa_ref[...] + sgb_ref[...] * b_ref[...]
    mix = jnp.dot(merged.astype(BF16), wo_ref[...], preferred_element_type=F32)
    x1 = x_ref[...] + _rms(mix, g1_ref[...])
    h2 = _rms(x1, g2_ref[...]).astype(BF16)
    gu = jnp.dot(h2, wgu_ref[...], preferred_element_type=F32)
    gate, up = gu[:, :d_ff], gu[:, d_ff:]
    act = (gate * _sigmoid(gate) * up).astype(BF16)
    f = jnp.dot(act, wd_ref[...], preferred_element_type=F32)
    y_ref[...] = x1 + _rms(f, g3_ref[...])


def _finish(x, pa, sgb, b, w_out, g_post, g_ffn_pre, w_gate_up, w_down, g_ffn_post, tm):
    N, D = x.shape
    d_ff = w_down.shape[0]
    row = pl.BlockSpec((tm, D), lambda i: (i, 0))
    const = lambda i: (0, 0)
    return pl.pallas_call(
        functools.partial(_out_kernel, d_ff=d_ff), grid=(N // tm,),
        in_specs=[row, row, row, row,
                  _resident((D, D), const), _resident((1, D), const), _resident((1, D), const),
                  _resident((D, 2 * d_ff), const), _resident((d_ff, D), const), _resident((1, D), const)],
        out_specs=row, out_shape=jax.ShapeDtypeStruct((N, D), F32), name="finish",
        compiler_params=pltpu.CompilerParams(dimension_semantics=("arbitrary",), vmem_limit_bytes=VMEM_LIMIT),
    )(x, pa, sgb, b, w_out, g_post, g_ffn_pre, w_gate_up, w_down, g_ffn_post)


def _tile_keys(n_keys, ts):
    return -(-n_keys // ts) * ts


def _layer(x, pos0, hist, k_cache, v_cache, ki_cache, wts, tm, tq, ts, tm_out):
    (w_in_r, w_pool, pool_scale, w_out, w_gate_up, w_down, g_pre, g_post, g_ffn_pre, g_ffn_post) = wts
    B, T, D = x.shape
    hist16 = jnp.concatenate([jnp.zeros((B, HALO // 2 - POOL_HIST, POOL_WIDTH), F32), hist], axis=1)
    own_keys_only = k_cache is None and T % ts == 0 and ts % tm == 0 and tq == LANES and tm % LANES == 0
    outs = _project(x, hist16, pos0, w_in_r, g_pre, w_pool, pool_scale, tm, ts if own_keys_only else None)
    u, q_att, k, v, qs, kiw, pa, sgb = outs[:8]
    ki = kiw[:, :, :IDX_DIM]
    if own_keys_only:
        kt, vt, ki4, wrow = outs[8:]
        b = _attend_t(qs, wrow, q_att, ki4, kt, vt, pos0, T, tq, ts)
    else:
        if k_cache is None:
            k_all, v_all, ki_all = k, v, ki
        else:
            k_all = jnp.concatenate([k_cache.reshape(B, -1, N_KV_HEADS * HEAD_DIM), k], axis=1)
            v_all = jnp.concatenate([v_cache.reshape(B, -1, N_KV_HEADS * HEAD_DIM), v], axis=1)
            ki_all = jnp.concatenate([ki_cache, ki], axis=1)
        n_keys = k_all.shape[1]
        lp = _tile_keys(n_keys, ts)
        pad = ((0, 0), (0, lp - n_keys), (0, 0))
        kt = jnp.pad(k_all.astype(BF16), pad)
        vt = jnp.pad(v_all.astype(BF16), pad)
        ki4 = _split_keys(jnp.pad(jnp.concatenate([ki_all, ki_all], axis=-1), pad), ts)
        b = _attend(qs, kiw, q_att, ki4, kt, vt, pos0, n_keys, tq, ts)
    y = _finish(x.reshape(B * T, D), pa.reshape(B * T, D), sgb.reshape(B * T, D), b.reshape(B * T, D),
                w_out, g_post, g_ffn_pre, w_gate_up, w_down, g_ffn_post, tm_out).reshape(B, T, D)
    new_pool = jnp.concatenate([hist, u], axis=1)[:, T:]
    return (y, k.reshape(B, T, N_KV_HEADS, HEAD_DIM), v.reshape(B, T, N_KV_HEADS, HEAD_DIM), ki, new_pool)


def _relayout_w_in(w_in):
    d = w_in.shape[0]
    o_kiw = POOL_WIDTH + N_HEADS * HEAD_DIM + 2 * N_KV_HEADS * HEAD_DIM + N_IDX_HEADS * IDX_DIM
    narrow = IDX_DIM + N_IDX_HEADS
    padded = jnp.concatenate([w_in[:, :o_kiw + narrow], jnp.zeros((d, LANES - narrow), w_in.dtype),
                              w_in[:, o_kiw + narrow:]], axis=1)
    return padded.astype(BF16)


def kernel(x_prompt, x_sample, cache_k, cache_v, cache_k_idx, state_pool, w_in, w_pool, pool_scale, w_out,
           w_gate_up, w_down, norm_mix_pre, norm_mix_post, norm_ffn_pre, norm_ffn_post):
    depth = w_in.shape[0]
    past = cache_k.shape[2]
    t_p, t_s = x_prompt.shape[1], x_sample.shape[1]
    hist_p = jnp.zeros((x_prompt.shape[0], POOL_HIST, POOL_WIDTH), x_prompt.dtype)
    xp, xs = x_prompt, x_sample
    outs = [[] for _ in range(8)]
    for l in range(depth):
        wts = (_relayout_w_in(w_in[l]), w_pool[l].astype(BF16), pool_scale[l][None, :], w_out[l].astype(BF16),
               w_gate_up[l].astype(BF16), w_down[l].astype(BF16), norm_mix_pre[l][None, :],
               norm_mix_post[l][None, :], norm_ffn_pre[l][None, :], norm_ffn_post[l][None, :])
        tm_p = min(256, t_p)
        tq_p = min(128, t_p)
        xp, k1, v1, ki1, p1 = _layer(xp, 0, hist_p, None, None, None, wts, tm_p, tq_p, 512, tm_p)
        n_s = xs.shape[0] * t_s
        xs, k2, v2, ki2, p2 = _layer(xs, past, state_pool[l], cache_k[l], cache_v[l], cache_k_idx[l], wts,
                                     t_s, t_s, 512, min(256, n_s))
        for lst, val in zip(outs, (k1, v1, ki1, p1, k2, v2, ki2, p2)):
            lst.append(val)
    return (xp, xs) + tuple(jnp.stack(o) for o in outs)
```

```python
import functools

import jax
import jax.numpy as jnp
from jax import lax
from jax.experimental import pallas as pl
from jax.experimental.pallas import tpu as pltpu

F32 = jnp.float32
BF16 = jnp.bfloat16
I32 = jnp.int32

LANES = 128
CHUNK = 64
POOL_WINDOWS = (2, 4, 8, 16)
N_POOL_GROUPS = 4
POOL_GROUP_WIDTH = 128
POOL_WIDTH = N_POOL_GROUPS * POOL_GROUP_WIDTH
POOL_HIST = 15
N_HEADS = 8
N_KV_HEADS = 2
HEAD_DIM = 128
GQA_GROUP = N_HEADS // N_KV_HEADS
ROPE_THETA = 500000.0
N_IDX_HEADS = 8
IDX_DIM = 64
TOPK_MAX = 256
RMS_EPS = 1e-6
ATTN_SCALE = HEAD_DIM ** -0.5
IDX_SCALE = (N_IDX_HEADS ** -0.5) * (IDX_DIM ** -0.5)

LOG2_E = 1.4426950408889634
VALUE_PIVOT_STEPS = 28
MAX_FIXED_REFERENCE = 40.0
SPLIT_ROWS = 1536
KEY_ROWS = 64
ROW_CHUNK = 32
HALO = 32
VMEM_LIMIT = 56 * 1024 * 1024

_NEG_INF = float("-inf")
_POS_INF = float("inf")
_INADMISSIBLE = -(2 ** 31)
_NO_KEY = 2 ** 31 - 1
_KEY_MIN_NORMAL = 0x00800000
_KEY_NEG_ZERO = -1
_M_INIT = -1e30


def _resident(block_shape, index_map):
    return pl.BlockSpec(block_shape, index_map, pipeline_mode=pl.Buffered(1))


def _rope(xs, cos, sin, half, period):
    lane = lax.broadcasted_iota(I32, xs.shape, 1)
    ahead = pltpu.roll(xs, LANES - half, 1)
    behind = pltpu.roll(xs, half, 1)
    partner = jnp.where((lane & (period - 1)) < half, ahead, behind)
    return xs * cos + partner * sin


def _sigmoid(x):
    return 1.0 / (1.0 + jnp.exp(-x))


def _proj_kernel(x_ref, g_ref, w_ref, hist_ref, cq_ref, sq_ref, ci_ref, si_ref, wpool_ref, pscale_ref,
                 u_ref, qatt_ref, k_ref, v_ref, qs_ref, kiw_ref, pa_ref, sgb_ref, *rest, tm, pos0, d_model, key_tiles):
    if key_tiles:
        kt_ref, vt_ref, ki4_ref, wrow_ref, e_ref, s2_ref, s4_ref, s8_ref = rest
    else:
        e_ref, s2_ref, s4_ref, s8_ref = rest
    i = pl.program_id(1)
    x = x_ref[0]
    h = x * lax.rsqrt(jnp.mean(x * x, axis=-1, keepdims=True) + RMS_EPS) * g_ref[...]
    proj = jnp.dot(h.astype(BF16), w_ref[...], preferred_element_type=F32)

    o_q = POOL_WIDTH
    o_k = o_q + N_HEADS * HEAD_DIM
    o_v = o_k + N_KV_HEADS * HEAD_DIM
    o_qi = o_v + N_KV_HEADS * HEAD_DIM
    o_kiw = o_qi + N_IDX_HEADS * IDX_DIM
    o_ga = o_kiw + LANES
    o_gb = o_ga + d_model

    cq, sq, ci, si = cq_ref[...], sq_ref[...], ci_ref[...], si_ref[...]
    half_q = HEAD_DIM // 8
    half_i = IDX_DIM // 8

    for hd in range(N_HEADS):
        qh = _rope(proj[:, o_q + hd * HEAD_DIM:o_q + (hd + 1) * HEAD_DIM], cq, sq, half_q, HEAD_DIM)
        qatt_ref[0, hd] = (qh * (ATTN_SCALE * LOG2_E)).astype(BF16)
    for kh in range(N_KV_HEADS):
        k_h = _rope(proj[:, o_k + kh * HEAD_DIM:o_k + (kh + 1) * HEAD_DIM], cq, sq, half_q, HEAD_DIM)
        k_ref[0, :, kh * HEAD_DIM:(kh + 1) * HEAD_DIM] = k_h
        if key_tiles:
            kt_ref[0, kh, 0] = k_h.T.astype(BF16)
            vt_ref[0, kh, 0] = proj[:, o_v + kh * HEAD_DIM:o_v + (kh + 1) * HEAD_DIM].astype(BF16)
    v_ref[0] = proj[:, o_v:o_qi]

    for pr in range(N_IDX_HEADS // 2):
        qi2 = _rope(proj[:, o_qi + pr * LANES:o_qi + (pr + 1) * LANES], ci, si, half_i, IDX_DIM)
        hi = qi2.astype(BF16).astype(F32)
        lo = qi2 - hi
        lane2 = lax.broadcasted_iota(I32, qi2.shape, 1)
        first = jnp.where(lane2 < IDX_DIM, hi, pltpu.roll(lo, IDX_DIM, 1))
        second = jnp.where(lane2 < IDX_DIM, pltpu.roll(hi, IDX_DIM, 1), lo)
        for sub, slab in enumerate((first, second)):
            hd = 2 * pr + sub
            if key_tiles:
                for sb in range(tm // LANES):
                    slab_t = slab[sb * LANES:(sb + 1) * LANES].T.astype(BF16)
                    qs_ref[0, sb, 0:LANES, hd * LANES:(hd + 1) * LANES] = slab_t
                    qs_ref[0, sb, LANES:2 * LANES, hd * LANES:(hd + 1) * LANES] = slab_t
            else:
                qs_ref[0, hd, :, 0:LANES] = slab.astype(BF16)
                qs_ref[0, hd, :, LANES:2 * LANES] = slab.astype(BF16)

    kiw = _rope(proj[:, o_kiw:o_kiw + LANES], ci, si, half_i, IDX_DIM)
    lane = lax.broadcasted_iota(I32, kiw.shape, 1)
    kiw_full = jnp.where(lane < IDX_DIM, kiw, proj[:, o_kiw:o_kiw + LANES] * IDX_SCALE)
    kiw_ref[0] = kiw_full
    if key_tiles:
        ki_hi = kiw.astype(BF16).astype(F32)
        ki_lo = kiw - ki_hi
        ki4_ref[0, 0, :, 0:LANES] = jnp.where(lane < IDX_DIM, ki_hi, pltpu.roll(ki_hi, IDX_DIM, 1)).astype(BF16)
        ki4_ref[0, 0, :, LANES:2 * LANES] = jnp.where(lane < IDX_DIM, ki_lo, pltpu.roll(ki_lo, IDX_DIM, 1)).astype(BF16)
        for sb in range(tm // LANES):
            wrow_ref[0, sb] = kiw_full[sb * LANES:(sb + 1) * LANES].T[IDX_DIM:IDX_DIM + N_IDX_HEADS]

    u = proj[:, 0:POOL_WIDTH]
    u_ref[0] = u

    @pl.when(i == 0)
    def _():
        e_ref[0:HALO // 2, :] = jnp.zeros((HALO // 2, POOL_WIDTH), F32)
        e_ref[HALO // 2:HALO, :] = hist_ref[0]

    e_ref[HALO:HALO + tm, :] = u
    n2, n4, n8 = tm + 24, tm + 16, tm + 8
    s2_ref[8:8 + n2, :] = e_ref[8:8 + n2, :] + e_ref[7:7 + n2, :]
    s4_ref[16:16 + n4, :] = s2_ref[16:16 + n4, :] + s2_ref[14:14 + n4, :]
    s8_ref[24:24 + n8, :] = s4_ref[24:24 + n8, :] + s4_ref[20:20 + n8, :]
    s16 = s8_ref[HALO:HALO + tm, :] + s8_ref[HALO - 8:HALO - 8 + tm, :]
    wins = (s2_ref[HALO:HALO + tm, :], s4_ref[HALO:HALO + tm, :], s8_ref[HALO:HALO + tm, :], s16)
    e_ref[HALO // 2:HALO, :] = e_ref[HALO // 2 + tm:HALO + tm, :]

    pos = pos0 + i * tm + lax.broadcasted_iota(I32, (tm, POOL_GROUP_WIDTH), 0)
    a_parts = []
    for g, w in enumerate(POOL_WINDOWS):
        lo_l, hi_l = g * POOL_GROUP_WIDTH, (g + 1) * POOL_GROUP_WIDTH
        cnt = jnp.minimum(pos + 1, w).astype(F32)
        pooled = wins[g][:, lo_l:hi_l] / cnt - u[:, lo_l:hi_l]
        a_parts.append(jnp.dot(pooled.astype(BF16), wpool_ref[g], preferred_element_type=F32))
    a = jnp.concatenate(a_parts, axis=-1) * pscale_ref[...]
    pa_ref[0] = _sigmoid(proj[:, o_ga:o_gb]) * a
    sgb_ref[0] = _sigmoid(proj[:, o_gb:o_gb + d_model])


def _project(x, hist16, pos0, w_in_r, g_pre, w_pool, pool_scale, tm, key_tile=None):
    B, T, D = x.shape
    W = w_in_r.shape[1]
    pos = pos0 + jnp.arange(T, dtype=I32)
    cq, sq = _rope_tables(pos, HEAD_DIM)
    ci, si = _rope_tables(pos, IDX_DIM)
    row = lambda b, i: (b, i, 0)
    tab = lambda b, i: (i, 0)
    const2 = lambda b, i: (0, 0)
    out_shape = (
        jax.ShapeDtypeStruct((B, T, POOL_WIDTH), F32),
        jax.ShapeDtypeStruct((B, N_HEADS, T, HEAD_DIM), BF16),
        jax.ShapeDtypeStruct((B, T, N_KV_HEADS * HEAD_DIM), F32),
        jax.ShapeDtypeStruct((B, T, N_KV_HEADS * HEAD_DIM), F32),
        jax.ShapeDtypeStruct((B, N_IDX_HEADS, T, 4 * IDX_DIM), BF16),
        jax.ShapeDtypeStruct((B, T, LANES), F32),
        jax.ShapeDtypeStruct((B, T, D), F32),
        jax.ShapeDtypeStruct((B, T, D), F32),
    )
    out_specs = (
        pl.BlockSpec((1, tm, POOL_WIDTH), row),
        pl.BlockSpec((1, N_HEADS, tm, HEAD_DIM), lambda b, i: (b, 0, i, 0)),
        pl.BlockSpec((1, tm, N_KV_HEADS * HEAD_DIM), row),
        pl.BlockSpec((1, tm, N_KV_HEADS * HEAD_DIM), row),
        pl.BlockSpec((1, N_IDX_HEADS, tm, 4 * IDX_DIM), lambda b, i: (b, 0, i, 0)),
        pl.BlockSpec((1, tm, LANES), row),
        pl.BlockSpec((1, tm, D), row),
        pl.BlockSpec((1, tm, D), row),
    )
    in_specs = [
        pl.BlockSpec((1, tm, D), row),
        _resident((1, D), const2),
        _resident((D, W), const2),
        pl.BlockSpec((1, HALO // 2, POOL_WIDTH), lambda b, i: (b, 0, 0)),
        pl.BlockSpec((tm, LANES), tab), pl.BlockSpec((tm, LANES), tab),
        pl.BlockSpec((tm, LANES), tab), pl.BlockSpec((tm, LANES), tab),
        _resident((N_POOL_GROUPS, POOL_GROUP_WIDTH, D // N_POOL_GROUPS), lambda b, i: (0, 0, 0)),
        _resident((1, D), const2),
    ]
    if key_tile is not None:
        assert key_tile % tm == 0 and T % key_tile == 0 and tm % LANES == 0
        per = key_tile // tm
        n_tiles = T // key_tile
        sub = tm // LANES
        out_shape = out_shape[:4] + (
            jax.ShapeDtypeStruct((B, T // LANES, 4 * IDX_DIM, N_IDX_HEADS * LANES), BF16),
        ) + out_shape[5:] + (
            jax.ShapeDtypeStruct((B, N_KV_HEADS, n_tiles, HEAD_DIM, key_tile), BF16),
            jax.ShapeDtypeStruct((B, N_KV_HEADS, n_tiles, key_tile, HEAD_DIM), BF16),
            jax.ShapeDtypeStruct((B, n_tiles, key_tile, 4 * IDX_DIM), BF16),
            jax.ShapeDtypeStruct((B, T // LANES, N_IDX_HEADS, LANES), F32),
        )
        out_specs = out_specs[:4] + (
            pl.BlockSpec((1, sub, 4 * IDX_DIM, N_IDX_HEADS * LANES), lambda b, i: (b, i, 0, 0)),
        ) + out_specs[5:] + (
            pl.BlockSpec((1, N_KV_HEADS, 1, HEAD_DIM, tm), lambda b, i: (b, 0, i // per, 0, i % per)),
            pl.BlockSpec((1, N_KV_HEADS, 1, tm, HEAD_DIM), lambda b, i: (b, 0, i // per, i % per, 0)),
            pl.BlockSpec((1, 1, tm, 4 * IDX_DIM), lambda b, i: (b, i // per, i % per, 0)),
            pl.BlockSpec((1, sub, N_IDX_HEADS, LANES), lambda b, i: (b, i, 0, 0)),
        )
    scratch = [pltpu.VMEM((HALO + tm, POOL_WIDTH), F32) for _ in range(4)]
    return pl.pallas_call(
        functools.partial(_proj_kernel, tm=tm, pos0=pos0, d_model=D, key_tiles=key_tile is not None),
        grid=(B, T // tm), in_specs=in_specs, out_specs=out_specs, out_shape=out_shape,
        scratch_shapes=scratch, name="proj",
        compiler_params=pltpu.CompilerParams(dimension_semantics=("arbitrary", "arbitrary"),
                                             vmem_limit_bytes=VMEM_LIMIT),
    )(x, g_pre, w_in_r, hist16, cq, sq, ci, si, w_pool, pool_scale)


def _rope_tables(pos, dim):
    rot = dim // 4
    half = rot // 2
    inv = ROPE_THETA ** (-jnp.arange(half, dtype=F32) / half)
    ang = pos.astype(F32)[:, None] * inv[None, :]
    cos, sin = jnp.cos(ang), jnp.sin(ang)
    rest = dim - rot
    n = pos.shape[0]
    c = jnp.concatenate([cos, cos, jnp.ones((n, rest), F32)], axis=-1)
    s = jnp.concatenate([-sin, sin, jnp.zeros((n, rest), F32)], axis=-1)
    return jnp.tile(c, (1, LANES // dim)), jnp.tile(s, (1, LANES // dim))


def _split_kernel(k2_ref, out_ref):
    k2 = k2_ref[0]
    hi = k2.astype(BF16)
    out_ref[0, :, 0:LANES] = hi
    out_ref[0, :, LANES:2 * LANES] = (k2 - hi.astype(F32)).astype(BF16)


def _split_keys(ki2, ts):
    B, Lp, _ = ki2.shape
    rows = ts * max(1, SPLIT_ROWS // ts)
    rows = rows if Lp % rows == 0 else ts
    return pl.pallas_call(
        _split_kernel, grid=(B, Lp // rows),
        in_specs=[pl.BlockSpec((1, rows, LANES), lambda b, t: (b, t, 0))],
        out_specs=pl.BlockSpec((1, rows, 2 * LANES), lambda b, t: (b, t, 0)),
        out_shape=jax.ShapeDtypeStruct((B, Lp, 2 * LANES), BF16), name="split_keys",
        compiler_params=pltpu.CompilerParams(dimension_semantics=("arbitrary", "arbitrary")),
    )(ki2)


def _key_of(x):
    b = pltpu.bitcast(x, I32)
    return b ^ ((b >> 31) & 0x7FFFFFFF)


def _float_of(k):
    return pltpu.bitcast(k ^ ((k >> 31) & 0x7FFFFFFF), F32)


def _row_sum(x):
    return jnp.broadcast_to(jnp.sum(x, axis=-1, keepdims=True), x.shape)


def _attn_kernel(qs_ref, kiw_ref, q_ref, ki_ref, kt_ref, v_ref, o_ref,
                 sc_ref, mm_ref, wb_ref, s_ref, m_ref, acc_ref, bias_ref, p_ref, alpha_ref,
                 *, tq, ts, q0, n_keys, k_sel):
    qi = pl.program_id(1)
    n_tiles_total = ki_ref.shape[1] // ts
    nc = ts // LANES
    nt_dims = (((1,), (1,)), ((), ()))

    def key_rows(t):
        return pl.ds(pl.multiple_of(t * ts, ts), ts)

    row = lax.broadcasted_iota(I32, (tq, LANES), 0)
    lane = lax.broadcasted_iota(I32, (tq, LANES), 1)
    qpos = q0 + qi * tq + row
    n_adm = jnp.minimum((qpos // CHUNK + 1) * CHUNK, n_keys)
    last_adm = jnp.minimum(((q0 + (qi + 1) * tq - 1) // CHUNK + 1) * CHUNK, n_keys)
    n_t = jnp.minimum((last_adm + ts - 1) // ts, n_tiles_total)

    kiw = kiw_ref[0]
    for hd in range(N_IDX_HEADS):
        wb_ref[hd] = jnp.broadcast_to(kiw[:, IDX_DIM + hd:IDX_DIM + hd + 1], (tq, LANES))
    qs2 = qs_ref[0].reshape(N_IDX_HEADS * tq, 4 * IDX_DIM)

    def idx_logits(t, slot):
        s_ref[slot] = lax.dot_general(qs2, ki_ref[0, key_rows(t), :], nt_dims,
                                      preferred_element_type=F32)

    rc = min(tq, ROW_CHUNK)
    mm_ref[0] = jnp.full((tq, LANES), _POS_INF, F32)
    mm_ref[1] = jnp.full((tq, LANES), _NEG_INF, F32)

    def score_tile(t, slot, masked):
        for r0 in range(0, tq, rc):
            rows = slice(r0, r0 + rc)
            mn, mx = mm_ref[0, rows], mm_ref[1, rows]
            for c in range(nc):
                cs = slice(c * LANES, (c + 1) * LANES)
                acc = None
                for hd in range(N_IDX_HEADS):
                    r = jnp.maximum(s_ref[slot, hd * tq + r0:hd * tq + r0 + rc, cs], 0.0) * wb_ref[hd, rows]
                    acc = r if acc is None else acc + r
                key = _key_of(acc)
                if masked:
                    qpos_c = q0 + qi * tq + r0 + lax.broadcasted_iota(I32, (rc, LANES), 0)
                    n_adm_c = jnp.minimum((qpos_c // CHUNK + 1) * CHUNK, n_keys)
                    adm = (t * ts + c * LANES + lax.broadcasted_iota(I32, (rc, LANES), 1)) < n_adm_c
                    key = jnp.where(adm, key, _INADMISSIBLE)
                    mx = jnp.maximum(mx, jnp.where(adm, acc, _NEG_INF))
                    mn = jnp.minimum(mn, jnp.where(adm, acc, _POS_INF))
                else:
                    mx = jnp.maximum(mx, acc)
                    mn = jnp.minimum(mn, acc)
                sc_ref[t, rows, cs] = key
            mm_ref[0, rows] = mn
            mm_ref[1, rows] = mx

    n_full = jnp.minimum(jnp.minimum(((q0 + qi * tq) // CHUNK + 1) * CHUNK, n_keys) // ts, n_t)
    last_full = jnp.maximum(n_full - 1, 0)
    idx_logits(0, 0)

    def score_pair(u, carry):
        t0 = 2 * u
        t1 = jnp.minimum(t0 + 1, last_full)
        idx_logits(t1, 1)
        score_tile(t0, 0, False)
        idx_logits(jnp.minimum(t0 + 2, last_full), 0)
        score_tile(t1, 1, False)
        return carry

    lax.fori_loop(0, (n_full + 1) // 2, score_pair, 0)

    def score_tail(t, carry):
        idx_logits(t, 0)
        score_tile(t, 0, True)
        return carry

    lax.fori_loop(n_full, n_t, score_tail, 0)
    rmin = jnp.broadcast_to(jnp.min(mm_ref[0], axis=-1, keepdims=True), (tq, LANES))
    rmax = jnp.broadcast_to(jnp.max(mm_ref[1], axis=-1, keepdims=True), (tq, LANES))

    def count_ge(thr):
        def body(t, cnt):
            for c in range(nc):
                cnt = cnt + jnp.where(sc_ref[t, :, c * LANES:(c + 1) * LANES] >= thr, 1.0, 0.0)
            return cnt
        return _row_sum(lax.fori_loop(0, n_t, body, jnp.zeros((tq, LANES), F32)))

    def bis_cond(st):
        return st[0] > 0

    kf = float(k_sel)
    take_all = n_adm <= k_sel
    lo0 = _key_of(rmin) - 1
    hi0 = _key_of(rmax) + 2
    c_pos = count_ge(jnp.full((tq, LANES), _KEY_MIN_NORMAL, I32))
    c_nn = count_ge(jnp.full((tq, LANES), _KEY_NEG_ZERO, I32))
    pos = c_pos >= kf
    neg = c_nn < kf
    lo = jnp.where(pos, _KEY_MIN_NORMAL, jnp.where(neg, lo0, _KEY_NEG_ZERO))
    hi = jnp.where(pos, hi0, jnp.where(neg, _KEY_NEG_ZERO, _KEY_MIN_NORMAL))
    clo = jnp.where(pos, c_pos, jnp.where(neg, n_adm.astype(F32), c_nn))
    chi = jnp.where(pos, 0.0, jnp.where(neg, c_nn, c_pos))
    exact0 = jnp.where(clo == kf, 1, 0)
    act0 = jnp.where(take_all, 0, jnp.where(exact0 > 0, 0, jnp.where(hi > lo + 1, 1, 0)))

    def bis_body(st):
        _, it, lo, hi, clo, chi, exact, act = st
        kmid = (lo & hi) + ((lo ^ hi) >> 1)
        vmid = _key_of((_float_of(lo) + _float_of(hi)) * 0.5)
        by_value = jnp.where(it < VALUE_PIVOT_STEPS, jnp.where(vmid > lo, jnp.where(vmid < hi, 1, 0), 0), 0)
        mid = jnp.where(by_value > 0, vmid, kmid)
        cnt = count_ge(mid)
        up = jnp.where(act > 0, jnp.where(cnt >= kf, 1, 0), 0)
        dn = act - up
        lo = jnp.where(up > 0, mid, lo)
        clo = jnp.where(up > 0, cnt, clo)
        hi = jnp.where(dn > 0, mid, hi)
        chi = jnp.where(dn > 0, cnt, chi)
        hit = jnp.where(up > 0, jnp.where(cnt == kf, 1, 0), 0)
        exact = exact + hit
        act = jnp.where(act > 0, jnp.where(hit > 0, 0, jnp.where(hi > lo + 1, 1, 0)), 0)
        return jnp.max(act), it + 1, lo, hi, clo, chi, exact, act

    _, _, lo, _, clo, chi, exact, _ = lax.while_loop(
        bis_cond, bis_body, (jnp.max(act0), jnp.int32(0), lo, hi, clo, chi, exact0, act0))
    thr = jnp.where(take_all, _INADMISSIBLE + 1, lo)
    thr_up = jnp.where(take_all, _INADMISSIBLE + 1, lo + 1)

    need = kf - chi
    tied = jnp.where(take_all, 0, jnp.where(exact > 0, 0, jnp.where(clo > kf, 1, 0)))

    def count_tied_below(cut):
        def body(t, cnt):
            for c in range(nc):
                s = sc_ref[t, :, c * LANES:(c + 1) * LANES]
                col = t * ts + c * LANES + lane
                cnt = cnt + jnp.where(s == thr, jnp.where(col < cut, 1.0, 0.0), 0.0)
            return cnt
        return _row_sum(lax.fori_loop(0, n_t, body, jnp.zeros((tq, LANES), F32)))

    def cut_body(st):
        _, lo_c, hi_c, act = st
        mid = (lo_c + hi_c) >> 1
        cnt = count_tied_below(mid)
        on = act > 0
        le = cnt <= need
        lo_c = jnp.where(on, jnp.where(le, mid, lo_c), lo_c)
        hi_c = jnp.where(on, jnp.where(le, hi_c, mid), hi_c)
        act = jnp.where(on, jnp.where(hi_c - lo_c > 1, 1, 0), 0)
        return jnp.max(act), lo_c, hi_c, act

    _, cut_lo, _, _ = lax.while_loop(
        bis_cond, cut_body,
        (jnp.max(tied), jnp.zeros((tq, LANES), I32), jnp.full((tq, LANES), n_tiles_total * ts + 1, I32), tied))
    cut = jnp.where(tied > 0, cut_lo, n_tiles_total * ts + 1)

    m_ref[...] = jnp.full(m_ref.shape, _M_INIT, F32)
    acc_ref[...] = jnp.zeros(acc_ref.shape, F32)
    rc = min(tq, ROW_CHUNK)
    ones = jnp.ones((ts, LANES), BF16)
    last = n_t - 1

    def logits(t, slot):
        for g in range(N_KV_HEADS):
            qg = q_ref[0, g * GQA_GROUP:(g + 1) * GQA_GROUP].reshape(GQA_GROUP * tq, HEAD_DIM)
            s_ref[slot, g * GQA_GROUP * tq:(g + 1) * GQA_GROUP * tq] = lax.dot_general(
                qg, kt_ref[0, key_rows(t), g * HEAD_DIM:(g + 1) * HEAD_DIM], nt_dims,
                preferred_element_type=F32)

    def softmax_pv(t, slot, live):
        for c in range(nc):
            col = t * ts + c * LANES + lane
            bound = jnp.where(live, jnp.where(col < cut, thr, thr_up), _NO_KEY)
            bias_ref[slot, :, c * LANES:(c + 1) * LANES] = jnp.where(
                sc_ref[t, :, c * LANES:(c + 1) * LANES] >= bound, 0.0, _NEG_INF)
        for g in range(N_KV_HEADS):
            for j in range(GQA_GROUP):
                hd = g * GQA_GROUP + j
                for r0 in range(0, tq, rc):
                    r1 = j * tq + r0
                    r2 = hd * tq + r0
                    s = [s_ref[slot, r2:r2 + rc, c * LANES:(c + 1) * LANES]
                         + bias_ref[slot, r0:r0 + rc, c * LANES:(c + 1) * LANES] for c in range(nc)]
                    mx = s[0]
                    for c in range(1, nc):
                        mx = jnp.maximum(mx, s[c])
                    m_old = m_ref[hd, r0:r0 + rc]
                    m_new = jnp.maximum(m_old, jnp.broadcast_to(jnp.max(mx, axis=-1, keepdims=True), (rc, LANES)))
                    alpha_ref[slot, g, r1:r1 + rc] = jnp.exp2(m_old - m_new)
                    for c in range(nc):
                        p_ref[slot, g, r1:r1 + rc, c * LANES:(c + 1) * LANES] = jnp.exp2(s[c] - m_new).astype(BF16)
                    m_ref[hd, r0:r0 + rc] = m_new
            v_aug = jnp.concatenate([v_ref[0, key_rows(t), g * HEAD_DIM:(g + 1) * HEAD_DIM], ones], axis=-1)
            pv = jnp.dot(p_ref[slot, g], v_aug, preferred_element_type=F32)
            alpha = alpha_ref[slot, g]
            for half in range(2):
                cols = slice(half * LANES, (half + 1) * LANES)
                acc_ref[g, :, cols] = alpha * acc_ref[g, :, cols] + pv[:, cols]

    logits(0, 0)

    def attn_pair(u, carry):
        t0 = 2 * u
        t1 = jnp.minimum(t0 + 1, last)
        logits(t1, 1)
        softmax_pv(t0, 0, True)
        logits(jnp.minimum(t0 + 2, last), 0)
        softmax_pv(t1, 1, t0 + 1 < n_t)
        return carry

    lax.fori_loop(0, (n_t + 1) // 2, attn_pair, 0)
    for hd in range(N_HEADS):
        g, j = divmod(hd, GQA_GROUP)
        o_ref[0, :, hd * HEAD_DIM:(hd + 1) * HEAD_DIM] = (
            acc_ref[g, j * tq:(j + 1) * tq, 0:HEAD_DIM] / acc_ref[g, j * tq:(j + 1) * tq, HEAD_DIM:2 * HEAD_DIM])


def _attn_t_kernel(qst_ref, wrow_ref, q_ref, ki_ref, kt_ref, v_ref, o_ref,
                   sc_ref, mm_ref, st_ref, s_ref, m_ref, acc_ref, bias_ref, p_ref, alpha_ref, kmax_ref,
                   *, tq, ts, q0, n_keys, k_sel):
    qi = pl.program_id(1)
    n_tiles_total = ki_ref.shape[1]
    nc = ts // LANES
    rk = KEY_ROWS
    qpos = q0 + qi * tq + lax.broadcasted_iota(I32, (1, tq), 1)
    n_adm = jnp.minimum((qpos // CHUNK + 1) * CHUNK, n_keys)
    last_adm = jnp.minimum(((q0 + (qi + 1) * tq - 1) // CHUNK + 1) * CHUNK, n_keys)
    n_t = jnp.minimum((last_adm + ts - 1) // ts, n_tiles_total)
    n_full = jnp.minimum(jnp.minimum(((q0 + qi * tq) // CHUNK + 1) * CHUNK, n_keys) // ts, n_t)
    last_full = jnp.maximum(n_full - 1, 0)
    qst = qst_ref[0, 0]
    wr = wrow_ref[0, 0]

    mm_ref[0] = jnp.full((rk, tq), _POS_INF, F32)
    mm_ref[1] = jnp.full((rk, tq), _NEG_INF, F32)

    def idx_logits(t, slot):
        st_ref[slot] = jnp.dot(ki_ref[0, t], qst, preferred_element_type=F32)

    def score_tile(t, slot, masked):
        mn, mx = mm_ref[0], mm_ref[1]
        for r0 in range(0, ts, rk):
            acc = None
            for hd in range(N_IDX_HEADS):
                r = jnp.maximum(st_ref[slot, r0:r0 + rk, hd * tq:(hd + 1) * tq], 0.0) * wr[hd:hd + 1, :]
                acc = r if acc is None else acc + r
            key = _key_of(acc)
            if masked:
                adm = (t * ts + r0 + lax.broadcasted_iota(I32, (rk, tq), 0)) < n_adm
                key = jnp.where(adm, key, _INADMISSIBLE)
                mx = jnp.maximum(mx, jnp.where(adm, acc, _NEG_INF))
                mn = jnp.minimum(mn, jnp.where(adm, acc, _POS_INF))
            else:
                mx = jnp.maximum(mx, acc)
                mn = jnp.minimum(mn, acc)
            sc_ref[t, r0:r0 + rk, :] = key
        mm_ref[0] = mn
        mm_ref[1] = mx

    idx_logits(0, 0)

    def score_pair(u, carry):
        t0 = 2 * u
        t1 = jnp.minimum(t0 + 1, last_full)
        idx_logits(t1, 1)
        score_tile(t0, 0, False)
        idx_logits(jnp.minimum(t0 + 2, last_full), 0)
        score_tile(t1, 1, False)
        return carry

    lax.fori_loop(0, (n_full + 1) // 2, score_pair, 0)

    def score_tail(t, carry):
        idx_logits(t, 0)
        score_tile(t, 0, True)
        return carry

    lax.fori_loop(n_full, n_t, score_tail, 0)
    rmin = jnp.min(mm_ref[0], axis=0, keepdims=True)
    rmax = jnp.max(mm_ref[1], axis=0, keepdims=True)

    def count_ge(thr):
        def body(t, cnt):
            for r0 in range(0, ts, rk):
                cnt = cnt + jnp.where(sc_ref[t, r0:r0 + rk, :] >= thr, 1.0, 0.0)
            return cnt
        return jnp.sum(lax.fori_loop(0, n_t, body, jnp.zeros((rk, tq), F32)), axis=0, keepdims=True)

    def bis_cond(st):
        return st[0] > 0

    kf = float(k_sel)
    take_all = n_adm <= k_sel
    lo0 = _key_of(rmin) - 1
    hi0 = _key_of(rmax) + 2
    c_pos = count_ge(jnp.full((1, tq), _KEY_MIN_NORMAL, I32))
    c_nn = count_ge(jnp.full((1, tq), _KEY_NEG_ZERO, I32))
    pos = c_pos >= kf
    neg = c_nn < kf
    lo = jnp.where(pos, _KEY_MIN_NORMAL, jnp.where(neg, lo0, _KEY_NEG_ZERO))
    hi = jnp.where(pos, hi0, jnp.where(neg, _KEY_NEG_ZERO, _KEY_MIN_NORMAL))
    clo = jnp.where(pos, c_pos, jnp.where(neg, n_adm.astype(F32), c_nn))
    chi = jnp.where(pos, 0.0, jnp.where(neg, c_nn, c_pos))
    exact0 = jnp.where(clo == kf, 1, 0)
    act0 = jnp.where(take_all, 0, jnp.where(exact0 > 0, 0, jnp.where(hi > lo + 1, 1, 0)))

    def bis_step(it, lo, hi, clo, chi, exact, act):
        kmid = (lo & hi) + ((lo ^ hi) >> 1)
        vmid = _key_of((_float_of(lo) + _float_of(hi)) * 0.5)
        by_value = jnp.where(it < VALUE_PIVOT_STEPS, jnp.where(vmid > lo, jnp.where(vmid < hi, 1, 0), 0), 0)
        mid = jnp.where(by_value > 0, vmid, kmid)
        cnt = count_ge(mid)
        up = jnp.where(act > 0, jnp.where(cnt >= kf, 1, 0), 0)
        dn = act - up
        lo = jnp.where(up > 0, mid, lo)
        clo = jnp.where(up > 0, cnt, clo)
        hi = jnp.where(dn > 0, mid, hi)
        chi = jnp.where(dn > 0, cnt, chi)
        hit = jnp.where(up > 0, jnp.where(cnt == kf, 1, 0), 0)
        exact = exact + hit
        act = jnp.where(act > 0, jnp.where(hit > 0, 0, jnp.where(hi > lo + 1, 1, 0)), 0)
        return it + 1, lo, hi, clo, chi, exact, act

    def any_active(act):
        return jnp.max(act.astype(F32))

    def bis_body(st):
        st = bis_step(*bis_step(*st[1:]))
        return (any_active(st[-1]),) + st

    _, _, lo, _, clo, chi, exact, _ = lax.while_loop(
        bis_cond, bis_body, (any_active(act0), jnp.int32(0), lo, hi, clo, chi, exact0, act0))
    thr = jnp.where(take_all, _INADMISSIBLE + 1, lo)
    thr_up = jnp.where(take_all, _INADMISSIBLE + 1, lo + 1)

    need = kf - chi
    tied = jnp.where(take_all, 0, jnp.where(exact > 0, 0, jnp.where(clo > kf, 1, 0)))

    def count_tied_below(cut):
        def body(t, cnt):
            for r0 in range(0, ts, rk):
                kidx = t * ts + r0 + lax.broadcasted_iota(I32, (rk, tq), 0)
                cnt = cnt + jnp.where(sc_ref[t, r0:r0 + rk, :] == thr, jnp.where(kidx < cut, 1.0, 0.0), 0.0)
            return cnt
        return jnp.sum(lax.fori_loop(0, n_t, body, jnp.zeros((rk, tq), F32)), axis=0, keepdims=True)

    def cut_body(st):
        _, lo_c, hi_c, act = st
        mid = (lo_c + hi_c) >> 1
        cnt = count_tied_below(mid)
        on = act > 0
        le = cnt <= need
        lo_c = jnp.where(on, jnp.where(le, mid, lo_c), lo_c)
        hi_c = jnp.where(on, jnp.where(le, hi_c, mid), hi_c)
        act = jnp.where(on, jnp.where(hi_c - lo_c > 1, 1, 0), 0)
        return jnp.max(act), lo_c, hi_c, act

    _, cut_lo, _, _ = lax.while_loop(
        bis_cond, cut_body,
        (jnp.max(tied), jnp.zeros((1, tq), I32), jnp.full((1, tq), n_tiles_total * ts + 1, I32), tied))
    cut = jnp.where(tied > 0, cut_lo, n_tiles_total * ts + 1)

    acc_ref[...] = jnp.zeros(acc_ref.shape, F32)
    rc = min(tq, ROW_CHUNK)
    ones = jnp.ones((ts, LANES), BF16)
    last = n_t - 1

    def logits(t, slot):
        for g in range(N_KV_HEADS):
            qg = q_ref[0, g * GQA_GROUP:(g + 1) * GQA_GROUP].reshape(GQA_GROUP * tq, HEAD_DIM)
            s_ref[slot, g * GQA_GROUP * tq:(g + 1) * GQA_GROUP * tq] = jnp.dot(
                qg, kt_ref[0, g, t], preferred_element_type=F32)

    def selection_bias(t, c, live, masked_value):
        kidx = t * ts + c * LANES + lax.broadcasted_iota(I32, (LANES, tq), 0)
        bound = jnp.where(live, jnp.where(kidx < cut, thr, thr_up), _NO_KEY)
        return jnp.where(sc_ref[t, c * LANES:(c + 1) * LANES, :] >= bound, 0.0, masked_value).T

    def value_matmul(t, slot, g):
        v_aug = jnp.concatenate([v_ref[0, g, t], ones], axis=-1)
        return jnp.dot(p_ref[slot, g], v_aug, preferred_element_type=F32)

    def softmax_pv(t, slot, live):
        for c in range(nc):
            bias_ref[slot, :, c * LANES:(c + 1) * LANES] = selection_bias(t, c, live, _NEG_INF)
        for g in range(N_KV_HEADS):
            for j in range(GQA_GROUP):
                hd = g * GQA_GROUP + j
                for r0 in range(0, tq, rc):
                    r1 = j * tq + r0
                    r2 = hd * tq + r0
                    s = [s_ref[slot, r2:r2 + rc, c * LANES:(c + 1) * LANES]
                         + bias_ref[slot, r0:r0 + rc, c * LANES:(c + 1) * LANES] for c in range(nc)]
                    mx = s[0]
                    for c in range(1, nc):
                        mx = jnp.maximum(mx, s[c])
                    m_old = m_ref[hd, r0:r0 + rc]
                    m_new = jnp.maximum(m_old, jnp.broadcast_to(jnp.max(mx, axis=-1, keepdims=True), (rc, LANES)))
                    alpha_ref[slot, g, r1:r1 + rc] = jnp.exp2(m_old - m_new)
                    m_ref[hd, r0:r0 + rc] = m_new
                    for c in range(nc):
                        p_ref[slot, g, r1:r1 + rc, c * LANES:(c + 1) * LANES] = jnp.exp2(s[c] - m_new).astype(BF16)
            pv = value_matmul(t, slot, g)
            for half in range(2):
                cols = slice(half * LANES, (half + 1) * LANES)
                acc_ref[g, :, cols] = alpha_ref[slot, g] * acc_ref[g, :, cols] + pv[:, cols]

    def attend_running():
        logits(0, 0)

        def attn_pair(u, carry):
            t0 = 2 * u
            t1 = jnp.minimum(t0 + 1, last)
            logits(t1, 1)
            softmax_pv(t0, 0, True)
            logits(jnp.minimum(t0 + 2, last), 0)
            softmax_pv(t1, 1, t0 + 1 < n_t)
            return carry

        lax.fori_loop(0, (n_t + 1) // 2, attn_pair, 0)

    def softmax_pv_fixed(t, slot, live):
        for c in range(nc):
            bias_ref[slot, :, c * LANES:(c + 1) * LANES] = selection_bias(t, c, live, _NEG_INF)
        for g in range(N_KV_HEADS):
            for j in range(GQA_GROUP):
                hd = g * GQA_GROUP + j
                for r0 in range(0, tq, rc):
                    ref_pt = m_ref[hd, r0:r0 + rc]
                    for c in range(nc):
                        cs = slice(c * LANES, (c + 1) * LANES)
                        s = s_ref[slot, hd * tq + r0:hd * tq + r0 + rc, cs] + bias_ref[slot, r0:r0 + rc, cs]
                        p_ref[slot, g, j * tq + r0:j * tq + r0 + rc, cs] = jnp.exp2(s - ref_pt).astype(BF16)
            pv = value_matmul(t, slot, g)
            for half in range(2):
                cols = slice(half * LANES, (half + 1) * LANES)
                acc_ref[g, :, cols] = acc_ref[g, :, cols] + pv[:, cols]

    def attend_fixed():
        logits(0, 0)

        def attn_pair(u, carry):
            t0 = 2 * u
            t1 = jnp.minimum(t0 + 1, last)
            logits(t1, 1)
            softmax_pv_fixed(t0, 0, True)
            logits(jnp.minimum(t0 + 2, last), 0)
            softmax_pv_fixed(t1, 1, t0 + 1 < n_t)
            return carry

        lax.fori_loop(0, (n_t + 1) // 2, attn_pair, 0)

    @pl.when(qi == 0)
    def _():
        for g in range(N_KV_HEADS):
            def sq_norm_max(t, best):
                kk = kt_ref[0, g, t].astype(F32)
                return jnp.maximum(best, jnp.sum(kk * kk, axis=0, keepdims=True))
            best = lax.fori_loop(0, n_tiles_total, sq_norm_max, jnp.zeros((1, ts), F32))
            kmax_ref[g] = jnp.broadcast_to(jnp.max(best, axis=-1, keepdims=True), (8, LANES))

    top = jnp.zeros((tq, LANES), F32)
    for hd in range(N_HEADS):
        qf = q_ref[0, hd].astype(F32)
        q_sq = _row_sum(qf * qf)
        ref_pt = jnp.sqrt(q_sq * kmax_ref[hd // GQA_GROUP, 0:1, :]) * 1.001 + 1e-3
        m_ref[hd] = ref_pt
        top = jnp.maximum(top, ref_pt)
    fits = jnp.max(top) <= MAX_FIXED_REFERENCE

    @pl.when(fits)
    def _():
        attend_fixed()

    @pl.when(jnp.logical_not(fits))
    def _():
        m_ref[...] = jnp.full(m_ref.shape, _M_INIT, F32)
        attend_running()

    for hd in range(N_HEADS):
        g, j = divmod(hd, GQA_GROUP)
        o_ref[0, :, hd * HEAD_DIM:(hd + 1) * HEAD_DIM] = (
            acc_ref[g, j * tq:(j + 1) * tq, 0:HEAD_DIM] / acc_ref[g, j * tq:(j + 1) * tq, HEAD_DIM:2 * HEAD_DIM])


def _attend_t(qst, wrow, q_att, ki4, kt, v, q0, n_keys, tq, ts):
    B, _, T, _ = q_att.shape
    n_tiles = ki4.shape[1]
    k_sel = min(TOPK_MAX, n_keys // 4)
    kern = functools.partial(_attn_t_kernel, tq=tq, ts=ts, q0=q0, n_keys=n_keys, k_sel=k_sel)
    return pl.pallas_call(
        kern, grid=(B, T // tq),
        in_specs=[
            pl.BlockSpec((1, 1, 4 * IDX_DIM, N_IDX_HEADS * tq), lambda b, i: (b, i, 0, 0)),
            pl.BlockSpec((1, 1, N_IDX_HEADS, tq), lambda b, i: (b, i, 0, 0)),
            pl.BlockSpec((1, N_HEADS, tq, HEAD_DIM), lambda b, i: (b, 0, i, 0)),
            _resident((1, n_tiles, ts, 4 * IDX_DIM), lambda b, i: (b, 0, 0, 0)),
            _resident((1, N_KV_HEADS, n_tiles, HEAD_DIM, ts), lambda b, i: (b, 0, 0, 0, 0)),
            _resident((1, N_KV_HEADS, n_tiles, ts, HEAD_DIM), lambda b, i: (b, 0, 0, 0, 0)),
        ],
        out_specs=pl.BlockSpec((1, tq, N_HEADS * HEAD_DIM), lambda b, i: (b, i, 0)),
        out_shape=jax.ShapeDtypeStruct((B, T, N_HEADS * HEAD_DIM), F32),
        scratch_shapes=[
            pltpu.VMEM((n_tiles, ts, tq), I32),
            pltpu.VMEM((2, KEY_ROWS, tq), F32),
            pltpu.VMEM((2, ts, N_IDX_HEADS * tq), F32),
            pltpu.VMEM((2, N_HEADS * tq, ts), F32),
            pltpu.VMEM((N_HEADS, tq, LANES), F32),
            pltpu.VMEM((N_KV_HEADS, GQA_GROUP * tq, 2 * HEAD_DIM), F32),
            pltpu.VMEM((2, tq, ts), F32),
            pltpu.VMEM((2, N_KV_HEADS, GQA_GROUP * tq, ts), BF16),
            pltpu.VMEM((2, N_KV_HEADS, GQA_GROUP * tq, LANES), F32),
            pltpu.VMEM((N_KV_HEADS, 8, LANES), F32),
        ],
        name="attend_t",
        compiler_params=pltpu.CompilerParams(dimension_semantics=("arbitrary", "arbitrary"),
                                             vmem_limit_bytes=VMEM_LIMIT),
    )(qst, wrow, q_att, ki4, kt, v)


def _attend(qs, kiw, q_att, ki4, kt, v, q0, n_keys, tq, ts):
    B, _, T, _ = qs.shape
    lp = ki4.shape[1]
    n_tiles = lp // ts
    k_sel = min(TOPK_MAX, n_keys // 4)
    kern = functools.partial(_attn_kernel, tq=tq, ts=ts, q0=q0, n_keys=n_keys, k_sel=k_sel)
    keys = pl.BlockSpec((1, lp, 2 * LANES), lambda b, i: (b, 0, 0))
    return pl.pallas_call(
        kern, grid=(B, T // tq),
        in_specs=[
            pl.BlockSpec((1, N_IDX_HEADS, tq, 4 * IDX_DIM), lambda b, i: (b, 0, i, 0)),
            pl.BlockSpec((1, tq, LANES), lambda b, i: (b, i, 0)),
            pl.BlockSpec((1, N_HEADS, tq, HEAD_DIM), lambda b, i: (b, 0, i, 0)),
            keys, keys, keys,
        ],
        out_specs=pl.BlockSpec((1, tq, N_HEADS * HEAD_DIM), lambda b, i: (b, i, 0)),
        out_shape=jax.ShapeDtypeStruct((B, T, N_HEADS * HEAD_DIM), F32),
        scratch_shapes=[
            pltpu.VMEM((n_tiles, tq, ts), I32),
            pltpu.VMEM((2, tq, LANES), F32),
            pltpu.VMEM((N_IDX_HEADS, tq, LANES), F32),
            pltpu.VMEM((2, N_HEADS * tq, ts), F32),
            pltpu.VMEM((N_HEADS, tq, LANES), F32),
            pltpu.VMEM((N_KV_HEADS, GQA_GROUP * tq, 2 * HEAD_DIM), F32),
            pltpu.VMEM((2, tq, ts), F32),
            pltpu.VMEM((2, N_KV_HEADS, GQA_GROUP * tq, ts), BF16),
            pltpu.VMEM((2, N_KV_HEADS, GQA_GROUP * tq, LANES), F32),
        ],
        name="attend",
        compiler_params=pltpu.CompilerParams(dimension_semantics=("arbitrary", "arbitrary"),
                                             vmem_limit_bytes=VMEM_LIMIT),
    )(qs, kiw, q_att, ki4, kt, v)


def _rms(x, g):
    return x * lax.rsqrt(jnp.mean(x * x, axis=-1, keepdims=True) + RMS_EPS) * g


def _out_kernel(x_ref, pa_ref, sgb_ref, b_ref, wo_ref, g1_ref, g2_ref, wgu_ref, wd_ref, g3_ref, y_ref, *, d_ff):
    merged = pa_ref[...] + sgb_ref[...] * b_ref[...]
    mix = jnp.dot(merged.astype(BF16), wo_ref[...], preferred_element_type=F32)
    x1 = x_ref[...] + _rms(mix, g1_ref[...])
    h2 = _rms(x1, g2_ref[...]).astype(BF16)
    gu = jnp.dot(h2, wgu_ref[...], preferred_element_type=F32)
    gate, up = gu[:, :d_ff], gu[:, d_ff:]
    act = (gate * _sigmoid(gate) * up).astype(BF16)
    f = jnp.dot(act, wd_ref[...], preferred_element_type=F32)
    y_ref[...] = x1 + _rms(f, g3_ref[...])


def _finish(x, pa, sgb, b, w_out, g_post, g_ffn_pre, w_gate_up, w_down, g_ffn_post, tm):
    N, D = x.shape
    d_ff = w_down.shape[0]
    row = pl.BlockSpec((tm, D), lambda i: (i, 0))
    const = lambda i: (0, 0)
    return pl.pallas_call(
        functools.partial(_out_kernel, d_ff=d_ff), grid=(N // tm,),
        in_specs=[row, row, row, row,
                  _resident((D, D), const), _resident((1, D), const), _resident((1, D), const),
                  _resident((D, 2 * d_ff), const), _resident((d_ff, D), const), _resident((1, D), const)],
        out_specs=row, out_shape=jax.ShapeDtypeStruct((N, D), F32), name="finish",
        compiler_params=pltpu.CompilerParams(dimension_semantics=("arbitrary",), vmem_limit_bytes=VMEM_LIMIT),
    )(x, pa, sgb, b, w_out, g_post, g_ffn_pre, w_gate_up, w_down, g_ffn_post)


def _tile_keys(n_keys, ts):
    return -(-n_keys // ts) * ts


def _layer(x, pos0, hist, k_cache, v_cache, ki_cache, wts, tm, tq, ts, tm_out):
    (w_in_r, w_pool, pool_scale, w_out, w_gate_up, w_down, g_pre, g_post, g_ffn_pre, g_ffn_post) = wts
    B, T, D = x.shape
    hist16 = jnp.concatenate([jnp.zeros((B, HALO // 2 - POOL_HIST, POOL_WIDTH), F32), hist], axis=1)
    own_keys_only = k_cache is None and T % ts == 0 and ts % tm == 0 and tq == LANES and tm % LANES == 0
    outs = _project(x, hist16, pos0, w_in_r, g_pre, w_pool, pool_scale, tm, ts if own_keys_only else None)
    u, q_att, k, v, qs, kiw, pa, sgb = outs[:8]
    ki = kiw[:, :, :IDX_DIM]
    if own_keys_only:
        kt, vt, ki4, wrow = outs[8:]
        b = _attend_t(qs, wrow, q_att, ki4, kt, vt, pos0, T, tq, ts)
    else:
        if k_cache is None:
            k_all, v_all, ki_all = k, v, ki
        else:
            k_all = jnp.concatenate([k_cache.reshape(B, -1, N_KV_HEADS * HEAD_DIM), k], axis=1)
            v_all = jnp.concatenate([v_cache.reshape(B, -1, N_KV_HEADS * HEAD_DIM), v], axis=1)
            ki_all = jnp.concatenate([ki_cache, ki], axis=1)
        n_keys = k_all.shape[1]
        lp = _tile_keys(n_keys, ts)
        pad = ((0, 0), (0, lp - n_keys), (0, 0))
        kt = jnp.pad(k_all.astype(BF16), pad)
        vt = jnp.pad(v_all.astype(BF16), pad)
        ki4 = _split_keys(jnp.pad(jnp.concatenate([ki_all, ki_all], axis=-1), pad), ts)
        b = _attend(qs, kiw, q_att, ki4, kt, vt, pos0, n_keys, tq, ts)
    y = _finish(x.reshape(B * T, D), pa.reshape(B * T, D), sgb.reshape(B * T, D), b.reshape(B * T, D),
                w_out, g_post, g_ffn_pre, w_gate_up, w_down, g_ffn_post, tm_out).reshape(B, T, D)
    new_pool = jnp.concatenate([hist, u], axis=1)[:, T:]
    return (y, k.reshape(B, T, N_KV_HEADS, HEAD_DIM), v.reshape(B, T, N_KV_HEADS, HEAD_DIM), ki, new_pool)


def _relayout_w_in(w_in):
    d = w_in.shape[0]
    o_kiw = POOL_WIDTH + N_HEADS * HEAD_DIM + 2 * N_KV_HEADS * HEAD_DIM + N_IDX_HEADS * IDX_DIM
    narrow = IDX_DIM + N_IDX_HEADS
    padded = jnp.concatenate([w_in[:, :o_kiw + narrow], jnp.zeros((d, LANES - narrow), w_in.dtype),
                              w_in[:, o_kiw + narrow:]], axis=1)
    return padded.astype(BF16)


def kernel(x_prompt, x_sample, cache_k, cache_v, cache_k_idx, state_pool, w_in, w_pool, pool_scale, w_out,
           w_gate_up, w_down, norm_mix_pre, norm_mix_post, norm_ffn_pre, norm_ffn_post):
    depth = w_in.shape[0]
    past = cache_k.shape[2]
    t_p, t_s = x_prompt.shape[1], x_sample.shape[1]
    hist_p = jnp.zeros((x_prompt.shape[0], POOL_HIST, POOL_WIDTH), x_prompt.dtype)
    xp, xs = x_prompt, x_sample
    outs = [[] for _ in range(8)]
    for l in range(depth):
        wts = (_relayout_w_in(w_in[l]), w_pool[l].astype(BF16), pool_scale[l][None, :], w_out[l].astype(BF16),
               w_gate_up[l].astype(BF16), w_down[l].astype(BF16), norm_mix_pre[l][None, :],
               norm_mix_post[l][None, :], norm_ffn_pre[l][None, :], norm_ffn_post[l][None, :])
        tm_p = min(256, t_p)
        tq_p = min(128, t_p)
        xp, k1, v1, ki1, p1 = _layer(xp, 0, hist_p, None, None, None, wts, tm_p, tq_p, 512, tm_p)
        n_s = xs.shape[0] * t_s
        xs, k2, v2, ki2, p2 = _layer(xs, past, state_pool[l], cache_k[l], cache_v[l], cache_k_idx[l], wts,
                                     t_s, t_s, 512, min(256, n_s))
        for lst, val in zip(outs, (k1, v1, ki1, p1, k2, v2, ki2, p2)):
            lst.append(val)
    return (xp, xs) + tuple(jnp.stack(o) for o in outs)
```

```python
import functools

import jax
import jax.numpy as jnp
from jax import lax
from jax.experimental import pallas as pl
from jax.experimental.pallas import tpu as pltpu

F32 = jnp.float32
BF16 = jnp.bfloat16
I32 = jnp.int32

LANES = 128
CHUNK = 64
POOL_WINDOWS = (2, 4, 8, 16)
N_POOL_GROUPS = 4
POOL_GROUP_WIDTH = 128
POOL_WIDTH = N_POOL_GROUPS * POOL_GROUP_WIDTH
POOL_HIST = 15
N_HEADS = 8
N_KV_HEADS = 2
HEAD_DIM = 128
GQA_GROUP = N_HEADS // N_KV_HEADS
ROPE_THETA = 500000.0
N_IDX_HEADS = 8
IDX_DIM = 64
TOPK_MAX = 256
RMS_EPS = 1e-6
ATTN_SCALE = HEAD_DIM ** -0.5
IDX_SCALE = (N_IDX_HEADS ** -0.5) * (IDX_DIM ** -0.5)

LOG2_E = 1.4426950408889634
VALUE_PIVOT_STEPS = 28
MAX_FIXED_REFERENCE = 40.0
SPLIT_ROWS = 1536
KEY_ROWS = 64
ROW_CHUNK = 32
HALO = 32
VMEM_LIMIT = 56 * 1024 * 1024

_NEG_INF = float("-inf")
_POS_INF = float("inf")
_INADMISSIBLE = -(2 ** 31)
_NO_KEY = 2 ** 31 - 1
_KEY_MIN_NORMAL = 0x00800000
_KEY_NEG_ZERO = -1
_M_INIT = -1e30


def _resident(block_shape, index_map):
    return pl.BlockSpec(block_shape, index_map, pipeline_mode=pl.Buffered(1))


def _rope(xs, cos, sin, half, period):
    lane = lax.broadcasted_iota(I32, xs.shape, 1)
    ahead = pltpu.roll(xs, LANES - half, 1)
    behind = pltpu.roll(xs, half, 1)
    partner = jnp.where((lane & (period - 1)) < half, ahead, behind)
    return xs * cos + partner * sin


def _sigmoid(x):
    return 1.0 / (1.0 + jnp.exp(-x))


def _proj_kernel(x_ref, g_ref, w_ref, hist_ref, cq_ref, sq_ref, ci_ref, si_ref, wpool_ref, pscale_ref,
                 u_ref, qatt_ref, k_ref, v_ref, qs_ref, kiw_ref, pa_ref, sgb_ref, *rest, tm, pos0, d_model, key_tiles):
    if key_tiles:
        kt_ref, vt_ref, ki4_ref, wrow_ref, e_ref, s2_ref, s4_ref, s8_ref = rest
    else:
        e_ref, s2_ref, s4_ref, s8_ref = rest
    i = pl.program_id(1)
    x = x_ref[0]
    h = x * lax.rsqrt(jnp.mean(x * x, axis=-1, keepdims=True) + RMS_EPS) * g_ref[...]
    proj = jnp.dot(h.astype(BF16), w_ref[...], preferred_element_type=F32)

    o_q = POOL_WIDTH
    o_k = o_q + N_HEADS * HEAD_DIM
    o_v = o_k + N_KV_HEADS * HEAD_DIM
    o_qi = o_v + N_KV_HEADS * HEAD_DIM
    o_kiw = o_qi + N_IDX_HEADS * IDX_DIM
    o_ga = o_kiw + LANES
    o_gb = o_ga + d_model

    cq, sq, ci, si = cq_ref[...], sq_ref[...], ci_ref[...], si_ref[...]
    half_q = HEAD_DIM // 8
    half_i = IDX_DIM // 8

    for hd in range(N_HEADS):
        qh = _rope(proj[:, o_q + hd * HEAD_DIM:o_q + (hd + 1) * HEAD_DIM], cq, sq, half_q, HEAD_DIM)
        qatt_ref[0, hd] = (qh * (ATTN_SCALE * LOG2_E)).astype(BF16)
    for kh in range(N_KV_HEADS):
        k_h = _rope(proj[:, o_k + kh * HEAD_DIM:o_k + (kh + 1) * HEAD_DIM], cq, sq, half_q, HEAD_DIM)
        k_ref[0, :, kh * HEAD_DIM:(kh + 1) * HEAD_DIM] = k_h
        if key_tiles:
            kt_ref[0, kh, 0] = k_h.T.astype(BF16)
            vt_ref[0, kh, 0] = proj[:, o_v + kh * HEAD_DIM:o_v + (kh + 1) * HEAD_DIM].astype(BF16)
    v_ref[0] = proj[:, o_v:o_qi]

    for pr in range(N_IDX_HEADS // 2):
        qi2 = _rope(proj[:, o_qi + pr * LANES:o_qi + (pr + 1) * LANES], ci, si, half_i, IDX_DIM)
        hi = qi2.astype(BF16).astype(F32)
        lo = qi2 - hi
        lane2 = lax.broadcasted_iota(I32, qi2.shape, 1)
        first = jnp.where(lane2 < IDX_DIM, hi, pltpu.roll(lo, IDX_DIM, 1))
        second = jnp.where(lane2 < IDX_DIM, pltpu.roll(hi, IDX_DIM, 1), lo)
        for sub, slab in enumerate((first, second)):
            hd = 2 * pr + sub
            if key_tiles:
                for sb in range(tm // LANES):
                    slab_t = slab[sb * LANES:(sb + 1) * LANES].T.astype(BF16)
                    qs_ref[0, sb, 0:LANES, hd * LANES:(hd + 1) * LANES] = slab_t
                    qs_ref[0, sb, LANES:2 * LANES, hd * LANES:(hd + 1) * LANES] = slab_t
            else:
                qs_ref[0, hd, :, 0:LANES] = slab.astype(BF16)
                qs_ref[0, hd, :, LANES:2 * LANES] = slab.astype(BF16)

    kiw = _rope(proj[:, o_kiw:o_kiw + LANES], ci, si, half_i, IDX_DIM)
    lane = lax.broadcasted_iota(I32, kiw.shape, 1)
    kiw_full = jnp.where(lane < IDX_DIM, kiw, proj[:, o_kiw:o_kiw + LANES] * IDX_SCALE)
    kiw_ref[0] = kiw_full
    if key_tiles:
        ki_hi = kiw.astype(BF16).astype(F32)
        ki_lo = kiw - ki_hi
        ki4_ref[0, 0, :, 0:LANES] = jnp.where(lane < IDX_DIM, ki_hi, pltpu.roll(ki_hi, IDX_DIM, 1)).astype(BF16)
        ki4_ref[0, 0, :, LANES:2 * LANES] = jnp.where(lane < IDX_DIM, ki_lo, pltpu.roll(ki_lo, IDX_DIM, 1)).astype(BF16)
        for sb in range(tm // LANES):
            wrow_ref[0, sb] = kiw_full[sb * LANES:(sb + 1) * LANES].T[IDX_DIM:IDX_DIM + N_IDX_HEADS]

    u = proj[:, 0:POOL_WIDTH]
    u_ref[0] = u

    @pl.when(i == 0)
    def _():
        e_ref[0:HALO // 2, :] = jnp.zeros((HALO // 2, POOL_WIDTH), F32)
        e_ref[HALO // 2:HALO, :] = hist_ref[0]

    e_ref[HALO:HALO + tm, :] = u
    n2, n4, n8 = tm + 24, tm + 16, tm + 8
    s2_ref[8:8 + n2, :] = e_ref[8:8 + n2, :] + e_ref[7:7 + n2, :]
    s4_ref[16:16 + n4, :] = s2_ref[16:16 + n4, :] + s2_ref[14:14 + n4, :]
    s8_ref[24:24 + n8, :] = s4_ref[24:24 + n8, :] + s4_ref[20:20 + n8, :]
    s16 = s8_ref[HALO:HALO + tm, :] + s8_ref[HALO - 8:HALO - 8 + tm, :]
    wins = (s2_ref[HALO:HALO + tm, :], s4_ref[HALO:HALO + tm, :], s8_ref[HALO:HALO + tm, :], s16)
    e_ref[HALO // 2:HALO, :] = e_ref[HALO // 2 + tm:HALO + tm, :]

    pos = pos0 + i * tm + lax.broadcasted_iota(I32, (tm, POOL_GROUP_WIDTH), 0)
    a_parts = []
    for g, w in enumerate(POOL_WINDOWS):
        lo_l, hi_l = g * POOL_GROUP_WIDTH, (g + 1) * POOL_GROUP_WIDTH
        cnt = jnp.minimum(pos + 1, w).astype(F32)
        pooled = wins[g][:, lo_l:hi_l] / cnt - u[:, lo_l:hi_l]
        a_parts.append(jnp.dot(pooled.astype(BF16), wpool_ref[g], preferred_element_type=F32))
    a = jnp.concatenate(a_parts, axis=-1) * pscale_ref[...]
    pa_ref[0] = _sigmoid(proj[:, o_ga:o_gb]) * a
    sgb_ref[0] = _sigmoid(proj[:, o_gb:o_gb + d_model])


def _project(x, hist16, pos0, w_in_r, g_pre, w_pool, pool_scale, tm, key_tile=None):
    B, T, D = x.shape
    W = w_in_r.shape[1]
    pos = pos0 + jnp.arange(T, dtype=I32)
    cq, sq = _rope_tables(pos, HEAD_DIM)
    ci, si = _rope_tables(pos, IDX_DIM)
    row = lambda b, i: (b, i, 0)
    tab = lambda b, i: (i, 0)
    const2 = lambda b, i: (0, 0)
    out_shape = (
        jax.ShapeDtypeStruct((B, T, POOL_WIDTH), F32),
        jax.ShapeDtypeStruct((B, N_HEADS, T, HEAD_DIM), BF16),
        jax.ShapeDtypeStruct((B, T, N_KV_HEADS * HEAD_DIM), F32),
        jax.ShapeDtypeStruct((B, T, N_KV_HEADS * HEAD_DIM), F32),
        jax.ShapeDtypeStruct((B, N_IDX_HEADS, T, 4 * IDX_DIM), BF16),
        jax.ShapeDtypeStruct((B, T, LANES), F32),
        jax.ShapeDtypeStruct((B, T, D), F32),
        jax.ShapeDtypeStruct((B, T, D), F32),
    )
    out_specs = (
        pl.BlockSpec((1, tm, POOL_WIDTH), row),
        pl.BlockSpec((1, N_HEADS, tm, HEAD_DIM), lambda b, i: (b, 0, i, 0)),
        pl.BlockSpec((1, tm, N_KV_HEADS * HEAD_DIM), row),
        pl.BlockSpec((1, tm, N_KV_HEADS * HEAD_DIM), row),
        pl.BlockSpec((1, N_IDX_HEADS, tm, 4 * IDX_DIM), lambda b, i: (b, 0, i, 0)),
        pl.BlockSpec((1, tm, LANES), row),
        pl.BlockSpec((1, tm, D), row),
        pl.BlockSpec((1, tm, D), row),
    )
    in_specs = [
        pl.BlockSpec((1, tm, D), row),
        _resident((1, D), const2),
        _resident((D, W), const2),
        pl.BlockSpec((1, HALO // 2, POOL_WIDTH), lambda b, i: (b, 0, 0)),
        pl.BlockSpec((tm, LANES), tab), pl.BlockSpec((tm, LANES), tab),
        pl.BlockSpec((tm, LANES), tab), pl.BlockSpec((tm, LANES), tab),
        _resident((N_POOL_GROUPS, POOL_GROUP_WIDTH, D // N_POOL_GROUPS), lambda b, i: (0, 0, 0)),
        _resident((1, D), const2),
    ]
    if key_tile is not None:
        assert key_tile % tm == 0 and T % key_tile == 0 and tm % LANES == 0
        per = key_tile // tm
        n_tiles = T // key_tile
        sub = tm // LANES
        out_shape = out_shape[:4] + (
            jax.ShapeDtypeStruct((B, T // LANES, 4 * IDX_DIM, N_IDX_HEADS * LANES), BF16),
        ) + out_shape[5:] + (
            jax.ShapeDtypeStruct((B, N_KV_HEADS, n_tiles, HEAD_DIM, key_tile), BF16),
            jax.ShapeDtypeStruct((B, N_KV_HEADS, n_tiles, key_tile, HEAD_DIM), BF16),
            jax.ShapeDtypeStruct((B, n_tiles, key_tile, 4 * IDX_DIM), BF16),
            jax.ShapeDtypeStruct((B, T // LANES, N_IDX_HEADS, LANES), F32),
        )
        out_specs = out_specs[:4] + (
            pl.BlockSpec((1, sub, 4 * IDX_DIM, N_IDX_HEADS * LANES), lambda b, i: (b, i, 0, 0)),
        ) + out_specs[5:] + (
            pl.BlockSpec((1, N_KV_HEADS, 1, HEAD_DIM, tm), lambda b, i: (b, 0, i // per, 0, i % per)),
            pl.BlockSpec((1, N_KV_HEADS, 1, tm, HEAD_DIM), lambda b, i: (b, 0, i // per, i % per, 0)),
            pl.BlockSpec((1, 1, tm, 4 * IDX_DIM), lambda b, i: (b, i // per, i % per, 0)),
            pl.BlockSpec((1, sub, N_IDX_HEADS, LANES), lambda b, i: (b, i, 0, 0)),
        )
    scratch = [pltpu.VMEM((HALO + tm, POOL_WIDTH), F32) for _ in range(4)]
    return pl.pallas_call(
        functools.partial(_proj_kernel, tm=tm, pos0=pos0, d_model=D, key_tiles=key_tile is not None),
        grid=(B, T // tm), in_specs=in_specs, out_specs=out_specs, out_shape=out_shape,
        scratch_shapes=scratch, name="proj",
        compiler_params=pltpu.CompilerParams(dimension_semantics=("arbitrary", "arbitrary"),
                                             vmem_limit_bytes=VMEM_LIMIT),
    )(x, g_pre, w_in_r, hist16, cq, sq, ci, si, w_pool, pool_scale)


def _rope_tables(pos, dim):
    rot = dim // 4
    half = rot // 2
    inv = ROPE_THETA ** (-jnp.arange(half, dtype=F32) / half)
    ang = pos.astype(F32)[:, None] * inv[None, :]
    cos, sin = jnp.cos(ang), jnp.sin(ang)
    rest = dim - rot
    n = pos.shape[0]
    c = jnp.concatenate([cos, cos, jnp.ones((n, rest), F32)], axis=-1)
    s = jnp.concatenate([-sin, sin, jnp.zeros((n, rest), F32)], axis=-1)
    return jnp.tile(c, (1, LANES // dim)), jnp.tile(s, (1, LANES // dim))


def _split_kernel(k2_ref, out_ref):
    k2 = k2_ref[0]
    hi = k2.astype(BF16)
    out_ref[0, :, 0:LANES] = hi
    out_ref[0, :, LANES:2 * LANES] = (k2 - hi.astype(F32)).astype(BF16)


def _split_keys(ki2, ts):
    B, Lp, _ = ki2.shape
    rows = ts * max(1, SPLIT_ROWS // ts)
    rows = rows if Lp % rows == 0 else ts
    return pl.pallas_call(
        _split_kernel, grid=(B, Lp // rows),
        in_specs=[pl.BlockSpec((1, rows, LANES), lambda b, t: (b, t, 0))],
        out_specs=pl.BlockSpec((1, rows, 2 * LANES), lambda b, t: (b, t, 0)),
        out_shape=jax.ShapeDtypeStruct((B, Lp, 2 * LANES), BF16), name="split_keys",
        compiler_params=pltpu.CompilerParams(dimension_semantics=("arbitrary", "arbitrary")),
    )(ki2)


def _key_of(x):
    b = pltpu.bitcast(x, I32)
    return b ^ ((b >> 31) & 0x7FFFFFFF)


def _float_of(k):
    return pltpu.bitcast(k ^ ((k >> 31) & 0x7FFFFFFF), F32)


def _row_sum(x):
    return jnp.broadcast_to(jnp.sum(x, axis=-1, keepdims=True), x.shape)


def _attn_kernel(qs_ref, kiw_ref, q_ref, ki_ref, kt_ref, v_ref, o_ref,
                 sc_ref, mm_ref, wb_ref, s_ref, m_ref, acc_ref, bias_ref, p_ref, alpha_ref,
                 *, tq, ts, q0, n_keys, k_sel):
    qi = pl.program_id(1)
    n_tiles_total = ki_ref.shape[1] // ts
    nc = ts // LANES
    nt_dims = (((1,), (1,)), ((), ()))

    def key_rows(t):
        return pl.ds(pl.multiple_of(t * ts, ts), ts)

    row = lax.broadcasted_iota(I32, (tq, LANES), 0)
    lane = lax.broadcasted_iota(I32, (tq, LANES), 1)
    qpos = q0 + qi * tq + row
    n_adm = jnp.minimum((qpos // CHUNK + 1) * CHUNK, n_keys)
    last_adm = jnp.minimum(((q0 + (qi + 1) * tq - 1) // CHUNK + 1) * CHUNK, n_keys)
    n_t = jnp.minimum((last_adm + ts - 1) // ts, n_tiles_total)

    kiw = kiw_ref[0]
    for hd in range(N_IDX_HEADS):
        wb_ref[hd] = jnp.broadcast_to(kiw[:, IDX_DIM + hd:IDX_DIM + hd + 1], (tq, LANES))
    qs2 = qs_ref[0].reshape(N_IDX_HEADS * tq, 4 * IDX_DIM)

    def idx_logits(t, slot):
        s_ref[slot] = lax.dot_general(qs2, ki_ref[0, key_rows(t), :], nt_dims,
                                      preferred_element_type=F32)

    rc = min(tq, ROW_CHUNK)
    mm_ref[0] = jnp.full((tq, LANES), _POS_INF, F32)
    mm_ref[1] = jnp.full((tq, LANES), _NEG_INF, F32)

    def score_tile(t, slot, masked):
        for r0 in range(0, tq, rc):
            rows = slice(r0, r0 + rc)
            mn, mx = mm_ref[0, rows], mm_ref[1, rows]
            for c in range(nc):
                cs = slice(c * LANES, (c + 1) * LANES)
                acc = None
                for hd in range(N_IDX_HEADS):
                    r = jnp.maximum(s_ref[slot, hd * tq + r0:hd * tq + r0 + rc, cs], 0.0) * wb_ref[hd, rows]
                    acc = r if acc is None else acc + r
                key = _key_of(acc)
                if masked:
                    qpos_c = q0 + qi * tq + r0 + lax.broadcasted_iota(I32, (rc, LANES), 0)
                    n_adm_c = jnp.minimum((qpos_c // CHUNK + 1) * CHUNK, n_keys)
                    adm = (t * ts + c * LANES + lax.broadcasted_iota(I32, (rc, LANES), 1)) < n_adm_c
                    key = jnp.where(adm, key, _INADMISSIBLE)
                    mx = jnp.maximum(mx, jnp.where(adm, acc, _NEG_INF))
                    mn = jnp.minimum(mn, jnp.where(adm, acc, _POS_INF))
                else:
                    mx = jnp.maximum(mx, acc)
                    mn = jnp.minimum(mn, acc)
                sc_ref[t, rows, cs] = key
            mm_ref[0, rows] = mn
            mm_ref[1, rows] = mx

    n_full = jnp.minimum(jnp.minimum(((q0 + qi * tq) // CHUNK + 1) * CHUNK, n_keys) // ts, n_t)
    last_full = jnp.maximum(n_full - 1, 0)
    idx_logits(0, 0)

    def score_pair(u, carry):
        t0 = 2 * u
        t1 = jnp.minimum(t0 + 1, last_full)
        idx_logits(t1, 1)
        score_tile(t0, 0, False)
        idx_logits(jnp.minimum(t0 + 2, last_full), 0)
        score_tile(t1, 1, False)
        return carry

    lax.fori_loop(0, (n_full + 1) // 2, score_pair, 0)

    def score_tail(t, carry):
        idx_logits(t, 0)
        score_tile(t, 0, True)
        return carry

    lax.fori_loop(n_full, n_t, score_tail, 0)
    rmin = jnp.broadcast_to(jnp.min(mm_ref[0], axis=-1, keepdims=True), (tq, LANES))
    rmax = jnp.broadcast_to(jnp.max(mm_ref[1], axis=-1, keepdims=True), (tq, LANES))

    def count_ge(thr):
        def body(t, cnt):
            for c in range(nc):
                cnt = cnt + jnp.where(sc_ref[t, :, c * LANES:(c + 1) * LANES] >= thr, 1.0, 0.0)
            return cnt
        return _row_sum(lax.fori_loop(0, n_t, body, jnp.zeros((tq, LANES), F32)))

    def bis_cond(st):
        return st[0] > 0

    kf = float(k_sel)
    take_all = n_adm <= k_sel
    lo0 = _key_of(rmin) - 1
    hi0 = _key_of(rmax) + 2
    c_pos = count_ge(jnp.full((tq, LANES), _KEY_MIN_NORMAL, I32))
    c_nn = count_ge(jnp.full((tq, LANES), _KEY_NEG_ZERO, I32))
    pos = c_pos >= kf
    neg = c_nn < kf
    lo = jnp.where(pos, _KEY_MIN_NORMAL, jnp.where(neg, lo0, _KEY_NEG_ZERO))
    hi = jnp.where(pos, hi0, jnp.where(neg, _KEY_NEG_ZERO, _KEY_MIN_NORMAL))
    clo = jnp.where(pos, c_pos, jnp.where(neg, n_adm.astype(F32), c_nn))
    chi = jnp.where(pos, 0.0, jnp.where(neg, c_nn, c_pos))
    exact0 = jnp.where(clo == kf, 1, 0)
    act0 = jnp.where(take_all, 0, jnp.where(exact0 > 0, 0, jnp.where(hi > lo + 1, 1, 0)))

    def bis_step(it, lo, hi, clo, chi, exact, act):
        kmid = (lo & hi) + ((lo ^ hi) >> 1)
        vmid = _key_of((_float_of(lo) + _float_of(hi)) * 0.5)
        by_value = jnp.where(it < VALUE_PIVOT_STEPS, jnp.where(vmid > lo, jnp.where(vmid < hi, 1, 0), 0), 0)
        mid = jnp.where(by_value > 0, vmid, kmid)
        cnt = count_ge(mid)
        up = jnp.where(act > 0, jnp.where(cnt >= kf, 1, 0), 0)
        dn = act - up
        lo = jnp.where(up > 0, mid, lo)
        clo = jnp.where(up > 0, cnt, clo)
        hi = jnp.where(dn > 0, mid, hi)
        chi = jnp.where(dn > 0, cnt, chi)
        hit = jnp.where(up > 0, jnp.where(cnt == kf, 1, 0), 0)
        exact = exact + hit
        act = jnp.where(act > 0, jnp.where(hit > 0, 0, jnp.where(hi > lo + 1, 1, 0)), 0)
        return it + 1, lo, hi, clo, chi, exact, act

    def any_active(act):
        return jnp.max(act.astype(F32))

    def bis_body(st):
        st = bis_step(*bis_step(*st[1:]))
        return (any_active(st[-1]),) + st

    _, _, lo, _, clo, chi, exact, _ = lax.while_loop(
        bis_cond, bis_body, (any_active(act0), jnp.int32(0), lo, hi, clo, chi, exact0, act0))
    thr = jnp.where(take_all, _INADMISSIBLE + 1, lo)
    thr_up = jnp.where(take_all, _INADMISSIBLE + 1, lo + 1)

    need = kf - chi
    tied = jnp.where(take_all, 0, jnp.where(exact > 0, 0, jnp.where(clo > kf, 1, 0)))

    def count_tied_below(cut):
        def body(t, cnt):
            for c in range(nc):
                s = sc_ref[t, :, c * LANES:(c + 1) * LANES]
                col = t * ts + c * LANES + lane
                cnt = cnt + jnp.where(s == thr, jnp.where(col < cut, 1.0, 0.0), 0.0)
            return cnt
        return _row_sum(lax.fori_loop(0, n_t, body, jnp.zeros((tq, LANES), F32)))

    def cut_body(st):
        _, lo_c, hi_c, act = st
        mid = (lo_c + hi_c) >> 1
        cnt = count_tied_below(mid)
        on = act > 0
        le = cnt <= need
        lo_c = jnp.where(on, jnp.where(le, mid, lo_c), lo_c)
        hi_c = jnp.where(on, jnp.where(le, hi_c, mid), hi_c)
        act = jnp.where(on, jnp.where(hi_c - lo_c > 1, 1, 0), 0)
        return jnp.max(act), lo_c, hi_c, act

    _, cut_lo, _, _ = lax.while_loop(
        bis_cond, cut_body,
        (jnp.max(tied), jnp.zeros((tq, LANES), I32), jnp.full((tq, LANES), n_tiles_total * ts + 1, I32), tied))
    cut = jnp.where(tied > 0, cut_lo, n_tiles_total * ts + 1)

    m_ref[...] = jnp.full(m_ref.shape, _M_INIT, F32)
    acc_ref[...] = jnp.zeros(acc_ref.shape, F32)
    rc = min(tq, ROW_CHUNK)
    ones = jnp.ones((ts, LANES), BF16)
    last = n_t - 1

    def logits(t, slot):
        for g in range(N_KV_HEADS):
            qg = q_ref[0, g * GQA_GROUP:(g + 1) * GQA_GROUP].reshape(GQA_GROUP * tq, HEAD_DIM)
            s_ref[slot, g * GQA_GROUP * tq:(g + 1) * GQA_GROUP * tq] = lax.dot_general(
                qg, kt_ref[0, key_rows(t), g * HEAD_DIM:(g + 1) * HEAD_DIM], nt_dims,
                preferred_element_type=F32)

    def softmax_pv(t, slot, live):
        for c in range(nc):
            col = t * ts + c * LANES + lane
            bound = jnp.where(live, jnp.where(col < cut, thr, thr_up), _NO_KEY)
            bias_ref[slot, :, c * LANES:(c + 1) * LANES] = jnp.where(
                sc_ref[t, :, c * LANES:(c + 1) * LANES] >= bound, 0.0, _NEG_INF)
        for g in range(N_KV_HEADS):
            for j in range(GQA_GROUP):
                hd = g * GQA_GROUP + j
                for r0 in range(0, tq, rc):
                    r1 = j * tq + r0
                    r2 = hd * tq + r0
                    s = [s_ref[slot, r2:r2 + rc, c * LANES:(c + 1) * LANES]
                         + bias_ref[slot, r0:r0 + rc, c * LANES:(c + 1) * LANES] for c in range(nc)]
                    mx = s[0]
                    for c in range(1, nc):
                        mx = jnp.maximum(mx, s[c])
                    m_old = m_ref[hd, r0:r0 + rc]
                    m_new = jnp.maximum(m_old, jnp.broadcast_to(jnp.max(mx, axis=-1, keepdims=True), (rc, LANES)))
                    alpha_ref[slot, g, r1:r1 + rc] = jnp.exp2(m_old - m_new)
                    for c in range(nc):
                        p_ref[slot, g, r1:r1 + rc, c * LANES:(c + 1) * LANES] = jnp.exp2(s[c] - m_new).astype(BF16)
                    m_ref[hd, r0:r0 + rc] = m_new
            v_aug = jnp.concatenate([v_ref[0, key_rows(t), g * HEAD_DIM:(g + 1) * HEAD_DIM], ones], axis=-1)
            pv = jnp.dot(p_ref[slot, g], v_aug, preferred_element_type=F32)
            alpha = alpha_ref[slot, g]
            for half in range(2):
                cols = slice(half * LANES, (half + 1) * LANES)
                acc_ref[g, :, cols] = alpha * acc_ref[g, :, cols] + pv[:, cols]

    logits(0, 0)

    def attn_pair(u, carry):
        t0 = 2 * u
        t1 = jnp.minimum(t0 + 1, last)
        logits(t1, 1)
        softmax_pv(t0, 0, True)
        logits(jnp.minimum(t0 + 2, last), 0)
        softmax_pv(t1, 1, t0 + 1 < n_t)
        return carry

    lax.fori_loop(0, (n_t + 1) // 2, attn_pair, 0)
    for hd in range(N_HEADS):
        g, j = divmod(hd, GQA_GROUP)
        o_ref[0, :, hd * HEAD_DIM:(hd + 1) * HEAD_DIM] = (
            acc_ref[g, j * tq:(j + 1) * tq, 0:HEAD_DIM] / acc_ref[g, j * tq:(j + 1) * tq, HEAD_DIM:2 * HEAD_DIM])


def _attn_t_kernel(qst_ref, wrow_ref, q_ref, ki_ref, kt_ref, v_ref, o_ref,
                   sc_ref, mm_ref, st_ref, s_ref, m_ref, acc_ref, bias_ref, p_ref, alpha_ref, kmax_ref,
                   *, tq, ts, q0, n_keys, k_sel):
    qi = pl.program_id(1)
    n_tiles_total = ki_ref.shape[1]
    nc = ts // LANES
    rk = KEY_ROWS
    qpos = q0 + qi * tq + lax.broadcasted_iota(I32, (1, tq), 1)
    n_adm = jnp.minimum((qpos // CHUNK + 1) * CHUNK, n_keys)
    last_adm = jnp.minimum(((q0 + (qi + 1) * tq - 1) // CHUNK + 1) * CHUNK, n_keys)
    n_t = jnp.minimum((last_adm + ts - 1) // ts, n_tiles_total)
    n_full = jnp.minimum(jnp.minimum(((q0 + qi * tq) // CHUNK + 1) * CHUNK, n_keys) // ts, n_t)
    last_full = jnp.maximum(n_full - 1, 0)
    qst = qst_ref[0, 0]
    wr = wrow_ref[0, 0]

    mm_ref[0] = jnp.full((rk, tq), _POS_INF, F32)
    mm_ref[1] = jnp.full((rk, tq), _NEG_INF, F32)

    def idx_logits(t, slot):
        st_ref[slot] = jnp.dot(ki_ref[0, t], qst, preferred_element_type=F32)

    def score_tile(t, slot, masked):
        mn, mx = mm_ref[0], mm_ref[1]
        for r0 in range(0, ts, rk):
            acc = None
            for hd in range(N_IDX_HEADS):
                r = jnp.maximum(st_ref[slot, r0:r0 + rk, hd * tq:(hd + 1) * tq], 0.0) * wr[hd:hd + 1, :]
                acc = r if acc is None else acc + r
            key = _key_of(acc)
            if masked:
                adm = (t * ts + r0 + lax.broadcasted_iota(I32, (rk, tq), 0)) < n_adm
                key = jnp.where(adm, key, _INADMISSIBLE)
                mx = jnp.maximum(mx, jnp.where(adm, acc, _NEG_INF))
                mn = jnp.minimum(mn, jnp.where(adm, acc, _POS_INF))
            else:
                mx = jnp.maximum(mx, acc)
                mn = jnp.minimum(mn, acc)
            sc_ref[t, r0:r0 + rk, :] = key
        mm_ref[0] = mn
        mm_ref[1] = mx

    idx_logits(0, 0)

    def score_pair(u, carry):
        t0 = 2 * u
        t1 = jnp.minimum(t0 + 1, last_full)
        idx_logits(t1, 1)
        score_tile(t0, 0, False)
        idx_logits(jnp.minimum(t0 + 2, last_full), 0)
        score_tile(t1, 1, False)
        return carry

    lax.fori_loop(0, (n_full + 1) // 2, score_pair, 0)

    def score_tail(t, carry):
        idx_logits(t, 0)
        score_tile(t, 0, True)
        return carry

    lax.fori_loop(n_full, n_t, score_tail, 0)
    rmin = jnp.min(mm_ref[0], axis=0, keepdims=True)
    rmax = jnp.max(mm_ref[1], axis=0, keepdims=True)

    def count_ge(thr):
        def body(t, cnt):
            for r0 in range(0, ts, rk):
                cnt = cnt + jnp.where(sc_ref[t, r0:r0 + rk, :] >= thr, 1.0, 0.0)
            return cnt
        return jnp.sum(lax.fori_loop(0, n_t, body, jnp.zeros((rk, tq), F32)), axis=0, keepdims=True)

    def bis_cond(st):
        return st[0] > 0

    kf = float(k_sel)
    take_all = n_adm <= k_sel
    lo0 = _key_of(rmin) - 1
    hi0 = _key_of(rmax) + 2
    c_pos = count_ge(jnp.full((1, tq), _KEY_MIN_NORMAL, I32))
    c_nn = count_ge(jnp.full((1, tq), _KEY_NEG_ZERO, I32))
    pos = c_pos >= kf
    neg = c_nn < kf
    lo = jnp.where(pos, _KEY_MIN_NORMAL, jnp.where(neg, lo0, _KEY_NEG_ZERO))
    hi = jnp.where(pos, hi0, jnp.where(neg, _KEY_NEG_ZERO, _KEY_MIN_NORMAL))
    clo = jnp.where(pos, c_pos, jnp.where(neg, n_adm.astype(F32), c_nn))
    chi = jnp.where(pos, 0.0, jnp.where(neg, c_nn, c_pos))
    exact0 = jnp.where(clo == kf, 1, 0)
    act0 = jnp.where(take_all, 0, jnp.where(exact0 > 0, 0, jnp.where(hi > lo + 1, 1, 0)))

    def bis_step(it, lo, hi, clo, chi, exact, act):
        kmid = (lo & hi) + ((lo ^ hi) >> 1)
        vmid = _key_of((_float_of(lo) + _float_of(hi)) * 0.5)
        by_value = jnp.where(it < VALUE_PIVOT_STEPS, jnp.where(vmid > lo, jnp.where(vmid < hi, 1, 0), 0), 0)
        mid = jnp.where(by_value > 0, vmid, kmid)
        cnt = count_ge(mid)
        up = jnp.where(act > 0, jnp.where(cnt >= kf, 1, 0), 0)
        dn = act - up
        lo = jnp.where(up > 0, mid, lo)
        clo = jnp.where(up > 0, cnt, clo)
        hi = jnp.where(dn > 0, mid, hi)
        chi = jnp.where(dn > 0, cnt, chi)
        hit = jnp.where(up > 0, jnp.where(cnt == kf, 1, 0), 0)
        exact = exact + hit
        act = jnp.where(act > 0, jnp.where(hit > 0, 0, jnp.where(hi > lo + 1, 1, 0)), 0)
        return it + 1, lo, hi, clo, chi, exact, act

    def any_active(act):
        return jnp.max(act.astype(F32))

    def bis_body(st):
        st = bis_step(*bis_step(*st[1:]))
        return (any_active(st[-1]),) + st

    _, _, lo, _, clo, chi, exact, _ = lax.while_loop(
        bis_cond, bis_body, (any_active(act0), jnp.int32(0), lo, hi, clo, chi, exact0, act0))
    thr = jnp.where(take_all, _INADMISSIBLE + 1, lo)
    thr_up = jnp.where(take_all, _INADMISSIBLE + 1, lo + 1)

    need = kf - chi
    tied = jnp.where(take_all, 0, jnp.where(exact > 0, 0, jnp.where(clo > kf, 1, 0)))

    def count_tied_below(cut):
        def body(t, cnt):
            for r0 in range(0, ts, rk):
                kidx = t * ts + r0 + lax.broadcasted_iota(I32, (rk, tq), 0)
                cnt = cnt + jnp.where(sc_ref[t, r0:r0 + rk, :] == thr, jnp.where(kidx < cut, 1.0, 0.0), 0.0)
            return cnt
        return jnp.sum(lax.fori_loop(0, n_t, body, jnp.zeros((rk, tq), F32)), axis=0, keepdims=True)

    def cut_body(st):
        _, lo_c, hi_c, act = st
        mid = (lo_c + hi_c) >> 1
        cnt = count_tied_below(mid)
        on = act > 0
        le = cnt <= need
        lo_c = jnp.where(on, jnp.where(le, mid, lo_c), lo_c)
        hi_c = jnp.where(on, jnp.where(le, hi_c, mid), hi_c)
        act = jnp.where(on, jnp.where(hi_c - lo_c > 1, 1, 0), 0)
        return jnp.max(act), lo_c, hi_c, act

    _, cut_lo, _, _ = lax.while_loop(
        bis_cond, cut_body,
        (jnp.max(tied), jnp.zeros((1, tq), I32), jnp.full((1, tq), n_tiles_total * ts + 1, I32), tied))
    cut = jnp.where(tied > 0, cut_lo, n_tiles_total * ts + 1)

    acc_ref[...] = jnp.zeros(acc_ref.shape, F32)
    rc = min(tq, ROW_CHUNK)
    ones = jnp.ones((ts, LANES), BF16)
    last = n_t - 1

    def logits(t, slot):
        for g in range(N_KV_HEADS):
            qg = q_ref[0, g * GQA_GROUP:(g + 1) * GQA_GROUP].reshape(GQA_GROUP * tq, HEAD_DIM)
            s_ref[slot, g * GQA_GROUP * tq:(g + 1) * GQA_GROUP * tq] = jnp.dot(
                qg, kt_ref[0, g, t], preferred_element_type=F32)

    def selection_bias(t, c, live, masked_value):
        kidx = t * ts + c * LANES + lax.broadcasted_iota(I32, (LANES, tq), 0)
        bound = jnp.where(live, jnp.where(kidx < cut, thr, thr_up), _NO_KEY)
        return jnp.where(sc_ref[t, c * LANES:(c + 1) * LANES, :] >= bound, 0.0, masked_value).T

    def value_matmul(t, slot, g):
        v_aug = jnp.concatenate([v_ref[0, g, t], ones], axis=-1)
        return jnp.dot(p_ref[slot, g], v_aug, preferred_element_type=F32)

    def softmax_pv(t, slot, live):
        for c in range(nc):
            bias_ref[slot, :, c * LANES:(c + 1) * LANES] = selection_bias(t, c, live, _NEG_INF)
        for g in range(N_KV_HEADS):
            for j in range(GQA_GROUP):
                hd = g * GQA_GROUP + j
                for r0 in range(0, tq, rc):
                    r1 = j * tq + r0
                    r2 = hd * tq + r0
                    s = [s_ref[slot, r2:r2 + rc, c * LANES:(c + 1) * LANES]
                         + bias_ref[slot, r0:r0 + rc, c * LANES:(c + 1) * LANES] for c in range(nc)]
                    mx = s[0]
                    for c in range(1, nc):
                        mx = jnp.maximum(mx, s[c])
                    m_old = m_ref[hd, r0:r0 + rc]
                    m_new = jnp.maximum(m_old, jnp.broadcast_to(jnp.max(mx, axis=-1, keepdims=True), (rc, LANES)))
                    alpha_ref[slot, g, r1:r1 + rc] = jnp.exp2(m_old - m_new)
                    m_ref[hd, r0:r0 + rc] = m_new
                    for c in range(nc):
                        p_ref[slot, g, r1:r1 + rc, c * LANES:(c + 1) * LANES] = jnp.exp2(s[c] - m_new).astype(BF16)
            pv = value_matmul(t, slot, g)
            for half in range(2):
                cols = slice(half * LANES, (half + 1) * LANES)
                acc_ref[g, :, cols] = alpha_ref[slot, g] * acc_ref[g, :, cols] + pv[:, cols]

    def attend_running():
        logits(0, 0)

        def attn_pair(u, carry):
            t0 = 2 * u
            t1 = jnp.minimum(t0 + 1, last)
            logits(t1, 1)
            softmax_pv(t0, 0, True)
            logits(jnp.minimum(t0 + 2, last), 0)
            softmax_pv(t1, 1, t0 + 1 < n_t)
            return carry

        lax.fori_loop(0, (n_t + 1) // 2, attn_pair, 0)

    def softmax_pv_fixed(t, slot, live):
        for c in range(nc):
            bias_ref[slot, :, c * LANES:(c + 1) * LANES] = selection_bias(t, c, live, _NEG_INF)
        for g in range(N_KV_HEADS):
            for j in range(GQA_GROUP):
                hd = g * GQA_GROUP + j
                for r0 in range(0, tq, rc):
                    ref_pt = m_ref[hd, r0:r0 + rc]
                    for c in range(nc):
                        cs = slice(c * LANES, (c + 1) * LANES)
                        s = s_ref[slot, hd * tq + r0:hd * tq + r0 + rc, cs] + bias_ref[slot, r0:r0 + rc, cs]
                        p_ref[slot, g, j * tq + r0:j * tq + r0 + rc, cs] = jnp.exp2(s - ref_pt).astype(BF16)
            pv = value_matmul(t, slot, g)
            for half in range(2):
                cols = slice(half * LANES, (half + 1) * LANES)
                acc_ref[g, :, cols] = acc_ref[g, :, cols] + pv[:, cols]

    def attend_fixed():
        logits(0, 0)

        def attn_pair(u, carry):
            t0 = 2 * u
            t1 = jnp.minimum(t0 + 1, last)
            logits(t1, 1)
            softmax_pv_fixed(t0, 0, True)
            logits(jnp.minimum(t0 + 2, last), 0)
            softmax_pv_fixed(t1, 1, t0 + 1 < n_t)
            return carry

        lax.fori_loop(0, (n_t + 1) // 2, attn_pair, 0)

    @pl.when(qi == 0)
    def _():
        for g in range(N_KV_HEADS):
            def sq_norm_max(t, best):
                kk = kt_ref[0, g, t].astype(F32)
                return jnp.maximum(best, jnp.sum(kk * kk, axis=0, keepdims=True))
            best = lax.fori_loop(0, n_tiles_total, sq_norm_max, jnp.zeros((1, ts), F32))
            kmax_ref[g] = jnp.broadcast_to(jnp.max(best, axis=-1, keepdims=True), (8, LANES))

    top = jnp.zeros((tq, LANES), F32)
    for hd in range(N_HEADS):
        qf = q_ref[0, hd].astype(F32)
        q_sq = _row_sum(qf * qf)
        ref_pt = jnp.sqrt(q_sq * kmax_ref[hd // GQA_GROUP, 0:1, :]) * 1.001 + 1e-3
        m_ref[hd] = ref_pt
        top = jnp.maximum(top, ref_pt)
    fits = jnp.max(top) <= MAX_FIXED_REFERENCE

    @pl.when(fits)
    def _():
        attend_fixed()

    @pl.when(jnp.logical_not(fits))
    def _():
        m_ref[...] = jnp.full(m_ref.shape, _M_INIT, F32)
        attend_running()

    for hd in range(N_HEADS):
        g, j = divmod(hd, GQA_GROUP)
        o_ref[0, :, hd * HEAD_DIM:(hd + 1) * HEAD_DIM] = (
            acc_ref[g, j * tq:(j + 1) * tq, 0:HEAD_DIM] / acc_ref[g, j * tq:(j + 1) * tq, HEAD_DIM:2 * HEAD_DIM])


def _attend_t(qst, wrow, q_att, ki4, kt, v, q0, n_keys, tq, ts):
    B, _, T, _ = q_att.shape
    n_tiles = ki4.shape[1]
    k_sel = min(TOPK_MAX, n_keys // 4)
    kern = functools.partial(_attn_t_kernel, tq=tq, ts=ts, q0=q0, n_keys=n_keys, k_sel=k_sel)
    return pl.pallas_call(
        kern, grid=(B, T // tq),
        in_specs=[
            pl.BlockSpec((1, 1, 4 * IDX_DIM, N_IDX_HEADS * tq), lambda b, i: (b, i, 0, 0)),
            pl.BlockSpec((1, 1, N_IDX_HEADS, tq), lambda b, i: (b, i, 0, 0)),
            pl.BlockSpec((1, N_HEADS, tq, HEAD_DIM), lambda b, i: (b, 0, i, 0)),
            _resident((1, n_tiles, ts, 4 * IDX_DIM), lambda b, i: (b, 0, 0, 0)),
            _resident((1, N_KV_HEADS, n_tiles, HEAD_DIM, ts), lambda b, i: (b, 0, 0, 0, 0)),
            _resident((1, N_KV_HEADS, n_tiles, ts, HEAD_DIM), lambda b, i: (b, 0, 0, 0, 0)),
        ],
        out_specs=pl.BlockSpec((1, tq, N_HEADS * HEAD_DIM), lambda b, i: (b, i, 0)),
        out_shape=jax.ShapeDtypeStruct((B, T, N_HEADS * HEAD_DIM), F32),
        scratch_shapes=[
            pltpu.VMEM((n_tiles, ts, tq), I32),
            pltpu.VMEM((2, KEY_ROWS, tq), F32),
            pltpu.VMEM((2, ts, N_IDX_HEADS * tq), F32),
            pltpu.VMEM((2, N_HEADS * tq, ts), F32),
            pltpu.VMEM((N_HEADS, tq, LANES), F32),
            pltpu.VMEM((N_KV_HEADS, GQA_GROUP * tq, 2 * HEAD_DIM), F32),
            pltpu.VMEM((2, tq, ts), F32),
            pltpu.VMEM((2, N_KV_HEADS, GQA_GROUP * tq, ts), BF16),
            pltpu.VMEM((2, N_KV_HEADS, GQA_GROUP * tq, LANES), F32),
            pltpu.VMEM((N_KV_HEADS, 8, LANES), F32),
        ],
        name="attend_t",
        compiler_params=pltpu.CompilerParams(dimension_semantics=("arbitrary", "arbitrary"),
                                             vmem_limit_bytes=VMEM_LIMIT),
    )(qst, wrow, q_att, ki4, kt, v)


def _attend(qs, kiw, q_att, ki4, kt, v, q0, n_keys, tq, ts):
    B, _, T, _ = qs.shape
    lp = ki4.shape[1]
    n_tiles = lp // ts
    k_sel = min(TOPK_MAX, n_keys // 4)
    kern = functools.partial(_attn_kernel, tq=tq, ts=ts, q0=q0, n_keys=n_keys, k_sel=k_sel)
    keys = pl.BlockSpec((1, lp, 2 * LANES), lambda b, i: (b, 0, 0))
    return pl.pallas_call(
        kern, grid=(B, T // tq),
        in_specs=[
            pl.BlockSpec((1, N_IDX_HEADS, tq, 4 * IDX_DIM), lambda b, i: (b, 0, i, 0)),
            pl.BlockSpec((1, tq, LANES), lambda b, i: (b, i, 0)),
            pl.BlockSpec((1, N_HEADS, tq, HEAD_DIM), lambda b, i: (b, 0, i, 0)),
            keys, keys, keys,
        ],
        out_specs=pl.BlockSpec((1, tq, N_HEADS * HEAD_DIM), lambda b, i: (b, i, 0)),
        out_shape=jax.ShapeDtypeStruct((B, T, N_HEADS * HEAD_DIM), F32),
        scratch_shapes=[
            pltpu.VMEM((n_tiles, tq, ts), I32),
            pltpu.VMEM((2, tq, LANES), F32),
            pltpu.VMEM((N_IDX_HEADS, tq, LANES), F32),
            pltpu.VMEM((2, N_HEADS * tq, ts), F32),
            pltpu.VMEM((N_HEADS, tq, LANES), F32),
            pltpu.VMEM((N_KV_HEADS, GQA_GROUP * tq, 2 * HEAD_DIM), F32),
            pltpu.VMEM((2, tq, ts), F32),
            pltpu.VMEM((2, N_KV_HEADS, GQA_GROUP * tq, ts), BF16),
            pltpu.VMEM((2, N_KV_HEADS, GQA_GROUP * tq, LANES), F32),
        ],
        name="attend",
        compiler_params=pltpu.CompilerParams(dimension_semantics=("arbitrary", "arbitrary"),
                                             vmem_limit_bytes=VMEM_LIMIT),
    )(qs, kiw, q_att, ki4, kt, v)


def _rms(x, g):
    return x * lax.rsqrt(jnp.mean(x * x, axis=-1, keepdims=True) + RMS_EPS) * g


def _out_kernel(x_ref, pa_ref, sgb_ref, b_ref, wo_ref, g1_ref, g2_ref, wgu_ref, wd_ref, g3_ref, y_ref, *, d_ff):
    merged = pa_ref[...] + sgb_ref[...] * b_ref[...]
    mix = jnp.dot(merged.astype(BF16), wo_ref[...], preferred_element_type=F32)
    x1 = x_ref[...] + _rms(mix, g1_ref[...])
    h2 = _rms(x1, g2_ref[...]).astype(BF16)
    gu = jnp.dot(h2, wgu_ref[...], preferred_element_type=F32)
    gate, up = gu[:, :d_ff], gu[:, d_ff:]
    act = (gate * _sigmoid(gate) * up).astype(BF16)
    f = jnp.dot(act, wd_ref[...], preferred_element_type=F32)
    y_ref[...] = x1 + _rms(f, g3_ref[...])


def _finish(x, pa, sgb, b, w_out, g_post, g_ffn_pre, w_gate_up, w_down, g_ffn_post, tm):
    N, D = x.shape
    d_ff = w_down.shape[0]
    row = pl.BlockSpec((tm, D), lambda i: (i, 0))
    const = lambda i: (0, 0)
    return pl.pallas_call(
        functools.partial(_out_kernel, d_ff=d_ff), grid=(N // tm,),
        in_specs=[row, row, row, row,
                  _resident((D, D), const), _resident((1, D), const), _resident((1, D), const),
                  _resident((D, 2 * d_ff), const), _resident((d_ff, D), const), _resident((1, D), const)],
        out_specs=row, out_shape=jax.ShapeDtypeStruct((N, D), F32), name="finish",
        compiler_params=pltpu.CompilerParams(dimension_semantics=("arbitrary",), vmem_limit_bytes=VMEM_LIMIT),
    )(x, pa, sgb, b, w_out, g_post, g_ffn_pre, w_gate_up, w_down, g_ffn_post)


def _tile_keys(n_keys, ts):
    return -(-n_keys // ts) * ts


def _layer(x, pos0, hist, k_cache, v_cache, ki_cache, wts, tm, tq, ts, tm_out):
    (w_in_r, w_pool, pool_scale, w_out, w_gate_up, w_down, g_pre, g_post, g_ffn_pre, g_ffn_post) = wts
    B, T, D = x.shape
    hist16 = jnp.concatenate([jnp.zeros((B, HALO // 2 - POOL_HIST, POOL_WIDTH), F32), hist], axis=1)
    own_keys_only = k_cache is None and T % ts == 0 and ts % tm == 0 and tq == LANES and tm % LANES == 0
    outs = _project(x, hist16, pos0, w_in_r, g_pre, w_pool, pool_scale, tm, ts if own_keys_only else None)
    u, q_att, k, v, qs, kiw, pa, sgb = outs[:8]
    ki = kiw[:, :, :IDX_DIM]
    if own_keys_only:
        kt, vt, ki4, wrow = outs[8:]
        b = _attend_t(qs, wrow, q_att, ki4, kt, vt, pos0, T, tq, ts)
    else:
        if k_cache is None:
            k_all, v_all, ki_all = k, v, ki
        else:
            k_all = jnp.concatenate([k_cache.reshape(B, -1, N_KV_HEADS * HEAD_DIM), k], axis=1)
            v_all = jnp.concatenate([v_cache.reshape(B, -1, N_KV_HEADS * HEAD_DIM), v], axis=1)
            ki_all = jnp.concatenate([ki_cache, ki], axis=1)
        n_keys = k_all.shape[1]
        lp = _tile_keys(n_keys, ts)
        pad = ((0, 0), (0, lp - n_keys), (0, 0))
        kt = jnp.pad(k_all.astype(BF16), pad)
        vt = jnp.pad(v_all.astype(BF16), pad)
        ki4 = _split_keys(jnp.pad(jnp.concatenate([ki_all, ki_all], axis=-1), pad), ts)
        b = _attend(qs, kiw, q_att, ki4, kt, vt, pos0, n_keys, tq, ts)
    y = _finish(x.reshape(B * T, D), pa.reshape(B * T, D), sgb.reshape(B * T, D), b.reshape(B * T, D),
                w_out, g_post, g_ffn_pre, w_gate_up, w_down, g_ffn_post, tm_out).reshape(B, T, D)
    new_pool = jnp.concatenate([hist, u], axis=1)[:, T:]
    return (y, k.reshape(B, T, N_KV_HEADS, HEAD_DIM), v.reshape(B, T, N_KV_HEADS, HEAD_DIM), ki, new_pool)


def _relayout_w_in(w_in):
    d = w_in.shape[0]
    o_kiw = POOL_WIDTH + N_HEADS * HEAD_DIM + 2 * N_KV_HEADS * HEAD_DIM + N_IDX_HEADS * IDX_DIM
    narrow = IDX_DIM + N_IDX_HEADS
    padded = jnp.concatenate([w_in[:, :o_kiw + narrow], jnp.zeros((d, LANES - narrow), w_in.dtype),
                              w_in[:, o_kiw + narrow:]], axis=1)
    return padded.astype(BF16)


def kernel(x_prompt, x_sample, cache_k, cache_v, cache_k_idx, state_pool, w_in, w_pool, pool_scale, w_out,
           w_gate_up, w_down, norm_mix_pre, norm_mix_post, norm_ffn_pre, norm_ffn_post):
    depth = w_in.shape[0]
    past = cache_k.shape[2]
    t_p, t_s = x_prompt.shape[1], x_sample.shape[1]
    hist_p = jnp.zeros((x_prompt.shape[0], POOL_HIST, POOL_WIDTH), x_prompt.dtype)
    xp, xs = x_prompt, x_sample
    outs = [[] for _ in range(8)]
    for l in range(depth):
        wts = (_relayout_w_in(w_in[l]), w_pool[l].astype(BF16), pool_scale[l][None, :], w_out[l].astype(BF16),
               w_gate_up[l].astype(BF16), w_down[l].astype(BF16), norm_mix_pre[l][None, :],
               norm_mix_post[l][None, :], norm_ffn_pre[l][None, :], norm_ffn_post[l][None, :])
        tm_p = min(256, t_p)
        tq_p = min(128, t_p)
        xp, k1, v1, ki1, p1 = _layer(xp, 0, hist_p, None, None, None, wts, tm_p, tq_p, 512, tm_p)
        n_s = xs.shape[0] * t_s
        xs, k2, v2, ki2, p2 = _layer(xs, past, state_pool[l], cache_k[l], cache_v[l], cache_k_idx[l], wts,
                                     t_s, t_s, 512, min(256, n_s))
        for lst, val in zip(outs, (k1, v1, ki1, p1, k2, v2, ki2, p2)):
            lst.append(val)
    return (xp, xs) + tuple(jnp.stack(o) for o in outs)
```

```python
import functools

import jax
import jax.numpy as jnp
from jax import lax
from jax.experimental import pallas as pl
from jax.experimental.pallas import tpu as pltpu

F32 = jnp.float32
BF16 = jnp.bfloat16
I32 = jnp.int32

LANES = 128
CHUNK = 64
POOL_WINDOWS = (2, 4, 8, 16)
N_POOL_GROUPS = 4
POOL_GROUP_WIDTH = 128
POOL_WIDTH = N_POOL_GROUPS * POOL_GROUP_WIDTH
POOL_HIST = 15
N_HEADS = 8
N_KV_HEADS = 2
HEAD_DIM = 128
GQA_GROUP = N_HEADS // N_KV_HEADS
ROPE_THETA = 500000.0
N_IDX_HEADS = 8
IDX_DIM = 64
TOPK_MAX = 256
RMS_EPS = 1e-6
ATTN_SCALE = HEAD_DIM ** -0.5
IDX_SCALE = (N_IDX_HEADS ** -0.5) * (IDX_DIM ** -0.5)

LOG2_E = 1.4426950408889634
VALUE_PIVOT_STEPS = 28
MAX_FIXED_REFERENCE = 40.0
SPLIT_ROWS = 1536
KEY_ROWS = 16
COUNT_ROWS = 32
ROW_CHUNK = 32
HALO = 32
VMEM_LIMIT = 56 * 1024 * 1024

_NEG_INF = float("-inf")
_POS_INF = float("inf")
_INADMISSIBLE = -(2 ** 31)
_NO_KEY = 2 ** 31 - 1
_KEY_MIN_NORMAL = 0x00800000
_KEY_NEG_ZERO = -1
_M_INIT = -1e30


def _resident(block_shape, index_map):
    return pl.BlockSpec(block_shape, index_map, pipeline_mode=pl.Buffered(1))


def _rope(xs, cos, sin, half, period):
    lane = lax.broadcasted_iota(I32, xs.shape, 1)
    ahead = pltpu.roll(xs, LANES - half, 1)
    behind = pltpu.roll(xs, half, 1)
    partner = jnp.where((lane & (period - 1)) < half, ahead, behind)
    return xs * cos + partner * sin


def _sigmoid(x):
    return 1.0 / (1.0 + jnp.exp(-x))


def _proj_kernel(x_ref, g_ref, w_ref, hist_ref, cq_ref, sq_ref, ci_ref, si_ref, wpool_ref, pscale_ref,
                 u_ref, qatt_ref, k_ref, v_ref, qs_ref, kiw_ref, pa_ref, sgb_ref, *rest, tm, pos0, d_model, key_tiles):
    if key_tiles:
        kt_ref, vt_ref, ki4_ref, wrow_ref, e_ref, s2_ref, s4_ref, s8_ref = rest
    else:
        e_ref, s2_ref, s4_ref, s8_ref = rest
    i = pl.program_id(1)
    x = x_ref[0]
    h = x * lax.rsqrt(jnp.mean(x * x, axis=-1, keepdims=True) + RMS_EPS) * g_ref[...]
    proj = jnp.dot(h.astype(BF16), w_ref[...], preferred_element_type=F32)

    o_q = POOL_WIDTH
    o_k = o_q + N_HEADS * HEAD_DIM
    o_v = o_k + N_KV_HEADS * HEAD_DIM
    o_qi = o_v + N_KV_HEADS * HEAD_DIM
    o_kiw = o_qi + N_IDX_HEADS * IDX_DIM
    o_ga = o_kiw + LANES
    o_gb = o_ga + d_model

    cq, sq, ci, si = cq_ref[...], sq_ref[...], ci_ref[...], si_ref[...]
    half_q = HEAD_DIM // 8
    half_i = IDX_DIM // 8

    for hd in range(N_HEADS):
        qh = _rope(proj[:, o_q + hd * HEAD_DIM:o_q + (hd + 1) * HEAD_DIM], cq, sq, half_q, HEAD_DIM)
        qatt_ref[0, hd] = (qh * (ATTN_SCALE * LOG2_E)).astype(BF16)
    for kh in range(N_KV_HEADS):
        k_h = _rope(proj[:, o_k + kh * HEAD_DIM:o_k + (kh + 1) * HEAD_DIM], cq, sq, half_q, HEAD_DIM)
        k_ref[0, :, kh * HEAD_DIM:(kh + 1) * HEAD_DIM] = k_h
        if key_tiles:
            kt_ref[0, kh, 0] = k_h.T.astype(BF16)
            vt_ref[0, kh, 0] = proj[:, o_v + kh * HEAD_DIM:o_v + (kh + 1) * HEAD_DIM].astype(BF16)
    v_ref[0] = proj[:, o_v:o_qi]

    for pr in range(N_IDX_HEADS // 2):
        qi2 = _rope(proj[:, o_qi + pr * LANES:o_qi + (pr + 1) * LANES], ci, si, half_i, IDX_DIM)
        hi = qi2.astype(BF16).astype(F32)
        lo = qi2 - hi
        lane2 = lax.broadcasted_iota(I32, qi2.shape, 1)
        first = jnp.where(lane2 < IDX_DIM, hi, pltpu.roll(lo, IDX_DIM, 1))
        second = jnp.where(lane2 < IDX_DIM, pltpu.roll(hi, IDX_DIM, 1), lo)
        for sub, slab in enumerate((first, second)):
            hd = 2 * pr + sub
            if key_tiles:
                for sb in range(tm // LANES):
                    slab_t = slab[sb * LANES:(sb + 1) * LANES].T.astype(BF16)
                    qs_ref[0, sb, 0:LANES, hd * LANES:(hd + 1) * LANES] = slab_t
                    qs_ref[0, sb, LANES:2 * LANES, hd * LANES:(hd + 1) * LANES] = slab_t
            else:
                qs_ref[0, hd, :, 0:LANES] = slab.astype(BF16)
                qs_ref[0, hd, :, LANES:2 * LANES] = slab.astype(BF16)

    kiw = _rope(proj[:, o_kiw:o_kiw + LANES], ci, si, half_i, IDX_DIM)
    lane = lax.broadcasted_iota(I32, kiw.shape, 1)
    kiw_full = jnp.where(lane < IDX_DIM, kiw, proj[:, o_kiw:o_kiw + LANES] * IDX_SCALE)
    kiw_ref[0] = kiw_full
    if key_tiles:
        ki_hi = kiw.astype(BF16).astype(F32)
        ki_lo = kiw - ki_hi
        ki4_ref[0, 0, :, 0:LANES] = jnp.where(lane < IDX_DIM, ki_hi, pltpu.roll(ki_hi, IDX_DIM, 1)).astype(BF16)
        ki4_ref[0, 0, :, LANES:2 * LANES] = jnp.where(lane < IDX_DIM, ki_lo, pltpu.roll(ki_lo, IDX_DIM, 1)).astype(BF16)
        for sb in range(tm // LANES):
            wrow_ref[0, sb] = kiw_full[sb * LANES:(sb + 1) * LANES].T[IDX_DIM:IDX_DIM + N_IDX_HEADS]

    u = proj[:, 0:POOL_WIDTH]
    u_ref[0] = u

    @pl.when(i == 0)
    def _():
        e_ref[0:HALO // 2, :] = jnp.zeros((HALO // 2, POOL_WIDTH), F32)
        e_ref[HALO // 2:HALO, :] = hist_ref[0]

    e_ref[HALO:HALO + tm, :] = u
    n2, n4, n8 = tm + 24, tm + 16, tm + 8
    s2_ref[8:8 + n2, :] = e_ref[8:8 + n2, :] + e_ref[7:7 + n2, :]
    s4_ref[16:16 + n4, :] = s2_ref[16:16 + n4, :] + s2_ref[14:14 + n4, :]
    s8_ref[24:24 + n8, :] = s4_ref[24:24 + n8, :] + s4_ref[20:20 + n8, :]
    s16 = s8_ref[HALO:HALO + tm, :] + s8_ref[HALO - 8:HALO - 8 + tm, :]
    wins = (s2_ref[HALO:HALO + tm, :], s4_ref[HALO:HALO + tm, :], s8_ref[HALO:HALO + tm, :], s16)
    e_ref[HALO // 2:HALO, :] = e_ref[HALO // 2 + tm:HALO + tm, :]

    pos = pos0 + i * tm + lax.broadcasted_iota(I32, (tm, POOL_GROUP_WIDTH), 0)
    a_parts = []
    for g, w in enumerate(POOL_WINDOWS):
        lo_l, hi_l = g * POOL_GROUP_WIDTH, (g + 1) * POOL_GROUP_WIDTH
        cnt = jnp.minimum(pos + 1, w).astype(F32)
        pooled = wins[g][:, lo_l:hi_l] / cnt - u[:, lo_l:hi_l]
        a_parts.append(jnp.dot(pooled.astype(BF16), wpool_ref[g], preferred_element_type=F32))
    a = jnp.concatenate(a_parts, axis=-1) * pscale_ref[...]
    pa_ref[0] = _sigmoid(proj[:, o_ga:o_gb]) * a
    sgb_ref[0] = _sigmoid(proj[:, o_gb:o_gb + d_model])


def _project(x, hist16, pos0, w_in_r, g_pre, w_pool, pool_scale, tm, key_tile=None):
    B, T, D = x.shape
    W = w_in_r.shape[1]
    pos = pos0 + jnp.arange(T, dtype=I32)
    cq, sq = _rope_tables(pos, HEAD_DIM)
    ci, si = _rope_tables(pos, IDX_DIM)
    row = lambda b, i: (b, i, 0)
    tab = lambda b, i: (i, 0)
    const2 = lambda b, i: (0, 0)
    out_shape = (
        jax.ShapeDtypeStruct((B, T, POOL_WIDTH), F32),
        jax.ShapeDtypeStruct((B, N_HEADS, T, HEAD_DIM), BF16),
        jax.ShapeDtypeStruct((B, T, N_KV_HEADS * HEAD_DIM), F32),
        jax.ShapeDtypeStruct((B, T, N_KV_HEADS * HEAD_DIM), F32),
        jax.ShapeDtypeStruct((B, N_IDX_HEADS, T, 4 * IDX_DIM), BF16),
        jax.ShapeDtypeStruct((B, T, LANES), F32),
        jax.ShapeDtypeStruct((B, T, D), F32),
        jax.ShapeDtypeStruct((B, T, D), F32),
    )
    out_specs = (
        pl.BlockSpec((1, tm, POOL_WIDTH), row),
        pl.BlockSpec((1, N_HEADS, tm, HEAD_DIM), lambda b, i: (b, 0, i, 0)),
        pl.BlockSpec((1, tm, N_KV_HEADS * HEAD_DIM), row),
        pl.BlockSpec((1, tm, N_KV_HEADS * HEAD_DIM), row),
        pl.BlockSpec((1, N_IDX_HEADS, tm, 4 * IDX_DIM), lambda b, i: (b, 0, i, 0)),
        pl.BlockSpec((1, tm, LANES), row),
        pl.BlockSpec((1, tm, D), row),
        pl.BlockSpec((1, tm, D), row),
    )
    in_specs = [
        pl.BlockSpec((1, tm, D), row),
        _resident((1, D), const2),
        _resident((D, W), const2),
        pl.BlockSpec((1, HALO // 2, POOL_WIDTH), lambda b, i: (b, 0, 0)),
        pl.BlockSpec((tm, LANES), tab), pl.BlockSpec((tm, LANES), tab),
        pl.BlockSpec((tm, LANES), tab), pl.BlockSpec((tm, LANES), tab),
        _resident((N_POOL_GROUPS, POOL_GROUP_WIDTH, D // N_POOL_GROUPS), lambda b, i: (0, 0, 0)),
        _resident((1, D), const2),
    ]
    if key_tile is not None:
        assert key_tile % tm == 0 and T % key_tile == 0 and tm % LANES == 0
        per = key_tile // tm
        n_tiles = T // key_tile
        sub = tm // LANES
        out_shape = out_shape[:4] + (
            jax.ShapeDtypeStruct((B, T // LANES, 4 * IDX_DIM, N_IDX_HEADS * LANES), BF16),
        ) + out_shape[5:] + (
            jax.ShapeDtypeStruct((B, N_KV_HEADS, n_tiles, HEAD_DIM, key_tile), BF16),
            jax.ShapeDtypeStruct((B, N_KV_HEADS, n_tiles, key_tile, HEAD_DIM), BF16),
            jax.ShapeDtypeStruct((B, n_tiles, key_tile, 4 * IDX_DIM), BF16),
            jax.ShapeDtypeStruct((B, T // LANES, N_IDX_HEADS, LANES), F32),
        )
        out_specs = out_specs[:4] + (
            pl.BlockSpec((1, sub, 4 * IDX_DIM, N_IDX_HEADS * LANES), lambda b, i: (b, i, 0, 0)),
        ) + out_specs[5:] + (
            pl.BlockSpec((1, N_KV_HEADS, 1, HEAD_DIM, tm), lambda b, i: (b, 0, i // per, 0, i % per)),
            pl.BlockSpec((1, N_KV_HEADS, 1, tm, HEAD_DIM), lambda b, i: (b, 0, i // per, i % per, 0)),
            pl.BlockSpec((1, 1, tm, 4 * IDX_DIM), lambda b, i: (b, i // per, i % per, 0)),
            pl.BlockSpec((1, sub, N_IDX_HEADS, LANES), lambda b, i: (b, i, 0, 0)),
        )
    scratch = [pltpu.VMEM((HALO + tm, POOL_WIDTH), F32) for _ in range(4)]
    return pl.pallas_call(
        functools.partial(_proj_kernel, tm=tm, pos0=pos0, d_model=D, key_tiles=key_tile is not None),
        grid=(B, T // tm), in_specs=in_specs, out_specs=out_specs, out_shape=out_shape,
        scratch_shapes=scratch, name="proj",
        compiler_params=pltpu.CompilerParams(dimension_semantics=("arbitrary", "arbitrary"),
                                             vmem_limit_bytes=VMEM_LIMIT),
    )(x, g_pre, w_in_r, hist16, cq, sq, ci, si, w_pool, pool_scale)


def _rope_tables(pos, dim):
    rot = dim // 4
    half = rot // 2
    inv = ROPE_THETA ** (-jnp.arange(half, dtype=F32) / half)
    ang = pos.astype(F32)[:, None] * inv[None, :]
    cos, sin = jnp.cos(ang), jnp.sin(ang)
    rest = dim - rot
    n = pos.shape[0]
    c = jnp.concatenate([cos, cos, jnp.ones((n, rest), F32)], axis=-1)
    s = jnp.concatenate([-sin, sin, jnp.zeros((n, rest), F32)], axis=-1)
    return jnp.tile(c, (1, LANES // dim)), jnp.tile(s, (1, LANES // dim))


def _split_kernel(k2_ref, out_ref):
    k2 = k2_ref[0]
    hi = k2.astype(BF16)
    out_ref[0, :, 0:LANES] = hi
    out_ref[0, :, LANES:2 * LANES] = (k2 - hi.astype(F32)).astype(BF16)


def _split_keys(ki2, ts):
    B, Lp, _ = ki2.shape
    rows = ts * max(1, SPLIT_ROWS // ts)
    rows = rows if Lp % rows == 0 else ts
    return pl.pallas_call(
        _split_kernel, grid=(B, Lp // rows),
        in_specs=[pl.BlockSpec((1, rows, LANES), lambda b, t: (b, t, 0))],
        out_specs=pl.BlockSpec((1, rows, 2 * LANES), lambda b, t: (b, t, 0)),
        out_shape=jax.ShapeDtypeStruct((B, Lp, 2 * LANES), BF16), name="split_keys",
        compiler_params=pltpu.CompilerParams(dimension_semantics=("arbitrary", "arbitrary")),
    )(ki2)


def _key_of(x):
    b = pltpu.bitcast(x, I32)
    return b ^ ((b >> 31) & 0x7FFFFFFF)


def _float_of(k):
    return pltpu.bitcast(k ^ ((k >> 31) & 0x7FFFFFFF), F32)


def _row_sum(x):
    return jnp.broadcast_to(jnp.sum(x, axis=-1, keepdims=True), x.shape)


def _attn_kernel(qs_ref, kiw_ref, q_ref, ki_ref, kt_ref, v_ref, o_ref,
                 sc_ref, mm_ref, wb_ref, s_ref, m_ref, acc_ref, bias_ref, p_ref, alpha_ref,
                 *, tq, ts, q0, n_keys, k_sel):
    qi = pl.program_id(1)
    n_tiles_total = ki_ref.shape[1] // ts
    nc = ts // LANES
    nt_dims = (((1,), (1,)), ((), ()))

    def key_rows(t):
        return pl.ds(pl.multiple_of(t * ts, ts), ts)

    row = lax.broadcasted_iota(I32, (tq, LANES), 0)
    lane = lax.broadcasted_iota(I32, (tq, LANES), 1)
    qpos = q0 + qi * tq + row
    n_adm = jnp.minimum((qpos // CHUNK + 1) * CHUNK, n_keys)
    last_adm = jnp.minimum(((q0 + (qi + 1) * tq - 1) // CHUNK + 1) * CHUNK, n_keys)
    n_t = jnp.minimum((last_adm + ts - 1) // ts, n_tiles_total)

    kiw = kiw_ref[0]
    for hd in range(N_IDX_HEADS):
        wb_ref[hd] = jnp.broadcast_to(kiw[:, IDX_DIM + hd:IDX_DIM + hd + 1], (tq, LANES))
    qs2 = qs_ref[0].reshape(N_IDX_HEADS * tq, 4 * IDX_DIM)

    def idx_logits(t, slot):
        s_ref[slot] = lax.dot_general(qs2, ki_ref[0, key_rows(t), :], nt_dims,
                                      preferred_element_type=F32)

    rc = min(tq, ROW_CHUNK)
    mm_ref[0] = jnp.full((tq, LANES), _POS_INF, F32)
    mm_ref[1] = jnp.full((tq, LANES), _NEG_INF, F32)

    def score_tile(t, slot, masked):
        for r0 in range(0, tq, rc):
            rows = slice(r0, r0 + rc)
            mn, mx = mm_ref[0, rows], mm_ref[1, rows]
            for c in range(nc):
                cs = slice(c * LANES, (c + 1) * LANES)
                acc = None
                for hd in range(N_IDX_HEADS):
                    r = jnp.maximum(s_ref[slot, hd * tq + r0:hd * tq + r0 + rc, cs], 0.0) * wb_ref[hd, rows]
                    acc = r if acc is None else acc + r
                key = _key_of(acc)
                if masked:
                    qpos_c = q0 + qi * tq + r0 + lax.broadcasted_iota(I32, (rc, LANES), 0)
                    n_adm_c = jnp.minimum((qpos_c // CHUNK + 1) * CHUNK, n_keys)
                    adm = (t * ts + c * LANES + lax.broadcasted_iota(I32, (rc, LANES), 1)) < n_adm_c
                    key = jnp.where(adm, key, _INADMISSIBLE)
                    mx = jnp.maximum(mx, jnp.where(adm, acc, _NEG_INF))
                    mn = jnp.minimum(mn, jnp.where(adm, acc, _POS_INF))
                else:
                    mx = jnp.maximum(mx, acc)
                    mn = jnp.minimum(mn, acc)
                sc_ref[t, rows, cs] = key
            mm_ref[0, rows] = mn
            mm_ref[1, rows] = mx

    n_full = jnp.minimum(jnp.minimum(((q0 + qi * tq) // CHUNK + 1) * CHUNK, n_keys) // ts, n_t)
    last_full = jnp.maximum(n_full - 1, 0)
    idx_logits(0, 0)

    def score_pair(u, carry):
        t0 = 2 * u
        t1 = jnp.minimum(t0 + 1, last_full)
        idx_logits(t1, 1)
        score_tile(t0, 0, False)
        idx_logits(jnp.minimum(t0 + 2, last_full), 0)
        score_tile(t1, 1, False)
        return carry

    lax.fori_loop(0, (n_full + 1) // 2, score_pair, 0)

    def score_tail(t, carry):
        idx_logits(t, 0)
        score_tile(t, 0, True)
        return carry

    lax.fori_loop(n_full, n_t, score_tail, 0)
    rmin = jnp.broadcast_to(jnp.min(mm_ref[0], axis=-1, keepdims=True), (tq, LANES))
    rmax = jnp.broadcast_to(jnp.max(mm_ref[1], axis=-1, keepdims=True), (tq, LANES))

    def count_ge(thr):
        def body(t, cnt):
            for c in range(nc):
                cnt = cnt + jnp.where(sc_ref[t, :, c * LANES:(c + 1) * LANES] >= thr, 1.0, 0.0)
            return cnt
        return _row_sum(lax.fori_loop(0, n_t, body, jnp.zeros((tq, LANES), F32)))

    def bis_cond(st):
        return st[0] > 0

    kf = float(k_sel)
    take_all = n_adm <= k_sel
    lo0 = _key_of(rmin) - 1
    hi0 = _key_of(rmax) + 2
    c_pos = count_ge(jnp.full((tq, LANES), _KEY_MIN_NORMAL, I32))
    c_nn = count_ge(jnp.full((tq, LANES), _KEY_NEG_ZERO, I32))
    pos = c_pos >= kf
    neg = c_nn < kf
    lo = jnp.where(pos, _KEY_MIN_NORMAL, jnp.where(neg, lo0, _KEY_NEG_ZERO))
    hi = jnp.where(pos, hi0, jnp.where(neg, _KEY_NEG_ZERO, _KEY_MIN_NORMAL))
    clo = jnp.where(pos, c_pos, jnp.where(neg, n_adm.astype(F32), c_nn))
    chi = jnp.where(pos, 0.0, jnp.where(neg, c_nn, c_pos))
    exact0 = jnp.where(clo == kf, 1, 0)
    act0 = jnp.where(take_all, 0, jnp.where(exact0 > 0, 0, jnp.where(hi > lo + 1, 1, 0)))

    def bis_step(it, lo, hi, clo, chi, exact, act):
        kmid = (lo & hi) + ((lo ^ hi) >> 1)
        vmid = _key_of((_float_of(lo) + _float_of(hi)) * 0.5)
        by_value = jnp.where(it < VALUE_PIVOT_STEPS, jnp.where(vmid > lo, jnp.where(vmid < hi, 1, 0), 0), 0)
        mid = jnp.where(by_value > 0, vmid, kmid)
        cnt = count_ge(mid)
        up = jnp.where(act > 0, jnp.where(cnt >= kf, 1, 0), 0)
        dn = act - up
        lo = jnp.where(up > 0, mid, lo)
        clo = jnp.where(up > 0, cnt, clo)
        hi = jnp.where(dn > 0, mid, hi)
        chi = jnp.where(dn > 0, cnt, chi)
        hit = jnp.where(up > 0, jnp.where(cnt == kf, 1, 0), 0)
        exact = exact + hit
        act = jnp.where(act > 0, jnp.where(hit > 0, 0, jnp.where(hi > lo + 1, 1, 0)), 0)
        return it + 1, lo, hi, clo, chi, exact, act

    def any_active(act):
        return jnp.max(act.astype(F32))

    def bis_body(st):
        st = bis_step(*bis_step(*st[1:]))
        return (any_active(st[-1]),) + st

    _, _, lo, _, clo, chi, exact, _ = lax.while_loop(
        bis_cond, bis_body, (any_active(act0), jnp.int32(0), lo, hi, clo, chi, exact0, act0))
    thr = jnp.where(take_all, _INADMISSIBLE + 1, lo)
    thr_up = jnp.where(take_all, _INADMISSIBLE + 1, lo + 1)

    need = kf - chi
    tied = jnp.where(take_all, 0, jnp.where(exact > 0, 0, jnp.where(clo > kf, 1, 0)))

    def count_tied_below(cut):
        def body(t, cnt):
            for c in range(nc):
                s = sc_ref[t, :, c * LANES:(c + 1) * LANES]
                col = t * ts + c * LANES + lane
                cnt = cnt + jnp.where(s == thr, jnp.where(col < cut, 1.0, 0.0), 0.0)
            return cnt
        return _row_sum(lax.fori_loop(0, n_t, body, jnp.zeros((tq, LANES), F32)))

    def cut_body(st):
        _, lo_c, hi_c, act = st
        mid = (lo_c + hi_c) >> 1
        cnt = count_tied_below(mid)
        on = act > 0
        le = cnt <= need
        lo_c = jnp.where(on, jnp.where(le, mid, lo_c), lo_c)
        hi_c = jnp.where(on, jnp.where(le, hi_c, mid), hi_c)
        act = jnp.where(on, jnp.where(hi_c - lo_c > 1, 1, 0), 0)
        return jnp.max(act), lo_c, hi_c, act

    _, cut_lo, _, _ = lax.while_loop(
        bis_cond, cut_body,
        (jnp.max(tied), jnp.zeros((tq, LANES), I32), jnp.full((tq, LANES), n_tiles_total * ts + 1, I32), tied))
    cut = jnp.where(tied > 0, cut_lo, n_tiles_total * ts + 1)

    m_ref[...] = jnp.full(m_ref.shape, _M_INIT, F32)
    acc_ref[...] = jnp.zeros(acc_ref.shape, F32)
    rc = min(tq, ROW_CHUNK)
    ones = jnp.ones((ts, LANES), BF16)
    last = n_t - 1

    def logits(t, slot):
        for g in range(N_KV_HEADS):
            qg = q_ref[0, g * GQA_GROUP:(g + 1) * GQA_GROUP].reshape(GQA_GROUP * tq, HEAD_DIM)
            s_ref[slot, g * GQA_GROUP * tq:(g + 1) * GQA_GROUP * tq] = lax.dot_general(
                qg, kt_ref[0, key_rows(t), g * HEAD_DIM:(g + 1) * HEAD_DIM], nt_dims,
                preferred_element_type=F32)

    def softmax_pv(t, slot, live):
        for c in range(nc):
            col = t * ts + c * LANES + lane
            bound = jnp.where(live, jnp.where(col < cut, thr, thr_up), _NO_KEY)
            bias_ref[slot, :, c * LANES:(c + 1) * LANES] = jnp.where(
                sc_ref[t, :, c * LANES:(c + 1) * LANES] >= bound, 0.0, _NEG_INF)
        for g in range(N_KV_HEADS):
            for j in range(GQA_GROUP):
                hd = g * GQA_GROUP + j
                for r0 in range(0, tq, rc):
                    r1 = j * tq + r0
                    r2 = hd * tq + r0
                    s = [s_ref[slot, r2:r2 + rc, c * LANES:(c + 1) * LANES]
                         + bias_ref[slot, r0:r0 + rc, c * LANES:(c + 1) * LANES] for c in range(nc)]
                    mx = s[0]
                    for c in range(1, nc):
                        mx = jnp.maximum(mx, s[c])
                    m_old = m_ref[hd, r0:r0 + rc]
                    m_new = jnp.maximum(m_old, jnp.broadcast_to(jnp.max(mx, axis=-1, keepdims=True), (rc, LANES)))
                    alpha_ref[slot, g, r1:r1 + rc] = jnp.exp2(m_old - m_new)
                    for c in range(nc):
                        p_ref[slot, g, r1:r1 + rc, c * LANES:(c + 1) * LANES] = jnp.exp2(s[c] - m_new).astype(BF16)
                    m_ref[hd, r0:r0 + rc] = m_new
            v_aug = jnp.concatenate([v_ref[0, key_rows(t), g * HEAD_DIM:(g + 1) * HEAD_DIM], ones], axis=-1)
            pv = jnp.dot(p_ref[slot, g], v_aug, preferred_element_type=F32)
            alpha = alpha_ref[slot, g]
            for half in range(2):
                cols = slice(half * LANES, (half + 1) * LANES)
                acc_ref[g, :, cols] = alpha * acc_ref[g, :, cols] + pv[:, cols]

    logits(0, 0)

    def attn_pair(u, carry):
        t0 = 2 * u
        t1 = jnp.minimum(t0 + 1, last)
        logits(t1, 1)
        softmax_pv(t0, 0, True)
        logits(jnp.minimum(t0 + 2, last), 0)
        softmax_pv(t1, 1, t0 + 1 < n_t)
        return carry

    lax.fori_loop(0, (n_t + 1) // 2, attn_pair, 0)
    for hd in range(N_HEADS):
        g, j = divmod(hd, GQA_GROUP)
        o_ref[0, :, hd * HEAD_DIM:(hd + 1) * HEAD_DIM] = (
            acc_ref[g, j * tq:(j + 1) * tq, 0:HEAD_DIM] / acc_ref[g, j * tq:(j + 1) * tq, HEAD_DIM:2 * HEAD_DIM])


def _attn_t_kernel(qst_ref, wrow_ref, q_ref, ki_ref, kt_ref, v_ref, o_ref,
                   sc_ref, mm_ref, st_ref, s_ref, m_ref, acc_ref, bias_ref, p_ref, alpha_ref, kmax_ref,
                   *, tq, ts, q0, n_keys, k_sel):
    qi = pl.program_id(1)
    n_tiles_total = ki_ref.shape[1]
    nc = ts // LANES
    rk = KEY_ROWS
    qpos = q0 + qi * tq + lax.broadcasted_iota(I32, (1, tq), 1)
    n_adm = jnp.minimum((qpos // CHUNK + 1) * CHUNK, n_keys)
    last_adm = jnp.minimum(((q0 + (qi + 1) * tq - 1) // CHUNK + 1) * CHUNK, n_keys)
    n_t = jnp.minimum((last_adm + ts - 1) // ts, n_tiles_total)
    n_full = jnp.minimum(jnp.minimum(((q0 + qi * tq) // CHUNK + 1) * CHUNK, n_keys) // ts, n_t)
    last_full = jnp.maximum(n_full - 1, 0)
    qst = qst_ref[0, 0]
    wr = wrow_ref[0, 0]

    mm_ref[0] = jnp.full((rk, tq), _POS_INF, F32)
    mm_ref[1] = jnp.full((rk, tq), _NEG_INF, F32)

    def idx_logits(t, slot):
        st_ref[slot] = jnp.dot(ki_ref[0, t], qst, preferred_element_type=F32)

    def score_tile(t, slot, masked):
        mn, mx = mm_ref[0], mm_ref[1]
        for r0 in range(0, ts, rk):
            acc = None
            for hd in range(N_IDX_HEADS):
                r = jnp.maximum(st_ref[slot, r0:r0 + rk, hd * tq:(hd + 1) * tq], 0.0) * wr[hd:hd + 1, :]
                acc = r if acc is None else acc + r
            key = _key_of(acc)
            if masked:
                adm = (t * ts + r0 + lax.broadcasted_iota(I32, (rk, tq), 0)) < n_adm
                key = jnp.where(adm, key, _INADMISSIBLE)
                mx = jnp.maximum(mx, jnp.where(adm, acc, _NEG_INF))
                mn = jnp.minimum(mn, jnp.where(adm, acc, _POS_INF))
            else:
                mx = jnp.maximum(mx, acc)
                mn = jnp.minimum(mn, acc)
            sc_ref[t, r0:r0 + rk, :] = key
        mm_ref[0] = mn
        mm_ref[1] = mx

    idx_logits(0, 0)

    def score_pair(u, carry):
        t0 = 2 * u
        t1 = jnp.minimum(t0 + 1, last_full)
        idx_logits(t1, 1)
        score_tile(t0, 0, False)
        idx_logits(jnp.minimum(t0 + 2, last_full), 0)
        score_tile(t1, 1, False)
        return carry

    lax.fori_loop(0, (n_full + 1) // 2, score_pair, 0)

    def score_tail(t, carry):
        idx_logits(t, 0)
        score_tile(t, 0, True)
        return carry

    lax.fori_loop(n_full, n_t, score_tail, 0)
    rmin = jnp.min(mm_ref[0], axis=0, keepdims=True)
    rmax = jnp.max(mm_ref[1], axis=0, keepdims=True)

    def count_ge(thr):
        def body(t, cnt):
            for r0 in range(0, ts, COUNT_ROWS):
                cnt = cnt + jnp.where(sc_ref[t, r0:r0 + COUNT_ROWS, :] >= thr, 1.0, 0.0)
            return cnt
        return jnp.sum(lax.fori_loop(0, n_t, body, jnp.zeros((COUNT_ROWS, tq), F32)), axis=0, keepdims=True)

    def bis_cond(st):
        return st[0] > 0

    kf = float(k_sel)
    take_all = n_adm <= k_sel
    lo0 = _key_of(rmin) - 1
    hi0 = _key_of(rmax) + 2
    c_pos = count_ge(jnp.full((1, tq), _KEY_MIN_NORMAL, I32))
    c_nn = count_ge(jnp.full((1, tq), _KEY_NEG_ZERO, I32))
    pos = c_pos >= kf
    neg = c_nn < kf
    lo = jnp.where(pos, _KEY_MIN_NORMAL, jnp.where(neg, lo0, _KEY_NEG_ZERO))
    hi = jnp.where(pos, hi0, jnp.where(neg, _KEY_NEG_ZERO, _KEY_MIN_NORMAL))
    clo = jnp.where(pos, c_pos, jnp.where(neg, n_adm.astype(F32), c_nn))
    chi = jnp.where(pos, 0.0, jnp.where(neg, c_nn, c_pos))
    exact0 = jnp.where(clo == kf, 1, 0)
    act0 = jnp.where(take_all, 0, jnp.where(exact0 > 0, 0, jnp.where(hi > lo + 1, 1, 0)))

    def bis_step(it, lo, hi, clo, chi, exact, act):
        kmid = (lo & hi) + ((lo ^ hi) >> 1)
        vmid = _key_of((_float_of(lo) + _float_of(hi)) * 0.5)
        by_value = jnp.where(it < VALUE_PIVOT_STEPS, jnp.where(vmid > lo, jnp.where(vmid < hi, 1, 0), 0), 0)
        mid = jnp.where(by_value > 0, vmid, kmid)
        cnt = count_ge(mid)
        up = jnp.where(act > 0, jnp.where(cnt >= kf, 1, 0), 0)
        dn = act - up
        lo = jnp.where(up > 0, mid, lo)
        clo = jnp.where(up > 0, cnt, clo)
        hi = jnp.where(dn > 0, mid, hi)
        chi = jnp.where(dn > 0, cnt, chi)
        hit = jnp.where(up > 0, jnp.where(cnt == kf, 1, 0), 0)
        exact = exact + hit
        act = jnp.where(act > 0, jnp.where(hit > 0, 0, jnp.where(hi > lo + 1, 1, 0)), 0)
        return it + 1, lo, hi, clo, chi, exact, act

    def any_active(act):
        return jnp.max(act.astype(F32))

    def bis_body(st):
        st = bis_step(*bis_step(*st[1:]))
        return (any_active(st[-1]),) + st

    _, _, lo, _, clo, chi, exact, _ = lax.while_loop(
        bis_cond, bis_body, (any_active(act0), jnp.int32(0), lo, hi, clo, chi, exact0, act0))
    thr = jnp.where(take_all, _INADMISSIBLE + 1, lo)
    thr_up = jnp.where(take_all, _INADMISSIBLE + 1, lo + 1)

    need = kf - chi
    tied = jnp.where(take_all, 0, jnp.where(exact > 0, 0, jnp.where(clo > kf, 1, 0)))

    def count_tied_below(cut):
        def body(t, cnt):
            for r0 in range(0, ts, COUNT_ROWS):
                kidx = t * ts + r0 + lax.broadcasted_iota(I32, (COUNT_ROWS, tq), 0)
                cnt = cnt + jnp.where(sc_ref[t, r0:r0 + COUNT_ROWS, :] == thr,
                                      jnp.where(kidx < cut, 1.0, 0.0), 0.0)
            return cnt
        return jnp.sum(lax.fori_loop(0, n_t, body, jnp.zeros((COUNT_ROWS, tq), F32)), axis=0, keepdims=True)

    def cut_body(st):
        _, lo_c, hi_c, act = st
        mid = (lo_c + hi_c) >> 1
        cnt = count_tied_below(mid)
        on = act > 0
        le = cnt <= need
        lo_c = jnp.where(on, jnp.where(le, mid, lo_c), lo_c)
        hi_c = jnp.where(on, jnp.where(le, hi_c, mid), hi_c)
        act = jnp.where(on, jnp.where(hi_c - lo_c > 1, 1, 0), 0)
        return jnp.max(act), lo_c, hi_c, act

    _, cut_lo, _, _ = lax.while_loop(
        bis_cond, cut_body,
        (jnp.max(tied), jnp.zeros((1, tq), I32), jnp.full((1, tq), n_tiles_total * ts + 1, I32), tied))
    cut = jnp.where(tied > 0, cut_lo, n_tiles_total * ts + 1)

    acc_ref[...] = jnp.zeros(acc_ref.shape, F32)
    rc = min(tq, ROW_CHUNK)
    ones = jnp.ones((ts, LANES), BF16)
    last = n_t - 1

    def logits(t, slot):
        for g in range(N_KV_HEADS):
            qg = q_ref[0, g * GQA_GROUP:(g + 1) * GQA_GROUP].reshape(GQA_GROUP * tq, HEAD_DIM)
            s_ref[slot, g * GQA_GROUP * tq:(g + 1) * GQA_GROUP * tq] = jnp.dot(
                qg, kt_ref[0, g, t], preferred_element_type=F32)

    def selection_bias(t, c, live, masked_value):
        kidx = t * ts + c * LANES + lax.broadcasted_iota(I32, (LANES, tq), 0)
        bound = jnp.where(live, jnp.where(kidx < cut, thr, thr_up), _NO_KEY)
        return jnp.where(sc_ref[t, c * LANES:(c + 1) * LANES, :] >= bound, 0.0, masked_value).T

    def value_matmul(t, slot, g):
        v_aug = jnp.concatenate([v_ref[0, g, t], ones], axis=-1)
        return jnp.dot(p_ref[slot, g], v_aug, preferred_element_type=F32)

    def softmax_pv(t, slot, live):
        for c in range(nc):
            bias_ref[slot, :, c * LANES:(c + 1) * LANES] = selection_bias(t, c, live, _NEG_INF)
        for g in range(N_KV_HEADS):
            for j in range(GQA_GROUP):
                hd = g * GQA_GROUP + j
                for r0 in range(0, tq, rc):
                    r1 = j * tq + r0
                    r2 = hd * tq + r0
                    s = [s_ref[slot, r2:r2 + rc, c * LANES:(c + 1) * LANES]
                         + bias_ref[slot, r0:r0 + rc, c * LANES:(c + 1) * LANES] for c in range(nc)]
                    mx = s[0]
                    for c in range(1, nc):
                        mx = jnp.maximum(mx, s[c])
                    m_old = m_ref[hd, r0:r0 + rc]
                    m_new = jnp.maximum(m_old, jnp.broadcast_to(jnp.max(mx, axis=-1, keepdims=True), (rc, LANES)))
                    alpha_ref[slot, g, r1:r1 + rc] = jnp.exp2(m_old - m_new)
                    m_ref[hd, r0:r0 + rc] = m_new
                    for c in range(nc):
                        p_ref[slot, g, r1:r1 + rc, c * LANES:(c + 1) * LANES] = jnp.exp2(s[c] - m_new).astype(BF16)
            pv = value_matmul(t, slot, g)
            for half in range(2):
                cols = slice(half * LANES, (half + 1) * LANES)
                acc_ref[g, :, cols] = alpha_ref[slot, g] * acc_ref[g, :, cols] + pv[:, cols]

    def attend_running():
        logits(0, 0)

        def attn_pair(u, carry):
            t0 = 2 * u
            t1 = jnp.minimum(t0 + 1, last)
            logits(t1, 1)
            softmax_pv(t0, 0, True)
            logits(jnp.minimum(t0 + 2, last), 0)
            softmax_pv(t1, 1, t0 + 1 < n_t)
            return carry

        lax.fori_loop(0, (n_t + 1) // 2, attn_pair, 0)

    def softmax_fixed(t, slot, live):
        for c in range(nc):
            bias_ref[slot, :, c * LANES:(c + 1) * LANES] = selection_bias(t, c, live, _NEG_INF)
        for g in range(N_KV_HEADS):
            for j in range(GQA_GROUP):
                hd = g * GQA_GROUP + j
                for r0 in range(0, tq, rc):
                    ref_pt = m_ref[hd, r0:r0 + rc]
                    for c in range(nc):
                        cs = slice(c * LANES, (c + 1) * LANES)
                        s = s_ref[slot, hd * tq + r0:hd * tq + r0 + rc, cs] + bias_ref[slot, r0:r0 + rc, cs]
                        p_ref[slot, g, j * tq + r0:j * tq + r0 + rc, cs] = jnp.exp2(s - ref_pt).astype(BF16)

    def attend_fixed():
        logits(0, 0)

        def attn_pair(u, carry):
            t0 = 2 * u
            t1 = jnp.minimum(t0 + 1, last)
            logits(t1, 1)
            softmax_fixed(t0, 0, True)
            logits(jnp.minimum(t0 + 2, last), 0)
            softmax_fixed(t1, 1, t0 + 1 < n_t)
            for g in range(N_KV_HEADS):
                pv = value_matmul(t0, 0, g) + value_matmul(t1, 1, g)
                for half in range(2):
                    cols = slice(half * LANES, (half + 1) * LANES)
                    acc_ref[g, :, cols] = acc_ref[g, :, cols] + pv[:, cols]
            return carry

        lax.fori_loop(0, (n_t + 1) // 2, attn_pair, 0)

    @pl.when(qi == 0)
    def _():
        for g in range(N_KV_HEADS):
            def sq_norm_max(t, best):
                kk = kt_ref[0, g, t].astype(F32)
                return jnp.maximum(best, jnp.sum(kk * kk, axis=0, keepdims=True))
            best = lax.fori_loop(0, n_tiles_total, sq_norm_max, jnp.zeros((1, ts), F32))
            kmax_ref[g] = jnp.broadcast_to(jnp.max(best, axis=-1, keepdims=True), (8, LANES))

    top = jnp.zeros((tq, LANES), F32)
    for hd in range(N_HEADS):
        qf = q_ref[0, hd].astype(F32)
        q_sq = _row_sum(qf * qf)
        ref_pt = jnp.sqrt(q_sq * kmax_ref[hd // GQA_GROUP, 0:1, :]) * 1.001 + 1e-3
        m_ref[hd] = ref_pt
        top = jnp.maximum(top, ref_pt)
    fits = jnp.max(top) <= MAX_FIXED_REFERENCE

    @pl.when(fits)
    def _():
        attend_fixed()

    @pl.when(jnp.logical_not(fits))
    def _():
        m_ref[...] = jnp.full(m_ref.shape, _M_INIT, F32)
        attend_running()

    for hd in range(N_HEADS):
        g, j = divmod(hd, GQA_GROUP)
        o_ref[0, :, hd * HEAD_DIM:(hd + 1) * HEAD_DIM] = (
            acc_ref[g, j * tq:(j + 1) * tq, 0:HEAD_DIM] / acc_ref[g, j * tq:(j + 1) * tq, HEAD_DIM:2 * HEAD_DIM])


def _attend_t(qst, wrow, q_att, ki4, kt, v, q0, n_keys, tq, ts):
    B, _, T, _ = q_att.shape
    n_tiles = ki4.shape[1]
    k_sel = min(TOPK_MAX, n_keys // 4)
    kern = functools.partial(_attn_t_kernel, tq=tq, ts=ts, q0=q0, n_keys=n_keys, k_sel=k_sel)
    return pl.pallas_call(
        kern, grid=(B, T // tq),
        in_specs=[
            pl.BlockSpec((1, 1, 4 * IDX_DIM, N_IDX_HEADS * tq), lambda b, i: (b, i, 0, 0)),
            pl.BlockSpec((1, 1, N_IDX_HEADS, tq), lambda b, i: (b, i, 0, 0)),
            pl.BlockSpec((1, N_HEADS, tq, HEAD_DIM), lambda b, i: (b, 0, i, 0)),
            _resident((1, n_tiles, ts, 4 * IDX_DIM), lambda b, i: (b, 0, 0, 0)),
            _resident((1, N_KV_HEADS, n_tiles, HEAD_DIM, ts), lambda b, i: (b, 0, 0, 0, 0)),
            _resident((1, N_KV_HEADS, n_tiles, ts, HEAD_DIM), lambda b, i: (b, 0, 0, 0, 0)),
        ],
        out_specs=pl.BlockSpec((1, tq, N_HEADS * HEAD_DIM), lambda b, i: (b, i, 0)),
        out_shape=jax.ShapeDtypeStruct((B, T, N_HEADS * HEAD_DIM), F32),
        scratch_shapes=[
            pltpu.VMEM((n_tiles, ts, tq), I32),
            pltpu.VMEM((2, KEY_ROWS, tq), F32),
            pltpu.VMEM((2, ts, N_IDX_HEADS * tq), F32),
            pltpu.VMEM((2, N_HEADS * tq, ts), F32),
            pltpu.VMEM((N_HEADS, tq, LANES), F32),
            pltpu.VMEM((N_KV_HEADS, GQA_GROUP * tq, 2 * HEAD_DIM), F32),
            pltpu.VMEM((2, tq, ts), F32),
            pltpu.VMEM((2, N_KV_HEADS, GQA_GROUP * tq, ts), BF16),
            pltpu.VMEM((2, N_KV_HEADS, GQA_GROUP * tq, LANES), F32),
            pltpu.VMEM((N_KV_HEADS, 8, LANES), F32),
        ],
        name="attend_t",
        compiler_params=pltpu.CompilerParams(dimension_semantics=("arbitrary", "arbitrary"),
                                             vmem_limit_bytes=VMEM_LIMIT),
    )(qst, wrow, q_att, ki4, kt, v)


def _attend(qs, kiw, q_att, ki4, kt, v, q0, n_keys, tq, ts):
    B, _, T, _ = qs.shape
    lp = ki4.shape[1]
    n_tiles = lp // ts
    k_sel = min(TOPK_MAX, n_keys // 4)
    kern = functools.partial(_attn_kernel, tq=tq, ts=ts, q0=q0, n_keys=n_keys, k_sel=k_sel)
    keys = pl.BlockSpec((1, lp, 2 * LANES), lambda b, i: (b, 0, 0))
    return pl.pallas_call(
        kern, grid=(B, T // tq),
        in_specs=[
            pl.BlockSpec((1, N_IDX_HEADS, tq, 4 * IDX_DIM), lambda b, i: (b, 0, i, 0)),
            pl.BlockSpec((1, tq, LANES), lambda b, i: (b, i, 0)),
            pl.BlockSpec((1, N_HEADS, tq, HEAD_DIM), lambda b, i: (b, 0, i, 0)),
            keys, keys, keys,
        ],
        out_specs=pl.BlockSpec((1, tq, N_HEADS * HEAD_DIM), lambda b, i: (b, i, 0)),
        out_shape=jax.ShapeDtypeStruct((B, T, N_HEADS * HEAD_DIM), F32),
        scratch_shapes=[
            pltpu.VMEM((n_tiles, tq, ts), I32),
            pltpu.VMEM((2, tq, LANES), F32),
            pltpu.VMEM((N_IDX_HEADS, tq, LANES), F32),
            pltpu.VMEM((2, N_HEADS * tq, ts), F32),
            pltpu.VMEM((N_HEADS, tq, LANES), F32),
            pltpu.VMEM((N_KV_HEADS, GQA_GROUP * tq, 2 * HEAD_DIM), F32),
            pltpu.VMEM((2, tq, ts), F32),
            pltpu.VMEM((2, N_KV_HEADS, GQA_GROUP * tq, ts), BF16),
            pltpu.VMEM((2, N_KV_HEADS, GQA_GROUP * tq, LANES), F32),
        ],
        name="attend",
        compiler_params=pltpu.CompilerParams(dimension_semantics=("arbitrary", "arbitrary"),
                                             vmem_limit_bytes=VMEM_LIMIT),
    )(qs, kiw, q_att, ki4, kt, v)


def _rms(x, g):
    return x * lax.rsqrt(jnp.mean(x * x, axis=-1, keepdims=True) + RMS_EPS) * g


def _out_kernel(x_ref, pa_ref, sgb_ref, b_ref, wo_ref, g1_ref, g2_ref, wgu_ref, wd_ref, g3_ref, y_ref, *, d_ff):
    merged = pa_ref[...] + sgb_ref[...] * b_ref[...]
    mix = jnp.dot(merged.astype(BF16), wo_ref[...], preferred_element_type=F32)
    x1 = x_ref[...] + _rms(mix, g1_ref[...])
    h2 = _rms(x1, g2_ref[...]).astype(BF16)
    gu = jnp.dot(h2, wgu_ref[...], preferred_element_type=F32)
    gate, up = gu[:, :d_ff], gu[:, d_ff:]
    act = (gate * _sigmoid(gate) * up).astype(BF16)
    f = jnp.dot(act, wd_ref[...], preferred_element_type=F32)
    y_ref[...] = x1 + _rms(f, g3_ref[...])


def _finish(x, pa, sgb, b, w_out, g_post, g_ffn_pre, w_gate_up, w_down, g_ffn_post, tm):
    N, D = x.shape
    d_ff = w_down.shape[0]
    row = pl.BlockSpec((tm, D), lambda i: (i, 0))
    const = lambda i: (0, 0)
    return pl.pallas_call(
        functools.partial(_out_kernel, d_ff=d_ff), grid=(N // tm,),
        in_specs=[row, row, row, row,
                  _resident((D, D), const), _resident((1, D), const), _resident((1, D), const),
                  _resident((D, 2 * d_ff), const), _resident((d_ff, D), const), _resident((1, D), const)],
        out_specs=row, out_shape=jax.ShapeDtypeStruct((N, D), F32), name="finish",
        compiler_params=pltpu.CompilerParams(dimension_semantics=("arbitrary",), vmem_limit_bytes=VMEM_LIMIT),
    )(x, pa, sgb, b, w_out, g_post, g_ffn_pre, w_gate_up, w_down, g_ffn_post)


def _tile_keys(n_keys, ts):
    return -(-n_keys // ts) * ts


def _layer(x, pos0, hist, k_cache, v_cache, ki_cache, wts, tm, tq, ts, tm_out):
    (w_in_r, w_pool, pool_scale, w_out, w_gate_up, w_down, g_pre, g_post, g_ffn_pre, g_ffn_post) = wts
    B, T, D = x.shape
    hist16 = jnp.concatenate([jnp.zeros((B, HALO // 2 - POOL_HIST, POOL_WIDTH), F32), hist], axis=1)
    own_keys_only = k_cache is None and T % ts == 0 and ts % tm == 0 and tq == LANES and tm % LANES == 0
    outs = _project(x, hist16, pos0, w_in_r, g_pre, w_pool, pool_scale, tm, ts if own_keys_only else None)
    u, q_att, k, v, qs, kiw, pa, sgb = outs[:8]
    ki = kiw[:, :, :IDX_DIM]
    if own_keys_only:
        kt, vt, ki4, wrow = outs[8:]
        b = _attend_t(qs, wrow, q_att, ki4, kt, vt, pos0, T, tq, ts)
    else:
        if k_cache is None:
            k_all, v_all, ki_all = k, v, ki
        else:
            k_all = jnp.concatenate([k_cache.reshape(B, -1, N_KV_HEADS * HEAD_DIM), k], axis=1)
            v_all = jnp.concatenate([v_cache.reshape(B, -1, N_KV_HEADS * HEAD_DIM), v], axis=1)
            ki_all = jnp.concatenate([ki_cache, ki], axis=1)
        n_keys = k_all.shape[1]
        lp = _tile_keys(n_keys, ts)
        pad = ((0, 0), (0, lp - n_keys), (0, 0))
        kt = jnp.pad(k_all.astype(BF16), pad)
        vt = jnp.pad(v_all.astype(BF16), pad)
        ki4 = _split_keys(jnp.pad(jnp.concatenate([ki_all, ki_all], axis=-1), pad), ts)
        b = _attend(qs, kiw, q_att, ki4, kt, vt, pos0, n_keys, tq, ts)
    y = _finish(x.reshape(B * T, D), pa.reshape(B * T, D), sgb.reshape(B * T, D), b.reshape(B * T, D),
                w_out, g_post, g_ffn_pre, w_gate_up, w_down, g_ffn_post, tm_out).reshape(B, T, D)
    new_pool = jnp.concatenate([hist, u], axis=1)[:, T:]
    return (y, k.reshape(B, T, N_KV_HEADS, HEAD_DIM), v.reshape(B, T, N_KV_HEADS, HEAD_DIM), ki, new_pool)


def _relayout_w_in(w_in):
    d = w_in.shape[0]
    o_kiw = POOL_WIDTH + N_HEADS * HEAD_DIM + 2 * N_KV_HEADS * HEAD_DIM + N_IDX_HEADS * IDX_DIM
    narrow = IDX_DIM + N_IDX_HEADS
    padded = jnp.concatenate([w_in[:, :o_kiw + narrow], jnp.zeros((d, LANES - narrow), w_in.dtype),
                              w_in[:, o_kiw + narrow:]], axis=1)
    return padded.astype(BF16)


def kernel(x_prompt, x_sample, cache_k, cache_v, cache_k_idx, state_pool, w_in, w_pool, pool_scale, w_out,
           w_gate_up, w_down, norm_mix_pre, norm_mix_post, norm_ffn_pre, norm_ffn_post):
    depth = w_in.shape[0]
    past = cache_k.shape[2]
    t_p, t_s = x_prompt.shape[1], x_sample.shape[1]
    hist_p = jnp.zeros((x_prompt.shape[0], POOL_HIST, POOL_WIDTH), x_prompt.dtype)
    xp, xs = x_prompt, x_sample
    outs = [[] for _ in range(8)]
    for l in range(depth):
        wts = (_relayout_w_in(w_in[l]), w_pool[l].astype(BF16), pool_scale[l][None, :], w_out[l].astype(BF16),
               w_gate_up[l].astype(BF16), w_down[l].astype(BF16), norm_mix_pre[l][None, :],
               norm_mix_post[l][None, :], norm_ffn_pre[l][None, :], norm_ffn_post[l][None, :])
        tm_p = min(256, t_p)
        tq_p = min(128, t_p)
        xp, k1, v1, ki1, p1 = _layer(xp, 0, hist_p, None, None, None, wts, tm_p, tq_p, 512, tm_p)
        n_s = xs.shape[0] * t_s
        xs, k2, v2, ki2, p2 = _layer(xs, past, state_pool[l], cache_k[l], cache_v[l], cache_k_idx[l], wts,
                                     t_s, t_s, 512, min(256, n_s))
        for lst, val in zip(outs, (k1, v1, ki1, p1, k2, v2, ki2, p2)):
            lst.append(val)
    return (xp, xs) + tuple(jnp.stack(o) for o in outs)
```

```python
import functools

import jax
import jax.numpy as jnp
from jax import lax
from jax.experimental import pallas as pl
from jax.experimental.pallas import tpu as pltpu

F32 = jnp.float32
BF16 = jnp.bfloat16
I32 = jnp.int32

LANES = 128
CHUNK = 64
POOL_WINDOWS = (2, 4, 8, 16)
N_POOL_GROUPS = 4
POOL_GROUP_WIDTH = 128
POOL_WIDTH = N_POOL_GROUPS * POOL_GROUP_WIDTH
POOL_HIST = 15
N_HEADS = 8
N_KV_HEADS = 2
HEAD_DIM = 128
GQA_GROUP = N_HEADS // N_KV_HEADS
ROPE_THETA = 500000.0
N_IDX_HEADS = 8
IDX_DIM = 64
TOPK_MAX = 256
RMS_EPS = 1e-6
ATTN_SCALE = HEAD_DIM ** -0.5
IDX_SCALE = (N_IDX_HEADS ** -0.5) * (IDX_DIM ** -0.5)

LOG2_E = 1.4426950408889634
VALUE_PIVOT_STEPS = 28
MAX_FIXED_REFERENCE = 40.0
SPLIT_ROWS = 1536
KEY_ROWS = 16
COUNT_ROWS = 32
ROW_CHUNK = 32
HALO = 32
VMEM_LIMIT = 56 * 1024 * 1024

_NEG_INF = float("-inf")
_POS_INF = float("inf")
_INADMISSIBLE = -(2 ** 31)
_NO_KEY = 2 ** 31 - 1
_KEY_MIN_NORMAL = 0x00800000
_KEY_NEG_ZERO = -1
_M_INIT = -1e30


def _resident(block_shape, index_map):
    return pl.BlockSpec(block_shape, index_map, pipeline_mode=pl.Buffered(1))


def _rope(xs, cos, sin, half, period):
    lane = lax.broadcasted_iota(I32, xs.shape, 1)
    ahead = pltpu.roll(xs, LANES - half, 1)
    behind = pltpu.roll(xs, half, 1)
    partner = jnp.where((lane & (period - 1)) < half, ahead, behind)
    return xs * cos + partner * sin


def _sigmoid(x):
    return 1.0 / (1.0 + jnp.exp(-x))


def _proj_kernel(x_ref, g_ref, w_ref, hist_ref, cq_ref, sq_ref, ci_ref, si_ref, wpool_ref, pscale_ref,
                 u_ref, qatt_ref, k_ref, v_ref, qs_ref, kiw_ref, pa_ref, sgb_ref, *rest, tm, pos0, d_model, key_tiles):
    if key_tiles:
        kt_ref, vt_ref, ki4_ref, wrow_ref, e_ref, s2_ref, s4_ref, s8_ref = rest
    else:
        e_ref, s2_ref, s4_ref, s8_ref = rest
    i = pl.program_id(1)
    x = x_ref[0]
    h = x * lax.rsqrt(jnp.mean(x * x, axis=-1, keepdims=True) + RMS_EPS) * g_ref[...]
    proj = jnp.dot(h.astype(BF16), w_ref[...], preferred_element_type=F32)

    o_q = POOL_WIDTH
    o_k = o_q + N_HEADS * HEAD_DIM
    o_v = o_k + N_KV_HEADS * HEAD_DIM
    o_qi = o_v + N_KV_HEADS * HEAD_DIM
    o_kiw = o_qi + N_IDX_HEADS * IDX_DIM
    o_ga = o_kiw + LANES
    o_gb = o_ga + d_model

    cq, sq, ci, si = cq_ref[...], sq_ref[...], ci_ref[...], si_ref[...]
    half_q = HEAD_DIM // 8
    half_i = IDX_DIM // 8

    for hd in range(N_HEADS):
        qh = _rope(proj[:, o_q + hd * HEAD_DIM:o_q + (hd + 1) * HEAD_DIM], cq, sq, half_q, HEAD_DIM)
        qatt_ref[0, hd] = (qh * (ATTN_SCALE * LOG2_E)).astype(BF16)
    for kh in range(N_KV_HEADS):
        k_h = _rope(proj[:, o_k + kh * HEAD_DIM:o_k + (kh + 1) * HEAD_DIM], cq, sq, half_q, HEAD_DIM)
        k_ref[0, :, kh * HEAD_DIM:(kh + 1) * HEAD_DIM] = k_h
        if key_tiles:
            kt_ref[0, kh, 0] = k_h.T.astype(BF16)
            vt_ref[0, kh, 0] = proj[:, o_v + kh * HEAD_DIM:o_v + (kh + 1) * HEAD_DIM].astype(BF16)
    v_ref[0] = proj[:, o_v:o_qi]

    for pr in range(N_IDX_HEADS // 2):
        qi2 = _rope(proj[:, o_qi + pr * LANES:o_qi + (pr + 1) * LANES], ci, si, half_i, IDX_DIM)
        hi = qi2.astype(BF16).astype(F32)
        lo = qi2 - hi
        lane2 = lax.broadcasted_iota(I32, qi2.shape, 1)
        first = jnp.where(lane2 < IDX_DIM, hi, pltpu.roll(lo, IDX_DIM, 1))
        second = jnp.where(lane2 < IDX_DIM, pltpu.roll(hi, IDX_DIM, 1), lo)
        for sub, slab in enumerate((first, second)):
            hd = 2 * pr + sub
            if key_tiles:
                for sb in range(tm // LANES):
                    slab_t = slab[sb * LANES:(sb + 1) * LANES].T.astype(BF16)
                    qs_ref[0, sb, 0:LANES, hd * LANES:(hd + 1) * LANES] = slab_t
                    qs_ref[0, sb, LANES:2 * LANES, hd * LANES:(hd + 1) * LANES] = slab_t
            else:
                qs_ref[0, hd, :, 0:LANES] = slab.astype(BF16)
                qs_ref[0, hd, :, LANES:2 * LANES] = slab.astype(BF16)

    kiw = _rope(proj[:, o_kiw:o_kiw + LANES], ci, si, half_i, IDX_DIM)
    lane = lax.broadcasted_iota(I32, kiw.shape, 1)
    kiw_full = jnp.where(lane < IDX_DIM, kiw, proj[:, o_kiw:o_kiw + LANES] * IDX_SCALE)
    kiw_ref[0] = kiw_full
    if key_tiles:
        ki_hi = kiw.astype(BF16).astype(F32)
        ki_lo = kiw - ki_hi
        ki4_ref[0, 0, :, 0:LANES] = jnp.where(lane < IDX_DIM, ki_hi, pltpu.roll(ki_hi, IDX_DIM, 1)).astype(BF16)
        ki4_ref[0, 0, :, LANES:2 * LANES] = jnp.where(lane < IDX_DIM, ki_lo, pltpu.roll(ki_lo, IDX_DIM, 1)).astype(BF16)
        for sb in range(tm // LANES):
            wrow_ref[0, sb] = kiw_full[sb * LANES:(sb + 1) * LANES].T[IDX_DIM:IDX_DIM + N_IDX_HEADS]

    u = proj[:, 0:POOL_WIDTH]
    u_ref[0] = u

    @pl.when(i == 0)
    def _():
        e_ref[0:HALO // 2, :] = jnp.zeros((HALO // 2, POOL_WIDTH), F32)
        e_ref[HALO // 2:HALO, :] = hist_ref[0]

    e_ref[HALO:HALO + tm, :] = u
    n2, n4, n8 = tm + 24, tm + 16, tm + 8
    s2_ref[8:8 + n2, :] = e_ref[8:8 + n2, :] + e_ref[7:7 + n2, :]
    s4_ref[16:16 + n4, :] = s2_ref[16:16 + n4, :] + s2_ref[14:14 + n4, :]
    s8_ref[24:24 + n8, :] = s4_ref[24:24 + n8, :] + s4_ref[20:20 + n8, :]
    s16 = s8_ref[HALO:HALO + tm, :] + s8_ref[HALO - 8:HALO - 8 + tm, :]
    wins = (s2_ref[HALO:HALO + tm, :], s4_ref[HALO:HALO + tm, :], s8_ref[HALO:HALO + tm, :], s16)
    e_ref[HALO // 2:HALO, :] = e_ref[HALO // 2 + tm:HALO + tm, :]

    pos = pos0 + i * tm + lax.broadcasted_iota(I32, (tm, POOL_GROUP_WIDTH), 0)
    a_parts = []
    for g, w in enumerate(POOL_WINDOWS):
        lo_l, hi_l = g * POOL_GROUP_WIDTH, (g + 1) * POOL_GROUP_WIDTH
        cnt = jnp.minimum(pos + 1, w).astype(F32)
        pooled = wins[g][:, lo_l:hi_l] / cnt - u[:, lo_l:hi_l]
        a_parts.append(jnp.dot(pooled.astype(BF16), wpool_ref[g], preferred_element_type=F32))
    a = jnp.concatenate(a_parts, axis=-1) * pscale_ref[...]
    pa_ref[0] = _sigmoid(proj[:, o_ga:o_gb]) * a
    sgb_ref[0] = _sigmoid(proj[:, o_gb:o_gb + d_model])


def _project(x, hist16, pos0, w_in_r, g_pre, w_pool, pool_scale, tm, key_tile=None):
    B, T, D = x.shape
    W = w_in_r.shape[1]
    pos = pos0 + jnp.arange(T, dtype=I32)
    cq, sq = _rope_tables(pos, HEAD_DIM)
    ci, si = _rope_tables(pos, IDX_DIM)
    row = lambda b, i: (b, i, 0)
    tab = lambda b, i: (i, 0)
    const2 = lambda b, i: (0, 0)
    out_shape = (
        jax.ShapeDtypeStruct((B, T, POOL_WIDTH), F32),
        jax.ShapeDtypeStruct((B, N_HEADS, T, HEAD_DIM), BF16),
        jax.ShapeDtypeStruct((B, T, N_KV_HEADS * HEAD_DIM), F32),
        jax.ShapeDtypeStruct((B, T, N_KV_HEADS * HEAD_DIM), F32),
        jax.ShapeDtypeStruct((B, N_IDX_HEADS, T, 4 * IDX_DIM), BF16),
        jax.ShapeDtypeStruct((B, T, LANES), F32),
        jax.ShapeDtypeStruct((B, T, D), F32),
        jax.ShapeDtypeStruct((B, T, D), F32),
    )
    out_specs = (
        pl.BlockSpec((1, tm, POOL_WIDTH), row),
        pl.BlockSpec((1, N_HEADS, tm, HEAD_DIM), lambda b, i: (b, 0, i, 0)),
        pl.BlockSpec((1, tm, N_KV_HEADS * HEAD_DIM), row),
        pl.BlockSpec((1, tm, N_KV_HEADS * HEAD_DIM), row),
        pl.BlockSpec((1, N_IDX_HEADS, tm, 4 * IDX_DIM), lambda b, i: (b, 0, i, 0)),
        pl.BlockSpec((1, tm, LANES), row),
        pl.BlockSpec((1, tm, D), row),
        pl.BlockSpec((1, tm, D), row),
    )
    in_specs = [
        pl.BlockSpec((1, tm, D), row),
        _resident((1, D), const2),
        _resident((D, W), const2),
        pl.BlockSpec((1, HALO // 2, POOL_WIDTH), lambda b, i: (b, 0, 0)),
        pl.BlockSpec((tm, LANES), tab), pl.BlockSpec((tm, LANES), tab),
        pl.BlockSpec((tm, LANES), tab), pl.BlockSpec((tm, LANES), tab),
        _resident((N_POOL_GROUPS, POOL_GROUP_WIDTH, D // N_POOL_GROUPS), lambda b, i: (0, 0, 0)),
        _resident((1, D), const2),
    ]
    if key_tile is not None:
        assert key_tile % tm == 0 and T % key_tile == 0 and tm % LANES == 0
        per = key_tile // tm
        n_tiles = T // key_tile
        sub = tm // LANES
        out_shape = out_shape[:4] + (
            jax.ShapeDtypeStruct((B, T // LANES, 4 * IDX_DIM, N_IDX_HEADS * LANES), BF16),
        ) + out_shape[5:] + (
            jax.ShapeDtypeStruct((B, N_KV_HEADS, n_tiles, HEAD_DIM, key_tile), BF16),
            jax.ShapeDtypeStruct((B, N_KV_HEADS, n_tiles, key_tile, HEAD_DIM), BF16),
            jax.ShapeDtypeStruct((B, n_tiles, key_tile, 4 * IDX_DIM), BF16),
            jax.ShapeDtypeStruct((B, T // LANES, N_IDX_HEADS, LANES), F32),
        )
        out_specs = out_specs[:4] + (
            pl.BlockSpec((1, sub, 4 * IDX_DIM, N_IDX_HEADS * LANES), lambda b, i: (b, i, 0, 0)),
        ) + out_specs[5:] + (
            pl.BlockSpec((1, N_KV_HEADS, 1, HEAD_DIM, tm), lambda b, i: (b, 0, i // per, 0, i % per)),
            pl.BlockSpec((1, N_KV_HEADS, 1, tm, HEAD_DIM), lambda b, i: (b, 0, i // per, i % per, 0)),
            pl.BlockSpec((1, 1, tm, 4 * IDX_DIM), lambda b, i: (b, i // per, i % per, 0)),
            pl.BlockSpec((1, sub, N_IDX_HEADS, LANES), lambda b, i: (b, i, 0, 0)),
        )
    scratch = [pltpu.VMEM((HALO + tm, POOL_WIDTH), F32) for _ in range(4)]
    return pl.pallas_call(
        functools.partial(_proj_kernel, tm=tm, pos0=pos0, d_model=D, key_tiles=key_tile is not None),
        grid=(B, T // tm), in_specs=in_specs, out_specs=out_specs, out_shape=out_shape,
        scratch_shapes=scratch, name="proj",
        compiler_params=pltpu.CompilerParams(dimension_semantics=("arbitrary", "arbitrary"),
                                             vmem_limit_bytes=VMEM_LIMIT),
    )(x, g_pre, w_in_r, hist16, cq, sq, ci, si, w_pool, pool_scale)


def _rope_tables(pos, dim):
    rot = dim // 4
    half = rot // 2
    inv = ROPE_THETA ** (-jnp.arange(half, dtype=F32) / half)
    ang = pos.astype(F32)[:, None] * inv[None, :]
    cos, sin = jnp.cos(ang), jnp.sin(ang)
    rest = dim - rot
    n = pos.shape[0]
    c = jnp.concatenate([cos, cos, jnp.ones((n, rest), F32)], axis=-1)
    s = jnp.concatenate([-sin, sin, jnp.zeros((n, rest), F32)], axis=-1)
    return jnp.tile(c, (1, LANES // dim)), jnp.tile(s, (1, LANES // dim))


def _split_kernel(k2_ref, out_ref):
    k2 = k2_ref[0]
    hi = k2.astype(BF16)
    out_ref[0, :, 0:LANES] = hi
    out_ref[0, :, LANES:2 * LANES] = (k2 - hi.astype(F32)).astype(BF16)


def _split_keys(ki2, ts):
    B, Lp, _ = ki2.shape
    rows = ts * max(1, SPLIT_ROWS // ts)
    rows = rows if Lp % rows == 0 else ts
    return pl.pallas_call(
        _split_kernel, grid=(B, Lp // rows),
        in_specs=[pl.BlockSpec((1, rows, LANES), lambda b, t: (b, t, 0))],
        out_specs=pl.BlockSpec((1, rows, 2 * LANES), lambda b, t: (b, t, 0)),
        out_shape=jax.ShapeDtypeStruct((B, Lp, 2 * LANES), BF16), name="split_keys",
        compiler_params=pltpu.CompilerParams(dimension_semantics=("arbitrary", "arbitrary")),
    )(ki2)


def _key_of(x):
    b = pltpu.bitcast(x, I32)
    return b ^ ((b >> 31) & 0x7FFFFFFF)


def _float_of(k):
    return pltpu.bitcast(k ^ ((k >> 31) & 0x7FFFFFFF), F32)


def _row_sum(x):
    return jnp.broadcast_to(jnp.sum(x, axis=-1, keepdims=True), x.shape)


def _attn_kernel(qs_ref, kiw_ref, q_ref, ki_ref, kt_ref, v_ref, o_ref,
                 sc_ref, mm_ref, wb_ref, s_ref, m_ref, acc_ref, bias_ref, p_ref, alpha_ref,
                 *, tq, ts, q0, n_keys, k_sel):
    qi = pl.program_id(1)
    n_tiles_total = ki_ref.shape[1] // ts
    nc = ts // LANES
    nt_dims = (((1,), (1,)), ((), ()))

    def key_rows(t):
        return pl.ds(pl.multiple_of(t * ts, ts), ts)

    row = lax.broadcasted_iota(I32, (tq, LANES), 0)
    lane = lax.broadcasted_iota(I32, (tq, LANES), 1)
    qpos = q0 + qi * tq + row
    n_adm = jnp.minimum((qpos // CHUNK + 1) * CHUNK, n_keys)
    last_adm = jnp.minimum(((q0 + (qi + 1) * tq - 1) // CHUNK + 1) * CHUNK, n_keys)
    n_t = jnp.minimum((last_adm + ts - 1) // ts, n_tiles_total)

    kiw = kiw_ref[0]
    for hd in range(N_IDX_HEADS):
        wb_ref[hd] = jnp.broadcast_to(kiw[:, IDX_DIM + hd:IDX_DIM + hd + 1], (tq, LANES))
    qs2 = qs_ref[0].reshape(N_IDX_HEADS * tq, 4 * IDX_DIM)

    def idx_logits(t, slot):
        s_ref[slot] = lax.dot_general(qs2, ki_ref[0, key_rows(t), :], nt_dims,
                                      preferred_element_type=F32)

    rc = min(tq, ROW_CHUNK)
    mm_ref[0] = jnp.full((tq, LANES), _POS_INF, F32)
    mm_ref[1] = jnp.full((tq, LANES), _NEG_INF, F32)

    def score_tile(t, slot, masked):
        for r0 in range(0, tq, rc):
            rows = slice(r0, r0 + rc)
            mn, mx = mm_ref[0, rows], mm_ref[1, rows]
            for c in range(nc):
                cs = slice(c * LANES, (c + 1) * LANES)
                acc = None
                for hd in range(N_IDX_HEADS):
                    r = jnp.maximum(s_ref[slot, hd * tq + r0:hd * tq + r0 + rc, cs], 0.0) * wb_ref[hd, rows]
                    acc = r if acc is None else acc + r
                key = _key_of(acc)
                if masked:
                    qpos_c = q0 + qi * tq + r0 + lax.broadcasted_iota(I32, (rc, LANES), 0)
                    n_adm_c = jnp.minimum((qpos_c // CHUNK + 1) * CHUNK, n_keys)
                    adm = (t * ts + c * LANES + lax.broadcasted_iota(I32, (rc, LANES), 1)) < n_adm_c
                    key = jnp.where(adm, key, _INADMISSIBLE)
                    mx = jnp.maximum(mx, jnp.where(adm, acc, _NEG_INF))
                    mn = jnp.minimum(mn, jnp.where(adm, acc, _POS_INF))
                else:
                    mx = jnp.maximum(mx, acc)
                    mn = jnp.minimum(mn, acc)
                sc_ref[t, rows, cs] = key
            mm_ref[0, rows] = mn
            mm_ref[1, rows] = mx

    n_full = jnp.minimum(jnp.minimum(((q0 + qi * tq) // CHUNK + 1) * CHUNK, n_keys) // ts, n_t)
    last_full = jnp.maximum(n_full - 1, 0)
    idx_logits(0, 0)

    def score_pair(u, carry):
        t0 = 2 * u
        t1 = jnp.minimum(t0 + 1, last_full)
        idx_logits(t1, 1)
        score_tile(t0, 0, False)
        idx_logits(jnp.minimum(t0 + 2, last_full), 0)
        score_tile(t1, 1, False)
        return carry

    lax.fori_loop(0, (n_full + 1) // 2, score_pair, 0)

    def score_tail(t, carry):
        idx_logits(t, 0)
        score_tile(t, 0, True)
        return carry

    lax.fori_loop(n_full, n_t, score_tail, 0)
    rmin = jnp.broadcast_to(jnp.min(mm_ref[0], axis=-1, keepdims=True), (tq, LANES))
    rmax = jnp.broadcast_to(jnp.max(mm_ref[1], axis=-1, keepdims=True), (tq, LANES))

    def count_ge(thr):
        def body(t, cnt):
            for c in range(nc):
                cnt = cnt + jnp.where(sc_ref[t, :, c * LANES:(c + 1) * LANES] >= thr, 1.0, 0.0)
            return cnt
        return _row_sum(lax.fori_loop(0, n_t, body, jnp.zeros((tq, LANES), F32)))

    def bis_cond(st):
        return st[0] > 0

    kf = float(k_sel)
    take_all = n_adm <= k_sel
    lo0 = _key_of(rmin) - 1
    hi0 = _key_of(rmax) + 2
    c_pos = count_ge(jnp.full((tq, LANES), _KEY_MIN_NORMAL, I32))
    c_nn = count_ge(jnp.full((tq, LANES), _KEY_NEG_ZERO, I32))
    pos = c_pos >= kf
    neg = c_nn < kf
    lo = jnp.where(pos, _KEY_MIN_NORMAL, jnp.where(neg, lo0, _KEY_NEG_ZERO))
    hi = jnp.where(pos, hi0, jnp.where(neg, _KEY_NEG_ZERO, _KEY_MIN_NORMAL))
    clo = jnp.where(pos, c_pos, jnp.where(neg, n_adm.astype(F32), c_nn))
    chi = jnp.where(pos, 0.0, jnp.where(neg, c_nn, c_pos))
    exact0 = jnp.where(clo == kf, 1, 0)
    act0 = jnp.where(take_all, 0, jnp.where(exact0 > 0, 0, jnp.where(hi > lo + 1, 1, 0)))

    def bis_step(it, lo, hi, clo, chi, exact, act):
        kmid = (lo & hi) + ((lo ^ hi) >> 1)
        vmid = _key_of((_float_of(lo) + _float_of(hi)) * 0.5)
        by_value = jnp.where(it < VALUE_PIVOT_STEPS, jnp.where(vmid > lo, jnp.where(vmid < hi, 1, 0), 0), 0)
        mid = jnp.where(by_value > 0, vmid, kmid)
        cnt = count_ge(mid)
        up = jnp.where(act > 0, jnp.where(cnt >= kf, 1, 0), 0)
        dn = act - up
        lo = jnp.where(up > 0, mid, lo)
        clo = jnp.where(up > 0, cnt, clo)
        hi = jnp.where(dn > 0, mid, hi)
        chi = jnp.where(dn > 0, cnt, chi)
        hit = jnp.where(up > 0, jnp.where(cnt == kf, 1, 0), 0)
        exact = exact + hit
        act = jnp.where(act > 0, jnp.where(hit > 0, 0, jnp.where(hi > lo + 1, 1, 0)), 0)
        return it + 1, lo, hi, clo, chi, exact, act

    def any_active(act):
        return jnp.max(act.astype(F32))

    def bis_body(st):
        st = bis_step(*bis_step(*st[1:]))
        return (any_active(st[-1]),) + st

    _, _, lo, _, clo, chi, exact, _ = lax.while_loop(
        bis_cond, bis_body, (any_active(act0), jnp.int32(0), lo, hi, clo, chi, exact0, act0))
    thr = jnp.where(take_all, _INADMISSIBLE + 1, lo)
    thr_up = jnp.where(take_all, _INADMISSIBLE + 1, lo + 1)

    need = kf - chi
    tied = jnp.where(take_all, 0, jnp.where(exact > 0, 0, jnp.where(clo > kf, 1, 0)))

    def count_tied_below(cut):
        def body(t, cnt):
            for c in range(nc):
                s = sc_ref[t, :, c * LANES:(c + 1) * LANES]
                col = t * ts + c * LANES + lane
                cnt = cnt + jnp.where(s == thr, jnp.where(col < cut, 1.0, 0.0), 0.0)
            return cnt
        return _row_sum(lax.fori_loop(0, n_t, body, jnp.zeros((tq, LANES), F32)))

    def cut_body(st):
        _, lo_c, hi_c, act = st
        mid = (lo_c + hi_c) >> 1
        cnt = count_tied_below(mid)
        on = act > 0
        le = cnt <= need
        lo_c = jnp.where(on, jnp.where(le, mid, lo_c), lo_c)
        hi_c = jnp.where(on, jnp.where(le, hi_c, mid), hi_c)
        act = jnp.where(on, jnp.where(hi_c - lo_c > 1, 1, 0), 0)
        return jnp.max(act), lo_c, hi_c, act

    _, cut_lo, _, _ = lax.while_loop(
        bis_cond, cut_body,
        (jnp.max(tied), jnp.zeros((tq, LANES), I32), jnp.full((tq, LANES), n_tiles_total * ts + 1, I32), tied))
    cut = jnp.where(tied > 0, cut_lo, n_tiles_total * ts + 1)

    m_ref[...] = jnp.full(m_ref.shape, _M_INIT, F32)
    acc_ref[...] = jnp.zeros(acc_ref.shape, F32)
    rc = min(tq, ROW_CHUNK)
    ones = jnp.ones((ts, LANES), BF16)
    last = n_t - 1

    def logits(t, slot):
        for g in range(N_KV_HEADS):
            qg = q_ref[0, g * GQA_GROUP:(g + 1) * GQA_GROUP].reshape(GQA_GROUP * tq, HEAD_DIM)
            s_ref[slot, g * GQA_GROUP * tq:(g + 1) * GQA_GROUP * tq] = lax.dot_general(
                qg, kt_ref[0, key_rows(t), g * HEAD_DIM:(g + 1) * HEAD_DIM], nt_dims,
                preferred_element_type=F32)

    def softmax_pv(t, slot, live):
        for c in range(nc):
            col = t * ts + c * LANES + lane
            bound = jnp.where(live, jnp.where(col < cut, thr, thr_up), _NO_KEY)
            bias_ref[slot, :, c * LANES:(c + 1) * LANES] = jnp.where(
                sc_ref[t, :, c * LANES:(c + 1) * LANES] >= bound, 0.0, _NEG_INF)
        for g in range(N_KV_HEADS):
            for j in range(GQA_GROUP):
                hd = g * GQA_GROUP + j
                for r0 in range(0, tq, rc):
                    r1 = j * tq + r0
                    r2 = hd * tq + r0
                    s = [s_ref[slot, r2:r2 + rc, c * LANES:(c + 1) * LANES]
                         + bias_ref[slot, r0:r0 + rc, c * LANES:(c + 1) * LANES] for c in range(nc)]
                    mx = s[0]
                    for c in range(1, nc):
                        mx = jnp.maximum(mx, s[c])
                    m_old = m_ref[hd, r0:r0 + rc]
                    m_new = jnp.maximum(m_old, jnp.broadcast_to(jnp.max(mx, axis=-1, keepdims=True), (rc, LANES)))
                    alpha_ref[slot, g, r1:r1 + rc] = jnp.exp2(m_old - m_new)
                    for c in range(nc):
                        p_ref[slot, g, r1:r1 + rc, c * LANES:(c + 1) * LANES] = jnp.exp2(s[c] - m_new).astype(BF16)
                    m_ref[hd, r0:r0 + rc] = m_new
            v_aug = jnp.concatenate([v_ref[0, key_rows(t), g * HEAD_DIM:(g + 1) * HEAD_DIM], ones], axis=-1)
            pv = jnp.dot(p_ref[slot, g], v_aug, preferred_element_type=F32)
            alpha = alpha_ref[slot, g]
            for half in range(2):
                cols = slice(half * LANES, (half + 1) * LANES)
                acc_ref[g, :, cols] = alpha * acc_ref[g, :, cols] + pv[:, cols]

    logits(0, 0)

    def attn_pair(u, carry):
        t0 = 2 * u
        t1 = jnp.minimum(t0 + 1, last)
        logits(t1, 1)
        softmax_pv(t0, 0, True)
        logits(jnp.minimum(t0 + 2, last), 0)
        softmax_pv(t1, 1, t0 + 1 < n_t)
        return carry

    lax.fori_loop(0, (n_t + 1) // 2, attn_pair, 0)
    for hd in range(N_HEADS):
        g, j = divmod(hd, GQA_GROUP)
        o_ref[0, :, hd * HEAD_DIM:(hd + 1) * HEAD_DIM] = (
            acc_ref[g, j * tq:(j + 1) * tq, 0:HEAD_DIM] / acc_ref[g, j * tq:(j + 1) * tq, HEAD_DIM:2 * HEAD_DIM])


def _attn_t_kernel(qst_ref, wrow_ref, q_ref, ki_ref, kt_ref, v_ref, o_ref,
                   sc_ref, mm_ref, st_ref, s_ref, m_ref, acc_ref, bias_ref, p_ref, alpha_ref, kmax_ref,
                   *, tq, ts, q0, n_keys, k_sel):
    qi = pl.program_id(1)
    n_tiles_total = ki_ref.shape[1]
    nc = ts // LANES
    rk = KEY_ROWS
    qpos = q0 + qi * tq + lax.broadcasted_iota(I32, (1, tq), 1)
    n_adm = jnp.minimum((qpos // CHUNK + 1) * CHUNK, n_keys)
    last_adm = jnp.minimum(((q0 + (qi + 1) * tq - 1) // CHUNK + 1) * CHUNK, n_keys)
    n_t = jnp.minimum((last_adm + ts - 1) // ts, n_tiles_total)
    n_full = jnp.minimum(jnp.minimum(((q0 + qi * tq) // CHUNK + 1) * CHUNK, n_keys) // ts, n_t)
    last_full = jnp.maximum(n_full - 1, 0)
    qst = qst_ref[0, 0]
    wr = wrow_ref[0, 0]

    mm_ref[0] = jnp.full((rk, tq), _POS_INF, F32)
    mm_ref[1] = jnp.full((rk, tq), _NEG_INF, F32)

    def idx_logits(t, slot):
        st_ref[slot] = jnp.dot(ki_ref[0, t], qst, preferred_element_type=F32)

    def score_tile(t, slot, masked):
        mn, mx = mm_ref[0], mm_ref[1]
        for r0 in range(0, ts, rk):
            acc = None
            for hd in range(N_IDX_HEADS):
                r = jnp.maximum(st_ref[slot, r0:r0 + rk, hd * tq:(hd + 1) * tq], 0.0) * wr[hd:hd + 1, :]
                acc = r if acc is None else acc + r
            key = _key_of(acc)
            if masked:
                adm = (t * ts + r0 + lax.broadcasted_iota(I32, (rk, tq), 0)) < n_adm
                key = jnp.where(adm, key, _INADMISSIBLE)
                mx = jnp.maximum(mx, jnp.where(adm, acc, _NEG_INF))
                mn = jnp.minimum(mn, jnp.where(adm, acc, _POS_INF))
            else:
                mx = jnp.maximum(mx, acc)
                mn = jnp.minimum(mn, acc)
            sc_ref[t, r0:r0 + rk, :] = key
        mm_ref[0] = mn
        mm_ref[1] = mx

    idx_logits(0, 0)

    def score_pair(u, carry):
        t0 = 2 * u
        t1 = jnp.minimum(t0 + 1, last_full)
        idx_logits(t1, 1)
        score_tile(t0, 0, False)
        idx_logits(jnp.minimum(t0 + 2, last_full), 0)
        score_tile(t1, 1, False)
        return carry

    lax.fori_loop(0, (n_full + 1) // 2, score_pair, 0)

    def score_tail(t, carry):
        idx_logits(t, 0)
        score_tile(t, 0, True)
        return carry

    lax.fori_loop(n_full, n_t, score_tail, 0)
    rmin = jnp.min(mm_ref[0], axis=0, keepdims=True)
    rmax = jnp.max(mm_ref[1], axis=0, keepdims=True)

    def count_ge(thr):
        def body(t, cnt):
            for r0 in range(0, ts, COUNT_ROWS):
                cnt = cnt + jnp.where(sc_ref[t, r0:r0 + COUNT_ROWS, :] >= thr, 1.0, 0.0)
            return cnt
        return jnp.sum(lax.fori_loop(0, n_t, body, jnp.zeros((COUNT_ROWS, tq), F32)), axis=0, keepdims=True)

    def bis_cond(st):
        return st[0] > 0

    kf = float(k_sel)
    take_all = n_adm <= k_sel
    lo0 = _key_of(rmin) - 1
    hi0 = _key_of(rmax) + 2
    c_pos = count_ge(jnp.full((1, tq), _KEY_MIN_NORMAL, I32))
    pos = c_pos >= kf
    lo = jnp.where(pos, _KEY_MIN_NORMAL, jnp.minimum(lo0, _KEY_NEG_ZERO - 1))
    hi = jnp.where(pos, hi0, _KEY_MIN_NORMAL)
    clo = jnp.where(pos, c_pos, n_adm.astype(F32))
    chi = jnp.where(pos, 0.0, c_pos)
    exact0 = jnp.where(clo == kf, 1, 0)
    act0 = jnp.where(take_all, 0, jnp.where(exact0 > 0, 0, jnp.where(hi > lo + 1, 1, 0)))

    def bis_step(it, lo, hi, clo, chi, exact, act):
        kmid = (lo & hi) + ((lo ^ hi) >> 1)
        vmid = _key_of((_float_of(lo) + _float_of(hi)) * 0.5)
        by_value = jnp.where(it < VALUE_PIVOT_STEPS, jnp.where(vmid > lo, jnp.where(vmid < hi, 1, 0), 0), 0)
        mid = jnp.where(by_value > 0, vmid, kmid)
        at_zero = jnp.where(lo < _KEY_NEG_ZERO, jnp.where(hi == _KEY_MIN_NORMAL, 1, 0), 0)
        mid = jnp.where(at_zero > 0, _KEY_NEG_ZERO, mid)
        cnt = count_ge(mid)
        up = jnp.where(act > 0, jnp.where(cnt >= kf, 1, 0), 0)
        dn = act - up
        lo = jnp.where(up > 0, mid, lo)
        clo = jnp.where(up > 0, cnt, clo)
        hi = jnp.where(dn > 0, mid, hi)
        chi = jnp.where(dn > 0, cnt, chi)
        hit = jnp.where(up > 0, jnp.where(cnt == kf, 1, 0), 0)
        exact = exact + hit
        act = jnp.where(act > 0, jnp.where(hit > 0, 0, jnp.where(hi > lo + 1, 1, 0)), 0)
        return it + 1, lo, hi, clo, chi, exact, act

    def any_active(act):
        return jnp.max(act.astype(F32))

    def bis_body(st):
        st = bis_step(*bis_step(*st[1:]))
        return (any_active(st[-1]),) + st

    _, _, lo, _, clo, chi, exact, _ = lax.while_loop(
        bis_cond, bis_body, (any_active(act0), jnp.int32(0), lo, hi, clo, chi, exact0, act0))
    thr = jnp.where(take_all, _INADMISSIBLE + 1, lo)
    thr_up = jnp.where(take_all, _INADMISSIBLE + 1, lo + 1)

    need = kf - chi
    tied = jnp.where(take_all, 0, jnp.where(exact > 0, 0, jnp.where(clo > kf, 1, 0)))

    def count_tied_below(cut):
        def body(t, cnt):
            for r0 in range(0, ts, COUNT_ROWS):
                kidx = t * ts + r0 + lax.broadcasted_iota(I32, (COUNT_ROWS, tq), 0)
                cnt = cnt + jnp.where(sc_ref[t, r0:r0 + COUNT_ROWS, :] == thr,
                                      jnp.where(kidx < cut, 1.0, 0.0), 0.0)
            return cnt
        return jnp.sum(lax.fori_loop(0, n_t, body, jnp.zeros((COUNT_ROWS, tq), F32)), axis=0, keepdims=True)

    def cut_body(st):
        _, lo_c, hi_c, act = st
        mid = (lo_c + hi_c) >> 1
        cnt = count_tied_below(mid)
        on = act > 0
        le = cnt <= need
        lo_c = jnp.where(on, jnp.where(le, mid, lo_c), lo_c)
        hi_c = jnp.where(on, jnp.where(le, hi_c, mid), hi_c)
        act = jnp.where(on, jnp.where(hi_c - lo_c > 1, 1, 0), 0)
        return jnp.max(act), lo_c, hi_c, act

    _, cut_lo, _, _ = lax.while_loop(
        bis_cond, cut_body,
        (jnp.max(tied), jnp.zeros((1, tq), I32), jnp.full((1, tq), n_tiles_total * ts + 1, I32), tied))
    cut = jnp.where(tied > 0, cut_lo, n_tiles_total * ts + 1)

    acc_ref[...] = jnp.zeros(acc_ref.shape, F32)
    rc = min(tq, ROW_CHUNK)
    ones = jnp.ones((ts, LANES), BF16)
    last = n_t - 1

    def logits(t, slot):
        for g in range(N_KV_HEADS):
            qg = q_ref[0, g * GQA_GROUP:(g + 1) * GQA_GROUP].reshape(GQA_GROUP * tq, HEAD_DIM)
            s_ref[slot, g * GQA_GROUP * tq:(g + 1) * GQA_GROUP * tq] = jnp.dot(
                qg, kt_ref[0, g, t], preferred_element_type=F32)

    def selection_bias(t, c, live, masked_value):
        kidx = t * ts + c * LANES + lax.broadcasted_iota(I32, (LANES, tq), 0)
        bound = jnp.where(live, jnp.where(kidx < cut, thr, thr_up), _NO_KEY)
        return jnp.where(sc_ref[t, c * LANES:(c + 1) * LANES, :] >= bound, 0.0, masked_value).T

    def value_matmul(t, slot, g):
        v_aug = jnp.concatenate([v_ref[0, g, t], ones], axis=-1)
        return jnp.dot(p_ref[slot, g], v_aug, preferred_element_type=F32)

    def softmax_pv(t, slot, live):
        for c in range(nc):
            bias_ref[slot, :, c * LANES:(c + 1) * LANES] = selection_bias(t, c, live, _NEG_INF)
        for g in range(N_KV_HEADS):
            for j in range(GQA_GROUP):
                hd = g * GQA_GROUP + j
                for r0 in range(0, tq, rc):
                    r1 = j * tq + r0
                    r2 = hd * tq + r0
                    s = [s_ref[slot, r2:r2 + rc, c * LANES:(c + 1) * LANES]
                         + bias_ref[slot, r0:r0 + rc, c * LANES:(c + 1) * LANES] for c in range(nc)]
                    mx = s[0]
                    for c in range(1, nc):
                        mx = jnp.maximum(mx, s[c])
                    m_old = m_ref[hd, r0:r0 + rc]
                    m_new = jnp.maximum(m_old, jnp.broadcast_to(jnp.max(mx, axis=-1, keepdims=True), (rc, LANES)))
                    alpha_ref[slot, g, r1:r1 + rc] = jnp.exp2(m_old - m_new)
                    m_ref[hd, r0:r0 + rc] = m_new
                    for c in range(nc):
                        p_ref[slot, g, r1:r1 + rc, c * LANES:(c + 1) * LANES] = jnp.exp2(s[c] - m_new).astype(BF16)
            pv = value_matmul(t, slot, g)
            for half in range(2):
                cols = slice(half * LANES, (half + 1) * LANES)
                acc_ref[g, :, cols] = alpha_ref[slot, g] * acc_ref[g, :, cols] + pv[:, cols]

    def attend_running():
        def attn_pair(u, carry):
            t0 = 2 * u
            t1 = jnp.minimum(t0 + 1, last)
            logits(t1, 1)
            softmax_pv(t0, 0, True)
            logits(jnp.minimum(t0 + 2, last), 0)
            softmax_pv(t1, 1, t0 + 1 < n_t)
            return carry

        lax.fori_loop(0, (n_t + 1) // 2, attn_pair, 0)

    def softmax_fixed(t, slot, live):
        for c in range(nc):
            bias_ref[slot, :, c * LANES:(c + 1) * LANES] = selection_bias(t, c, live, _NEG_INF)
        for g in range(N_KV_HEADS):
            for j in range(GQA_GROUP):
                hd = g * GQA_GROUP + j
                for r0 in range(0, tq, rc):
                    ref_pt = m_ref[hd, r0:r0 + rc]
                    for c in range(nc):
                        cs = slice(c * LANES, (c + 1) * LANES)
                        s = s_ref[slot, hd * tq + r0:hd * tq + r0 + rc, cs] + bias_ref[slot, r0:r0 + rc, cs]
                        p_ref[slot, g, j * tq + r0:j * tq + r0 + rc, cs] = jnp.exp2(s - ref_pt).astype(BF16)

    def attend_fixed():
        def attn_pair(u, carry):
            t0 = 2 * u
            t1 = jnp.minimum(t0 + 1, last)
            logits(t1, 1)
            softmax_fixed(t0, 0, True)
            logits(jnp.minimum(t0 + 2, last), 0)
            softmax_fixed(t1, 1, t0 + 1 < n_t)
            for g in range(N_KV_HEADS):
                pv = value_matmul(t0, 0, g) + value_matmul(t1, 1, g)
                for half in range(2):
                    cols = slice(half * LANES, (half + 1) * LANES)
                    acc_ref[g, :, cols] = acc_ref[g, :, cols] + pv[:, cols]
            return carry

        lax.fori_loop(0, (n_t + 1) // 2, attn_pair, 0)

    @pl.when(qi == 0)
    def _():
        for g in range(N_KV_HEADS):
            def sq_norm_max(t, best):
                kk = kt_ref[0, g, t].astype(F32)
                return jnp.maximum(best, jnp.sum(kk * kk, axis=0, keepdims=True))
            best = lax.fori_loop(0, n_tiles_total, sq_norm_max, jnp.zeros((1, ts), F32))
            kmax_ref[g] = jnp.broadcast_to(jnp.max(best, axis=-1, keepdims=True), (8, LANES))

    logits(0, 0)
    top = jnp.zeros((tq, LANES), F32)
    for hd in range(N_HEADS):
        qf = q_ref[0, hd].astype(F32)
        q_sq = _row_sum(qf * qf)
        ref_pt = jnp.sqrt(q_sq * kmax_ref[hd // GQA_GROUP, 0:1, :]) * 1.001 + 1e-3
        m_ref[hd] = ref_pt
        top = jnp.maximum(top, ref_pt)
    fits = jnp.max(top) <= MAX_FIXED_REFERENCE

    @pl.when(fits)
    def _():
        attend_fixed()

    @pl.when(jnp.logical_not(fits))
    def _():
        m_ref[...] = jnp.full(m_ref.shape, _M_INIT, F32)
        attend_running()

    for hd in range(N_HEADS):
        g, j = divmod(hd, GQA_GROUP)
        o_ref[0, :, hd * HEAD_DIM:(hd + 1) * HEAD_DIM] = (
            acc_ref[g, j * tq:(j + 1) * tq, 0:HEAD_DIM] / acc_ref[g, j * tq:(j + 1) * tq, HEAD_DIM:2 * HEAD_DIM])


def _attend_t(qst, wrow, q_att, ki4, kt, v, q0, n_keys, tq, ts):
    B, _, T, _ = q_att.shape
    n_tiles = ki4.shape[1]
    k_sel = min(TOPK_MAX, n_keys // 4)
    kern = functools.partial(_attn_t_kernel, tq=tq, ts=ts, q0=q0, n_keys=n_keys, k_sel=k_sel)
    return pl.pallas_call(
        kern, grid=(B, T // tq),
        in_specs=[
            pl.BlockSpec((1, 1, 4 * IDX_DIM, N_IDX_HEADS * tq), lambda b, i: (b, i, 0, 0)),
            pl.BlockSpec((1, 1, N_IDX_HEADS, tq), lambda b, i: (b, i, 0, 0)),
            pl.BlockSpec((1, N_HEADS, tq, HEAD_DIM), lambda b, i: (b, 0, i, 0)),
            _resident((1, n_tiles, ts, 4 * IDX_DIM), lambda b, i: (b, 0, 0, 0)),
            _resident((1, N_KV_HEADS, n_tiles, HEAD_DIM, ts), lambda b, i: (b, 0, 0, 0, 0)),
            _resident((1, N_KV_HEADS, n_tiles, ts, HEAD_DIM), lambda b, i: (b, 0, 0, 0, 0)),
        ],
        out_specs=pl.BlockSpec((1, tq, N_HEADS * HEAD_DIM), lambda b, i: (b, i, 0)),
        out_shape=jax.ShapeDtypeStruct((B, T, N_HEADS * HEAD_DIM), F32),
        scratch_shapes=[
            pltpu.VMEM((n_tiles, ts, tq), I32),
            pltpu.VMEM((2, KEY_ROWS, tq), F32),
            pltpu.VMEM((2, ts, N_IDX_HEADS * tq), F32),
            pltpu.VMEM((2, N_HEADS * tq, ts), F32),
            pltpu.VMEM((N_HEADS, tq, LANES), F32),
            pltpu.VMEM((N_KV_HEADS, GQA_GROUP * tq, 2 * HEAD_DIM), F32),
            pltpu.VMEM((2, tq, ts), F32),
            pltpu.VMEM((2, N_KV_HEADS, GQA_GROUP * tq, ts), BF16),
            pltpu.VMEM((2, N_KV_HEADS, GQA_GROUP * tq, LANES), F32),
            pltpu.VMEM((N_KV_HEADS, 8, LANES), F32),
        ],
        name="attend_t",
        compiler_params=pltpu.CompilerParams(dimension_semantics=("arbitrary", "arbitrary"),
                                             vmem_limit_bytes=VMEM_LIMIT),
    )(qst, wrow, q_att, ki4, kt, v)


def _attend(qs, kiw, q_att, ki4, kt, v, q0, n_keys, tq, ts):
    B, _, T, _ = qs.shape
    lp = ki4.shape[1]
    n_tiles = lp // ts
    k_sel = min(TOPK_MAX, n_keys // 4)
    kern = functools.partial(_attn_kernel, tq=tq, ts=ts, q0=q0, n_keys=n_keys, k_sel=k_sel)
    keys = pl.BlockSpec((1, lp, 2 * LANES), lambda b, i: (b, 0, 0))
    return pl.pallas_call(
        kern, grid=(B, T // tq),
        in_specs=[
            pl.BlockSpec((1, N_IDX_HEADS, tq, 4 * IDX_DIM), lambda b, i: (b, 0, i, 0)),
            pl.BlockSpec((1, tq, LANES), lambda b, i: (b, i, 0)),
            pl.BlockSpec((1, N_HEADS, tq, HEAD_DIM), lambda b, i: (b, 0, i, 0)),
            keys, keys, keys,
        ],
        out_specs=pl.BlockSpec((1, tq, N_HEADS * HEAD_DIM), lambda b, i: (b, i, 0)),
        out_shape=jax.ShapeDtypeStruct((B, T, N_HEADS * HEAD_DIM), F32),
        scratch_shapes=[
            pltpu.VMEM((n_tiles, tq, ts), I32),
            pltpu.VMEM((2, tq, LANES), F32),
            pltpu.VMEM((N_IDX_HEADS, tq, LANES), F32),
            pltpu.VMEM((2, N_HEADS * tq, ts), F32),
            pltpu.VMEM((N_HEADS, tq, LANES), F32),
            pltpu.VMEM((N_KV_HEADS, GQA_GROUP * tq, 2 * HEAD_DIM), F32),
            pltpu.VMEM((2, tq, ts), F32),
            pltpu.VMEM((2, N_KV_HEADS, GQA_GROUP * tq, ts), BF16),
            pltpu.VMEM((2, N_KV_HEADS, GQA_GROUP * tq, LANES), F32),
        ],
        name="attend",
        compiler_params=pltpu.CompilerParams(dimension_semantics=("arbitrary", "arbitrary"),
                                             vmem_limit_bytes=VMEM_LIMIT),
    )(qs, kiw, q_att, ki4, kt, v)


def _rms(x, g):
    return x * lax.rsqrt(jnp.mean(x * x, axis=-1, keepdims=True) + RMS_EPS) * g


def _out_kernel(x_ref, pa_ref, sgb_ref, b_ref, wo_ref, g1_ref, g2_ref, wgu_ref, wd_ref, g3_ref, y_ref, *, d_ff):
    merged = pa_ref[...] + sgb_ref[...] * b_ref[...]
    mix = jnp.dot(merged.astype(BF16), wo_ref[...], preferred_element_type=F32)
    x1 = x_ref[...] + _rms(mix, g1_ref[...])
    h2 = _rms(x1, g2_ref[...]).astype(BF16)
    gu = jnp.dot(h2, wgu_ref[...], preferred_element_type=F32)
    gate, up = gu[:, :d_ff], gu[:, d_ff:]
    act = (gate * _sigmoid(gate) * up).astype(BF16)
    f = jnp.dot(act, wd_ref[...], preferred_element_type=F32)
    y_ref[...] = x1 + _rms(f, g3_ref[...])


def _finish(x, pa, sgb, b, w_out, g_post, g_ffn_pre, w_gate_up, w_down, g_ffn_post, tm):
    N, D = x.shape
    d_ff = w_down.shape[0]
    row = pl.BlockSpec((tm, D), lambda i: (i, 0))
    const = lambda i: (0, 0)
    return pl.pallas_call(
        functools.partial(_out_kernel, d_ff=d_ff), grid=(N // tm,),
        in_specs=[row, row, row, row,
                  _resident((D, D), const), _resident((1, D), const), _resident((1, D), const),
                  _resident((D, 2 * d_ff), const), _resident((d_ff, D), const), _resident((1, D), const)],
        out_specs=row, out_shape=jax.ShapeDtypeStruct((N, D), F32), name="finish",
        compiler_params=pltpu.CompilerParams(dimension_semantics=("arbitrary",), vmem_limit_bytes=VMEM_LIMIT),
    )(x, pa, sgb, b, w_out, g_post, g_ffn_pre, w_gate_up, w_down, g_ffn_post)


def _tile_keys(n_keys, ts):
    return -(-n_keys // ts) * ts


def _layer(x, pos0, hist, k_cache, v_cache, ki_cache, wts, tm, tq, ts, tm_out):
    (w_in_r, w_pool, pool_scale, w_out, w_gate_up, w_down, g_pre, g_post, g_ffn_pre, g_ffn_post) = wts
    B, T, D = x.shape
    hist16 = jnp.concatenate([jnp.zeros((B, HALO // 2 - POOL_HIST, POOL_WIDTH), F32), hist], axis=1)
    own_keys_only = k_cache is None and T % ts == 0 and ts % tm == 0 and tq == LANES and tm % LANES == 0
    outs = _project(x, hist16, pos0, w_in_r, g_pre, w_pool, pool_scale, tm, ts if own_keys_only else None)
    u, q_att, k, v, qs, kiw, pa, sgb = outs[:8]
    ki = kiw[:, :, :IDX_DIM]
    if own_keys_only:
        kt, vt, ki4, wrow = outs[8:]
        b = _attend_t(qs, wrow, q_att, ki4, kt, vt, pos0, T, tq, ts)
    else:
        if k_cache is None:
            k_all, v_all, ki_all = k, v, ki
        else:
            k_all = jnp.concatenate([k_cache.reshape(B, -1, N_KV_HEADS * HEAD_DIM), k], axis=1)
            v_all = jnp.concatenate([v_cache.reshape(B, -1, N_KV_HEADS * HEAD_DIM), v], axis=1)
            ki_all = jnp.concatenate([ki_cache, ki], axis=1)
        n_keys = k_all.shape[1]
        lp = _tile_keys(n_keys, ts)
        pad = ((0, 0), (0, lp - n_keys), (0, 0))
        kt = jnp.pad(k_all.astype(BF16), pad)
        vt = jnp.pad(v_all.astype(BF16), pad)
        ki4 = _split_keys(jnp.pad(jnp.concatenate([ki_all, ki_all], axis=-1), pad), ts)
        b = _attend(qs, kiw, q_att, ki4, kt, vt, pos0, n_keys, tq, ts)
    y = _finish(x.reshape(B * T, D), pa.reshape(B * T, D), sgb.reshape(B * T, D), b.reshape(B * T, D),
                w_out, g_post, g_ffn_pre, w_gate_up, w_down, g_ffn_post, tm_out).reshape(B, T, D)
    new_pool = jnp.concatenate([hist, u], axis=1)[:, T:]
    return (y, k.reshape(B, T, N_KV_HEADS, HEAD_DIM), v.reshape(B, T, N_KV_HEADS, HEAD_DIM), ki, new_pool)


def _relayout_w_in(w_in):
    d = w_in.shape[0]
    o_kiw = POOL_WIDTH + N_HEADS * HEAD_DIM + 2 * N_KV_HEADS * HEAD_DIM + N_IDX_HEADS * IDX_DIM
    narrow = IDX_DIM + N_IDX_HEADS
    padded = jnp.concatenate([w_in[:, :o_kiw + narrow], jnp.zeros((d, LANES - narrow), w_in.dtype),
                              w_in[:, o_kiw + narrow:]], axis=1)
    return padded.astype(BF16)


def kernel(x_prompt, x_sample, cache_k, cache_v, cache_k_idx, state_pool, w_in, w_pool, pool_scale, w_out,
           w_gate_up, w_down, norm_mix_pre, norm_mix_post, norm_ffn_pre, norm_ffn_post):
    depth = w_in.shape[0]
    past = cache_k.shape[2]
    t_p, t_s = x_prompt.shape[1], x_sample.shape[1]
    hist_p = jnp.zeros((x_prompt.shape[0], POOL_HIST, POOL_WIDTH), x_prompt.dtype)
    xp, xs = x_prompt, x_sample
    outs = [[] for _ in range(8)]
    for l in range(depth):
        wts = (_relayout_w_in(w_in[l]), w_pool[l].astype(BF16), pool_scale[l][None, :], w_out[l].astype(BF16),
               w_gate_up[l].astype(BF16), w_down[l].astype(BF16), norm_mix_pre[l][None, :],
               norm_mix_post[l][None, :], norm_ffn_pre[l][None, :], norm_ffn_post[l][None, :])
        tm_p = min(256, t_p)
        tq_p = min(128, t_p)
        xp, k1, v1, ki1, p1 = _layer(xp, 0, hist_p, None, None, None, wts, tm_p, tq_p, 512, tm_p)
        n_s = xs.shape[0] * t_s
        xs, k2, v2, ki2, p2 = _layer(xs, past, state_pool[l], cache_k[l], cache_v[l], cache_k_idx[l], wts,
                                     t_s, t_s, 512, min(256, n_s))
        for lst, val in zip(outs, (k1, v1, ki1, p1, k2, v2, ki2, p2)):
            lst.append(val)
    return (xp, xs) + tuple(jnp.stack(o) for o in outs)
```

```python
import functools

import jax
import jax.numpy as jnp
from jax import lax
from jax.experimental import pallas as pl
from jax.experimental.pallas import tpu as pltpu

F32 = jnp.float32
BF16 = jnp.bfloat16
I32 = jnp.int32

LANES = 128
CHUNK = 64
POOL_WINDOWS = (2, 4, 8, 16)
N_POOL_GROUPS = 4
POOL_GROUP_WIDTH = 128
POOL_WIDTH = N_POOL_GROUPS * POOL_GROUP_WIDTH
POOL_HIST = 15
N_HEADS = 8
N_KV_HEADS = 2
HEAD_DIM = 128
GQA_GROUP = N_HEADS // N_KV_HEADS
ROPE_THETA = 500000.0
N_IDX_HEADS = 8
IDX_DIM = 64
TOPK_MAX = 256
RMS_EPS = 1e-6
ATTN_SCALE = HEAD_DIM ** -0.5
IDX_SCALE = (N_IDX_HEADS ** -0.5) * (IDX_DIM ** -0.5)

LOG2_E = 1.4426950408889634
VALUE_PIVOT_STEPS = 28
MAX_FIXED_REFERENCE = 40.0
SPLIT_ROWS = 1536
KEY_ROWS = 16
COUNT_ROWS = 32
ROW_CHUNK = 32
HALO = 32
VMEM_LIMIT = 56 * 1024 * 1024

_NEG_INF = float("-inf")
_POS_INF = float("inf")
_INADMISSIBLE = -(2 ** 31)
_NO_KEY = 2 ** 31 - 1
_KEY_MIN_NORMAL = 0x00800000
_KEY_NEG_ZERO = -1
_M_INIT = -1e30


def _resident(block_shape, index_map):
    return pl.BlockSpec(block_shape, index_map, pipeline_mode=pl.Buffered(1))


def _rope(xs, cos, sin, half, period):
    lane = lax.broadcasted_iota(I32, xs.shape, 1)
    ahead = pltpu.roll(xs, LANES - half, 1)
    behind = pltpu.roll(xs, half, 1)
    partner = jnp.where((lane & (period - 1)) < half, ahead, behind)
    return xs * cos + partner * sin


def _sigmoid(x):
    return 1.0 / (1.0 + jnp.exp(-x))


def _proj_kernel(x_ref, g_ref, w_ref, hist_ref, cq_ref, sq_ref, ci_ref, si_ref, wpool_ref, pscale_ref,
                 u_ref, qatt_ref, k_ref, v_ref, qs_ref, kiw_ref, pa_ref, sgb_ref, *rest, tm, pos0, d_model, key_tiles):
    if key_tiles:
        kt_ref, vt_ref, ki4_ref, wrow_ref, e_ref, s2_ref, s4_ref, s8_ref = rest
    else:
        e_ref, s2_ref, s4_ref, s8_ref = rest
    i = pl.program_id(1)
    x = x_ref[0]
    h = x * lax.rsqrt(jnp.mean(x * x, axis=-1, keepdims=True) + RMS_EPS) * g_ref[...]
    o_q = POOL_WIDTH
    o_k = o_q + N_HEADS * HEAD_DIM
    o_v = o_k + N_KV_HEADS * HEAD_DIM
    o_qi = o_v + N_KV_HEADS * HEAD_DIM
    o_kiw = o_qi + N_IDX_HEADS * IDX_DIM
    o_ga = o_kiw + LANES
    o_gb = o_ga + d_model

    proj = jnp.dot(h.astype(BF16), w_ref[...], preferred_element_type=F32)

    cq, sq, ci, si = cq_ref[...], sq_ref[...], ci_ref[...], si_ref[...]
    half_q = HEAD_DIM // 8
    half_i = IDX_DIM // 8

    for hd in range(N_HEADS):
        qh = _rope(proj[:, o_q + hd * HEAD_DIM:o_q + (hd + 1) * HEAD_DIM], cq, sq, half_q, HEAD_DIM)
        qatt_ref[0, hd] = (qh * (ATTN_SCALE * LOG2_E)).astype(BF16)
    for kh in range(N_KV_HEADS):
        k_h = _rope(proj[:, o_k + kh * HEAD_DIM:o_k + (kh + 1) * HEAD_DIM], cq, sq, half_q, HEAD_DIM)
        k_ref[0, :, kh * HEAD_DIM:(kh + 1) * HEAD_DIM] = k_h
        if key_tiles:
            kt_ref[0, kh, 0] = k_h.T.astype(BF16)
            vt_ref[0, kh, 0] = proj[:, o_v + kh * HEAD_DIM:o_v + (kh + 1) * HEAD_DIM].astype(BF16)
    v_ref[0] = proj[:, o_v:o_qi]

    for pr in range(N_IDX_HEADS // 2):
        qi2 = _rope(proj[:, o_qi + pr * LANES:o_qi + (pr + 1) * LANES], ci, si, half_i, IDX_DIM)
        hi = qi2.astype(BF16).astype(F32)
        lo = qi2 - hi
        lane2 = lax.broadcasted_iota(I32, qi2.shape, 1)
        first = jnp.where(lane2 < IDX_DIM, hi, pltpu.roll(lo, IDX_DIM, 1))
        second = jnp.where(lane2 < IDX_DIM, pltpu.roll(hi, IDX_DIM, 1), lo)
        for sub, slab in enumerate((first, second)):
            hd = 2 * pr + sub
            if key_tiles:
                for sb in range(tm // LANES):
                    slab_t = slab[sb * LANES:(sb + 1) * LANES].T.astype(BF16)
                    qs_ref[0, sb, 0:LANES, hd * LANES:(hd + 1) * LANES] = slab_t
                    qs_ref[0, sb, LANES:2 * LANES, hd * LANES:(hd + 1) * LANES] = slab_t
            else:
                qs_ref[0, hd, :, 0:LANES] = slab.astype(BF16)
                qs_ref[0, hd, :, LANES:2 * LANES] = slab.astype(BF16)

    kiw = _rope(proj[:, o_kiw:o_kiw + LANES], ci, si, half_i, IDX_DIM)
    lane = lax.broadcasted_iota(I32, kiw.shape, 1)
    kiw_full = jnp.where(lane < IDX_DIM, kiw, proj[:, o_kiw:o_kiw + LANES] * IDX_SCALE)
    kiw_ref[0] = kiw_full
    if key_tiles:
        ki_hi = kiw.astype(BF16).astype(F32)
        ki_lo = kiw - ki_hi
        ki4_ref[0, 0, :, 0:LANES] = jnp.where(lane < IDX_DIM, ki_hi, pltpu.roll(ki_hi, IDX_DIM, 1)).astype(BF16)
        ki4_ref[0, 0, :, LANES:2 * LANES] = jnp.where(lane < IDX_DIM, ki_lo, pltpu.roll(ki_lo, IDX_DIM, 1)).astype(BF16)
        for sb in range(tm // LANES):
            wrow_ref[0, sb] = kiw_full[sb * LANES:(sb + 1) * LANES].T[IDX_DIM:IDX_DIM + N_IDX_HEADS]

    u = proj[:, 0:POOL_WIDTH]
    u_ref[0] = u

    @pl.when(i == 0)
    def _():
        e_ref[0:HALO // 2, :] = jnp.zeros((HALO // 2, POOL_WIDTH), F32)
        e_ref[HALO // 2:HALO, :] = hist_ref[0]

    e_ref[HALO:HALO + tm, :] = u
    n2, n4, n8 = tm + 24, tm + 16, tm + 8
    s2_ref[8:8 + n2, :] = e_ref[8:8 + n2, :] + e_ref[7:7 + n2, :]
    s4_ref[16:16 + n4, :] = s2_ref[16:16 + n4, :] + s2_ref[14:14 + n4, :]
    s8_ref[24:24 + n8, :] = s4_ref[24:24 + n8, :] + s4_ref[20:20 + n8, :]
    s16 = s8_ref[HALO:HALO + tm, :] + s8_ref[HALO - 8:HALO - 8 + tm, :]
    wins = (s2_ref[HALO:HALO + tm, :], s4_ref[HALO:HALO + tm, :], s8_ref[HALO:HALO + tm, :], s16)
    e_ref[HALO // 2:HALO, :] = e_ref[HALO // 2 + tm:HALO + tm, :]

    pos = pos0 + i * tm + lax.broadcasted_iota(I32, (tm, POOL_GROUP_WIDTH), 0)
    a_parts = []
    for g, w in enumerate(POOL_WINDOWS):
        lo_l, hi_l = g * POOL_GROUP_WIDTH, (g + 1) * POOL_GROUP_WIDTH
        cnt = jnp.minimum(pos + 1, w).astype(F32)
        pooled = wins[g][:, lo_l:hi_l] / cnt - u[:, lo_l:hi_l]
        a_parts.append(jnp.dot(pooled.astype(BF16), wpool_ref[g], preferred_element_type=F32))
    a = jnp.concatenate(a_parts, axis=-1) * pscale_ref[...]
    pa_ref[0] = _sigmoid(proj[:, o_ga:o_gb]) * a
    sgb_ref[0] = _sigmoid(proj[:, o_gb:o_gb + d_model])


def _project(x, hist16, pos0, w_in_r, g_pre, w_pool, pool_scale, tm, key_tile=None):
    B, T, D = x.shape
    W = w_in_r.shape[1]
    pos = pos0 + jnp.arange(T, dtype=I32)
    cq, sq = _rope_tables(pos, HEAD_DIM)
    ci, si = _rope_tables(pos, IDX_DIM)
    row = lambda b, i: (b, i, 0)
    tab = lambda b, i: (i, 0)
    const2 = lambda b, i: (0, 0)
    out_shape = (
        jax.ShapeDtypeStruct((B, T, POOL_WIDTH), F32),
        jax.ShapeDtypeStruct((B, N_HEADS, T, HEAD_DIM), BF16),
        jax.ShapeDtypeStruct((B, T, N_KV_HEADS * HEAD_DIM), F32),
        jax.ShapeDtypeStruct((B, T, N_KV_HEADS * HEAD_DIM), F32),
        jax.ShapeDtypeStruct((B, N_IDX_HEADS, T, 4 * IDX_DIM), BF16),
        jax.ShapeDtypeStruct((B, T, LANES), F32),
        jax.ShapeDtypeStruct((B, T, D), F32),
        jax.ShapeDtypeStruct((B, T, D), F32),
    )
    out_specs = (
        pl.BlockSpec((1, tm, POOL_WIDTH), row),
        pl.BlockSpec((1, N_HEADS, tm, HEAD_DIM), lambda b, i: (b, 0, i, 0)),
        pl.BlockSpec((1, tm, N_KV_HEADS * HEAD_DIM), row),
        pl.BlockSpec((1, tm, N_KV_HEADS * HEAD_DIM), row),
        pl.BlockSpec((1, N_IDX_HEADS, tm, 4 * IDX_DIM), lambda b, i: (b, 0, i, 0)),
        pl.BlockSpec((1, tm, LANES), row),
        pl.BlockSpec((1, tm, D), row),
        pl.BlockSpec((1, tm, D), row),
    )
    in_specs = [
        pl.BlockSpec((1, tm, D), row),
        _resident((1, D), const2),
        _resident((D, W), const2),
        pl.BlockSpec((1, HALO // 2, POOL_WIDTH), lambda b, i: (b, 0, 0)),
        pl.BlockSpec((tm, LANES), tab), pl.BlockSpec((tm, LANES), tab),
        pl.BlockSpec((tm, LANES), tab), pl.BlockSpec((tm, LANES), tab),
        _resident((N_POOL_GROUPS, POOL_GROUP_WIDTH, D // N_POOL_GROUPS), lambda b, i: (0, 0, 0)),
        _resident((1, D), const2),
    ]
    if key_tile is not None:
        assert key_tile % tm == 0 and T % key_tile == 0 and tm % LANES == 0
        per = key_tile // tm
        n_tiles = T // key_tile
        sub = tm // LANES
        out_shape = out_shape[:4] + (
            jax.ShapeDtypeStruct((B, T // LANES, 4 * IDX_DIM, N_IDX_HEADS * LANES), BF16),
        ) + out_shape[5:] + (
            jax.ShapeDtypeStruct((B, N_KV_HEADS, n_tiles, HEAD_DIM, key_tile), BF16),
            jax.ShapeDtypeStruct((B, N_KV_HEADS, n_tiles, key_tile, HEAD_DIM), BF16),
            jax.ShapeDtypeStruct((B, n_tiles, key_tile, 4 * IDX_DIM), BF16),
            jax.ShapeDtypeStruct((B, T // LANES, N_IDX_HEADS, LANES), F32),
        )
        out_specs = out_specs[:4] + (
            pl.BlockSpec((1, sub, 4 * IDX_DIM, N_IDX_HEADS * LANES), lambda b, i: (b, i, 0, 0)),
        ) + out_specs[5:] + (
            pl.BlockSpec((1, N_KV_HEADS, 1, HEAD_DIM, tm), lambda b, i: (b, 0, i // per, 0, i % per)),
            pl.BlockSpec((1, N_KV_HEADS, 1, tm, HEAD_DIM), lambda b, i: (b, 0, i // per, i % per, 0)),
            pl.BlockSpec((1, 1, tm, 4 * IDX_DIM), lambda b, i: (b, i // per, i % per, 0)),
            pl.BlockSpec((1, sub, N_IDX_HEADS, LANES), lambda b, i: (b, i, 0, 0)),
        )
    scratch = [pltpu.VMEM((HALO + tm, POOL_WIDTH), F32) for _ in range(4)]
    return pl.pallas_call(
        functools.partial(_proj_kernel, tm=tm, pos0=pos0, d_model=D, key_tiles=key_tile is not None),
        grid=(B, T // tm), in_specs=in_specs, out_specs=out_specs, out_shape=out_shape,
        scratch_shapes=scratch, name="proj",
        compiler_params=pltpu.CompilerParams(dimension_semantics=("arbitrary", "arbitrary"),
                                             vmem_limit_bytes=VMEM_LIMIT),
    )(x, g_pre, w_in_r, hist16, cq, sq, ci, si, w_pool, pool_scale)


def _rope_tables(pos, dim):
    rot = dim // 4
    half = rot // 2
    inv = ROPE_THETA ** (-jnp.arange(half, dtype=F32) / half)
    ang = pos.astype(F32)[:, None] * inv[None, :]
    cos, sin = jnp.cos(ang), jnp.sin(ang)
    rest = dim - rot
    n = pos.shape[0]
    c = jnp.concatenate([cos, cos, jnp.ones((n, rest), F32)], axis=-1)
    s = jnp.concatenate([-sin, sin, jnp.zeros((n, rest), F32)], axis=-1)
    return jnp.tile(c, (1, LANES // dim)), jnp.tile(s, (1, LANES // dim))


def _split_kernel(k_ref, out_ref):
    k = k_ref[0]
    hi = k.astype(BF16).astype(F32)
    lo = k - hi
    out_ref[0, :, 0:LANES] = jnp.concatenate([hi, hi], axis=-1).astype(BF16)
    out_ref[0, :, LANES:2 * LANES] = jnp.concatenate([lo, lo], axis=-1).astype(BF16)


def _split_keys(ki, ts):
    B, Lp, _ = ki.shape
    rows = ts * max(1, SPLIT_ROWS // ts)
    rows = rows if Lp % rows == 0 else ts
    return pl.pallas_call(
        _split_kernel, grid=(B, Lp // rows),
        in_specs=[pl.BlockSpec((1, rows, IDX_DIM), lambda b, t: (b, t, 0))],
        out_specs=pl.BlockSpec((1, rows, 4 * IDX_DIM), lambda b, t: (b, t, 0)),
        out_shape=jax.ShapeDtypeStruct((B, Lp, 4 * IDX_DIM), BF16), name="split_keys",
        compiler_params=pltpu.CompilerParams(dimension_semantics=("arbitrary", "arbitrary")),
    )(ki)


def _key_of(x):
    b = pltpu.bitcast(x, I32)
    return b ^ ((b >> 31) & 0x7FFFFFFF)


def _float_of(k):
    return pltpu.bitcast(k ^ ((k >> 31) & 0x7FFFFFFF), F32)


def _row_sum(x):
    return jnp.broadcast_to(jnp.sum(x, axis=-1, keepdims=True), x.shape)


def _attn_kernel(qs_ref, kiw_ref, q_ref, ki_ref, kt_ref, v_ref, o_ref,
                 sc_ref, mm_ref, wb_ref, s_ref, m_ref, acc_ref, bias_ref, p_ref, alpha_ref,
                 *, tq, ts, q0, n_keys, k_sel):
    qi = pl.program_id(1)
    n_tiles_total = ki_ref.shape[1] // ts
    nc = ts // LANES
    nt_dims = (((1,), (1,)), ((), ()))

    def key_rows(t):
        return pl.ds(pl.multiple_of(t * ts, ts), ts)

    row = lax.broadcasted_iota(I32, (tq, LANES), 0)
    lane = lax.broadcasted_iota(I32, (tq, LANES), 1)
    qpos = q0 + qi * tq + row
    n_adm = jnp.minimum((qpos // CHUNK + 1) * CHUNK, n_keys)
    last_adm = jnp.minimum(((q0 + (qi + 1) * tq - 1) // CHUNK + 1) * CHUNK, n_keys)
    n_t = jnp.minimum((last_adm + ts - 1) // ts, n_tiles_total)

    kiw = kiw_ref[0]
    for hd in range(N_IDX_HEADS):
        wb_ref[hd] = jnp.broadcast_to(kiw[:, IDX_DIM + hd:IDX_DIM + hd + 1], (tq, LANES))
    qs2 = qs_ref[0].reshape(N_IDX_HEADS * tq, 4 * IDX_DIM)

    def idx_logits(t, slot):
        s_ref[slot] = lax.dot_general(qs2, ki_ref[0, key_rows(t), :], nt_dims,
                                      preferred_element_type=F32)

    rc = min(tq, ROW_CHUNK)
    mm_ref[0] = jnp.full((tq, LANES), _POS_INF, F32)
    mm_ref[1] = jnp.full((tq, LANES), _NEG_INF, F32)

    def score_tile(t, slot, masked):
        for r0 in range(0, tq, rc):
            rows = slice(r0, r0 + rc)
            mn, mx = mm_ref[0, rows], mm_ref[1, rows]
            for c in range(nc):
                cs = slice(c * LANES, (c + 1) * LANES)
                acc = None
                for hd in range(N_IDX_HEADS):
                    r = jnp.maximum(s_ref[slot, hd * tq + r0:hd * tq + r0 + rc, cs], 0.0) * wb_ref[hd, rows]
                    acc = r if acc is None else acc + r
                key = _key_of(acc)
                if masked:
                    qpos_c = q0 + qi * tq + r0 + lax.broadcasted_iota(I32, (rc, LANES), 0)
                    n_adm_c = jnp.minimum((qpos_c // CHUNK + 1) * CHUNK, n_keys)
                    adm = (t * ts + c * LANES + lax.broadcasted_iota(I32, (rc, LANES), 1)) < n_adm_c
                    key = jnp.where(adm, key, _INADMISSIBLE)
                    mx = jnp.maximum(mx, jnp.where(adm, acc, _NEG_INF))
                    mn = jnp.minimum(mn, jnp.where(adm, acc, _POS_INF))
                else:
                    mx = jnp.maximum(mx, acc)
                    mn = jnp.minimum(mn, acc)
                sc_ref[t, rows, cs] = key
            mm_ref[0, rows] = mn
            mm_ref[1, rows] = mx

    n_full = jnp.minimum(jnp.minimum(((q0 + qi * tq) // CHUNK + 1) * CHUNK, n_keys) // ts, n_t)
    last_full = jnp.maximum(n_full - 1, 0)
    idx_logits(0, 0)

    def score_pair(u, carry):
        t0 = 2 * u
        t1 = jnp.minimum(t0 + 1, last_full)
        idx_logits(t1, 1)
        score_tile(t0, 0, False)
        idx_logits(jnp.minimum(t0 + 2, last_full), 0)
        score_tile(t1, 1, False)
        return carry

    lax.fori_loop(0, (n_full + 1) // 2, score_pair, 0)

    def score_tail(t, carry):
        idx_logits(t, 0)
        score_tile(t, 0, True)
        return carry

    lax.fori_loop(n_full, n_t, score_tail, 0)
    rmin = jnp.broadcast_to(jnp.min(mm_ref[0], axis=-1, keepdims=True), (tq, LANES))
    rmax = jnp.broadcast_to(jnp.max(mm_ref[1], axis=-1, keepdims=True), (tq, LANES))

    def count_ge(thr):
        def body(t, cnt):
            for c in range(nc):
                cnt = cnt + jnp.where(sc_ref[t, :, c * LANES:(c + 1) * LANES] >= thr, 1.0, 0.0)
            return cnt
        return _row_sum(lax.fori_loop(0, n_t, body, jnp.zeros((tq, LANES), F32)))

    def bis_cond(st):
        return st[0] > 0

    kf = float(k_sel)
    take_all = n_adm <= k_sel
    lo0 = _key_of(rmin) - 1
    hi0 = _key_of(rmax) + 2
    c_pos = count_ge(jnp.full((tq, LANES), _KEY_MIN_NORMAL, I32))
    c_nn = count_ge(jnp.full((tq, LANES), _KEY_NEG_ZERO, I32))
    pos = c_pos >= kf
    neg = c_nn < kf
    lo = jnp.where(pos, _KEY_MIN_NORMAL, jnp.where(neg, lo0, _KEY_NEG_ZERO))
    hi = jnp.where(pos, hi0, jnp.where(neg, _KEY_NEG_ZERO, _KEY_MIN_NORMAL))
    clo = jnp.where(pos, c_pos, jnp.where(neg, n_adm.astype(F32), c_nn))
    chi = jnp.where(pos, 0.0, jnp.where(neg, c_nn, c_pos))
    exact0 = jnp.where(clo == kf, 1, 0)
    act0 = jnp.where(take_all, 0, jnp.where(exact0 > 0, 0, jnp.where(hi > lo + 1, 1, 0)))

    def bis_step(it, lo, hi, clo, chi, exact, act):
        kmid = (lo & hi) + ((lo ^ hi) >> 1)
        vmid = _key_of((_float_of(lo) + _float_of(hi)) * 0.5)
        by_value = jnp.where(it < VALUE_PIVOT_STEPS, jnp.where(vmid > lo, jnp.where(vmid < hi, 1, 0), 0), 0)
        mid = jnp.where(by_value > 0, vmid, kmid)
        cnt = count_ge(mid)
        up = jnp.where(act > 0, jnp.where(cnt >= kf, 1, 0), 0)
        dn = act - up
        lo = jnp.where(up > 0, mid, lo)
        clo = jnp.where(up > 0, cnt, clo)
        hi = jnp.where(dn > 0, mid, hi)
        chi = jnp.where(dn > 0, cnt, chi)
        hit = jnp.where(up > 0, jnp.where(cnt == kf, 1, 0), 0)
        exact = exact + hit
        act = jnp.where(act > 0, jnp.where(hit > 0, 0, jnp.where(hi > lo + 1, 1, 0)), 0)
        return it + 1, lo, hi, clo, chi, exact, act

    def any_active(act):
        return jnp.max(act.astype(F32))

    def bis_body(st):
        st = bis_step(*bis_step(*st[1:]))
        return (any_active(st[-1]),) + st

    _, _, lo, _, clo, chi, exact, _ = lax.while_loop(
        bis_cond, bis_body, (any_active(act0), jnp.int32(0), lo, hi, clo, chi, exact0, act0))
    thr = jnp.where(take_all, _INADMISSIBLE + 1, lo)
    thr_up = jnp.where(take_all, _INADMISSIBLE + 1, lo + 1)

    need = kf - chi
    tied = jnp.where(take_all, 0, jnp.where(exact > 0, 0, jnp.where(clo > kf, 1, 0)))

    def count_tied_below(cut):
        def body(t, cnt):
            for c in range(nc):
                s = sc_ref[t, :, c * LANES:(c + 1) * LANES]
                col = t * ts + c * LANES + lane
                cnt = cnt + jnp.where(s == thr, jnp.where(col < cut, 1.0, 0.0), 0.0)
            return cnt
        return _row_sum(lax.fori_loop(0, n_t, body, jnp.zeros((tq, LANES), F32)))

    def cut_body(st):
        _, lo_c, hi_c, act = st
        mid = (lo_c + hi_c) >> 1
        cnt = count_tied_below(mid)
        on = act > 0
        le = cnt <= need
        lo_c = jnp.where(on, jnp.where(le, mid, lo_c), lo_c)
        hi_c = jnp.where(on, jnp.where(le, hi_c, mid), hi_c)
        act = jnp.where(on, jnp.where(hi_c - lo_c > 1, 1, 0), 0)
        return jnp.max(act), lo_c, hi_c, act

    _, cut_lo, _, _ = lax.while_loop(
        bis_cond, cut_body,
        (jnp.max(tied), jnp.zeros((tq, LANES), I32), jnp.full((tq, LANES), n_tiles_total * ts + 1, I32), tied))
    cut = jnp.where(tied > 0, cut_lo, n_tiles_total * ts + 1)

    m_ref[...] = jnp.full(m_ref.shape, _M_INIT, F32)
    acc_ref[...] = jnp.zeros(acc_ref.shape, F32)
    rc = min(tq, ROW_CHUNK)
    ones = jnp.ones((ts, LANES), BF16)
    last = n_t - 1

    def logits(t, slot):
        for g in range(N_KV_HEADS):
            qg = q_ref[0, g * GQA_GROUP:(g + 1) * GQA_GROUP].reshape(GQA_GROUP * tq, HEAD_DIM)
            s_ref[slot, g * GQA_GROUP * tq:(g + 1) * GQA_GROUP * tq] = lax.dot_general(
                qg, kt_ref[0, key_rows(t), g * HEAD_DIM:(g + 1) * HEAD_DIM], nt_dims,
                preferred_element_type=F32)

    def softmax_pv(t, slot, live):
        for c in range(nc):
            col = t * ts + c * LANES + lane
            bound = jnp.where(live, jnp.where(col < cut, thr, thr_up), _NO_KEY)
            bias_ref[slot, :, c * LANES:(c + 1) * LANES] = jnp.where(
                sc_ref[t, :, c * LANES:(c + 1) * LANES] >= bound, 0.0, _NEG_INF)
        for g in range(N_KV_HEADS):
            for j in range(GQA_GROUP):
                hd = g * GQA_GROUP + j
                for r0 in range(0, tq, rc):
                    r1 = j * tq + r0
                    r2 = hd * tq + r0
                    s = [s_ref[slot, r2:r2 + rc, c * LANES:(c + 1) * LANES]
                         + bias_ref[slot, r0:r0 + rc, c * LANES:(c + 1) * LANES] for c in range(nc)]
                    mx = s[0]
                    for c in range(1, nc):
                        mx = jnp.maximum(mx, s[c])
                    m_old = m_ref[hd, r0:r0 + rc]
                    m_new = jnp.maximum(m_old, jnp.broadcast_to(jnp.max(mx, axis=-1, keepdims=True), (rc, LANES)))
                    alpha_ref[slot, g, r1:r1 + rc] = jnp.exp2(m_old - m_new)
                    for c in range(nc):
                        p_ref[slot, g, r1:r1 + rc, c * LANES:(c + 1) * LANES] = jnp.exp2(s[c] - m_new).astype(BF16)
                    m_ref[hd, r0:r0 + rc] = m_new
            v_aug = jnp.concatenate([v_ref[0, key_rows(t), g * HEAD_DIM:(g + 1) * HEAD_DIM], ones], axis=-1)
            pv = jnp.dot(p_ref[slot, g], v_aug, preferred_element_type=F32)
            alpha = alpha_ref[slot, g]
            for half in range(2):
                cols = slice(half * LANES, (half + 1) * LANES)
                acc_ref[g, :, cols] = alpha * acc_ref[g, :, cols] + pv[:, cols]

    logits(0, 0)

    def attn_pair(u, carry):
        t0 = 2 * u
        t1 = jnp.minimum(t0 + 1, last)
        logits(t1, 1)
        softmax_pv(t0, 0, True)
        logits(jnp.minimum(t0 + 2, last), 0)
        softmax_pv(t1, 1, t0 + 1 < n_t)
        return carry

    lax.fori_loop(0, (n_t + 1) // 2, attn_pair, 0)
    for hd in range(N_HEADS):
        g, j = divmod(hd, GQA_GROUP)
        o_ref[0, :, hd * HEAD_DIM:(hd + 1) * HEAD_DIM] = (
            acc_ref[g, j * tq:(j + 1) * tq, 0:HEAD_DIM] / acc_ref[g, j * tq:(j + 1) * tq, HEAD_DIM:2 * HEAD_DIM])


def _attn_t_kernel(qst_ref, wrow_ref, q_ref, ki_ref, kt_ref, v_ref, o_ref,
                   sc_ref, mm_ref, st_ref, s_ref, m_ref, acc_ref, bias_ref, p_ref, alpha_ref, kmax_ref,
                   *, tq, ts, q0, n_keys, k_sel):
    qi = pl.program_id(1)
    n_tiles_total = ki_ref.shape[1]
    nc = ts // LANES
    rk = KEY_ROWS
    qpos = q0 + qi * tq + lax.broadcasted_iota(I32, (1, tq), 1)
    n_adm = jnp.minimum((qpos // CHUNK + 1) * CHUNK, n_keys)
    last_adm = jnp.minimum(((q0 + (qi + 1) * tq - 1) // CHUNK + 1) * CHUNK, n_keys)
    n_t = jnp.minimum((last_adm + ts - 1) // ts, n_tiles_total)
    n_full = jnp.minimum(jnp.minimum(((q0 + qi * tq) // CHUNK + 1) * CHUNK, n_keys) // ts, n_t)
    last_full = jnp.maximum(n_full - 1, 0)
    qst = qst_ref[0, 0]
    wr = wrow_ref[0, 0]

    mm_ref[0] = jnp.full((rk, tq), _POS_INF, F32)
    mm_ref[1] = jnp.full((rk, tq), _NEG_INF, F32)

    def idx_logits(t, slot):
        st_ref[slot] = jnp.dot(ki_ref[0, t], qst, preferred_element_type=F32)

    def score_tile(t, slot, masked):
        mn, mx = mm_ref[0], mm_ref[1]
        for r0 in range(0, ts, rk):
            acc = None
            for hd in range(N_IDX_HEADS):
                r = jnp.maximum(st_ref[slot, r0:r0 + rk, hd * tq:(hd + 1) * tq], 0.0) * wr[hd:hd + 1, :]
                acc = r if acc is None else acc + r
            key = _key_of(acc)
            if masked:
                adm = (t * ts + r0 + lax.broadcasted_iota(I32, (rk, tq), 0)) < n_adm
                key = jnp.where(adm, key, _INADMISSIBLE)
                mx = jnp.maximum(mx, jnp.where(adm, acc, _NEG_INF))
                mn = jnp.minimum(mn, jnp.where(adm, acc, _POS_INF))
            else:
                mx = jnp.maximum(mx, acc)
                mn = jnp.minimum(mn, acc)
            sc_ref[t, r0:r0 + rk, :] = key
        mm_ref[0] = mn
        mm_ref[1] = mx

    idx_logits(0, 0)

    def score_pair(u, carry):
        t0 = 2 * u
        idx_logits(t0 + 1, 1)
        score_tile(t0, 0, False)
        idx_logits(jnp.minimum(t0 + 2, last_full), 0)
        score_tile(t0 + 1, 1, False)
        return carry

    lax.fori_loop(0, n_full // 2, score_pair, 0)

    @pl.when(n_full % 2 == 1)
    def _():
        score_tile(last_full, 0, False)

    def score_tail(t, carry):
        idx_logits(t, 0)
        score_tile(t, 0, True)
        return carry

    lax.fori_loop(n_full, n_t, score_tail, 0)
    rmin = jnp.min(mm_ref[0], axis=0, keepdims=True)
    rmax = jnp.max(mm_ref[1], axis=0, keepdims=True)

    def count_ge(thr):
        def body(t, cnt):
            for r0 in range(0, ts, COUNT_ROWS):
                cnt = cnt + jnp.where(sc_ref[t, r0:r0 + COUNT_ROWS, :] >= thr, 1.0, 0.0)
            return cnt
        return jnp.sum(lax.fori_loop(0, n_t, body, jnp.zeros((COUNT_ROWS, tq), F32)), axis=0, keepdims=True)

    def bis_cond(st):
        return st[0] > 0

    kf = float(k_sel)
    take_all = n_adm <= k_sel
    lo0 = _key_of(rmin) - 1
    hi0 = _key_of(rmax) + 2
    c_pos = count_ge(jnp.full((1, tq), _KEY_MIN_NORMAL, I32))
    pos = c_pos >= kf
    lo = jnp.where(pos, _KEY_MIN_NORMAL, jnp.minimum(lo0, _KEY_NEG_ZERO - 1))
    hi = jnp.where(pos, hi0, _KEY_MIN_NORMAL)
    clo = jnp.where(pos, c_pos, n_adm.astype(F32))
    chi = jnp.where(pos, 0.0, c_pos)
    exact0 = jnp.where(clo == kf, 1, 0)
    act0 = jnp.where(take_all, 0, jnp.where(exact0 > 0, 0, jnp.where(hi > lo + 1, 1, 0)))

    def bis_step(it, lo, hi, clo, chi, exact, act):
        kmid = (lo & hi) + ((lo ^ hi) >> 1)
        vmid = _key_of((_float_of(lo) + _float_of(hi)) * 0.5)
        by_value = jnp.where(it < VALUE_PIVOT_STEPS, jnp.where(vmid > lo, jnp.where(vmid < hi, 1, 0), 0), 0)
        mid = jnp.where(by_value > 0, vmid, kmid)
        at_zero = jnp.where(lo < _KEY_NEG_ZERO, jnp.where(hi == _KEY_MIN_NORMAL, 1, 0), 0)
        mid = jnp.where(at_zero > 0, _KEY_NEG_ZERO, mid)
        cnt = count_ge(mid)
        up = jnp.where(act > 0, jnp.where(cnt >= kf, 1, 0), 0)
        dn = act - up
        lo = jnp.where(up > 0, mid, lo)
        clo = jnp.where(up > 0, cnt, clo)
        hi = jnp.where(dn > 0, mid, hi)
        chi = jnp.where(dn > 0, cnt, chi)
        hit = jnp.where(up > 0, jnp.where(cnt == kf, 1, 0), 0)
        exact = exact + hit
        act = jnp.where(act > 0, jnp.where(hit > 0, 0, jnp.where(hi > lo + 1, 1, 0)), 0)
        return it + 1, lo, hi, clo, chi, exact, act

    def any_active(act):
        return jnp.max(act.astype(F32))

    def bis_body(st):
        st = bis_step(*bis_step(*st[1:]))
        return (any_active(st[-1]),) + st

    _, _, lo, _, clo, chi, exact, _ = lax.while_loop(
        bis_cond, bis_body, (any_active(act0), jnp.int32(0), lo, hi, clo, chi, exact0, act0))
    thr = jnp.where(take_all, _INADMISSIBLE + 1, lo)
    thr_up = jnp.where(take_all, _INADMISSIBLE + 1, lo + 1)

    need = kf - chi
    tied = jnp.where(take_all, 0, jnp.where(exact > 0, 0, jnp.where(clo > kf, 1, 0)))

    def count_tied_below(cut):
        def body(t, cnt):
            for r0 in range(0, ts, COUNT_ROWS):
                kidx = t * ts + r0 + lax.broadcasted_iota(I32, (COUNT_ROWS, tq), 0)
                cnt = cnt + jnp.where(sc_ref[t, r0:r0 + COUNT_ROWS, :] == thr,
                                      jnp.where(kidx < cut, 1.0, 0.0), 0.0)
            return cnt
        return jnp.sum(lax.fori_loop(0, n_t, body, jnp.zeros((COUNT_ROWS, tq), F32)), axis=0, keepdims=True)

    def cut_body(st):
        _, lo_c, hi_c, act = st
        mid = (lo_c + hi_c) >> 1
        cnt = count_tied_below(mid)
        on = act > 0
        le = cnt <= need
        lo_c = jnp.where(on, jnp.where(le, mid, lo_c), lo_c)
        hi_c = jnp.where(on, jnp.where(le, hi_c, mid), hi_c)
        act = jnp.where(on, jnp.where(hi_c - lo_c > 1, 1, 0), 0)
        return jnp.max(act), lo_c, hi_c, act

    _, cut_lo, _, _ = lax.while_loop(
        bis_cond, cut_body,
        (jnp.max(tied), jnp.zeros((1, tq), I32), jnp.full((1, tq), n_tiles_total * ts + 1, I32), tied))
    cut = jnp.where(tied > 0, cut_lo, n_tiles_total * ts + 1)

    acc_ref[...] = jnp.zeros(acc_ref.shape, F32)
    rc = min(tq, ROW_CHUNK)
    ones = jnp.ones((ts, LANES), BF16)
    last = n_t - 1

    def logits(t, slot):
        for g in range(N_KV_HEADS):
            qg = q_ref[0, g * GQA_GROUP:(g + 1) * GQA_GROUP].reshape(GQA_GROUP * tq, HEAD_DIM)
            s_ref[slot, g * GQA_GROUP * tq:(g + 1) * GQA_GROUP * tq] = jnp.dot(
                qg, kt_ref[0, g, t], preferred_element_type=F32)

    def selection_bias(t, c, live, masked_value):
        kidx = t * ts + c * LANES + lax.broadcasted_iota(I32, (LANES, tq), 0)
        bound = jnp.where(live, jnp.where(kidx < cut, thr, thr_up), _NO_KEY)
        return jnp.where(sc_ref[t, c * LANES:(c + 1) * LANES, :] >= bound, 0.0, masked_value).T

    def value_matmul(t, slot, g):
        v_aug = jnp.concatenate([v_ref[0, g, t], ones], axis=-1)
        return jnp.dot(p_ref[slot, g], v_aug, preferred_element_type=F32)

    def softmax_pv(t, slot, live):
        for c in range(nc):
            bias_ref[slot, :, c * LANES:(c + 1) * LANES] = selection_bias(t, c, live, _NEG_INF)
        for g in range(N_KV_HEADS):
            for j in range(GQA_GROUP):
                hd = g * GQA_GROUP + j
                for r0 in range(0, tq, rc):
                    r1 = j * tq + r0
                    r2 = hd * tq + r0
                    s = [s_ref[slot, r2:r2 + rc, c * LANES:(c + 1) * LANES]
                         + bias_ref[slot, r0:r0 + rc, c * LANES:(c + 1) * LANES] for c in range(nc)]
                    mx = s[0]
                    for c in range(1, nc):
                        mx = jnp.maximum(mx, s[c])
                    m_old = m_ref[hd, r0:r0 + rc]
                    m_new = jnp.maximum(m_old, jnp.broadcast_to(jnp.max(mx, axis=-1, keepdims=True), (rc, LANES)))
                    alpha_ref[slot, g, r1:r1 + rc] = jnp.exp2(m_old - m_new)
                    m_ref[hd, r0:r0 + rc] = m_new
                    for c in range(nc):
                        p_ref[slot, g, r1:r1 + rc, c * LANES:(c + 1) * LANES] = jnp.exp2(s[c] - m_new).astype(BF16)
            pv = value_matmul(t, slot, g)
            for half in range(2):
                cols = slice(half * LANES, (half + 1) * LANES)
                acc_ref[g, :, cols] = alpha_ref[slot, g] * acc_ref[g, :, cols] + pv[:, cols]

    def attend_running():
        def attn_pair(u, carry):
            t0 = 2 * u
            t1 = jnp.minimum(t0 + 1, last)
            logits(t1, 1)
            softmax_pv(t0, 0, True)
            logits(jnp.minimum(t0 + 2, last), 0)
            softmax_pv(t1, 1, t0 + 1 < n_t)
            return carry

        lax.fori_loop(0, (n_t + 1) // 2, attn_pair, 0)

    def softmax_fixed(t, slot, live):
        for c in range(nc):
            bias_ref[slot, :, c * LANES:(c + 1) * LANES] = selection_bias(t, c, live, _NEG_INF)
        for g in range(N_KV_HEADS):
            for j in range(GQA_GROUP):
                hd = g * GQA_GROUP + j
                for r0 in range(0, tq, rc):
                    ref_pt = m_ref[hd, r0:r0 + rc]
                    for c in range(nc):
                        cs = slice(c * LANES, (c + 1) * LANES)
                        s = s_ref[slot, hd * tq + r0:hd * tq + r0 + rc, cs] + bias_ref[slot, r0:r0 + rc, cs]
                        p_ref[slot, g, j * tq + r0:j * tq + r0 + rc, cs] = jnp.exp2(s - ref_pt).astype(BF16)

    def attend_fixed():
        def accumulate(g, pv):
            for half in range(2):
                cols = slice(half * LANES, (half + 1) * LANES)
                acc_ref[g, :, cols] = acc_ref[g, :, cols] + pv[:, cols]

        def attn_pair(u, carry):
            t0 = 2 * u
            logits(t0 + 1, 1)
            softmax_fixed(t0, 0, True)
            logits(jnp.minimum(t0 + 2, last), 0)
            softmax_fixed(t0 + 1, 1, True)
            for g in range(N_KV_HEADS):
                accumulate(g, value_matmul(t0, 0, g) + value_matmul(t0 + 1, 1, g))
            return carry

        lax.fori_loop(0, n_t // 2, attn_pair, 0)

        @pl.when(n_t % 2 == 1)
        def _():
            softmax_fixed(last, 0, True)
            for g in range(N_KV_HEADS):
                accumulate(g, value_matmul(last, 0, g))

    @pl.when(qi == 0)
    def _():
        for g in range(N_KV_HEADS):
            def sq_norm_max(t, best):
                kk = kt_ref[0, g, t].astype(F32)
                return jnp.maximum(best, jnp.sum(kk * kk, axis=0, keepdims=True))
            best = lax.fori_loop(0, n_tiles_total, sq_norm_max, jnp.zeros((1, ts), F32))
            kmax_ref[g] = jnp.broadcast_to(jnp.max(best, axis=-1, keepdims=True), (8, LANES))

    logits(0, 0)
    top = jnp.zeros((tq, LANES), F32)
    for hd in range(N_HEADS):
        qf = q_ref[0, hd].astype(F32)
        q_sq = _row_sum(qf * qf)
        ref_pt = jnp.sqrt(q_sq * kmax_ref[hd // GQA_GROUP, 0:1, :]) * 1.001 + 1e-3
        m_ref[hd] = ref_pt
        top = jnp.maximum(top, ref_pt)
    fits = jnp.max(top) <= MAX_FIXED_REFERENCE

    @pl.when(fits)
    def _():
        attend_fixed()

    @pl.when(jnp.logical_not(fits))
    def _():
        m_ref[...] = jnp.full(m_ref.shape, _M_INIT, F32)
        attend_running()

    for hd in range(N_HEADS):
        g, j = divmod(hd, GQA_GROUP)
        o_ref[0, :, hd * HEAD_DIM:(hd + 1) * HEAD_DIM] = (
            acc_ref[g, j * tq:(j + 1) * tq, 0:HEAD_DIM] / acc_ref[g, j * tq:(j + 1) * tq, HEAD_DIM:2 * HEAD_DIM])


def _attend_t(qst, wrow, q_att, ki4, kt, v, q0, n_keys, tq, ts):
    B, _, T, _ = q_att.shape
    n_tiles = ki4.shape[1]
    k_sel = min(TOPK_MAX, n_keys // 4)
    kern = functools.partial(_attn_t_kernel, tq=tq, ts=ts, q0=q0, n_keys=n_keys, k_sel=k_sel)
    return pl.pallas_call(
        kern, grid=(B, T // tq),
        in_specs=[
            pl.BlockSpec((1, 1, 4 * IDX_DIM, N_IDX_HEADS * tq), lambda b, i: (b, i, 0, 0)),
            pl.BlockSpec((1, 1, N_IDX_HEADS, tq), lambda b, i: (b, i, 0, 0)),
            pl.BlockSpec((1, N_HEADS, tq, HEAD_DIM), lambda b, i: (b, 0, i, 0)),
            _resident((1, n_tiles, ts, 4 * IDX_DIM), lambda b, i: (b, 0, 0, 0)),
            _resident((1, N_KV_HEADS, n_tiles, HEAD_DIM, ts), lambda b, i: (b, 0, 0, 0, 0)),
            _resident((1, N_KV_HEADS, n_tiles, ts, HEAD_DIM), lambda b, i: (b, 0, 0, 0, 0)),
        ],
        out_specs=pl.BlockSpec((1, tq, N_HEADS * HEAD_DIM), lambda b, i: (b, i, 0)),
        out_shape=jax.ShapeDtypeStruct((B, T, N_HEADS * HEAD_DIM), F32),
        scratch_shapes=[
            pltpu.VMEM((n_tiles, ts, tq), I32),
            pltpu.VMEM((2, KEY_ROWS, tq), F32),
            pltpu.VMEM((2, ts, N_IDX_HEADS * tq), F32),
            pltpu.VMEM((2, N_HEADS * tq, ts), F32),
            pltpu.VMEM((N_HEADS, tq, LANES), F32),
            pltpu.VMEM((N_KV_HEADS, GQA_GROUP * tq, 2 * HEAD_DIM), F32),
            pltpu.VMEM((2, tq, ts), F32),
            pltpu.VMEM((2, N_KV_HEADS, GQA_GROUP * tq, ts), BF16),
            pltpu.VMEM((2, N_KV_HEADS, GQA_GROUP * tq, LANES), F32),
            pltpu.VMEM((N_KV_HEADS, 8, LANES), F32),
        ],
        name="attend_t",
        compiler_params=pltpu.CompilerParams(dimension_semantics=("arbitrary", "arbitrary"),
                                             vmem_limit_bytes=VMEM_LIMIT),
    )(qst, wrow, q_att, ki4, kt, v)


def _attend(qs, kiw, q_att, ki4, kt, v, q0, n_keys, tq, ts):
    B, _, T, _ = qs.shape
    lp = ki4.shape[1]
    n_tiles = lp // ts
    k_sel = min(TOPK_MAX, n_keys // 4)
    kern = functools.partial(_attn_kernel, tq=tq, ts=ts, q0=q0, n_keys=n_keys, k_sel=k_sel)
    keys = pl.BlockSpec((1, lp, 2 * LANES), lambda b, i: (b, 0, 0))
    return pl.pallas_call(
        kern, grid=(B, T // tq),
        in_specs=[
            pl.BlockSpec((1, N_IDX_HEADS, tq, 4 * IDX_DIM), lambda b, i: (b, 0, i, 0)),
            pl.BlockSpec((1, tq, LANES), lambda b, i: (b, i, 0)),
            pl.BlockSpec((1, N_HEADS, tq, HEAD_DIM), lambda b, i: (b, 0, i, 0)),
            keys, keys, keys,
        ],
        out_specs=pl.BlockSpec((1, tq, N_HEADS * HEAD_DIM), lambda b, i: (b, i, 0)),
        out_shape=jax.ShapeDtypeStruct((B, T, N_HEADS * HEAD_DIM), F32),
        scratch_shapes=[
            pltpu.VMEM((n_tiles, tq, ts), I32),
            pltpu.VMEM((2, tq, LANES), F32),
            pltpu.VMEM((N_IDX_HEADS, tq, LANES), F32),
            pltpu.VMEM((2, N_HEADS * tq, ts), F32),
            pltpu.VMEM((N_HEADS, tq, LANES), F32),
            pltpu.VMEM((N_KV_HEADS, GQA_GROUP * tq, 2 * HEAD_DIM), F32),
            pltpu.VMEM((2, tq, ts), F32),
            pltpu.VMEM((2, N_KV_HEADS, GQA_GROUP * tq, ts), BF16),
            pltpu.VMEM((2, N_KV_HEADS, GQA_GROUP * tq, LANES), F32),
        ],
        name="attend",
        compiler_params=pltpu.CompilerParams(dimension_semantics=("arbitrary", "arbitrary"),
                                             vmem_limit_bytes=VMEM_LIMIT),
    )(qs, kiw, q_att, ki4, kt, v)


def _rms(x, g):
    return x * lax.rsqrt(jnp.mean(x * x, axis=-1, keepdims=True) + RMS_EPS) * g


def _out_kernel(x_ref, pa_ref, sgb_ref, b_ref, wo_ref, g1_ref, g2_ref, wgu_ref, wd_ref, g3_ref, y_ref, *, d_ff):
    merged = pa_ref[...] + sgb_ref[...] * b_ref[...]
    mix = jnp.dot(merged.astype(BF16), wo_ref[...], preferred_element_type=F32)
    x1 = x_ref[...] + _rms(mix, g1_ref[...])
    h2 = _rms(x1, g2_ref[...]).astype(BF16)
    gu = jnp.dot(h2, wgu_ref[...], preferred_element_type=F32)
    gate, up = gu[:, :d_ff], gu[:, d_ff:]
    act = (gate * _sigmoid(gate) * up).astype(BF16)
    f = jnp.dot(act, wd_ref[...], preferred_element_type=F32)
    y_ref[...] = x1 + _rms(f, g3_ref[...])


def _finish(x, pa, sgb, b, w_out, g_post, g_ffn_pre, w_gate_up, w_down, g_ffn_post, tm):
    N, D = x.shape
    d_ff = w_down.shape[0]
    row = pl.BlockSpec((tm, D), lambda i: (i, 0))
    const = lambda i: (0, 0)
    return pl.pallas_call(
        functools.partial(_out_kernel, d_ff=d_ff), grid=(N // tm,),
        in_specs=[row, row, row, row,
                  _resident((D, D), const), _resident((1, D), const), _resident((1, D), const),
                  _resident((D, 2 * d_ff), const), _resident((d_ff, D), const), _resident((1, D), const)],
        out_specs=row, out_shape=jax.ShapeDtypeStruct((N, D), F32), name="finish",
        compiler_params=pltpu.CompilerParams(dimension_semantics=("arbitrary",), vmem_limit_bytes=VMEM_LIMIT),
    )(x, pa, sgb, b, w_out, g_post, g_ffn_pre, w_gate_up, w_down, g_ffn_post)


def _tile_keys(n_keys, ts):
    return -(-n_keys // ts) * ts


def _layer(x, pos0, hist, k_cache, v_cache, ki_cache, wts, tm, tq, ts, tm_out):
    (w_in_r, w_pool, pool_scale, w_out, w_gate_up, w_down, g_pre, g_post, g_ffn_pre, g_ffn_post) = wts
    B, T, D = x.shape
    hist16 = jnp.concatenate([jnp.zeros((B, HALO // 2 - POOL_HIST, POOL_WIDTH), F32), hist], axis=1)
    own_keys_only = k_cache is None and T % ts == 0 and ts % tm == 0 and tq == LANES and tm % LANES == 0
    outs = _project(x, hist16, pos0, w_in_r, g_pre, w_pool, pool_scale, tm, ts if own_keys_only else None)
    u, q_att, k, v, qs, kiw, pa, sgb = outs[:8]
    ki = kiw[:, :, :IDX_DIM]
    if own_keys_only:
        kt, vt, ki4, wrow = outs[8:]
        b = _attend_t(qs, wrow, q_att, ki4, kt, vt, pos0, T, tq, ts)
    else:
        if k_cache is None:
            k_all, v_all, ki_all = k, v, ki
        else:
            k_all = jnp.concatenate([k_cache.reshape(B, -1, N_KV_HEADS * HEAD_DIM), k], axis=1)
            v_all = jnp.concatenate([v_cache.reshape(B, -1, N_KV_HEADS * HEAD_DIM), v], axis=1)
            ki_all = jnp.concatenate([ki_cache, ki], axis=1)
        n_keys = k_all.shape[1]
        lp = _tile_keys(n_keys, ts)
        pad = ((0, 0), (0, lp - n_keys), (0, 0))
        kt = jnp.pad(k_all.astype(BF16), pad)
        vt = jnp.pad(v_all.astype(BF16), pad)
        ki4 = _split_keys(jnp.pad(ki_all, pad), ts)
        b = _attend(qs, kiw, q_att, ki4, kt, vt, pos0, n_keys, tq, ts)
    y = _finish(x.reshape(B * T, D), pa.reshape(B * T, D), sgb.reshape(B * T, D), b.reshape(B * T, D),
                w_out, g_post, g_ffn_pre, w_gate_up, w_down, g_ffn_post, tm_out).reshape(B, T, D)
    new_pool = jnp.concatenate([hist, u], axis=1)[:, T:]
    return (y, k.reshape(B, T, N_KV_HEADS, HEAD_DIM), v.reshape(B, T, N_KV_HEADS, HEAD_DIM), ki, new_pool)


def _relayout_w_in(w_in):
    d = w_in.shape[0]
    o_kiw = POOL_WIDTH + N_HEADS * HEAD_DIM + 2 * N_KV_HEADS * HEAD_DIM + N_IDX_HEADS * IDX_DIM
    narrow = IDX_DIM + N_IDX_HEADS
    padded = jnp.concatenate([w_in[:, :o_kiw + narrow], jnp.zeros((d, LANES - narrow), w_in.dtype),
                              w_in[:, o_kiw + narrow:]], axis=1)
    return padded.astype(BF16)


def kernel(x_prompt, x_sample, cache_k, cache_v, cache_k_idx, state_pool, w_in, w_pool, pool_scale, w_out,
           w_gate_up, w_down, norm_mix_pre, norm_mix_post, norm_ffn_pre, norm_ffn_post):
    depth = w_in.shape[0]
    past = cache_k.shape[2]
    t_p, t_s = x_prompt.shape[1], x_sample.shape[1]
    hist_p = jnp.zeros((x_prompt.shape[0], POOL_HIST, POOL_WIDTH), x_prompt.dtype)
    xp, xs = x_prompt, x_sample
    outs = [[] for _ in range(8)]
    for l in range(depth):
        wts = (_relayout_w_in(w_in[l]), w_pool[l].astype(BF16), pool_scale[l][None, :], w_out[l].astype(BF16),
               w_gate_up[l].astype(BF16), w_down[l].astype(BF16), norm_mix_pre[l][None, :],
               norm_mix_post[l][None, :], norm_ffn_pre[l][None, :], norm_ffn_post[l][None, :])
        tm_p = min(256, t_p)
        tq_p = min(128, t_p)
        xp, k1, v1, ki1, p1 = _layer(xp, 0, hist_p, None, None, None, wts, tm_p, tq_p, 512, tm_p)
        n_s = xs.shape[0] * t_s
        xs, k2, v2, ki2, p2 = _layer(xs, past, state_pool[l], cache_k[l], cache_v[l], cache_k_idx[l], wts,
                                     t_s, t_s, 512, min(256, n_s))
        for lst, val in zip(outs, (k1, v1, ki1, p1, k2, v2, ki2, p2)):
            lst.append(val)
    return (xp, xs) + tuple(jnp.stack(o) for o in outs)
```

```python
import functools

import jax
import jax.numpy as jnp
from jax import lax
from jax.experimental import pallas as pl
from jax.experimental.pallas import tpu as pltpu

F32 = jnp.float32
BF16 = jnp.bfloat16
I32 = jnp.int32

LANES = 128
CHUNK = 64
POOL_WINDOWS = (2, 4, 8, 16)
N_POOL_GROUPS = 4
POOL_GROUP_WIDTH = 128
POOL_WIDTH = N_POOL_GROUPS * POOL_GROUP_WIDTH
POOL_HIST = 15
N_HEADS = 8
N_KV_HEADS = 2
HEAD_DIM = 128
GQA_GROUP = N_HEADS // N_KV_HEADS
ROPE_THETA = 500000.0
N_IDX_HEADS = 8
IDX_DIM = 64
TOPK_MAX = 256
RMS_EPS = 1e-6
ATTN_SCALE = HEAD_DIM ** -0.5
IDX_SCALE = (N_IDX_HEADS ** -0.5) * (IDX_DIM ** -0.5)

LOG2_E = 1.4426950408889634
VALUE_PIVOT_STEPS = 28
MAX_FIXED_REFERENCE = 40.0
SPLIT_ROWS = 1536
KEY_ROWS = 16
COUNT_ROWS = 32
ROW_CHUNK = 32
HALO = 32
VMEM_LIMIT = 56 * 1024 * 1024

_NEG_INF = float("-inf")
_POS_INF = float("inf")
_INADMISSIBLE = -(2 ** 31)
_NO_KEY = 2 ** 31 - 1
_KEY_MIN_NORMAL = 0x00800000
_KEY_NEG_ZERO = -1
_M_INIT = -1e30


def _resident(block_shape, index_map):
    return pl.BlockSpec(block_shape, index_map, pipeline_mode=pl.Buffered(1))


def _rope(xs, cos, sin, half, period):
    lane = lax.broadcasted_iota(I32, xs.shape, 1)
    ahead = pltpu.roll(xs, LANES - half, 1)
    behind = pltpu.roll(xs, half, 1)
    partner = jnp.where((lane & (period - 1)) < half, ahead, behind)
    return xs * cos + partner * sin


def _sigmoid(x):
    return 1.0 / (1.0 + jnp.exp(-x))


def _proj_kernel(x_ref, g_ref, w_ref, hist_ref, cq_ref, sq_ref, ci_ref, si_ref, wpool_ref, pscale_ref,
                 u_ref, qatt_ref, k_ref, v_ref, qs_ref, kiw_ref, pa_ref, sgb_ref, *rest, tm, pos0, d_model, key_tiles):
    if key_tiles:
        kt_ref, vt_ref, ki4_ref, wrow_ref, e_ref, s2_ref, s4_ref, s8_ref = rest
    else:
        e_ref, s2_ref, s4_ref, s8_ref = rest
    i = pl.program_id(1)
    x = x_ref[0]
    h = x * lax.rsqrt(jnp.mean(x * x, axis=-1, keepdims=True) + RMS_EPS) * g_ref[...]
    o_q = POOL_WIDTH
    o_k = o_q + N_HEADS * HEAD_DIM
    o_v = o_k + N_KV_HEADS * HEAD_DIM
    o_qi = o_v + N_KV_HEADS * HEAD_DIM
    o_kiw = o_qi + N_IDX_HEADS * IDX_DIM
    o_ga = o_kiw + LANES
    o_gb = o_ga + d_model

    proj = jnp.dot(h.astype(BF16), w_ref[...], preferred_element_type=F32)

    cq, sq, ci, si = cq_ref[...], sq_ref[...], ci_ref[...], si_ref[...]
    half_q = HEAD_DIM // 8
    half_i = IDX_DIM // 8

    for hd in range(N_HEADS):
        qh = _rope(proj[:, o_q + hd * HEAD_DIM:o_q + (hd + 1) * HEAD_DIM], cq, sq, half_q, HEAD_DIM)
        qatt_ref[0, hd] = (qh * (ATTN_SCALE * LOG2_E)).astype(BF16)
    for kh in range(N_KV_HEADS):
        k_h = _rope(proj[:, o_k + kh * HEAD_DIM:o_k + (kh + 1) * HEAD_DIM], cq, sq, half_q, HEAD_DIM)
        k_ref[0, :, kh * HEAD_DIM:(kh + 1) * HEAD_DIM] = k_h
        if key_tiles:
            kt_ref[0, kh, 0] = k_h.T.astype(BF16)
            vt_ref[0, kh, 0] = proj[:, o_v + kh * HEAD_DIM:o_v + (kh + 1) * HEAD_DIM].astype(BF16)
    v_ref[0] = proj[:, o_v:o_qi]

    for pr in range(N_IDX_HEADS // 2):
        qi2 = _rope(proj[:, o_qi + pr * LANES:o_qi + (pr + 1) * LANES], ci, si, half_i, IDX_DIM)
        hi = qi2.astype(BF16).astype(F32)
        lo = qi2 - hi
        lane2 = lax.broadcasted_iota(I32, qi2.shape, 1)
        first = jnp.where(lane2 < IDX_DIM, hi, pltpu.roll(lo, IDX_DIM, 1))
        second = jnp.where(lane2 < IDX_DIM, pltpu.roll(hi, IDX_DIM, 1), lo)
        for sub, slab in enumerate((first, second)):
            hd = 2 * pr + sub
            if key_tiles:
                for sb in range(tm // LANES):
                    slab_t = slab[sb * LANES:(sb + 1) * LANES].T.astype(BF16)
                    qs_ref[0, sb, 0:LANES, hd * LANES:(hd + 1) * LANES] = slab_t
                    qs_ref[0, sb, LANES:2 * LANES, hd * LANES:(hd + 1) * LANES] = slab_t
            else:
                qs_ref[0, hd, :, 0:LANES] = slab.astype(BF16)
                qs_ref[0, hd, :, LANES:2 * LANES] = slab.astype(BF16)

    kiw = _rope(proj[:, o_kiw:o_kiw + LANES], ci, si, half_i, IDX_DIM)
    lane = lax.broadcasted_iota(I32, kiw.shape, 1)
    kiw_full = jnp.where(lane < IDX_DIM, kiw, proj[:, o_kiw:o_kiw + LANES] * IDX_SCALE)
    kiw_ref[0] = kiw_full
    if key_tiles:
        ki_hi = kiw.astype(BF16).astype(F32)
        ki_lo = kiw - ki_hi
        ki4_ref[0, 0, :, 0:LANES] = jnp.where(lane < IDX_DIM, ki_hi, pltpu.roll(ki_hi, IDX_DIM, 1)).astype(BF16)
        ki4_ref[0, 0, :, LANES:2 * LANES] = jnp.where(lane < IDX_DIM, ki_lo, pltpu.roll(ki_lo, IDX_DIM, 1)).astype(BF16)
        for sb in range(tm // LANES):
            wrow_ref[0, sb] = kiw_full[sb * LANES:(sb + 1) * LANES].T[IDX_DIM:IDX_DIM + N_IDX_HEADS]

    u = proj[:, 0:POOL_WIDTH]
    u_ref[0] = u

    @pl.when(i == 0)
    def _():
        e_ref[0:HALO // 2, :] = jnp.zeros((HALO // 2, POOL_WIDTH), F32)
        e_ref[HALO // 2:HALO, :] = hist_ref[0]

    e_ref[HALO:HALO + tm, :] = u
    n2, n4, n8 = tm + 24, tm + 16, tm + 8
    s2_ref[8:8 + n2, :] = e_ref[8:8 + n2, :] + e_ref[7:7 + n2, :]
    s4_ref[16:16 + n4, :] = s2_ref[16:16 + n4, :] + s2_ref[14:14 + n4, :]
    s8_ref[24:24 + n8, :] = s4_ref[24:24 + n8, :] + s4_ref[20:20 + n8, :]
    s16 = s8_ref[HALO:HALO + tm, :] + s8_ref[HALO - 8:HALO - 8 + tm, :]
    wins = (s2_ref[HALO:HALO + tm, :], s4_ref[HALO:HALO + tm, :], s8_ref[HALO:HALO + tm, :], s16)
    e_ref[HALO // 2:HALO, :] = e_ref[HALO // 2 + tm:HALO + tm, :]

    pos = pos0 + i * tm + lax.broadcasted_iota(I32, (tm, POOL_GROUP_WIDTH), 0)
    a_parts = []
    for g, w in enumerate(POOL_WINDOWS):
        lo_l, hi_l = g * POOL_GROUP_WIDTH, (g + 1) * POOL_GROUP_WIDTH
        cnt = jnp.minimum(pos + 1, w).astype(F32)
        pooled = wins[g][:, lo_l:hi_l] / cnt - u[:, lo_l:hi_l]
        a_parts.append(jnp.dot(pooled.astype(BF16), wpool_ref[g], preferred_element_type=F32))
    a = jnp.concatenate(a_parts, axis=-1) * pscale_ref[...]
    pa_ref[0] = _sigmoid(proj[:, o_ga:o_gb]) * a
    sgb_ref[0] = _sigmoid(proj[:, o_gb:o_gb + d_model])


def _project(x, hist16, pos0, w_in_r, g_pre, w_pool, pool_scale, tm, key_tile=None):
    B, T, D = x.shape
    W = w_in_r.shape[1]
    pos = pos0 + jnp.arange(T, dtype=I32)
    cq, sq = _rope_tables(pos, HEAD_DIM)
    ci, si = _rope_tables(pos, IDX_DIM)
    row = lambda b, i: (b, i, 0)
    tab = lambda b, i: (i, 0)
    const2 = lambda b, i: (0, 0)
    out_shape = (
        jax.ShapeDtypeStruct((B, T, POOL_WIDTH), F32),
        jax.ShapeDtypeStruct((B, N_HEADS, T, HEAD_DIM), BF16),
        jax.ShapeDtypeStruct((B, T, N_KV_HEADS * HEAD_DIM), F32),
        jax.ShapeDtypeStruct((B, T, N_KV_HEADS * HEAD_DIM), F32),
        jax.ShapeDtypeStruct((B, N_IDX_HEADS, T, 4 * IDX_DIM), BF16),
        jax.ShapeDtypeStruct((B, T, LANES), F32),
        jax.ShapeDtypeStruct((B, T, D), F32),
        jax.ShapeDtypeStruct((B, T, D), F32),
    )
    out_specs = (
        pl.BlockSpec((1, tm, POOL_WIDTH), row),
        pl.BlockSpec((1, N_HEADS, tm, HEAD_DIM), lambda b, i: (b, 0, i, 0)),
        pl.BlockSpec((1, tm, N_KV_HEADS * HEAD_DIM), row),
        pl.BlockSpec((1, tm, N_KV_HEADS * HEAD_DIM), row),
        pl.BlockSpec((1, N_IDX_HEADS, tm, 4 * IDX_DIM), lambda b, i: (b, 0, i, 0)),
        pl.BlockSpec((1, tm, LANES), row),
        pl.BlockSpec((1, tm, D), row),
        pl.BlockSpec((1, tm, D), row),
    )
    in_specs = [
        pl.BlockSpec((1, tm, D), row),
        _resident((1, D), const2),
        _resident((D, W), const2),
        pl.BlockSpec((1, HALO // 2, POOL_WIDTH), lambda b, i: (b, 0, 0)),
        pl.BlockSpec((tm, LANES), tab), pl.BlockSpec((tm, LANES), tab),
        pl.BlockSpec((tm, LANES), tab), pl.BlockSpec((tm, LANES), tab),
        _resident((N_POOL_GROUPS, POOL_GROUP_WIDTH, D // N_POOL_GROUPS), lambda b, i: (0, 0, 0)),
        _resident((1, D), const2),
    ]
    if key_tile is not None:
        assert key_tile % tm == 0 and T % key_tile == 0 and tm % LANES == 0
        per = key_tile // tm
        n_tiles = T // key_tile
        sub = tm // LANES
        out_shape = out_shape[:4] + (
            jax.ShapeDtypeStruct((B, T // LANES, 4 * IDX_DIM, N_IDX_HEADS * LANES), BF16),
        ) + out_shape[5:] + (
            jax.ShapeDtypeStruct((B, N_KV_HEADS, n_tiles, HEAD_DIM, key_tile), BF16),
            jax.ShapeDtypeStruct((B, N_KV_HEADS, n_tiles, key_tile, HEAD_DIM), BF16),
            jax.ShapeDtypeStruct((B, n_tiles, key_tile, 4 * IDX_DIM), BF16),
            jax.ShapeDtypeStruct((B, T // LANES, N_IDX_HEADS, LANES), F32),
        )
        out_specs = out_specs[:4] + (
            pl.BlockSpec((1, sub, 4 * IDX_DIM, N_IDX_HEADS * LANES), lambda b, i: (b, i, 0, 0)),
        ) + out_specs[5:] + (
            pl.BlockSpec((1, N_KV_HEADS, 1, HEAD_DIM, tm), lambda b, i: (b, 0, i // per, 0, i % per)),
            pl.BlockSpec((1, N_KV_HEADS, 1, tm, HEAD_DIM), lambda b, i: (b, 0, i // per, i % per, 0)),
            pl.BlockSpec((1, 1, tm, 4 * IDX_DIM), lambda b, i: (b, i // per, i % per, 0)),
            pl.BlockSpec((1, sub, N_IDX_HEADS, LANES), lambda b, i: (b, i, 0, 0)),
        )
    scratch = [pltpu.VMEM((HALO + tm, POOL_WIDTH), F32) for _ in range(4)]
    return pl.pallas_call(
        functools.partial(_proj_kernel, tm=tm, pos0=pos0, d_model=D, key_tiles=key_tile is not None),
        grid=(B, T // tm), in_specs=in_specs, out_specs=out_specs, out_shape=out_shape,
        scratch_shapes=scratch, name="proj",
        compiler_params=pltpu.CompilerParams(dimension_semantics=("arbitrary", "arbitrary"),
                                             vmem_limit_bytes=VMEM_LIMIT),
    )(x, g_pre, w_in_r, hist16, cq, sq, ci, si, w_pool, pool_scale)


def _rope_tables(pos, dim):
    rot = dim // 4
    half = rot // 2
    inv = ROPE_THETA ** (-jnp.arange(half, dtype=F32) / half)
    ang = pos.astype(F32)[:, None] * inv[None, :]
    cos, sin = jnp.cos(ang), jnp.sin(ang)
    rest = dim - rot
    n = pos.shape[0]
    c = jnp.concatenate([cos, cos, jnp.ones((n, rest), F32)], axis=-1)
    s = jnp.concatenate([-sin, sin, jnp.zeros((n, rest), F32)], axis=-1)
    return jnp.tile(c, (1, LANES // dim)), jnp.tile(s, (1, LANES // dim))


def _split_kernel(k_ref, out_ref):
    k = k_ref[0]
    hi = k.astype(BF16).astype(F32)
    lo = k - hi
    out_ref[0, :, 0:LANES] = jnp.concatenate([hi, hi], axis=-1).astype(BF16)
    out_ref[0, :, LANES:2 * LANES] = jnp.concatenate([lo, lo], axis=-1).astype(BF16)


def _split_keys(ki, ts):
    B, Lp, _ = ki.shape
    rows = ts * max(1, SPLIT_ROWS // ts)
    rows = rows if Lp % rows == 0 else ts
    return pl.pallas_call(
        _split_kernel, grid=(B, Lp // rows),
        in_specs=[pl.BlockSpec((1, rows, IDX_DIM), lambda b, t: (b, t, 0))],
        out_specs=pl.BlockSpec((1, rows, 4 * IDX_DIM), lambda b, t: (b, t, 0)),
        out_shape=jax.ShapeDtypeStruct((B, Lp, 4 * IDX_DIM), BF16), name="split_keys",
        compiler_params=pltpu.CompilerParams(dimension_semantics=("arbitrary", "arbitrary")),
    )(ki)


def _key_of(x):
    b = pltpu.bitcast(x, I32)
    return b ^ ((b >> 31) & 0x7FFFFFFF)


def _float_of(k):
    return pltpu.bitcast(k ^ ((k >> 31) & 0x7FFFFFFF), F32)


def _row_sum(x):
    return jnp.broadcast_to(jnp.sum(x, axis=-1, keepdims=True), x.shape)


def _attn_kernel(qs_ref, kiw_ref, q_ref, ki_ref, kt_ref, v_ref, o_ref,
                 sc_ref, mm_ref, wb_ref, s_ref, m_ref, acc_ref, bias_ref, p_ref, alpha_ref,
                 *, tq, ts, q0, n_keys, k_sel):
    qi = pl.program_id(1)
    n_tiles_total = ki_ref.shape[1] // ts
    nc = ts // LANES
    nt_dims = (((1,), (1,)), ((), ()))

    def key_rows(t):
        return pl.ds(pl.multiple_of(t * ts, ts), ts)

    row = lax.broadcasted_iota(I32, (tq, LANES), 0)
    lane = lax.broadcasted_iota(I32, (tq, LANES), 1)
    qpos = q0 + qi * tq + row
    n_adm = jnp.minimum((qpos // CHUNK + 1) * CHUNK, n_keys)
    last_adm = jnp.minimum(((q0 + (qi + 1) * tq - 1) // CHUNK + 1) * CHUNK, n_keys)
    n_t = jnp.minimum((last_adm + ts - 1) // ts, n_tiles_total)

    kiw = kiw_ref[0]
    for hd in range(N_IDX_HEADS):
        wb_ref[hd] = jnp.broadcast_to(kiw[:, IDX_DIM + hd:IDX_DIM + hd + 1], (tq, LANES))
    qs2 = qs_ref[0].reshape(N_IDX_HEADS * tq, 4 * IDX_DIM)

    def idx_logits(t, slot):
        s_ref[slot] = lax.dot_general(qs2, ki_ref[0, key_rows(t), :], nt_dims,
                                      preferred_element_type=F32)

    rc = min(tq, ROW_CHUNK)
    mm_ref[0] = jnp.full((tq, LANES), _POS_INF, F32)
    mm_ref[1] = jnp.full((tq, LANES), _NEG_INF, F32)

    def score_tile(t, slot, masked):
        for r0 in range(0, tq, rc):
            rows = slice(r0, r0 + rc)
            mn, mx = mm_ref[0, rows], mm_ref[1, rows]
            for c in range(nc):
                cs = slice(c * LANES, (c + 1) * LANES)
                acc = None
                for hd in range(N_IDX_HEADS):
                    r = jnp.maximum(s_ref[slot, hd * tq + r0:hd * tq + r0 + rc, cs], 0.0) * wb_ref[hd, rows]
                    acc = r if acc is None else acc + r
                key = _key_of(acc)
                if masked:
                    qpos_c = q0 + qi * tq + r0 + lax.broadcasted_iota(I32, (rc, LANES), 0)
                    n_adm_c = jnp.minimum((qpos_c // CHUNK + 1) * CHUNK, n_keys)
                    adm = (t * ts + c * LANES + lax.broadcasted_iota(I32, (rc, LANES), 1)) < n_adm_c
                    key = jnp.where(adm, key, _INADMISSIBLE)
                    mx = jnp.maximum(mx, jnp.where(adm, acc, _NEG_INF))
                    mn = jnp.minimum(mn, jnp.where(adm, acc, _POS_INF))
                else:
                    mx = jnp.maximum(mx, acc)
                    mn = jnp.minimum(mn, acc)
                sc_ref[t, rows, cs] = key
            mm_ref[0, rows] = mn
            mm_ref[1, rows] = mx

    n_full = jnp.minimum(jnp.minimum(((q0 + qi * tq) // CHUNK + 1) * CHUNK, n_keys) // ts, n_t)
    last_full = jnp.maximum(n_full - 1, 0)
    idx_logits(0, 0)

    def score_pair(u, carry):
        t0 = 2 * u
        t1 = jnp.minimum(t0 + 1, last_full)
        idx_logits(t1, 1)
        score_tile(t0, 0, False)
        idx_logits(jnp.minimum(t0 + 2, last_full), 0)
        score_tile(t1, 1, False)
        return carry

    lax.fori_loop(0, (n_full + 1) // 2, score_pair, 0)

    def score_tail(t, carry):
        idx_logits(t, 0)
        score_tile(t, 0, True)
        return carry

    lax.fori_loop(n_full, n_t, score_tail, 0)
    rmin = jnp.broadcast_to(jnp.min(mm_ref[0], axis=-1, keepdims=True), (tq, LANES))
    rmax = jnp.broadcast_to(jnp.max(mm_ref[1], axis=-1, keepdims=True), (tq, LANES))

    def count_ge(thr):
        def body(t, cnt):
            for c in range(nc):
                cnt = cnt + jnp.where(sc_ref[t, :, c * LANES:(c + 1) * LANES] >= thr, 1.0, 0.0)
            return cnt
        return _row_sum(lax.fori_loop(0, n_t, body, jnp.zeros((tq, LANES), F32)))

    def bis_cond(st):
        return st[0] > 0

    kf = float(k_sel)
    take_all = n_adm <= k_sel
    lo0 = _key_of(rmin) - 1
    hi0 = _key_of(rmax) + 2
    c_pos = count_ge(jnp.full((tq, LANES), _KEY_MIN_NORMAL, I32))
    c_nn = count_ge(jnp.full((tq, LANES), _KEY_NEG_ZERO, I32))
    pos = c_pos >= kf
    neg = c_nn < kf
    lo = jnp.where(pos, _KEY_MIN_NORMAL, jnp.where(neg, lo0, _KEY_NEG_ZERO))
    hi = jnp.where(pos, hi0, jnp.where(neg, _KEY_NEG_ZERO, _KEY_MIN_NORMAL))
    clo = jnp.where(pos, c_pos, jnp.where(neg, n_adm.astype(F32), c_nn))
    chi = jnp.where(pos, 0.0, jnp.where(neg, c_nn, c_pos))
    exact0 = jnp.where(clo == kf, 1, 0)
    act0 = jnp.where(take_all, 0, jnp.where(exact0 > 0, 0, jnp.where(hi > lo + 1, 1, 0)))

    def bis_step(it, lo, hi, clo, chi, exact, act):
        kmid = (lo & hi) + ((lo ^ hi) >> 1)
        vmid = _key_of((_float_of(lo) + _float_of(hi)) * 0.5)
        by_value = jnp.where(it < VALUE_PIVOT_STEPS, jnp.where(vmid > lo, jnp.where(vmid < hi, 1, 0), 0), 0)
        mid = jnp.where(by_value > 0, vmid, kmid)
        cnt = count_ge(mid)
        up = jnp.where(act > 0, jnp.where(cnt >= kf, 1, 0), 0)
        dn = act - up
        lo = jnp.where(up > 0, mid, lo)
        clo = jnp.where(up > 0, cnt, clo)
        hi = jnp.where(dn > 0, mid, hi)
        chi = jnp.where(dn > 0, cnt, chi)
        hit = jnp.where(up > 0, jnp.where(cnt == kf, 1, 0), 0)
        exact = exact + hit
        act = jnp.where(act > 0, jnp.where(hit > 0, 0, jnp.where(hi > lo + 1, 1, 0)), 0)
        return it + 1, lo, hi, clo, chi, exact, act

    def any_active(act):
        return jnp.max(act.astype(F32))

    def bis_body(st):
        st = bis_step(*bis_step(*st[1:]))
        return (any_active(st[-1]),) + st

    _, _, lo, _, clo, chi, exact, _ = lax.while_loop(
        bis_cond, bis_body, (any_active(act0), jnp.int32(0), lo, hi, clo, chi, exact0, act0))
    thr = jnp.where(take_all, _INADMISSIBLE + 1, lo)
    thr_up = jnp.where(take_all, _INADMISSIBLE + 1, lo + 1)

    need = kf - chi
    tied = jnp.where(take_all, 0, jnp.where(exact > 0, 0, jnp.where(clo > kf, 1, 0)))

    def count_tied_below(cut):
        def body(t, cnt):
            for c in range(nc):
                s = sc_ref[t, :, c * LANES:(c + 1) * LANES]
                col = t * ts + c * LANES + lane
                cnt = cnt + jnp.where(s == thr, jnp.where(col < cut, 1.0, 0.0), 0.0)
            return cnt
        return _row_sum(lax.fori_loop(0, n_t, body, jnp.zeros((tq, LANES), F32)))

    def cut_body(st):
        _, lo_c, hi_c, act = st
        mid = (lo_c + hi_c) >> 1
        cnt = count_tied_below(mid)
        on = act > 0
        le = cnt <= need
        lo_c = jnp.where(on, jnp.where(le, mid, lo_c), lo_c)
        hi_c = jnp.where(on, jnp.where(le, hi_c, mid), hi_c)
        act = jnp.where(on, jnp.where(hi_c - lo_c > 1, 1, 0), 0)
        return jnp.max(act), lo_c, hi_c, act

    _, cut_lo, _, _ = lax.while_loop(
        bis_cond, cut_body,
        (jnp.max(tied), jnp.zeros((tq, LANES), I32), jnp.full((tq, LANES), n_tiles_total * ts + 1, I32), tied))
    cut = jnp.where(tied > 0, cut_lo, n_tiles_total * ts + 1)

    m_ref[...] = jnp.full(m_ref.shape, _M_INIT, F32)
    acc_ref[...] = jnp.zeros(acc_ref.shape, F32)
    rc = min(tq, ROW_CHUNK)
    ones = jnp.ones((ts, LANES), BF16)
    last = n_t - 1

    def logits(t, slot):
        for g in range(N_KV_HEADS):
            qg = q_ref[0, g * GQA_GROUP:(g + 1) * GQA_GROUP].reshape(GQA_GROUP * tq, HEAD_DIM)
            s_ref[slot, g * GQA_GROUP * tq:(g + 1) * GQA_GROUP * tq] = lax.dot_general(
                qg, kt_ref[0, key_rows(t), g * HEAD_DIM:(g + 1) * HEAD_DIM], nt_dims,
                preferred_element_type=F32)

    def softmax_pv(t, slot, live):
        for c in range(nc):
            col = t * ts + c * LANES + lane
            bound = jnp.where(live, jnp.where(col < cut, thr, thr_up), _NO_KEY)
            bias_ref[slot, :, c * LANES:(c + 1) * LANES] = jnp.where(
                sc_ref[t, :, c * LANES:(c + 1) * LANES] >= bound, 0.0, _NEG_INF)
        for g in range(N_KV_HEADS):
            for j in range(GQA_GROUP):
                hd = g * GQA_GROUP + j
                for r0 in range(0, tq, rc):
                    r1 = j * tq + r0
                    r2 = hd * tq + r0
                    s = [s_ref[slot, r2:r2 + rc, c * LANES:(c + 1) * LANES]
                         + bias_ref[slot, r0:r0 + rc, c * LANES:(c + 1) * LANES] for c in range(nc)]
                    mx = s[0]
                    for c in range(1, nc):
                        mx = jnp.maximum(mx, s[c])
                    m_old = m_ref[hd, r0:r0 + rc]
                    m_new = jnp.maximum(m_old, jnp.broadcast_to(jnp.max(mx, axis=-1, keepdims=True), (rc, LANES)))
                    alpha_ref[slot, g, r1:r1 + rc] = jnp.exp2(m_old - m_new)
                    for c in range(nc):
                        p_ref[slot, g, r1:r1 + rc, c * LANES:(c + 1) * LANES] = jnp.exp2(s[c] - m_new).astype(BF16)
                    m_ref[hd, r0:r0 + rc] = m_new
            v_aug = jnp.concatenate([v_ref[0, key_rows(t), g * HEAD_DIM:(g + 1) * HEAD_DIM], ones], axis=-1)
            pv = jnp.dot(p_ref[slot, g], v_aug, preferred_element_type=F32)
            alpha = alpha_ref[slot, g]
            for half in range(2):
                cols = slice(half * LANES, (half + 1) * LANES)
                acc_ref[g, :, cols] = alpha * acc_ref[g, :, cols] + pv[:, cols]

    logits(0, 0)

    def attn_pair(u, carry):
        t0 = 2 * u
        t1 = jnp.minimum(t0 + 1, last)
        logits(t1, 1)
        softmax_pv(t0, 0, True)
        logits(jnp.minimum(t0 + 2, last), 0)
        softmax_pv(t1, 1, t0 + 1 < n_t)
        return carry

    lax.fori_loop(0, (n_t + 1) // 2, attn_pair, 0)
    for hd in range(N_HEADS):
        g, j = divmod(hd, GQA_GROUP)
        o_ref[0, :, hd * HEAD_DIM:(hd + 1) * HEAD_DIM] = (
            acc_ref[g, j * tq:(j + 1) * tq, 0:HEAD_DIM] / acc_ref[g, j * tq:(j + 1) * tq, HEAD_DIM:2 * HEAD_DIM])


def _attn_t_kernel(qst_ref, wrow_ref, q_ref, ki_ref, kt_ref, v_ref, o_ref,
                   sc_ref, mm_ref, st_ref, s_ref, m_ref, acc_ref, bias_ref, p_ref, alpha_ref, kmax_ref,
                   *, tq, ts, q0, n_keys, k_sel):
    qi = pl.program_id(1)
    n_tiles_total = ki_ref.shape[1]
    nc = ts // LANES
    rk = KEY_ROWS
    qpos = q0 + qi * tq + lax.broadcasted_iota(I32, (1, tq), 1)
    n_adm = jnp.minimum((qpos // CHUNK + 1) * CHUNK, n_keys)
    last_adm = jnp.minimum(((q0 + (qi + 1) * tq - 1) // CHUNK + 1) * CHUNK, n_keys)
    n_t = jnp.minimum((last_adm + ts - 1) // ts, n_tiles_total)
    n_full = jnp.minimum(jnp.minimum(((q0 + qi * tq) // CHUNK + 1) * CHUNK, n_keys) // ts, n_t)
    last_full = jnp.maximum(n_full - 1, 0)
    qst = qst_ref[0, 0]
    wr = wrow_ref[0, 0]

    mm_ref[0] = jnp.full((rk, tq), _POS_INF, F32)
    mm_ref[1] = jnp.full((rk, tq), _NEG_INF, F32)
    mm_ref[2] = jnp.zeros((rk, tq), F32)

    def idx_logits(t, slot):
        st_ref[slot] = jnp.dot(ki_ref[0, t], qst, preferred_element_type=F32)

    def score_tile(t, slot, masked):
        mn, mx, n_pos = mm_ref[0], mm_ref[1], mm_ref[2]
        for r0 in range(0, ts, rk):
            acc = None
            for hd in range(N_IDX_HEADS):
                r = jnp.maximum(st_ref[slot, r0:r0 + rk, hd * tq:(hd + 1) * tq], 0.0) * wr[hd:hd + 1, :]
                acc = r if acc is None else acc + r
            key = _key_of(acc)
            if masked:
                adm = (t * ts + r0 + lax.broadcasted_iota(I32, (rk, tq), 0)) < n_adm
                key = jnp.where(adm, key, _INADMISSIBLE)
                mx = jnp.maximum(mx, jnp.where(adm, acc, _NEG_INF))
                mn = jnp.minimum(mn, jnp.where(adm, acc, _POS_INF))
            else:
                mx = jnp.maximum(mx, acc)
                mn = jnp.minimum(mn, acc)
            n_pos = n_pos + jnp.where(key >= _KEY_MIN_NORMAL, 1.0, 0.0)
            sc_ref[t, r0:r0 + rk, :] = key
        mm_ref[0] = mn
        mm_ref[1] = mx
        mm_ref[2] = n_pos

    idx_logits(0, 0)

    def score_pair(u, carry):
        t0 = 2 * u
        idx_logits(t0 + 1, 1)
        score_tile(t0, 0, False)
        idx_logits(jnp.minimum(t0 + 2, last_full), 0)
        score_tile(t0 + 1, 1, False)
        return carry

    lax.fori_loop(0, n_full // 2, score_pair, 0)

    @pl.when(n_full % 2 == 1)
    def _():
        score_tile(last_full, 0, False)

    def score_tail(t, carry):
        idx_logits(t, 0)
        score_tile(t, 0, True)
        return carry

    lax.fori_loop(n_full, n_t, score_tail, 0)
    rmin = jnp.min(mm_ref[0], axis=0, keepdims=True)
    rmax = jnp.max(mm_ref[1], axis=0, keepdims=True)

    def count_ge(thr):
        def body(t, cnt):
            for r0 in range(0, ts, COUNT_ROWS):
                cnt = cnt + jnp.where(sc_ref[t, r0:r0 + COUNT_ROWS, :] >= thr, 1.0, 0.0)
            return cnt
        return jnp.sum(lax.fori_loop(0, n_t, body, jnp.zeros((COUNT_ROWS, tq), F32)), axis=0, keepdims=True)

    def bis_cond(st):
        return st[0] > 0

    kf = float(k_sel)
    take_all = n_adm <= k_sel
    lo0 = _key_of(rmin) - 1
    hi0 = _key_of(rmax) + 2
    c_pos = jnp.sum(mm_ref[2], axis=0, keepdims=True)
    pos = c_pos >= kf
    lo = jnp.where(pos, _KEY_MIN_NORMAL, jnp.minimum(lo0, _KEY_NEG_ZERO - 1))
    hi = jnp.where(pos, hi0, _KEY_MIN_NORMAL)
    clo = jnp.where(pos, c_pos, n_adm.astype(F32))
    chi = jnp.where(pos, 0.0, c_pos)
    exact0 = jnp.where(clo == kf, 1, 0)
    act0 = jnp.where(take_all, 0, jnp.where(exact0 > 0, 0, jnp.where(hi > lo + 1, 1, 0)))

    def bis_step(it, lo, hi, clo, chi, exact, act):
        kmid = (lo & hi) + ((lo ^ hi) >> 1)
        vmid = _key_of((_float_of(lo) + _float_of(hi)) * 0.5)
        by_value = jnp.where(it < VALUE_PIVOT_STEPS, jnp.where(vmid > lo, jnp.where(vmid < hi, 1, 0), 0), 0)
        mid = jnp.where(by_value > 0, vmid, kmid)
        at_zero = jnp.where(lo < _KEY_NEG_ZERO, jnp.where(hi == _KEY_MIN_NORMAL, 1, 0), 0)
        mid = jnp.where(at_zero > 0, _KEY_NEG_ZERO, mid)
        cnt = count_ge(mid)
        up = jnp.where(act > 0, jnp.where(cnt >= kf, 1, 0), 0)
        dn = act - up
        lo = jnp.where(up > 0, mid, lo)
        clo = jnp.where(up > 0, cnt, clo)
        hi = jnp.where(dn > 0, mid, hi)
        chi = jnp.where(dn > 0, cnt, chi)
        hit = jnp.where(up > 0, jnp.where(cnt == kf, 1, 0), 0)
        exact = exact + hit
        act = jnp.where(act > 0, jnp.where(hit > 0, 0, jnp.where(hi > lo + 1, 1, 0)), 0)
        return it + 1, lo, hi, clo, chi, exact, act

    def any_active(act):
        return jnp.max(act.astype(F32))

    def bis_body(st):
        st = bis_step(*bis_step(*st[1:]))
        return (any_active(st[-1]),) + st

    _, _, lo, _, clo, chi, exact, _ = lax.while_loop(
        bis_cond, bis_body, (any_active(act0), jnp.int32(0), lo, hi, clo, chi, exact0, act0))
    thr = jnp.where(take_all, _INADMISSIBLE + 1, lo)
    thr_up = jnp.where(take_all, _INADMISSIBLE + 1, lo + 1)

    need = kf - chi
    tied = jnp.where(take_all, 0, jnp.where(exact > 0, 0, jnp.where(clo > kf, 1, 0)))

    def count_tied_below(cut):
        def body(t, cnt):
            for r0 in range(0, ts, COUNT_ROWS):
                kidx = t * ts + r0 + lax.broadcasted_iota(I32, (COUNT_ROWS, tq), 0)
                cnt = cnt + jnp.where(sc_ref[t, r0:r0 + COUNT_ROWS, :] == thr,
                                      jnp.where(kidx < cut, 1.0, 0.0), 0.0)
            return cnt
        return jnp.sum(lax.fori_loop(0, n_t, body, jnp.zeros((COUNT_ROWS, tq), F32)), axis=0, keepdims=True)

    def cut_body(st):
        _, lo_c, hi_c, act = st
        mid = (lo_c + hi_c) >> 1
        cnt = count_tied_below(mid)
        on = act > 0
        le = cnt <= need
        lo_c = jnp.where(on, jnp.where(le, mid, lo_c), lo_c)
        hi_c = jnp.where(on, jnp.where(le, hi_c, mid), hi_c)
        act = jnp.where(on, jnp.where(hi_c - lo_c > 1, 1, 0), 0)
        return any_active(act), lo_c, hi_c, act

    _, cut_lo, _, _ = lax.while_loop(
        bis_cond, cut_body,
        (any_active(tied), jnp.zeros((1, tq), I32), jnp.full((1, tq), n_tiles_total * ts + 1, I32), tied))
    cut = jnp.where(tied > 0, cut_lo, n_tiles_total * ts + 1)

    acc_ref[...] = jnp.zeros(acc_ref.shape, F32)
    rc = min(tq, ROW_CHUNK)
    ones = jnp.ones((ts, LANES), BF16)
    last = n_t - 1

    def logits(t, slot):
        for g in range(N_KV_HEADS):
            qg = q_ref[0, g * GQA_GROUP:(g + 1) * GQA_GROUP].reshape(GQA_GROUP * tq, HEAD_DIM)
            s_ref[slot, g * GQA_GROUP * tq:(g + 1) * GQA_GROUP * tq] = jnp.dot(
                qg, kt_ref[0, g, t], preferred_element_type=F32)

    def selection_bias(t, c, live, masked_value):
        kidx = t * ts + c * LANES + lax.broadcasted_iota(I32, (LANES, tq), 0)
        bound = jnp.where(live, jnp.where(kidx < cut, thr, thr_up), _NO_KEY)
        return jnp.where(sc_ref[t, c * LANES:(c + 1) * LANES, :] >= bound, 0.0, masked_value).T

    def value_matmul(t, slot, g):
        v_aug = jnp.concatenate([v_ref[0, g, t], ones], axis=-1)
        return jnp.dot(p_ref[slot, g], v_aug, preferred_element_type=F32)

    def softmax_pv(t, slot, live):
        for c in range(nc):
            bias_ref[slot, :, c * LANES:(c + 1) * LANES] = selection_bias(t, c, live, _NEG_INF)
        for g in range(N_KV_HEADS):
            for j in range(GQA_GROUP):
                hd = g * GQA_GROUP + j
                for r0 in range(0, tq, rc):
                    r1 = j * tq + r0
                    r2 = hd * tq + r0
                    s = [s_ref[slot, r2:r2 + rc, c * LANES:(c + 1) * LANES]
                         + bias_ref[slot, r0:r0 + rc, c * LANES:(c + 1) * LANES] for c in range(nc)]
                    mx = s[0]
                    for c in range(1, nc):
                        mx = jnp.maximum(mx, s[c])
                    m_old = m_ref[hd, r0:r0 + rc]
                    m_new = jnp.maximum(m_old, jnp.broadcast_to(jnp.max(mx, axis=-1, keepdims=True), (rc, LANES)))
                    alpha_ref[slot, g, r1:r1 + rc] = jnp.exp2(m_old - m_new)
                    m_ref[hd, r0:r0 + rc] = m_new
                    for c in range(nc):
                        p_ref[slot, g, r1:r1 + rc, c * LANES:(c + 1) * LANES] = jnp.exp2(s[c] - m_new).astype(BF16)
            pv = value_matmul(t, slot, g)
            for half in range(2):
                cols = slice(half * LANES, (half + 1) * LANES)
                acc_ref[g, :, cols] = alpha_ref[slot, g] * acc_ref[g, :, cols] + pv[:, cols]

    def attend_running():
        def attn_pair(u, carry):
            t0 = 2 * u
            t1 = jnp.minimum(t0 + 1, last)
            logits(t1, 1)
            softmax_pv(t0, 0, True)
            logits(jnp.minimum(t0 + 2, last), 0)
            softmax_pv(t1, 1, t0 + 1 < n_t)
            return carry

        lax.fori_loop(0, (n_t + 1) // 2, attn_pair, 0)

    def softmax_fixed(t, slot, live):
        for c in range(nc):
            bias_ref[slot, :, c * LANES:(c + 1) * LANES] = selection_bias(t, c, live, _NEG_INF)
        for g in range(N_KV_HEADS):
            for j in range(GQA_GROUP):
                hd = g * GQA_GROUP + j
                for r0 in range(0, tq, rc):
                    ref_pt = m_ref[hd, r0:r0 + rc]
                    for c in range(nc):
                        cs = slice(c * LANES, (c + 1) * LANES)
                        s = s_ref[slot, hd * tq + r0:hd * tq + r0 + rc, cs] + bias_ref[slot, r0:r0 + rc, cs]
                        p_ref[slot, g, j * tq + r0:j * tq + r0 + rc, cs] = jnp.exp2(s - ref_pt).astype(BF16)

    def attend_fixed():
        def accumulate(g, pv):
            for half in range(2):
                cols = slice(half * LANES, (half + 1) * LANES)
                acc_ref[g, :, cols] = acc_ref[g, :, cols] + pv[:, cols]

        def attn_pair(u, carry):
            t0 = 2 * u
            logits(t0 + 1, 1)
            softmax_fixed(t0, 0, True)
            logits(jnp.minimum(t0 + 2, last), 0)
            softmax_fixed(t0 + 1, 1, True)
            for g in range(N_KV_HEADS):
                accumulate(g, value_matmul(t0, 0, g) + value_matmul(t0 + 1, 1, g))
            return carry

        lax.fori_loop(0, n_t // 2, attn_pair, 0)

        @pl.when(n_t % 2 == 1)
        def _():
            softmax_fixed(last, 0, True)
            for g in range(N_KV_HEADS):
                accumulate(g, value_matmul(last, 0, g))

    @pl.when(qi == 0)
    def _():
        for g in range(N_KV_HEADS):
            def sq_norm_max(t, best):
                kk = kt_ref[0, g, t].astype(F32)
                return jnp.maximum(best, jnp.sum(kk * kk, axis=0, keepdims=True))
            best = lax.fori_loop(0, n_tiles_total, sq_norm_max, jnp.zeros((1, ts), F32))
            kmax_ref[g] = jnp.broadcast_to(jnp.max(best, axis=-1, keepdims=True), (8, LANES))

    logits(0, 0)
    top = jnp.zeros((tq, LANES), F32)
    for hd in range(N_HEADS):
        qf = q_ref[0, hd].astype(F32)
        q_sq = _row_sum(qf * qf)
        ref_pt = jnp.sqrt(q_sq * kmax_ref[hd // GQA_GROUP, 0:1, :]) * 1.001 + 1e-3
        m_ref[hd] = ref_pt
        top = jnp.maximum(top, ref_pt)
    fits = jnp.max(top) <= MAX_FIXED_REFERENCE

    @pl.when(fits)
    def _():
        attend_fixed()

    @pl.when(jnp.logical_not(fits))
    def _():
        m_ref[...] = jnp.full(m_ref.shape, _M_INIT, F32)
        attend_running()

    for hd in range(N_HEADS):
        g, j = divmod(hd, GQA_GROUP)
        o_ref[0, :, hd * HEAD_DIM:(hd + 1) * HEAD_DIM] = (
            acc_ref[g, j * tq:(j + 1) * tq, 0:HEAD_DIM] / acc_ref[g, j * tq:(j + 1) * tq, HEAD_DIM:2 * HEAD_DIM])


def _attend_t(qst, wrow, q_att, ki4, kt, v, q0, n_keys, tq, ts):
    B, _, T, _ = q_att.shape
    n_tiles = ki4.shape[1]
    k_sel = min(TOPK_MAX, n_keys // 4)
    kern = functools.partial(_attn_t_kernel, tq=tq, ts=ts, q0=q0, n_keys=n_keys, k_sel=k_sel)
    return pl.pallas_call(
        kern, grid=(B, T // tq),
        in_specs=[
            pl.BlockSpec((1, 1, 4 * IDX_DIM, N_IDX_HEADS * tq), lambda b, i: (b, i, 0, 0)),
            pl.BlockSpec((1, 1, N_IDX_HEADS, tq), lambda b, i: (b, i, 0, 0)),
            pl.BlockSpec((1, N_HEADS, tq, HEAD_DIM), lambda b, i: (b, 0, i, 0)),
            _resident((1, n_tiles, ts, 4 * IDX_DIM), lambda b, i: (b, 0, 0, 0)),
            _resident((1, N_KV_HEADS, n_tiles, HEAD_DIM, ts), lambda b, i: (b, 0, 0, 0, 0)),
            _resident((1, N_KV_HEADS, n_tiles, ts, HEAD_DIM), lambda b, i: (b, 0, 0, 0, 0)),
        ],
        out_specs=pl.BlockSpec((1, tq, N_HEADS * HEAD_DIM), lambda b, i: (b, i, 0)),
        out_shape=jax.ShapeDtypeStruct((B, T, N_HEADS * HEAD_DIM), F32),
        scratch_shapes=[
            pltpu.VMEM((n_tiles, ts, tq), I32),
            pltpu.VMEM((3, KEY_ROWS, tq), F32),
            pltpu.VMEM((2, ts, N_IDX_HEADS * tq), F32),
            pltpu.VMEM((2, N_HEADS * tq, ts), F32),
            pltpu.VMEM((N_HEADS, tq, LANES), F32),
            pltpu.VMEM((N_KV_HEADS, GQA_GROUP * tq, 2 * HEAD_DIM), F32),
            pltpu.VMEM((2, tq, ts), F32),
            pltpu.VMEM((2, N_KV_HEADS, GQA_GROUP * tq, ts), BF16),
            pltpu.VMEM((2, N_KV_HEADS, GQA_GROUP * tq, LANES), F32),
            pltpu.VMEM((N_KV_HEADS, 8, LANES), F32),
        ],
        name="attend_t",
        compiler_params=pltpu.CompilerParams(dimension_semantics=("arbitrary", "arbitrary"),
                                             vmem_limit_bytes=VMEM_LIMIT),
    )(qst, wrow, q_att, ki4, kt, v)


def _attend(qs, kiw, q_att, ki4, kt, v, q0, n_keys, tq, ts):
    B, _, T, _ = qs.shape
    lp = ki4.shape[1]
    n_tiles = lp // ts
    k_sel = min(TOPK_MAX, n_keys // 4)
    kern = functools.partial(_attn_kernel, tq=tq, ts=ts, q0=q0, n_keys=n_keys, k_sel=k_sel)
    keys = pl.BlockSpec((1, lp, 2 * LANES), lambda b, i: (b, 0, 0))
    return pl.pallas_call(
        kern, grid=(B, T // tq),
        in_specs=[
            pl.BlockSpec((1, N_IDX_HEADS, tq, 4 * IDX_DIM), lambda b, i: (b, 0, i, 0)),
            pl.BlockSpec((1, tq, LANES), lambda b, i: (b, i, 0)),
            pl.BlockSpec((1, N_HEADS, tq, HEAD_DIM), lambda b, i: (b, 0, i, 0)),
            keys, keys, keys,
        ],
        out_specs=pl.BlockSpec((1, tq, N_HEADS * HEAD_DIM), lambda b, i: (b, i, 0)),
        out_shape=jax.ShapeDtypeStruct((B, T, N_HEADS * HEAD_DIM), F32),
        scratch_shapes=[
            pltpu.VMEM((n_tiles, tq, ts), I32),
            pltpu.VMEM((2, tq, LANES), F32),
            pltpu.VMEM((N_IDX_HEADS, tq, LANES), F32),
            pltpu.VMEM((2, N_HEADS * tq, ts), F32),
            pltpu.VMEM((N_HEADS, tq, LANES), F32),
            pltpu.VMEM((N_KV_HEADS, GQA_GROUP * tq, 2 * HEAD_DIM), F32),
            pltpu.VMEM((2, tq, ts), F32),
            pltpu.VMEM((2, N_KV_HEADS, GQA_GROUP * tq, ts), BF16),
            pltpu.VMEM((2, N_KV_HEADS, GQA_GROUP * tq, LANES), F32),
        ],
        name="attend",
        compiler_params=pltpu.CompilerParams(dimension_semantics=("arbitrary", "arbitrary"),
                                             vmem_limit_bytes=VMEM_LIMIT),
    )(qs, kiw, q_att, ki4, kt, v)


def _rms(x, g):
    return x * lax.rsqrt(jnp.mean(x * x, axis=-1, keepdims=True) + RMS_EPS) * g


def _out_kernel(x_ref, pa_ref, sgb_ref, b_ref, wo_ref, g1_ref, g2_ref, wgu_ref, wd_ref, g3_ref, y_ref, *, d_ff):
    merged = pa_ref[...] + sgb_ref[...] * b_ref[...]
    mix = jnp.dot(merged.astype(BF16), wo_ref[...], preferred_element_type=F32)
    x1 = x_ref[...] + _rms(mix, g1_ref[...])
    h2 = _rms(x1, g2_ref[...]).astype(BF16)
    gu = jnp.dot(h2, wgu_ref[...], preferred_element_type=F32)
    gate, up = gu[:, :d_ff], gu[:, d_ff:]
    act = (gate * _sigmoid(gate) * up).astype(BF16)
    f = jnp.dot(act, wd_ref[...], preferred_element_type=F32)
    y_ref[...] = x1 + _rms(f, g3_ref[...])


def _finish(x, pa, sgb, b, w_out, g_post, g_ffn_pre, w_gate_up, w_down, g_ffn_post, tm):
    N, D = x.shape
    d_ff = w_down.shape[0]
    row = pl.BlockSpec((tm, D), lambda i: (i, 0))
    const = lambda i: (0, 0)
    return pl.pallas_call(
        functools.partial(_out_kernel, d_ff=d_ff), grid=(N // tm,),
        in_specs=[row, row, row, row,
                  _resident((D, D), const), _resident((1, D), const), _resident((1, D), const),
                  _resident((D, 2 * d_ff), const), _resident((d_ff, D), const), _resident((1, D), const)],
        out_specs=row, out_shape=jax.ShapeDtypeStruct((N, D), F32), name="finish",
        compiler_params=pltpu.CompilerParams(dimension_semantics=("arbitrary",), vmem_limit_bytes=VMEM_LIMIT),
    )(x, pa, sgb, b, w_out, g_post, g_ffn_pre, w_gate_up, w_down, g_ffn_post)


def _tile_keys(n_keys, ts):
    return -(-n_keys // ts) * ts


def _layer(x, pos0, hist, k_cache, v_cache, ki_cache, wts, tm, tq, ts, tm_out):
    (w_in_r, w_pool, pool_scale, w_out, w_gate_up, w_down, g_pre, g_post, g_ffn_pre, g_ffn_post) = wts
    B, T, D = x.shape
    hist16 = jnp.concatenate([jnp.zeros((B, HALO // 2 - POOL_HIST, POOL_WIDTH), F32), hist], axis=1)
    own_keys_only = k_cache is None and T % ts == 0 and ts % tm == 0 and tq == LANES and tm % LANES == 0
    outs = _project(x, hist16, pos0, w_in_r, g_pre, w_pool, pool_scale, tm, ts if own_keys_only else None)
    u, q_att, k, v, qs, kiw, pa, sgb = outs[:8]
    ki = kiw[:, :, :IDX_DIM]
    if own_keys_only:
        kt, vt, ki4, wrow = outs[8:]
        b = _attend_t(qs, wrow, q_att, ki4, kt, vt, pos0, T, tq, ts)
    else:
        if k_cache is None:
            k_all, v_all, ki_all = k, v, ki
        else:
            k_all = jnp.concatenate([k_cache.reshape(B, -1, N_KV_HEADS * HEAD_DIM), k], axis=1)
            v_all = jnp.concatenate([v_cache.reshape(B, -1, N_KV_HEADS * HEAD_DIM), v], axis=1)
            ki_all = jnp.concatenate([ki_cache, ki], axis=1)
        n_keys = k_all.shape[1]
        lp = _tile_keys(n_keys, ts)
        pad = ((0, 0), (0, lp - n_keys), (0, 0))
        kt = jnp.pad(k_all.astype(BF16), pad)
        vt = jnp.pad(v_all.astype(BF16), pad)
        ki4 = _split_keys(jnp.pad(ki_all, pad), ts)
        b = _attend(qs, kiw, q_att, ki4, kt, vt, pos0, n_keys, tq, ts)
    y = _finish(x.reshape(B * T, D), pa.reshape(B * T, D), sgb.reshape(B * T, D), b.reshape(B * T, D),
                w_out, g_post, g_ffn_pre, w_gate_up, w_down, g_ffn_post, tm_out).reshape(B, T, D)
    new_pool = jnp.concatenate([hist, u], axis=1)[:, T:]
    return (y, k.reshape(B, T, N_KV_HEADS, HEAD_DIM), v.reshape(B, T, N_KV_HEADS, HEAD_DIM), ki, new_pool)


def _relayout_w_in(w_in):
    d = w_in.shape[0]
    o_kiw = POOL_WIDTH + N_HEADS * HEAD_DIM + 2 * N_KV_HEADS * HEAD_DIM + N_IDX_HEADS * IDX_DIM
    narrow = IDX_DIM + N_IDX_HEADS
    padded = jnp.concatenate([w_in[:, :o_kiw + narrow], jnp.zeros((d, LANES - narrow), w_in.dtype),
                              w_in[:, o_kiw + narrow:]], axis=1)
    return padded.astype(BF16)


def kernel(x_prompt, x_sample, cache_k, cache_v, cache_k_idx, state_pool, w_in, w_pool, pool_scale, w_out,
           w_gate_up, w_down, norm_mix_pre, norm_mix_post, norm_ffn_pre, norm_ffn_post):
    depth = w_in.shape[0]
    past = cache_k.shape[2]
    t_p, t_s = x_prompt.shape[1], x_sample.shape[1]
    hist_p = jnp.zeros((x_prompt.shape[0], POOL_HIST, POOL_WIDTH), x_prompt.dtype)
    xp, xs = x_prompt, x_sample
    outs = [[] for _ in range(8)]
    for l in range(depth):
        wts = (_relayout_w_in(w_in[l]), w_pool[l].astype(BF16), pool_scale[l][None, :], w_out[l].astype(BF16),
               w_gate_up[l].astype(BF16), w_down[l].astype(BF16), norm_mix_pre[l][None, :],
               norm_mix_post[l][None, :], norm_ffn_pre[l][None, :], norm_ffn_post[l][None, :])
        tm_p = min(256, t_p)
        tq_p = min(128, t_p)
        xp, k1, v1, ki1, p1 = _layer(xp, 0, hist_p, None, None, None, wts, tm_p, tq_p, 512, tm_p)
        n_s = xs.shape[0] * t_s
        xs, k2, v2, ki2, p2 = _layer(xs, past, state_pool[l], cache_k[l], cache_v[l], cache_k_idx[l], wts,
                                     t_s, t_s, 512, min(256, n_s))
        for lst, val in zip(outs, (k1, v1, ki1, p1, k2, v2, ki2, p2)):
            lst.append(val)
    return (xp, xs) + tuple(jnp.stack(o) for o in outs)
```

```python
import functools

import jax
import jax.numpy as jnp
from jax import lax
from jax.experimental import pallas as pl
from jax.experimental.pallas import tpu as pltpu

F32 = jnp.float32
BF16 = jnp.bfloat16
I32 = jnp.int32

LANES = 128
CHUNK = 64
POOL_WINDOWS = (2, 4, 8, 16)
N_POOL_GROUPS = 4
POOL_GROUP_WIDTH = 128
POOL_WIDTH = N_POOL_GROUPS * POOL_GROUP_WIDTH
POOL_HIST = 15
N_HEADS = 8
N_KV_HEADS = 2
HEAD_DIM = 128
GQA_GROUP = N_HEADS // N_KV_HEADS
ROPE_THETA = 500000.0
N_IDX_HEADS = 8
IDX_DIM = 64
TOPK_MAX = 256
RMS_EPS = 1e-6
ATTN_SCALE = HEAD_DIM ** -0.5
IDX_SCALE = (N_IDX_HEADS ** -0.5) * (IDX_DIM ** -0.5)

LOG2_E = 1.4426950408889634
VALUE_PIVOT_STEPS = 28
MAX_FIXED_REFERENCE = 40.0
SPLIT_ROWS = 1536
KEY_ROWS = 16
COUNT_ROWS = 32
ROW_CHUNK = 32
HALO = 32
VMEM_LIMIT = 56 * 1024 * 1024

_NEG_INF = float("-inf")
_POS_INF = float("inf")
_INADMISSIBLE = -(2 ** 31)
_NO_KEY = 2 ** 31 - 1
_KEY_MIN_NORMAL = 0x00800000
_KEY_NEG_ZERO = -1
_M_INIT = -1e30


def _resident(block_shape, index_map):
    return pl.BlockSpec(block_shape, index_map, pipeline_mode=pl.Buffered(1))


def _rope(xs, cos, sin, half, period):
    lane = lax.broadcasted_iota(I32, xs.shape, 1)
    ahead = pltpu.roll(xs, LANES - half, 1)
    behind = pltpu.roll(xs, half, 1)
    partner = jnp.where((lane & (period - 1)) < half, ahead, behind)
    return xs * cos + partner * sin


def _sigmoid(x):
    return 1.0 / (1.0 + jnp.exp(-x))


def _proj_kernel(x_ref, g_ref, w_ref, hist_ref, cq_ref, sq_ref, ci_ref, si_ref, wpool_ref, pscale_ref,
                 u_ref, qatt_ref, k_ref, v_ref, qs_ref, kiw_ref, pa_ref, sgb_ref, *rest, tm, pos0, d_model, key_tiles):
    if key_tiles:
        kt_ref, vt_ref, ki4_ref, wrow_ref, e_ref, s2_ref, s4_ref, s8_ref = rest
    else:
        e_ref, s2_ref, s4_ref, s8_ref = rest
    i = pl.program_id(1)
    x = x_ref[0]
    h = x * lax.rsqrt(jnp.mean(x * x, axis=-1, keepdims=True) + RMS_EPS) * g_ref[...]
    o_q = POOL_WIDTH
    o_k = o_q + N_HEADS * HEAD_DIM
    o_v = o_k + N_KV_HEADS * HEAD_DIM
    o_qi = o_v + N_KV_HEADS * HEAD_DIM
    o_kiw = o_qi + N_IDX_HEADS * IDX_DIM
    o_ga = o_kiw + LANES
    o_gb = o_ga + d_model

    proj = jnp.dot(h.astype(BF16), w_ref[...], preferred_element_type=F32)

    cq, sq, ci, si = cq_ref[...], sq_ref[...], ci_ref[...], si_ref[...]
    half_q = HEAD_DIM // 8
    half_i = IDX_DIM // 8

    for hd in range(N_HEADS):
        qh = _rope(proj[:, o_q + hd * HEAD_DIM:o_q + (hd + 1) * HEAD_DIM], cq, sq, half_q, HEAD_DIM)
        qatt_ref[0, hd] = (qh * (ATTN_SCALE * LOG2_E)).astype(BF16)
    for kh in range(N_KV_HEADS):
        k_h = _rope(proj[:, o_k + kh * HEAD_DIM:o_k + (kh + 1) * HEAD_DIM], cq, sq, half_q, HEAD_DIM)
        k_ref[0, :, kh * HEAD_DIM:(kh + 1) * HEAD_DIM] = k_h
        if key_tiles:
            kt_ref[0, kh, 0] = k_h.T.astype(BF16)
            vt_ref[0, kh, 0] = proj[:, o_v + kh * HEAD_DIM:o_v + (kh + 1) * HEAD_DIM].astype(BF16)
    v_ref[0] = proj[:, o_v:o_qi]

    for pr in range(N_IDX_HEADS // 2):
        qi2 = _rope(proj[:, o_qi + pr * LANES:o_qi + (pr + 1) * LANES], ci, si, half_i, IDX_DIM)
        hi = qi2.astype(BF16).astype(F32)
        lo = qi2 - hi
        lane2 = lax.broadcasted_iota(I32, qi2.shape, 1)
        first = jnp.where(lane2 < IDX_DIM, hi, pltpu.roll(lo, IDX_DIM, 1))
        second = jnp.where(lane2 < IDX_DIM, pltpu.roll(hi, IDX_DIM, 1), lo)
        for sub, slab in enumerate((first, second)):
            hd = 2 * pr + sub
            if key_tiles:
                for sb in range(tm // LANES):
                    slab_t = slab[sb * LANES:(sb + 1) * LANES].T.astype(BF16)
                    qs_ref[0, sb, 0:LANES, hd * LANES:(hd + 1) * LANES] = slab_t
                    qs_ref[0, sb, LANES:2 * LANES, hd * LANES:(hd + 1) * LANES] = slab_t
            else:
                qs_ref[0, hd, :, 0:LANES] = slab.astype(BF16)
                qs_ref[0, hd, :, LANES:2 * LANES] = slab.astype(BF16)

    kiw = _rope(proj[:, o_kiw:o_kiw + LANES], ci, si, half_i, IDX_DIM)
    lane = lax.broadcasted_iota(I32, kiw.shape, 1)
    kiw_full = jnp.where(lane < IDX_DIM, kiw, proj[:, o_kiw:o_kiw + LANES] * IDX_SCALE)
    kiw_ref[0] = kiw_full
    if key_tiles:
        ki_hi = kiw.astype(BF16).astype(F32)
        ki_lo = kiw - ki_hi
        ki4_ref[0, 0, :, 0:LANES] = jnp.where(lane < IDX_DIM, ki_hi, pltpu.roll(ki_hi, IDX_DIM, 1)).astype(BF16)
        ki4_ref[0, 0, :, LANES:2 * LANES] = jnp.where(lane < IDX_DIM, ki_lo, pltpu.roll(ki_lo, IDX_DIM, 1)).astype(BF16)
        for sb in range(tm // LANES):
            wrow_ref[0, sb] = kiw_full[sb * LANES:(sb + 1) * LANES].T[IDX_DIM:IDX_DIM + N_IDX_HEADS]

    u = proj[:, 0:POOL_WIDTH]
    u_ref[0] = u

    @pl.when(i == 0)
    def _():
        e_ref[0:HALO // 2, :] = jnp.zeros((HALO // 2, POOL_WIDTH), F32)
        e_ref[HALO // 2:HALO, :] = hist_ref[0]

    e_ref[HALO:HALO + tm, :] = u
    n2, n4, n8 = tm + 24, tm + 16, tm + 8
    s2_ref[8:8 + n2, :] = e_ref[8:8 + n2, :] + e_ref[7:7 + n2, :]
    s4_ref[16:16 + n4, :] = s2_ref[16:16 + n4, :] + s2_ref[14:14 + n4, :]
    s8_ref[24:24 + n8, :] = s4_ref[24:24 + n8, :] + s4_ref[20:20 + n8, :]
    s16 = s8_ref[HALO:HALO + tm, :] + s8_ref[HALO - 8:HALO - 8 + tm, :]
    wins = (s2_ref[HALO:HALO + tm, :], s4_ref[HALO:HALO + tm, :], s8_ref[HALO:HALO + tm, :], s16)
    e_ref[HALO // 2:HALO, :] = e_ref[HALO // 2 + tm:HALO + tm, :]

    pos = pos0 + i * tm + lax.broadcasted_iota(I32, (tm, POOL_GROUP_WIDTH), 0)
    a_parts = []
    for g, w in enumerate(POOL_WINDOWS):
        lo_l, hi_l = g * POOL_GROUP_WIDTH, (g + 1) * POOL_GROUP_WIDTH
        cnt = jnp.minimum(pos + 1, w).astype(F32)
        pooled = wins[g][:, lo_l:hi_l] / cnt - u[:, lo_l:hi_l]
        a_parts.append(jnp.dot(pooled.astype(BF16), wpool_ref[g], preferred_element_type=F32))
    a = jnp.concatenate(a_parts, axis=-1) * pscale_ref[...]
    pa_ref[0] = _sigmoid(proj[:, o_ga:o_gb]) * a
    sgb_ref[0] = _sigmoid(proj[:, o_gb:o_gb + d_model])


def _project(x, hist16, pos0, w_in_r, g_pre, w_pool, pool_scale, tm, key_tile=None):
    B, T, D = x.shape
    W = w_in_r.shape[1]
    pos = pos0 + jnp.arange(T, dtype=I32)
    cq, sq = _rope_tables(pos, HEAD_DIM)
    ci, si = _rope_tables(pos, IDX_DIM)
    row = lambda b, i: (b, i, 0)
    tab = lambda b, i: (i, 0)
    const2 = lambda b, i: (0, 0)
    out_shape = (
        jax.ShapeDtypeStruct((B, T, POOL_WIDTH), F32),
        jax.ShapeDtypeStruct((B, N_HEADS, T, HEAD_DIM), BF16),
        jax.ShapeDtypeStruct((B, T, N_KV_HEADS * HEAD_DIM), F32),
        jax.ShapeDtypeStruct((B, T, N_KV_HEADS * HEAD_DIM), F32),
        jax.ShapeDtypeStruct((B, N_IDX_HEADS, T, 4 * IDX_DIM), BF16),
        jax.ShapeDtypeStruct((B, T, LANES), F32),
        jax.ShapeDtypeStruct((B, T, D), F32),
        jax.ShapeDtypeStruct((B, T, D), F32),
    )
    out_specs = (
        pl.BlockSpec((1, tm, POOL_WIDTH), row),
        pl.BlockSpec((1, N_HEADS, tm, HEAD_DIM), lambda b, i: (b, 0, i, 0)),
        pl.BlockSpec((1, tm, N_KV_HEADS * HEAD_DIM), row),
        pl.BlockSpec((1, tm, N_KV_HEADS * HEAD_DIM), row),
        pl.BlockSpec((1, N_IDX_HEADS, tm, 4 * IDX_DIM), lambda b, i: (b, 0, i, 0)),
        pl.BlockSpec((1, tm, LANES), row),
        pl.BlockSpec((1, tm, D), row),
        pl.BlockSpec((1, tm, D), row),
    )
    in_specs = [
        pl.BlockSpec((1, tm, D), row),
        _resident((1, D), const2),
        _resident((D, W), const2),
        pl.BlockSpec((1, HALO // 2, POOL_WIDTH), lambda b, i: (b, 0, 0)),
        pl.BlockSpec((tm, LANES), tab), pl.BlockSpec((tm, LANES), tab),
        pl.BlockSpec((tm, LANES), tab), pl.BlockSpec((tm, LANES), tab),
        _resident((N_POOL_GROUPS, POOL_GROUP_WIDTH, D // N_POOL_GROUPS), lambda b, i: (0, 0, 0)),
        _resident((1, D), const2),
    ]
    if key_tile is not None:
        assert key_tile % tm == 0 and T % key_tile == 0 and tm % LANES == 0
        per = key_tile // tm
        n_tiles = T // key_tile
        sub = tm // LANES
        out_shape = out_shape[:4] + (
            jax.ShapeDtypeStruct((B, T // LANES, 4 * IDX_DIM, N_IDX_HEADS * LANES), BF16),
        ) + out_shape[5:] + (
            jax.ShapeDtypeStruct((B, N_KV_HEADS, n_tiles, HEAD_DIM, key_tile), BF16),
            jax.ShapeDtypeStruct((B, N_KV_HEADS, n_tiles, key_tile, HEAD_DIM), BF16),
            jax.ShapeDtypeStruct((B, n_tiles, key_tile, 4 * IDX_DIM), BF16),
            jax.ShapeDtypeStruct((B, T // LANES, N_IDX_HEADS, LANES), F32),
        )
        out_specs = out_specs[:4] + (
            pl.BlockSpec((1, sub, 4 * IDX_DIM, N_IDX_HEADS * LANES), lambda b, i: (b, i, 0, 0)),
        ) + out_specs[5:] + (
            pl.BlockSpec((1, N_KV_HEADS, 1, HEAD_DIM, tm), lambda b, i: (b, 0, i // per, 0, i % per)),
            pl.BlockSpec((1, N_KV_HEADS, 1, tm, HEAD_DIM), lambda b, i: (b, 0, i // per, i % per, 0)),
            pl.BlockSpec((1, 1, tm, 4 * IDX_DIM), lambda b, i: (b, i // per, i % per, 0)),
            pl.BlockSpec((1, sub, N_IDX_HEADS, LANES), lambda b, i: (b, i, 0, 0)),
        )
    scratch = [pltpu.VMEM((HALO + tm, POOL_WIDTH), F32) for _ in range(4)]
    return pl.pallas_call(
        functools.partial(_proj_kernel, tm=tm, pos0=pos0, d_model=D, key_tiles=key_tile is not None),
        grid=(B, T // tm), in_specs=in_specs, out_specs=out_specs, out_shape=out_shape,
        scratch_shapes=scratch, name="proj",
        compiler_params=pltpu.CompilerParams(dimension_semantics=("arbitrary", "arbitrary"),
                                             vmem_limit_bytes=VMEM_LIMIT),
    )(x, g_pre, w_in_r, hist16, cq, sq, ci, si, w_pool, pool_scale)


def _rope_tables(pos, dim):
    rot = dim // 4
    half = rot // 2
    inv = ROPE_THETA ** (-jnp.arange(half, dtype=F32) / half)
    ang = pos.astype(F32)[:, None] * inv[None, :]
    cos, sin = jnp.cos(ang), jnp.sin(ang)
    rest = dim - rot
    n = pos.shape[0]
    c = jnp.concatenate([cos, cos, jnp.ones((n, rest), F32)], axis=-1)
    s = jnp.concatenate([-sin, sin, jnp.zeros((n, rest), F32)], axis=-1)
    return jnp.tile(c, (1, LANES // dim)), jnp.tile(s, (1, LANES // dim))


def _split_kernel(k_ref, out_ref):
    k = k_ref[0]
    hi = k.astype(BF16).astype(F32)
    lo = k - hi
    out_ref[0, :, 0:LANES] = jnp.concatenate([hi, hi], axis=-1).astype(BF16)
    out_ref[0, :, LANES:2 * LANES] = jnp.concatenate([lo, lo], axis=-1).astype(BF16)


def _split_keys(ki, ts):
    B, Lp, _ = ki.shape
    rows = ts * max(1, SPLIT_ROWS // ts)
    rows = rows if Lp % rows == 0 else ts
    return pl.pallas_call(
        _split_kernel, grid=(B, Lp // rows),
        in_specs=[pl.BlockSpec((1, rows, IDX_DIM), lambda b, t: (b, t, 0))],
        out_specs=pl.BlockSpec((1, rows, 4 * IDX_DIM), lambda b, t: (b, t, 0)),
        out_shape=jax.ShapeDtypeStruct((B, Lp, 4 * IDX_DIM), BF16), name="split_keys",
        compiler_params=pltpu.CompilerParams(dimension_semantics=("arbitrary", "arbitrary")),
    )(ki)


def _key_of(x):
    b = pltpu.bitcast(x, I32)
    return b ^ ((b >> 31) & 0x7FFFFFFF)


def _float_of(k):
    return pltpu.bitcast(k ^ ((k >> 31) & 0x7FFFFFFF), F32)


def _row_sum(x):
    return jnp.broadcast_to(jnp.sum(x, axis=-1, keepdims=True), x.shape)


def _attn_kernel(qs_ref, kiw_ref, q_ref, ki_ref, kt_ref, v_ref, o_ref,
                 sc_ref, mm_ref, wb_ref, s_ref, m_ref, acc_ref, bias_ref, p_ref, alpha_ref,
                 *, tq, ts, q0, n_keys, k_sel):
    qi = pl.program_id(1)
    n_tiles_total = ki_ref.shape[1] // ts
    nc = ts // LANES
    nt_dims = (((1,), (1,)), ((), ()))

    def key_rows(t):
        return pl.ds(pl.multiple_of(t * ts, ts), ts)

    row = lax.broadcasted_iota(I32, (tq, LANES), 0)
    lane = lax.broadcasted_iota(I32, (tq, LANES), 1)
    qpos = q0 + qi * tq + row
    n_adm = jnp.minimum((qpos // CHUNK + 1) * CHUNK, n_keys)
    last_adm = jnp.minimum(((q0 + (qi + 1) * tq - 1) // CHUNK + 1) * CHUNK, n_keys)
    n_t = jnp.minimum((last_adm + ts - 1) // ts, n_tiles_total)

    kiw = kiw_ref[0]
    for hd in range(N_IDX_HEADS):
        wb_ref[hd] = jnp.broadcast_to(kiw[:, IDX_DIM + hd:IDX_DIM + hd + 1], (tq, LANES))
    qs2 = qs_ref[0].reshape(N_IDX_HEADS * tq, 4 * IDX_DIM)

    def idx_logits(t, slot):
        s_ref[slot] = lax.dot_general(qs2, ki_ref[0, key_rows(t), :], nt_dims,
                                      preferred_element_type=F32)

    rc = min(tq, ROW_CHUNK)
    mm_ref[0] = jnp.full((tq, LANES), _POS_INF, F32)
    mm_ref[1] = jnp.full((tq, LANES), _NEG_INF, F32)

    def score_tile(t, slot, masked):
        for r0 in range(0, tq, rc):
            rows = slice(r0, r0 + rc)
            mn, mx = mm_ref[0, rows], mm_ref[1, rows]
            for c in range(nc):
                cs = slice(c * LANES, (c + 1) * LANES)
                acc = None
                for hd in range(N_IDX_HEADS):
                    r = jnp.maximum(s_ref[slot, hd * tq + r0:hd * tq + r0 + rc, cs], 0.0) * wb_ref[hd, rows]
                    acc = r if acc is None else acc + r
                key = _key_of(acc)
                if masked:
                    qpos_c = q0 + qi * tq + r0 + lax.broadcasted_iota(I32, (rc, LANES), 0)
                    n_adm_c = jnp.minimum((qpos_c // CHUNK + 1) * CHUNK, n_keys)
                    adm = (t * ts + c * LANES + lax.broadcasted_iota(I32, (rc, LANES), 1)) < n_adm_c
                    key = jnp.where(adm, key, _INADMISSIBLE)
                    mx = jnp.maximum(mx, jnp.where(adm, acc, _NEG_INF))
                    mn = jnp.minimum(mn, jnp.where(adm, acc, _POS_INF))
                else:
                    mx = jnp.maximum(mx, acc)
                    mn = jnp.minimum(mn, acc)
                sc_ref[t, rows, cs] = key
            mm_ref[0, rows] = mn
            mm_ref[1, rows] = mx

    n_full = jnp.minimum(jnp.minimum(((q0 + qi * tq) // CHUNK + 1) * CHUNK, n_keys) // ts, n_t)
    last_full = jnp.maximum(n_full - 1, 0)
    idx_logits(0, 0)

    def score_pair(u, carry):
        t0 = 2 * u
        t1 = jnp.minimum(t0 + 1, last_full)
        idx_logits(t1, 1)
        score_tile(t0, 0, False)
        idx_logits(jnp.minimum(t0 + 2, last_full), 0)
        score_tile(t1, 1, False)
        return carry

    lax.fori_loop(0, (n_full + 1) // 2, score_pair, 0)

    def score_tail(t, carry):
        idx_logits(t, 0)
        score_tile(t, 0, True)
        return carry

    lax.fori_loop(n_full, n_t, score_tail, 0)
    rmin = jnp.broadcast_to(jnp.min(mm_ref[0], axis=-1, keepdims=True), (tq, LANES))
    rmax = jnp.broadcast_to(jnp.max(mm_ref[1], axis=-1, keepdims=True), (tq, LANES))

    def count_ge(thr):
        def body(t, cnt):
            for c in range(nc):
                cnt = cnt + jnp.where(sc_ref[t, :, c * LANES:(c + 1) * LANES] >= thr, 1.0, 0.0)
            return cnt
        return _row_sum(lax.fori_loop(0, n_t, body, jnp.zeros((tq, LANES), F32)))

    def bis_cond(st):
        return st[0] > 0

    kf = float(k_sel)
    take_all = n_adm <= k_sel
    lo0 = _key_of(rmin) - 1
    hi0 = _key_of(rmax) + 2
    c_pos = count_ge(jnp.full((tq, LANES), _KEY_MIN_NORMAL, I32))
    c_nn = count_ge(jnp.full((tq, LANES), _KEY_NEG_ZERO, I32))
    pos = c_pos >= kf
    neg = c_nn < kf
    lo = jnp.where(pos, _KEY_MIN_NORMAL, jnp.where(neg, lo0, _KEY_NEG_ZERO))
    hi = jnp.where(pos, hi0, jnp.where(neg, _KEY_NEG_ZERO, _KEY_MIN_NORMAL))
    clo = jnp.where(pos, c_pos, jnp.where(neg, n_adm.astype(F32), c_nn))
    chi = jnp.where(pos, 0.0, jnp.where(neg, c_nn, c_pos))
    exact0 = jnp.where(clo == kf, 1, 0)
    act0 = jnp.where(take_all, 0, jnp.where(exact0 > 0, 0, jnp.where(hi > lo + 1, 1, 0)))

    def bis_step(it, lo, hi, clo, chi, exact, act):
        kmid = (lo & hi) + ((lo ^ hi) >> 1)
        vmid = _key_of((_float_of(lo) + _float_of(hi)) * 0.5)
        by_value = jnp.where(it < VALUE_PIVOT_STEPS, jnp.where(vmid > lo, jnp.where(vmid < hi, 1, 0), 0), 0)
        mid = jnp.where(by_value > 0, vmid, kmid)
        cnt = count_ge(mid)
        up = jnp.where(act > 0, jnp.where(cnt >= kf, 1, 0), 0)
        dn = act - up
        lo = jnp.where(up > 0, mid, lo)
        clo = jnp.where(up > 0, cnt, clo)
        hi = jnp.where(dn > 0, mid, hi)
        chi = jnp.where(dn > 0, cnt, chi)
        hit = jnp.where(up > 0, jnp.where(cnt == kf, 1, 0), 0)
        exact = exact + hit
        act = jnp.where(act > 0, jnp.where(hit > 0, 0, jnp.where(hi > lo + 1, 1, 0)), 0)
        return it + 1, lo, hi, clo, chi, exact, act

    def any_active(act):
        return jnp.max(act.astype(F32))

    def bis_body(st):
        st = bis_step(*bis_step(*st[1:]))
        return (any_active(st[-1]),) + st

    _, _, lo, _, clo, chi, exact, _ = lax.while_loop(
        bis_cond, bis_body, (any_active(act0), jnp.int32(0), lo, hi, clo, chi, exact0, act0))
    thr = jnp.where(take_all, _INADMISSIBLE + 1, lo)
    thr_up = jnp.where(take_all, _INADMISSIBLE + 1, lo + 1)

    need = kf - chi
    tied = jnp.where(take_all, 0, jnp.where(exact > 0, 0, jnp.where(clo > kf, 1, 0)))

    def count_tied_below(cut):
        def body(t, cnt):
            for c in range(nc):
                s = sc_ref[t, :, c * LANES:(c + 1) * LANES]
                col = t * ts + c * LANES + lane
                cnt = cnt + jnp.where(s == thr, jnp.where(col < cut, 1.0, 0.0), 0.0)
            return cnt
        return _row_sum(lax.fori_loop(0, n_t, body, jnp.zeros((tq, LANES), F32)))

    def cut_body(st):
        _, lo_c, hi_c, act = st
        mid = (lo_c + hi_c) >> 1
        cnt = count_tied_below(mid)
        on = act > 0
        le = cnt <= need
        lo_c = jnp.where(on, jnp.where(le, mid, lo_c), lo_c)
        hi_c = jnp.where(on, jnp.where(le, hi_c, mid), hi_c)
        act = jnp.where(on, jnp.where(hi_c - lo_c > 1, 1, 0), 0)
        return jnp.max(act), lo_c, hi_c, act

    _, cut_lo, _, _ = lax.while_loop(
        bis_cond, cut_body,
        (jnp.max(tied), jnp.zeros((tq, LANES), I32), jnp.full((tq, LANES), n_tiles_total * ts + 1, I32), tied))
    cut = jnp.where(tied > 0, cut_lo, n_tiles_total * ts + 1)

    m_ref[...] = jnp.full(m_ref.shape, _M_INIT, F32)
    acc_ref[...] = jnp.zeros(acc_ref.shape, F32)
    rc = min(tq, ROW_CHUNK)
    ones = jnp.ones((ts, LANES), BF16)
    last = n_t - 1

    def logits(t, slot):
        for g in range(N_KV_HEADS):
            qg = q_ref[0, g * GQA_GROUP:(g + 1) * GQA_GROUP].reshape(GQA_GROUP * tq, HEAD_DIM)
            s_ref[slot, g * GQA_GROUP * tq:(g + 1) * GQA_GROUP * tq] = lax.dot_general(
                qg, kt_ref[0, key_rows(t), g * HEAD_DIM:(g + 1) * HEAD_DIM], nt_dims,
                preferred_element_type=F32)

    def softmax_pv(t, slot, live):
        for c in range(nc):
            col = t * ts + c * LANES + lane
            bound = jnp.where(live, jnp.where(col < cut, thr, thr_up), _NO_KEY)
            bias_ref[slot, :, c * LANES:(c + 1) * LANES] = jnp.where(
                sc_ref[t, :, c * LANES:(c + 1) * LANES] >= bound, 0.0, _NEG_INF)
        for g in range(N_KV_HEADS):
            for j in range(GQA_GROUP):
                hd = g * GQA_GROUP + j
                for r0 in range(0, tq, rc):
                    r1 = j * tq + r0
                    r2 = hd * tq + r0
                    s = [s_ref[slot, r2:r2 + rc, c * LANES:(c + 1) * LANES]
                         + bias_ref[slot, r0:r0 + rc, c * LANES:(c + 1) * LANES] for c in range(nc)]
                    mx = s[0]
                    for c in range(1, nc):
                        mx = jnp.maximum(mx, s[c])
                    m_old = m_ref[hd, r0:r0 + rc]
                    m_new = jnp.maximum(m_old, jnp.broadcast_to(jnp.max(mx, axis=-1, keepdims=True), (rc, LANES)))
                    alpha_ref[slot, g, r1:r1 + rc] = jnp.exp2(m_old - m_new)
                    for c in range(nc):
                        p_ref[slot, g, r1:r1 + rc, c * LANES:(c + 1) * LANES] = jnp.exp2(s[c] - m_new).astype(BF16)
                    m_ref[hd, r0:r0 + rc] = m_new
            v_aug = jnp.concatenate([v_ref[0, key_rows(t), g * HEAD_DIM:(g + 1) * HEAD_DIM], ones], axis=-1)
            pv = jnp.dot(p_ref[slot, g], v_aug, preferred_element_type=F32)
            alpha = alpha_ref[slot, g]
            for half in range(2):
                cols = slice(half * LANES, (half + 1) * LANES)
                acc_ref[g, :, cols] = alpha * acc_ref[g, :, cols] + pv[:, cols]

    logits(0, 0)

    def attn_pair(u, carry):
        t0 = 2 * u
        t1 = jnp.minimum(t0 + 1, last)
        logits(t1, 1)
        softmax_pv(t0, 0, True)
        logits(jnp.minimum(t0 + 2, last), 0)
        softmax_pv(t1, 1, t0 + 1 < n_t)
        return carry

    lax.fori_loop(0, (n_t + 1) // 2, attn_pair, 0)
    for hd in range(N_HEADS):
        g, j = divmod(hd, GQA_GROUP)
        o_ref[0, :, hd * HEAD_DIM:(hd + 1) * HEAD_DIM] = (
            acc_ref[g, j * tq:(j + 1) * tq, 0:HEAD_DIM] / acc_ref[g, j * tq:(j + 1) * tq, HEAD_DIM:2 * HEAD_DIM])


def _attn_t_kernel(qst_ref, wrow_ref, q_ref, ki_ref, kt_ref, v_ref, o_ref,
                   sc_ref, mm_ref, st_ref, s_ref, m_ref, acc_ref, bias_ref, p_ref, alpha_ref, kmax_ref,
                   *, tq, ts, q0, n_keys, k_sel):
    qi = pl.program_id(1)
    n_tiles_total = ki_ref.shape[1]
    nc = ts // LANES
    rk = KEY_ROWS
    qpos = q0 + qi * tq + lax.broadcasted_iota(I32, (1, tq), 1)
    n_adm = jnp.minimum((qpos // CHUNK + 1) * CHUNK, n_keys)
    last_adm = jnp.minimum(((q0 + (qi + 1) * tq - 1) // CHUNK + 1) * CHUNK, n_keys)
    n_t = jnp.minimum((last_adm + ts - 1) // ts, n_tiles_total)
    n_full = jnp.minimum(jnp.minimum(((q0 + qi * tq) // CHUNK + 1) * CHUNK, n_keys) // ts, n_t)
    last_full = jnp.maximum(n_full - 1, 0)
    qst = qst_ref[0, 0]
    wr = wrow_ref[0, 0]

    mm_ref[0] = jnp.full((rk, tq), _POS_INF, F32)
    mm_ref[1] = jnp.full((rk, tq), _NEG_INF, F32)
    mm_ref[2] = jnp.zeros((rk, tq), F32)

    def idx_logits(t, slot):
        st_ref[slot] = jnp.dot(ki_ref[0, t], qst, preferred_element_type=F32)

    def score_tile(t, slot, masked):
        mn, mx, n_pos = mm_ref[0], mm_ref[1], mm_ref[2]
        for r0 in range(0, ts, rk):
            acc = None
            for hd in range(N_IDX_HEADS):
                r = jnp.maximum(st_ref[slot, r0:r0 + rk, hd * tq:(hd + 1) * tq], 0.0) * wr[hd:hd + 1, :]
                acc = r if acc is None else acc + r
            key = _key_of(acc)
            if masked:
                adm = (t * ts + r0 + lax.broadcasted_iota(I32, (rk, tq), 0)) < n_adm
                key = jnp.where(adm, key, _INADMISSIBLE)
                mx = jnp.maximum(mx, jnp.where(adm, acc, _NEG_INF))
                mn = jnp.minimum(mn, jnp.where(adm, acc, _POS_INF))
            else:
                mx = jnp.maximum(mx, acc)
                mn = jnp.minimum(mn, acc)
            n_pos = n_pos + jnp.where(key >= _KEY_MIN_NORMAL, 1.0, 0.0)
            sc_ref[t, r0:r0 + rk, :] = key
        mm_ref[0] = mn
        mm_ref[1] = mx
        mm_ref[2] = n_pos

    idx_logits(0, 0)

    def score_pair(u, carry):
        t0 = 2 * u
        idx_logits(t0 + 1, 1)
        score_tile(t0, 0, False)
        idx_logits(jnp.minimum(t0 + 2, last_full), 0)
        score_tile(t0 + 1, 1, False)
        return carry

    lax.fori_loop(0, n_full // 2, score_pair, 0)

    @pl.when(n_full % 2 == 1)
    def _():
        score_tile(last_full, 0, False)

    def score_tail(t, carry):
        idx_logits(t, 0)
        score_tile(t, 0, True)
        return carry

    lax.fori_loop(n_full, n_t, score_tail, 0)
    rmin = jnp.min(mm_ref[0], axis=0, keepdims=True)
    rmax = jnp.max(mm_ref[1], axis=0, keepdims=True)

    def count_ge(thr):
        def body(t, cnt):
            for r0 in range(0, ts, COUNT_ROWS):
                cnt = cnt + jnp.where(sc_ref[t, r0:r0 + COUNT_ROWS, :] >= thr, 1.0, 0.0)
            return cnt
        return jnp.sum(lax.fori_loop(0, n_t, body, jnp.zeros((COUNT_ROWS, tq), F32)), axis=0, keepdims=True)

    def bis_cond(st):
        return st[0] > 0

    kf = float(k_sel)
    take_all = n_adm <= k_sel
    lo0 = _key_of(rmin) - 1
    hi0 = _key_of(rmax) + 2
    c_pos = jnp.sum(mm_ref[2], axis=0, keepdims=True)
    pos = c_pos >= kf
    lo = jnp.where(pos, _KEY_MIN_NORMAL, jnp.minimum(lo0, _KEY_NEG_ZERO - 1))
    hi = jnp.where(pos, hi0, _KEY_MIN_NORMAL)
    clo = jnp.where(pos, c_pos, n_adm.astype(F32))
    chi = jnp.where(pos, 0.0, c_pos)
    exact0 = jnp.where(clo == kf, 1, 0)
    act0 = jnp.where(take_all, 0, jnp.where(exact0 > 0, 0, jnp.where(hi > lo + 1, 1, 0)))

    def bis_step(it, lo, hi, clo, chi, exact, act):
        kmid = (lo & hi) + ((lo ^ hi) >> 1)
        vmid = _key_of((_float_of(lo) + _float_of(hi)) * 0.5)
        by_value = jnp.where(it < VALUE_PIVOT_STEPS, jnp.where(vmid > lo, jnp.where(vmid < hi, 1, 0), 0), 0)
        mid = jnp.where(by_value > 0, vmid, kmid)
        at_zero = jnp.where(lo < _KEY_NEG_ZERO, jnp.where(hi == _KEY_MIN_NORMAL, 1, 0), 0)
        mid = jnp.where(at_zero > 0, _KEY_NEG_ZERO, mid)
        cnt = count_ge(mid)
        up = jnp.where(act > 0, jnp.where(cnt >= kf, 1, 0), 0)
        dn = act - up
        lo = jnp.where(up > 0, mid, lo)
        clo = jnp.where(up > 0, cnt, clo)
        hi = jnp.where(dn > 0, mid, hi)
        chi = jnp.where(dn > 0, cnt, chi)
        hit = jnp.where(up > 0, jnp.where(cnt == kf, 1, 0), 0)
        exact = exact + hit
        act = jnp.where(act > 0, jnp.where(hit > 0, 0, jnp.where(hi > lo + 1, 1, 0)), 0)
        return it + 1, lo, hi, clo, chi, exact, act

    def any_active(act):
        return jnp.max(act.astype(F32))

    def bis_body(st):
        st = bis_step(*bis_step(*st[1:]))
        return (any_active(st[-1]),) + st

    _, _, lo, _, clo, chi, exact, _ = lax.while_loop(
        bis_cond, bis_body, (any_active(act0), jnp.int32(0), lo, hi, clo, chi, exact0, act0))
    thr = jnp.where(take_all, _INADMISSIBLE + 1, lo)
    thr_up = jnp.where(take_all, _INADMISSIBLE + 1, lo + 1)

    need = kf - chi
    tied = jnp.where(take_all, 0, jnp.where(exact > 0, 0, jnp.where(clo > kf, 1, 0)))

    def count_tied_below(cut):
        def body(t, cnt):
            for r0 in range(0, ts, COUNT_ROWS):
                kidx = t * ts + r0 + lax.broadcasted_iota(I32, (COUNT_ROWS, tq), 0)
                cnt = cnt + jnp.where(sc_ref[t, r0:r0 + COUNT_ROWS, :] == thr,
                                      jnp.where(kidx < cut, 1.0, 0.0), 0.0)
            return cnt
        return jnp.sum(lax.fori_loop(0, n_t, body, jnp.zeros((COUNT_ROWS, tq), F32)), axis=0, keepdims=True)

    def cut_body(st):
        _, lo_c, hi_c, act = st
        mid = (lo_c + hi_c) >> 1
        cnt = count_tied_below(mid)
        on = act > 0
        le = cnt <= need
        lo_c = jnp.where(on, jnp.where(le, mid, lo_c), lo_c)
        hi_c = jnp.where(on, jnp.where(le, hi_c, mid), hi_c)
        act = jnp.where(on, jnp.where(hi_c - lo_c > 1, 1, 0), 0)
        return any_active(act), lo_c, hi_c, act

    _, cut_lo, _, _ = lax.while_loop(
        bis_cond, cut_body,
        (any_active(tied), jnp.zeros((1, tq), I32), jnp.full((1, tq), n_tiles_total * ts + 1, I32), tied))
    cut = jnp.where(tied > 0, cut_lo, n_tiles_total * ts + 1)

    acc_ref[...] = jnp.zeros(acc_ref.shape, F32)
    rc = min(tq, ROW_CHUNK)
    ones = jnp.ones((ts, LANES), BF16)
    last = n_t - 1

    def logits(t, slot):
        for g in range(N_KV_HEADS):
            qg = q_ref[0, g * GQA_GROUP:(g + 1) * GQA_GROUP].reshape(GQA_GROUP * tq, HEAD_DIM)
            s_ref[slot, g * GQA_GROUP * tq:(g + 1) * GQA_GROUP * tq] = jnp.dot(
                qg, kt_ref[0, g, t], preferred_element_type=F32)

    def selection_bias(t, c, live, masked_value):
        kidx = t * ts + c * LANES + lax.broadcasted_iota(I32, (LANES, tq), 0)
        bound = jnp.where(live, jnp.where(kidx < cut, thr, thr_up), _NO_KEY)
        return jnp.where(sc_ref[t, c * LANES:(c + 1) * LANES, :] >= bound, 0.0, masked_value).T

    def value_matmul(t, slot, g):
        v_aug = jnp.concatenate([v_ref[0, g, t], ones], axis=-1)
        return jnp.dot(p_ref[slot, g], v_aug, preferred_element_type=F32)

    def softmax_pv(t, slot, live):
        for c in range(nc):
            bias_ref[slot, :, c * LANES:(c + 1) * LANES] = selection_bias(t, c, live, _NEG_INF)
        for g in range(N_KV_HEADS):
            for j in range(GQA_GROUP):
                hd = g * GQA_GROUP + j
                for r0 in range(0, tq, rc):
                    r1 = j * tq + r0
                    r2 = hd * tq + r0
                    s = [s_ref[slot, r2:r2 + rc, c * LANES:(c + 1) * LANES]
                         + bias_ref[slot, r0:r0 + rc, c * LANES:(c + 1) * LANES] for c in range(nc)]
                    mx = s[0]
                    for c in range(1, nc):
                        mx = jnp.maximum(mx, s[c])
                    m_old = m_ref[hd, r0:r0 + rc]
                    m_new = jnp.maximum(m_old, jnp.broadcast_to(jnp.max(mx, axis=-1, keepdims=True), (rc, LANES)))
                    alpha_ref[slot, g, r1:r1 + rc] = jnp.exp2(m_old - m_new)
                    m_ref[hd, r0:r0 + rc] = m_new
                    for c in range(nc):
                        p_ref[slot, g, r1:r1 + rc, c * LANES:(c + 1) * LANES] = jnp.exp2(s[c] - m_new).astype(BF16)
            pv = value_matmul(t, slot, g)
            for half in range(2):
                cols = slice(half * LANES, (half + 1) * LANES)
                acc_ref[g, :, cols] = alpha_ref[slot, g] * acc_ref[g, :, cols] + pv[:, cols]

    def attend_running():
        def attn_pair(u, carry):
            t0 = 2 * u
            t1 = jnp.minimum(t0 + 1, last)
            logits(t1, 1)
            softmax_pv(t0, 0, True)
            logits(jnp.minimum(t0 + 2, last), 0)
            softmax_pv(t1, 1, t0 + 1 < n_t)
            return carry

        lax.fori_loop(0, (n_t + 1) // 2, attn_pair, 0)

    def softmax_fixed(t, slot):
        for c in range(nc):
            bias_ref[slot, :, c * LANES:(c + 1) * LANES] = selection_bias(t, c, True, _NEG_INF)
        for g in range(N_KV_HEADS):
            for j in range(GQA_GROUP):
                hd = g * GQA_GROUP + j
                for r0 in range(0, tq, rc):
                    ref_pt = m_ref[hd, r0:r0 + rc]
                    for c in range(nc):
                        cs = slice(c * LANES, (c + 1) * LANES)
                        s = s_ref[slot, hd * tq + r0:hd * tq + r0 + rc, cs] + bias_ref[slot, r0:r0 + rc, cs]
                        p_ref[slot, g, j * tq + r0:j * tq + r0 + rc, cs] = jnp.exp2(s - ref_pt).astype(BF16)

    def attend_fixed():
        def accumulate(g, pv):
            for half in range(2):
                cols = slice(half * LANES, (half + 1) * LANES)
                acc_ref[g, :, cols] = acc_ref[g, :, cols] + pv[:, cols]

        def attn_pair(u, carry):
            t0 = 2 * u
            logits(t0 + 1, 1)
            softmax_fixed(t0, 0)
            logits(jnp.minimum(t0 + 2, last), 0)
            softmax_fixed(t0 + 1, 1)
            for g in range(N_KV_HEADS):
                accumulate(g, value_matmul(t0, 0, g) + value_matmul(t0 + 1, 1, g))
            return carry

        lax.fori_loop(0, n_t // 2, attn_pair, 0)

        @pl.when(n_t % 2 == 1)
        def _():
            softmax_fixed(last, 0)
            for g in range(N_KV_HEADS):
                accumulate(g, value_matmul(last, 0, g))

    @pl.when(qi == 0)
    def _():
        for g in range(N_KV_HEADS):
            def sq_norm_max(t, best):
                kk = kt_ref[0, g, t].astype(F32)
                return jnp.maximum(best, jnp.sum(kk * kk, axis=0, keepdims=True))
            best = lax.fori_loop(0, n_tiles_total, sq_norm_max, jnp.zeros((1, ts), F32))
            kmax_ref[g] = jnp.broadcast_to(jnp.max(best, axis=-1, keepdims=True), (8, LANES))

    logits(0, 0)
    top = jnp.zeros((tq, LANES), F32)
    for hd in range(N_HEADS):
        qf = q_ref[0, hd].astype(F32)
        q_sq = _row_sum(qf * qf)
        ref_pt = jnp.sqrt(q_sq * kmax_ref[hd // GQA_GROUP, 0:1, :]) * 1.001 + 1e-3
        m_ref[hd] = ref_pt
        top = jnp.maximum(top, ref_pt)
    fits = jnp.max(top) <= MAX_FIXED_REFERENCE

    @pl.when(fits)
    def _():
        attend_fixed()

    @pl.when(jnp.logical_not(fits))
    def _():
        m_ref[...] = jnp.full(m_ref.shape, _M_INIT, F32)
        attend_running()

    for hd in range(N_HEADS):
        g, j = divmod(hd, GQA_GROUP)
        o_ref[0, :, hd * HEAD_DIM:(hd + 1) * HEAD_DIM] = (
            acc_ref[g, j * tq:(j + 1) * tq, 0:HEAD_DIM] / acc_ref[g, j * tq:(j + 1) * tq, HEAD_DIM:2 * HEAD_DIM])


def _attend_t(qst, wrow, q_att, ki4, kt, v, q0, n_keys, tq, ts):
    B, _, T, _ = q_att.shape
    n_tiles = ki4.shape[1]
    k_sel = min(TOPK_MAX, n_keys // 4)
    kern = functools.partial(_attn_t_kernel, tq=tq, ts=ts, q0=q0, n_keys=n_keys, k_sel=k_sel)
    return pl.pallas_call(
        kern, grid=(B, T // tq),
        in_specs=[
            pl.BlockSpec((1, 1, 4 * IDX_DIM, N_IDX_HEADS * tq), lambda b, i: (b, i, 0, 0)),
            pl.BlockSpec((1, 1, N_IDX_HEADS, tq), lambda b, i: (b, i, 0, 0)),
            pl.BlockSpec((1, N_HEADS, tq, HEAD_DIM), lambda b, i: (b, 0, i, 0)),
            _resident((1, n_tiles, ts, 4 * IDX_DIM), lambda b, i: (b, 0, 0, 0)),
            _resident((1, N_KV_HEADS, n_tiles, HEAD_DIM, ts), lambda b, i: (b, 0, 0, 0, 0)),
            _resident((1, N_KV_HEADS, n_tiles, ts, HEAD_DIM), lambda b, i: (b, 0, 0, 0, 0)),
        ],
        out_specs=pl.BlockSpec((1, tq, N_HEADS * HEAD_DIM), lambda b, i: (b, i, 0)),
        out_shape=jax.ShapeDtypeStruct((B, T, N_HEADS * HEAD_DIM), F32),
        scratch_shapes=[
            pltpu.VMEM((n_tiles, ts, tq), I32),
            pltpu.VMEM((3, KEY_ROWS, tq), F32),
            pltpu.VMEM((2, ts, N_IDX_HEADS * tq), F32),
            pltpu.VMEM((2, N_HEADS * tq, ts), F32),
            pltpu.VMEM((N_HEADS, tq, LANES), F32),
            pltpu.VMEM((N_KV_HEADS, GQA_GROUP * tq, 2 * HEAD_DIM), F32),
            pltpu.VMEM((2, tq, ts), F32),
            pltpu.VMEM((2, N_KV_HEADS, GQA_GROUP * tq, ts), BF16),
            pltpu.VMEM((2, N_KV_HEADS, GQA_GROUP * tq, LANES), F32),
            pltpu.VMEM((N_KV_HEADS, 8, LANES), F32),
        ],
        name="attend_t",
        compiler_params=pltpu.CompilerParams(dimension_semantics=("arbitrary", "arbitrary"),
                                             vmem_limit_bytes=VMEM_LIMIT),
    )(qst, wrow, q_att, ki4, kt, v)


def _attend(qs, kiw, q_att, ki4, kt, v, q0, n_keys, tq, ts):
    B, _, T, _ = qs.shape
    lp = ki4.shape[1]
    n_tiles = lp // ts
    k_sel = min(TOPK_MAX, n_keys // 4)
    kern = functools.partial(_attn_kernel, tq=tq, ts=ts, q0=q0, n_keys=n_keys, k_sel=k_sel)
    keys = pl.BlockSpec((1, lp, 2 * LANES), lambda b, i: (b, 0, 0))
    return pl.pallas_call(
        kern, grid=(B, T // tq),
        in_specs=[
            pl.BlockSpec((1, N_IDX_HEADS, tq, 4 * IDX_DIM), lambda b, i: (b, 0, i, 0)),
            pl.BlockSpec((1, tq, LANES), lambda b, i: (b, i, 0)),
            pl.BlockSpec((1, N_HEADS, tq, HEAD_DIM), lambda b, i: (b, 0, i, 0)),
            keys, keys, keys,
        ],
        out_specs=pl.BlockSpec((1, tq, N_HEADS * HEAD_DIM), lambda b, i: (b, i, 0)),
        out_shape=jax.ShapeDtypeStruct((B, T, N_HEADS * HEAD_DIM), F32),
        scratch_shapes=[
            pltpu.VMEM((n_tiles, tq, ts), I32),
            pltpu.VMEM((2, tq, LANES), F32),
            pltpu.VMEM((N_IDX_HEADS, tq, LANES), F32),
            pltpu.VMEM((2, N_HEADS * tq, ts), F32),
            pltpu.VMEM((N_HEADS, tq, LANES), F32),
            pltpu.VMEM((N_KV_HEADS, GQA_GROUP * tq, 2 * HEAD_DIM), F32),
            pltpu.VMEM((2, tq, ts), F32),
            pltpu.VMEM((2, N_KV_HEADS, GQA_GROUP * tq, ts), BF16),
            pltpu.VMEM((2, N_KV_HEADS, GQA_GROUP * tq, LANES), F32),
        ],
        name="attend",
        compiler_params=pltpu.CompilerParams(dimension_semantics=("arbitrary", "arbitrary"),
                                             vmem_limit_bytes=VMEM_LIMIT),
    )(qs, kiw, q_att, ki4, kt, v)


def _rms(x, g):
    return x * lax.rsqrt(jnp.mean(x * x, axis=-1, keepdims=True) + RMS_EPS) * g


def _out_kernel(x_ref, pa_ref, sgb_ref, b_ref, wo_ref, g1_ref, g2_ref, wgu_ref, wd_ref, g3_ref, y_ref, *, d_ff):
    merged = pa_ref[...] + sgb_ref[...] * b_ref[...]
    mix = jnp.dot(merged.astype(BF16), wo_ref[...], preferred_element_type=F32)
    x1 = x_ref[...] + _rms(mix, g1_ref[...])
    h2 = _rms(x1, g2_ref[...]).astype(BF16)
    gu = jnp.dot(h2, wgu_ref[...], preferred_element_type=F32)
    gate, up = gu[:, :d_ff], gu[:, d_ff:]
    act = (gate * _sigmoid(gate) * up).astype(BF16)
    f = jnp.dot(act, wd_ref[...], preferred_element_type=F32)
    y_ref[...] = x1 + _rms(f, g3_ref[...])


def _finish(x, pa, sgb, b, w_out, g_post, g_ffn_pre, w_gate_up, w_down, g_ffn_post, tm):
    N, D = x.shape
    d_ff = w_down.shape[0]
    row = pl.BlockSpec((tm, D), lambda i: (i, 0))
    const = lambda i: (0, 0)
    return pl.pallas_call(
        functools.partial(_out_kernel, d_ff=d_ff), grid=(N // tm,),
        in_specs=[row, row, row, row,
                  _resident((D, D), const), _resident((1, D), const), _resident((1, D), const),
                  _resident((D, 2 * d_ff), const), _resident((d_ff, D), const), _resident((1, D), const)],
        out_specs=row, out_shape=jax.ShapeDtypeStruct((N, D), F32), name="finish",
        compiler_params=pltpu.CompilerParams(dimension_semantics=("arbitrary",), vmem_limit_bytes=VMEM_LIMIT),
    )(x, pa, sgb, b, w_out, g_post, g_ffn_pre, w_gate_up, w_down, g_ffn_post)


def _tile_keys(n_keys, ts):
    return -(-n_keys // ts) * ts


def _layer(x, pos0, hist, k_cache, v_cache, ki_cache, wts, tm, tq, ts, tm_out):
    (w_in_r, w_pool, pool_scale, w_out, w_gate_up, w_down, g_pre, g_post, g_ffn_pre, g_ffn_post) = wts
    B, T, D = x.shape
    hist16 = jnp.concatenate([jnp.zeros((B, HALO // 2 - POOL_HIST, POOL_WIDTH), F32), hist], axis=1)
    own_keys_only = k_cache is None and T % ts == 0 and ts % tm == 0 and tq == LANES and tm % LANES == 0
    outs = _project(x, hist16, pos0, w_in_r, g_pre, w_pool, pool_scale, tm, ts if own_keys_only else None)
    u, q_att, k, v, qs, kiw, pa, sgb = outs[:8]
    ki = kiw[:, :, :IDX_DIM]
    if own_keys_only:
        kt, vt, ki4, wrow = outs[8:]
        b = _attend_t(qs, wrow, q_att, ki4, kt, vt, pos0, T, tq, ts)
    else:
        if k_cache is None:
            k_all, v_all, ki_all = k, v, ki
        else:
            k_all = jnp.concatenate([k_cache.reshape(B, -1, N_KV_HEADS * HEAD_DIM), k], axis=1)
            v_all = jnp.concatenate([v_cache.reshape(B, -1, N_KV_HEADS * HEAD_DIM), v], axis=1)
            ki_all = jnp.concatenate([ki_cache, ki], axis=1)
        n_keys = k_all.shape[1]
        lp = _tile_keys(n_keys, ts)
        pad = ((0, 0), (0, lp - n_keys), (0, 0))
        kt = jnp.pad(k_all.astype(BF16), pad)
        vt = jnp.pad(v_all.astype(BF16), pad)
        ki4 = _split_keys(jnp.pad(ki_all, pad), ts)
        b = _attend(qs, kiw, q_att, ki4, kt, vt, pos0, n_keys, tq, ts)
    y = _finish(x.reshape(B * T, D), pa.reshape(B * T, D), sgb.reshape(B * T, D), b.reshape(B * T, D),
                w_out, g_post, g_ffn_pre, w_gate_up, w_down, g_ffn_post, tm_out).reshape(B, T, D)
    new_pool = jnp.concatenate([hist, u], axis=1)[:, T:]
    return (y, k.reshape(B, T, N_KV_HEADS, HEAD_DIM), v.reshape(B, T, N_KV_HEADS, HEAD_DIM), ki, new_pool)


def _relayout_w_in(w_in):
    d = w_in.shape[0]
    o_kiw = POOL_WIDTH + N_HEADS * HEAD_DIM + 2 * N_KV_HEADS * HEAD_DIM + N_IDX_HEADS * IDX_DIM
    narrow = IDX_DIM + N_IDX_HEADS
    padded = jnp.concatenate([w_in[:, :o_kiw + narrow], jnp.zeros((d, LANES - narrow), w_in.dtype),
                              w_in[:, o_kiw + narrow:]], axis=1)
    return padded.astype(BF16)


def kernel(x_prompt, x_sample, cache_k, cache_v, cache_k_idx, state_pool, w_in, w_pool, pool_scale, w_out,
           w_gate_up, w_down, norm_mix_pre, norm_mix_post, norm_ffn_pre, norm_ffn_post):
    depth = w_in.shape[0]
    past = cache_k.shape[2]
    t_p, t_s = x_prompt.shape[1], x_sample.shape[1]
    hist_p = jnp.zeros((x_prompt.shape[0], POOL_HIST, POOL_WIDTH), x_prompt.dtype)
    xp, xs = x_prompt, x_sample
    outs = [[] for _ in range(8)]
    for l in range(depth):
        wts = (_relayout_w_in(w_in[l]), w_pool[l].astype(BF16), pool_scale[l][None, :], w_out[l].astype(BF16),
               w_gate_up[l].astype(BF16), w_down[l].astype(BF16), norm_mix_pre[l][None, :],
               norm_mix_post[l][None, :], norm_ffn_pre[l][None, :], norm_ffn_post[l][None, :])
        tm_p = min(256, t_p)
        tq_p = min(128, t_p)
        xp, k1, v1, ki1, p1 = _layer(xp, 0, hist_p, None, None, None, wts, tm_p, tq_p, 512, tm_p)
        n_s = xs.shape[0] * t_s
        xs, k2, v2, ki2, p2 = _layer(xs, past, state_pool[l], cache_k[l], cache_v[l], cache_k_idx[l], wts,
                                     t_s, t_s, 512, min(256, n_s))
        for lst, val in zip(outs, (k1, v1, ki1, p1, k2, v2, ki2, p2)):
            lst.append(val)
    return (xp, xs) + tuple(jnp.stack(o) for o in outs)
```

```python
import functools

import jax
import jax.numpy as jnp
from jax import lax
from jax.experimental import pallas as pl
from jax.experimental.pallas import tpu as pltpu

F32 = jnp.float32
BF16 = jnp.bfloat16
I32 = jnp.int32

LANES = 128
CHUNK = 64
POOL_WINDOWS = (2, 4, 8, 16)
N_POOL_GROUPS = 4
POOL_GROUP_WIDTH = 128
POOL_WIDTH = N_POOL_GROUPS * POOL_GROUP_WIDTH
POOL_HIST = 15
N_HEADS = 8
N_KV_HEADS = 2
HEAD_DIM = 128
GQA_GROUP = N_HEADS // N_KV_HEADS
ROPE_THETA = 500000.0
N_IDX_HEADS = 8
IDX_DIM = 64
TOPK_MAX = 256
RMS_EPS = 1e-6
ATTN_SCALE = HEAD_DIM ** -0.5
IDX_SCALE = (N_IDX_HEADS ** -0.5) * (IDX_DIM ** -0.5)

LOG2_E = 1.4426950408889634
VALUE_PIVOT_STEPS = 28
MAX_FIXED_REFERENCE = 40.0
SPLIT_ROWS = 1536
UNTESTED_STEPS = 12
KEY_ROWS = 16
COUNT_ROWS = 32
ROW_CHUNK = 32
HALO = 32
VMEM_LIMIT = 56 * 1024 * 1024

_NEG_INF = float("-inf")
_POS_INF = float("inf")
_INADMISSIBLE = -(2 ** 31)
_NO_KEY = 2 ** 31 - 1
_KEY_MIN_NORMAL = 0x00800000
_KEY_NEG_ZERO = -1
_M_INIT = -1e30


def _resident(block_shape, index_map):
    return pl.BlockSpec(block_shape, index_map, pipeline_mode=pl.Buffered(1))


def _rope(xs, cos, sin, half, period):
    lane = lax.broadcasted_iota(I32, xs.shape, 1)
    ahead = pltpu.roll(xs, LANES - half, 1)
    behind = pltpu.roll(xs, half, 1)
    partner = jnp.where((lane & (period - 1)) < half, ahead, behind)
    return xs * cos + partner * sin


def _sigmoid(x):
    return 1.0 / (1.0 + jnp.exp(-x))


def _proj_kernel(x_ref, g_ref, w_ref, hist_ref, cq_ref, sq_ref, ci_ref, si_ref, wpool_ref, pscale_ref,
                 u_ref, qatt_ref, k_ref, v_ref, qs_ref, kiw_ref, pa_ref, sgb_ref, *rest, tm, pos0, d_model, key_tiles):
    if key_tiles:
        kt_ref, vt_ref, ki4_ref, wrow_ref, e_ref, s2_ref, s4_ref, s8_ref = rest
    else:
        e_ref, s2_ref, s4_ref, s8_ref = rest
    i = pl.program_id(1)
    x = x_ref[0]
    h = x * lax.rsqrt(jnp.mean(x * x, axis=-1, keepdims=True) + RMS_EPS) * g_ref[...]
    o_q = POOL_WIDTH
    o_k = o_q + N_HEADS * HEAD_DIM
    o_v = o_k + N_KV_HEADS * HEAD_DIM
    o_qi = o_v + N_KV_HEADS * HEAD_DIM
    o_kiw = o_qi + N_IDX_HEADS * IDX_DIM
    o_ga = o_kiw + LANES
    o_gb = o_ga + d_model

    proj = jnp.dot(h.astype(BF16), w_ref[...], preferred_element_type=F32)

    cq, sq, ci, si = cq_ref[...], sq_ref[...], ci_ref[...], si_ref[...]
    half_q = HEAD_DIM // 8
    half_i = IDX_DIM // 8

    for hd in range(N_HEADS):
        qh = _rope(proj[:, o_q + hd * HEAD_DIM:o_q + (hd + 1) * HEAD_DIM], cq, sq, half_q, HEAD_DIM)
        qatt_ref[0, hd] = (qh * (ATTN_SCALE * LOG2_E)).astype(BF16)
    for kh in range(N_KV_HEADS):
        k_h = _rope(proj[:, o_k + kh * HEAD_DIM:o_k + (kh + 1) * HEAD_DIM], cq, sq, half_q, HEAD_DIM)
        k_ref[0, :, kh * HEAD_DIM:(kh + 1) * HEAD_DIM] = k_h
        if key_tiles:
            kt_ref[0, kh, 0] = k_h.T.astype(BF16)
            vt_ref[0, kh, 0] = proj[:, o_v + kh * HEAD_DIM:o_v + (kh + 1) * HEAD_DIM].astype(BF16)
    v_ref[0] = proj[:, o_v:o_qi]

    for pr in range(N_IDX_HEADS // 2):
        qi2 = _rope(proj[:, o_qi + pr * LANES:o_qi + (pr + 1) * LANES], ci, si, half_i, IDX_DIM)
        hi = qi2.astype(BF16).astype(F32)
        lo = qi2 - hi
        lane2 = lax.broadcasted_iota(I32, qi2.shape, 1)
        first = jnp.where(lane2 < IDX_DIM, hi, pltpu.roll(lo, IDX_DIM, 1))
        second = jnp.where(lane2 < IDX_DIM, pltpu.roll(hi, IDX_DIM, 1), lo)
        for sub, slab in enumerate((first, second)):
            hd = 2 * pr + sub
            if key_tiles:
                for sb in range(tm // LANES):
                    slab_t = slab[sb * LANES:(sb + 1) * LANES].T.astype(BF16)
                    qs_ref[0, sb, 0:LANES, hd * LANES:(hd + 1) * LANES] = slab_t
                    qs_ref[0, sb, LANES:2 * LANES, hd * LANES:(hd + 1) * LANES] = slab_t
            else:
                qs_ref[0, hd, :, 0:LANES] = slab.astype(BF16)
                qs_ref[0, hd, :, LANES:2 * LANES] = slab.astype(BF16)

    kiw = _rope(proj[:, o_kiw:o_kiw + LANES], ci, si, half_i, IDX_DIM)
    lane = lax.broadcasted_iota(I32, kiw.shape, 1)
    kiw_full = jnp.where(lane < IDX_DIM, kiw, proj[:, o_kiw:o_kiw + LANES] * IDX_SCALE)
    kiw_ref[0] = kiw_full
    if key_tiles:
        ki_hi = kiw.astype(BF16).astype(F32)
        ki_lo = kiw - ki_hi
        ki4_ref[0, 0, :, 0:LANES] = jnp.where(lane < IDX_DIM, ki_hi, pltpu.roll(ki_hi, IDX_DIM, 1)).astype(BF16)
        ki4_ref[0, 0, :, LANES:2 * LANES] = jnp.where(lane < IDX_DIM, ki_lo, pltpu.roll(ki_lo, IDX_DIM, 1)).astype(BF16)
        for sb in range(tm // LANES):
            wrow_ref[0, sb] = kiw_full[sb * LANES:(sb + 1) * LANES].T[IDX_DIM:IDX_DIM + N_IDX_HEADS]

    u = proj[:, 0:POOL_WIDTH]
    u_ref[0] = u

    @pl.when(i == 0)
    def _():
        e_ref[0:HALO // 2, :] = jnp.zeros((HALO // 2, POOL_WIDTH), F32)
        e_ref[HALO // 2:HALO, :] = hist_ref[0]

    e_ref[HALO:HALO + tm, :] = u
    n2, n4, n8 = tm + 24, tm + 16, tm + 8
    s2_ref[8:8 + n2, :] = e_ref[8:8 + n2, :] + e_ref[7:7 + n2, :]
    s4_ref[16:16 + n4, :] = s2_ref[16:16 + n4, :] + s2_ref[14:14 + n4, :]
    s8_ref[24:24 + n8, :] = s4_ref[24:24 + n8, :] + s4_ref[20:20 + n8, :]
    s16 = s8_ref[HALO:HALO + tm, :] + s8_ref[HALO - 8:HALO - 8 + tm, :]
    wins = (s2_ref[HALO:HALO + tm, :], s4_ref[HALO:HALO + tm, :], s8_ref[HALO:HALO + tm, :], s16)
    e_ref[HALO // 2:HALO, :] = e_ref[HALO // 2 + tm:HALO + tm, :]

    pos = pos0 + i * tm + lax.broadcasted_iota(I32, (tm, POOL_GROUP_WIDTH), 0)
    a_parts = []
    for g, w in enumerate(POOL_WINDOWS):
        lo_l, hi_l = g * POOL_GROUP_WIDTH, (g + 1) * POOL_GROUP_WIDTH
        cnt = jnp.minimum(pos + 1, w).astype(F32)
        pooled = wins[g][:, lo_l:hi_l] / cnt - u[:, lo_l:hi_l]
        a_parts.append(jnp.dot(pooled.astype(BF16), wpool_ref[g], preferred_element_type=F32))
    a = jnp.concatenate(a_parts, axis=-1) * pscale_ref[...]
    pa_ref[0] = _sigmoid(proj[:, o_ga:o_gb]) * a
    sgb_ref[0] = _sigmoid(proj[:, o_gb:o_gb + d_model])


def _project(x, hist16, pos0, w_in_r, g_pre, w_pool, pool_scale, tm, key_tile=None):
    B, T, D = x.shape
    W = w_in_r.shape[1]
    pos = pos0 + jnp.arange(T, dtype=I32)
    cq, sq = _rope_tables(pos, HEAD_DIM)
    ci, si = _rope_tables(pos, IDX_DIM)
    row = lambda b, i: (b, i, 0)
    tab = lambda b, i: (i, 0)
    const2 = lambda b, i: (0, 0)
    out_shape = (
        jax.ShapeDtypeStruct((B, T, POOL_WIDTH), F32),
        jax.ShapeDtypeStruct((B, N_HEADS, T, HEAD_DIM), BF16),
        jax.ShapeDtypeStruct((B, T, N_KV_HEADS * HEAD_DIM), F32),
        jax.ShapeDtypeStruct((B, T, N_KV_HEADS * HEAD_DIM), F32),
        jax.ShapeDtypeStruct((B, N_IDX_HEADS, T, 4 * IDX_DIM), BF16),
        jax.ShapeDtypeStruct((B, T, LANES), F32),
        jax.ShapeDtypeStruct((B, T, D), F32),
        jax.ShapeDtypeStruct((B, T, D), F32),
    )
    out_specs = (
        pl.BlockSpec((1, tm, POOL_WIDTH), row),
        pl.BlockSpec((1, N_HEADS, tm, HEAD_DIM), lambda b, i: (b, 0, i, 0)),
        pl.BlockSpec((1, tm, N_KV_HEADS * HEAD_DIM), row),
        pl.BlockSpec((1, tm, N_KV_HEADS * HEAD_DIM), row),
        pl.BlockSpec((1, N_IDX_HEADS, tm, 4 * IDX_DIM), lambda b, i: (b, 0, i, 0)),
        pl.BlockSpec((1, tm, LANES), row),
        pl.BlockSpec((1, tm, D), row),
        pl.BlockSpec((1, tm, D), row),
    )
    in_specs = [
        pl.BlockSpec((1, tm, D), row),
        _resident((1, D), const2),
        _resident((D, W), const2),
        pl.BlockSpec((1, HALO // 2, POOL_WIDTH), lambda b, i: (b, 0, 0)),
        pl.BlockSpec((tm, LANES), tab), pl.BlockSpec((tm, LANES), tab),
        pl.BlockSpec((tm, LANES), tab), pl.BlockSpec((tm, LANES), tab),
        _resident((N_POOL_GROUPS, POOL_GROUP_WIDTH, D // N_POOL_GROUPS), lambda b, i: (0, 0, 0)),
        _resident((1, D), const2),
    ]
    if key_tile is not None:
        assert key_tile % tm == 0 and T % key_tile == 0 and tm % LANES == 0
        per = key_tile // tm
        n_tiles = T // key_tile
        sub = tm // LANES
        out_shape = out_shape[:4] + (
            jax.ShapeDtypeStruct((B, T // LANES, 4 * IDX_DIM, N_IDX_HEADS * LANES), BF16),
        ) + out_shape[5:] + (
            jax.ShapeDtypeStruct((B, N_KV_HEADS, n_tiles, HEAD_DIM, key_tile), BF16),
            jax.ShapeDtypeStruct((B, N_KV_HEADS, n_tiles, key_tile, HEAD_DIM), BF16),
            jax.ShapeDtypeStruct((B, n_tiles, key_tile, 4 * IDX_DIM), BF16),
            jax.ShapeDtypeStruct((B, T // LANES, N_IDX_HEADS, LANES), F32),
        )
        out_specs = out_specs[:4] + (
            pl.BlockSpec((1, sub, 4 * IDX_DIM, N_IDX_HEADS * LANES), lambda b, i: (b, i, 0, 0)),
        ) + out_specs[5:] + (
            pl.BlockSpec((1, N_KV_HEADS, 1, HEAD_DIM, tm), lambda b, i: (b, 0, i // per, 0, i % per)),
            pl.BlockSpec((1, N_KV_HEADS, 1, tm, HEAD_DIM), lambda b, i: (b, 0, i // per, i % per, 0)),
            pl.BlockSpec((1, 1, tm, 4 * IDX_DIM), lambda b, i: (b, i // per, i % per, 0)),
            pl.BlockSpec((1, sub, N_IDX_HEADS, LANES), lambda b, i: (b, i, 0, 0)),
        )
    scratch = [pltpu.VMEM((HALO + tm, POOL_WIDTH), F32) for _ in range(4)]
    return pl.pallas_call(
        functools.partial(_proj_kernel, tm=tm, pos0=pos0, d_model=D, key_tiles=key_tile is not None),
        grid=(B, T // tm), in_specs=in_specs, out_specs=out_specs, out_shape=out_shape,
        scratch_shapes=scratch, name="proj",
        compiler_params=pltpu.CompilerParams(dimension_semantics=("arbitrary", "arbitrary"),
                                             vmem_limit_bytes=VMEM_LIMIT),
    )(x, g_pre, w_in_r, hist16, cq, sq, ci, si, w_pool, pool_scale)


def _rope_tables(pos, dim):
    rot = dim // 4
    half = rot // 2
    inv = ROPE_THETA ** (-jnp.arange(half, dtype=F32) / half)
    ang = pos.astype(F32)[:, None] * inv[None, :]
    cos, sin = jnp.cos(ang), jnp.sin(ang)
    rest = dim - rot
    n = pos.shape[0]
    c = jnp.concatenate([cos, cos, jnp.ones((n, rest), F32)], axis=-1)
    s = jnp.concatenate([-sin, sin, jnp.zeros((n, rest), F32)], axis=-1)
    return jnp.tile(c, (1, LANES // dim)), jnp.tile(s, (1, LANES // dim))


def _split_kernel(k_ref, out_ref):
    k = k_ref[0]
    hi = k.astype(BF16).astype(F32)
    lo = k - hi
    out_ref[0, :, 0:LANES] = jnp.concatenate([hi, hi], axis=-1).astype(BF16)
    out_ref[0, :, LANES:2 * LANES] = jnp.concatenate([lo, lo], axis=-1).astype(BF16)


def _split_keys(ki, ts):
    B, Lp, _ = ki.shape
    rows = ts * max(1, SPLIT_ROWS // ts)
    rows = rows if Lp % rows == 0 else ts
    return pl.pallas_call(
        _split_kernel, grid=(B, Lp // rows),
        in_specs=[pl.BlockSpec((1, rows, IDX_DIM), lambda b, t: (b, t, 0))],
        out_specs=pl.BlockSpec((1, rows, 4 * IDX_DIM), lambda b, t: (b, t, 0)),
        out_shape=jax.ShapeDtypeStruct((B, Lp, 4 * IDX_DIM), BF16), name="split_keys",
        compiler_params=pltpu.CompilerParams(dimension_semantics=("arbitrary", "arbitrary")),
    )(ki)


def _key_of(x):
    b = pltpu.bitcast(x, I32)
    return b ^ ((b >> 31) & 0x7FFFFFFF)


def _float_of(k):
    return pltpu.bitcast(k ^ ((k >> 31) & 0x7FFFFFFF), F32)


def _row_sum(x):
    return jnp.broadcast_to(jnp.sum(x, axis=-1, keepdims=True), x.shape)


def _attn_kernel(qs_ref, kiw_ref, q_ref, ki_ref, kt_ref, v_ref, o_ref,
                 sc_ref, mm_ref, wb_ref, s_ref, m_ref, acc_ref, bias_ref, p_ref, alpha_ref,
                 *, tq, ts, q0, n_keys, k_sel):
    qi = pl.program_id(1)
    n_tiles_total = ki_ref.shape[1] // ts
    nc = ts // LANES
    nt_dims = (((1,), (1,)), ((), ()))

    def key_rows(t):
        return pl.ds(pl.multiple_of(t * ts, ts), ts)

    row = lax.broadcasted_iota(I32, (tq, LANES), 0)
    lane = lax.broadcasted_iota(I32, (tq, LANES), 1)
    qpos = q0 + qi * tq + row
    n_adm = jnp.minimum((qpos // CHUNK + 1) * CHUNK, n_keys)
    last_adm = jnp.minimum(((q0 + (qi + 1) * tq - 1) // CHUNK + 1) * CHUNK, n_keys)
    n_t = jnp.minimum((last_adm + ts - 1) // ts, n_tiles_total)

    kiw = kiw_ref[0]
    for hd in range(N_IDX_HEADS):
        wb_ref[hd] = jnp.broadcast_to(kiw[:, IDX_DIM + hd:IDX_DIM + hd + 1], (tq, LANES))
    qs2 = qs_ref[0].reshape(N_IDX_HEADS * tq, 4 * IDX_DIM)

    def idx_logits(t, slot):
        s_ref[slot] = lax.dot_general(qs2, ki_ref[0, key_rows(t), :], nt_dims,
                                      preferred_element_type=F32)

    rc = min(tq, ROW_CHUNK)
    mm_ref[0] = jnp.full((tq, LANES), _POS_INF, F32)
    mm_ref[1] = jnp.full((tq, LANES), _NEG_INF, F32)

    def score_tile(t, slot, masked):
        for r0 in range(0, tq, rc):
            rows = slice(r0, r0 + rc)
            mn, mx = mm_ref[0, rows], mm_ref[1, rows]
            for c in range(nc):
                cs = slice(c * LANES, (c + 1) * LANES)
                acc = None
                for hd in range(N_IDX_HEADS):
                    r = jnp.maximum(s_ref[slot, hd * tq + r0:hd * tq + r0 + rc, cs], 0.0) * wb_ref[hd, rows]
                    acc = r if acc is None else acc + r
                key = _key_of(acc)
                if masked:
                    qpos_c = q0 + qi * tq + r0 + lax.broadcasted_iota(I32, (rc, LANES), 0)
                    n_adm_c = jnp.minimum((qpos_c // CHUNK + 1) * CHUNK, n_keys)
                    adm = (t * ts + c * LANES + lax.broadcasted_iota(I32, (rc, LANES), 1)) < n_adm_c
                    key = jnp.where(adm, key, _INADMISSIBLE)
                    mx = jnp.maximum(mx, jnp.where(adm, acc, _NEG_INF))
                    mn = jnp.minimum(mn, jnp.where(adm, acc, _POS_INF))
                else:
                    mx = jnp.maximum(mx, acc)
                    mn = jnp.minimum(mn, acc)
                sc_ref[t, rows, cs] = key
            mm_ref[0, rows] = mn
            mm_ref[1, rows] = mx

    n_full = jnp.minimum(jnp.minimum(((q0 + qi * tq) // CHUNK + 1) * CHUNK, n_keys) // ts, n_t)
    last_full = jnp.maximum(n_full - 1, 0)
    idx_logits(0, 0)

    def score_pair(u, carry):
        t0 = 2 * u
        t1 = jnp.minimum(t0 + 1, last_full)
        idx_logits(t1, 1)
        score_tile(t0, 0, False)
        idx_logits(jnp.minimum(t0 + 2, last_full), 0)
        score_tile(t1, 1, False)
        return carry

    lax.fori_loop(0, (n_full + 1) // 2, score_pair, 0)

    def score_tail(t, carry):
        idx_logits(t, 0)
        score_tile(t, 0, True)
        return carry

    lax.fori_loop(n_full, n_t, score_tail, 0)
    rmin = jnp.broadcast_to(jnp.min(mm_ref[0], axis=-1, keepdims=True), (tq, LANES))
    rmax = jnp.broadcast_to(jnp.max(mm_ref[1], axis=-1, keepdims=True), (tq, LANES))

    def count_ge(thr):
        def body(t, cnt):
            for c in range(nc):
                cnt = cnt + jnp.where(sc_ref[t, :, c * LANES:(c + 1) * LANES] >= thr, 1.0, 0.0)
            return cnt
        return _row_sum(lax.fori_loop(0, n_t, body, jnp.zeros((tq, LANES), F32)))

    def bis_cond(st):
        return st[0] > 0

    kf = float(k_sel)
    take_all = n_adm <= k_sel
    lo0 = _key_of(rmin) - 1
    hi0 = _key_of(rmax) + 2
    c_pos = count_ge(jnp.full((tq, LANES), _KEY_MIN_NORMAL, I32))
    c_nn = count_ge(jnp.full((tq, LANES), _KEY_NEG_ZERO, I32))
    pos = c_pos >= kf
    neg = c_nn < kf
    lo = jnp.where(pos, _KEY_MIN_NORMAL, jnp.where(neg, lo0, _KEY_NEG_ZERO))
    hi = jnp.where(pos, hi0, jnp.where(neg, _KEY_NEG_ZERO, _KEY_MIN_NORMAL))
    clo = jnp.where(pos, c_pos, jnp.where(neg, n_adm.astype(F32), c_nn))
    chi = jnp.where(pos, 0.0, jnp.where(neg, c_nn, c_pos))
    exact0 = jnp.where(clo == kf, 1, 0)
    act0 = jnp.where(take_all, 0, jnp.where(exact0 > 0, 0, jnp.where(hi > lo + 1, 1, 0)))

    def bis_step(it, lo, hi, clo, chi, exact, act):
        kmid = (lo & hi) + ((lo ^ hi) >> 1)
        vmid = _key_of((_float_of(lo) + _float_of(hi)) * 0.5)
        by_value = jnp.where(it < VALUE_PIVOT_STEPS, jnp.where(vmid > lo, jnp.where(vmid < hi, 1, 0), 0), 0)
        mid = jnp.where(by_value > 0, vmid, kmid)
        cnt = count_ge(mid)
        up = jnp.where(act > 0, jnp.where(cnt >= kf, 1, 0), 0)
        dn = act - up
        lo = jnp.where(up > 0, mid, lo)
        clo = jnp.where(up > 0, cnt, clo)
        hi = jnp.where(dn > 0, mid, hi)
        chi = jnp.where(dn > 0, cnt, chi)
        hit = jnp.where(up > 0, jnp.where(cnt == kf, 1, 0), 0)
        exact = exact + hit
        act = jnp.where(act > 0, jnp.where(hit > 0, 0, jnp.where(hi > lo + 1, 1, 0)), 0)
        return it + 1, lo, hi, clo, chi, exact, act

    def any_active(act):
        return jnp.max(act.astype(F32))

    def bis_body(st):
        st = bis_step(*bis_step(*st[1:]))
        return (any_active(st[-1]),) + st

    _, _, lo, _, clo, chi, exact, _ = lax.while_loop(
        bis_cond, bis_body, (any_active(act0), jnp.int32(0), lo, hi, clo, chi, exact0, act0))
    thr = jnp.where(take_all, _INADMISSIBLE + 1, lo)
    thr_up = jnp.where(take_all, _INADMISSIBLE + 1, lo + 1)

    need = kf - chi
    tied = jnp.where(take_all, 0, jnp.where(exact > 0, 0, jnp.where(clo > kf, 1, 0)))

    def count_tied_below(cut):
        def body(t, cnt):
            for c in range(nc):
                s = sc_ref[t, :, c * LANES:(c + 1) * LANES]
                col = t * ts + c * LANES + lane
                cnt = cnt + jnp.where(s == thr, jnp.where(col < cut, 1.0, 0.0), 0.0)
            return cnt
        return _row_sum(lax.fori_loop(0, n_t, body, jnp.zeros((tq, LANES), F32)))

    def cut_body(st):
        _, lo_c, hi_c, act = st
        mid = (lo_c + hi_c) >> 1
        cnt = count_tied_below(mid)
        on = act > 0
        le = cnt <= need
        lo_c = jnp.where(on, jnp.where(le, mid, lo_c), lo_c)
        hi_c = jnp.where(on, jnp.where(le, hi_c, mid), hi_c)
        act = jnp.where(on, jnp.where(hi_c - lo_c > 1, 1, 0), 0)
        return jnp.max(act), lo_c, hi_c, act

    _, cut_lo, _, _ = lax.while_loop(
        bis_cond, cut_body,
        (jnp.max(tied), jnp.zeros((tq, LANES), I32), jnp.full((tq, LANES), n_tiles_total * ts + 1, I32), tied))
    cut = jnp.where(tied > 0, cut_lo, n_tiles_total * ts + 1)

    m_ref[...] = jnp.full(m_ref.shape, _M_INIT, F32)
    acc_ref[...] = jnp.zeros(acc_ref.shape, F32)
    rc = min(tq, ROW_CHUNK)
    ones = jnp.ones((ts, LANES), BF16)
    last = n_t - 1

    def logits(t, slot):
        for g in range(N_KV_HEADS):
            qg = q_ref[0, g * GQA_GROUP:(g + 1) * GQA_GROUP].reshape(GQA_GROUP * tq, HEAD_DIM)
            s_ref[slot, g * GQA_GROUP * tq:(g + 1) * GQA_GROUP * tq] = lax.dot_general(
                qg, kt_ref[0, key_rows(t), g * HEAD_DIM:(g + 1) * HEAD_DIM], nt_dims,
                preferred_element_type=F32)

    def softmax_pv(t, slot, live):
        for c in range(nc):
            col = t * ts + c * LANES + lane
            bound = jnp.where(live, jnp.where(col < cut, thr, thr_up), _NO_KEY)
            bias_ref[slot, :, c * LANES:(c + 1) * LANES] = jnp.where(
                sc_ref[t, :, c * LANES:(c + 1) * LANES] >= bound, 0.0, _NEG_INF)
        for g in range(N_KV_HEADS):
            for j in range(GQA_GROUP):
                hd = g * GQA_GROUP + j
                for r0 in range(0, tq, rc):
                    r1 = j * tq + r0
                    r2 = hd * tq + r0
                    s = [s_ref[slot, r2:r2 + rc, c * LANES:(c + 1) * LANES]
                         + bias_ref[slot, r0:r0 + rc, c * LANES:(c + 1) * LANES] for c in range(nc)]
                    mx = s[0]
                    for c in range(1, nc):
                        mx = jnp.maximum(mx, s[c])
                    m_old = m_ref[hd, r0:r0 + rc]
                    m_new = jnp.maximum(m_old, jnp.broadcast_to(jnp.max(mx, axis=-1, keepdims=True), (rc, LANES)))
                    alpha_ref[slot, g, r1:r1 + rc] = jnp.exp2(m_old - m_new)
                    for c in range(nc):
                        p_ref[slot, g, r1:r1 + rc, c * LANES:(c + 1) * LANES] = jnp.exp2(s[c] - m_new).astype(BF16)
                    m_ref[hd, r0:r0 + rc] = m_new
            v_aug = jnp.concatenate([v_ref[0, key_rows(t), g * HEAD_DIM:(g + 1) * HEAD_DIM], ones], axis=-1)
            pv = jnp.dot(p_ref[slot, g], v_aug, preferred_element_type=F32)
            alpha = alpha_ref[slot, g]
            for half in range(2):
                cols = slice(half * LANES, (half + 1) * LANES)
                acc_ref[g, :, cols] = alpha * acc_ref[g, :, cols] + pv[:, cols]

    logits(0, 0)

    def attn_pair(u, carry):
        t0 = 2 * u
        t1 = jnp.minimum(t0 + 1, last)
        logits(t1, 1)
        softmax_pv(t0, 0, True)
        logits(jnp.minimum(t0 + 2, last), 0)
        softmax_pv(t1, 1, t0 + 1 < n_t)
        return carry

    lax.fori_loop(0, (n_t + 1) // 2, attn_pair, 0)
    for hd in range(N_HEADS):
        g, j = divmod(hd, GQA_GROUP)
        o_ref[0, :, hd * HEAD_DIM:(hd + 1) * HEAD_DIM] = (
            acc_ref[g, j * tq:(j + 1) * tq, 0:HEAD_DIM] / acc_ref[g, j * tq:(j + 1) * tq, HEAD_DIM:2 * HEAD_DIM])


def _attn_t_kernel(qst_ref, wrow_ref, q_ref, ki_ref, kt_ref, v_ref, o_ref,
                   sc_ref, mm_ref, st_ref, s_ref, m_ref, acc_ref, bias_ref, p_ref, alpha_ref, kmax_ref,
                   *, tq, ts, q0, n_keys, k_sel):
    qi = pl.program_id(1)
    n_tiles_total = ki_ref.shape[1]
    nc = ts // LANES
    rk = KEY_ROWS
    qpos = q0 + qi * tq + lax.broadcasted_iota(I32, (1, tq), 1)
    n_adm = jnp.minimum((qpos // CHUNK + 1) * CHUNK, n_keys)
    last_adm = jnp.minimum(((q0 + (qi + 1) * tq - 1) // CHUNK + 1) * CHUNK, n_keys)
    n_t = jnp.minimum((last_adm + ts - 1) // ts, n_tiles_total)
    n_full = jnp.minimum(jnp.minimum(((q0 + qi * tq) // CHUNK + 1) * CHUNK, n_keys) // ts, n_t)
    last_full = jnp.maximum(n_full - 1, 0)
    qst = qst_ref[0, 0]
    wr = wrow_ref[0, 0]

    mm_ref[0] = jnp.full((rk, tq), _POS_INF, F32)
    mm_ref[1] = jnp.full((rk, tq), _NEG_INF, F32)
    mm_ref[2] = jnp.zeros((rk, tq), F32)

    def idx_logits(t, slot):
        st_ref[slot] = jnp.dot(ki_ref[0, t], qst, preferred_element_type=F32)

    def score_tile(t, slot, masked):
        mn, mx, n_pos = mm_ref[0], mm_ref[1], mm_ref[2]
        for r0 in range(0, ts, rk):
            acc = None
            for hd in range(N_IDX_HEADS):
                r = jnp.maximum(st_ref[slot, r0:r0 + rk, hd * tq:(hd + 1) * tq], 0.0) * wr[hd:hd + 1, :]
                acc = r if acc is None else acc + r
            key = _key_of(acc)
            if masked:
                adm = (t * ts + r0 + lax.broadcasted_iota(I32, (rk, tq), 0)) < n_adm
                key = jnp.where(adm, key, _INADMISSIBLE)
                mx = jnp.maximum(mx, jnp.where(adm, acc, _NEG_INF))
                mn = jnp.minimum(mn, jnp.where(adm, acc, _POS_INF))
            else:
                mx = jnp.maximum(mx, acc)
                mn = jnp.minimum(mn, acc)
            n_pos = n_pos + jnp.where(key >= _KEY_MIN_NORMAL, 1.0, 0.0)
            sc_ref[t, r0:r0 + rk, :] = key
        mm_ref[0] = mn
        mm_ref[1] = mx
        mm_ref[2] = n_pos

    idx_logits(0, 0)

    def score_pair(u, carry):
        t0 = 2 * u
        idx_logits(t0 + 1, 1)
        score_tile(t0, 0, False)
        idx_logits(jnp.minimum(t0 + 2, last_full), 0)
        score_tile(t0 + 1, 1, False)
        return carry

    lax.fori_loop(0, n_full // 2, score_pair, 0)

    @pl.when(n_full % 2 == 1)
    def _():
        score_tile(last_full, 0, False)

    def score_tail(t, carry):
        idx_logits(t, 0)
        score_tile(t, 0, True)
        return carry

    lax.fori_loop(n_full, n_t, score_tail, 0)
    rmin = jnp.min(mm_ref[0], axis=0, keepdims=True)
    rmax = jnp.max(mm_ref[1], axis=0, keepdims=True)

    def count_ge(thr):
        def body(t, cnt):
            for r0 in range(0, ts, COUNT_ROWS):
                cnt = cnt + jnp.where(sc_ref[t, r0:r0 + COUNT_ROWS, :] >= thr, 1.0, 0.0)
            return cnt
        return jnp.sum(lax.fori_loop(0, n_t, body, jnp.zeros((COUNT_ROWS, tq), F32)), axis=0, keepdims=True)

    def bis_cond(st):
        return st[0] > 0

    kf = float(k_sel)
    take_all = n_adm <= k_sel
    lo0 = _key_of(rmin) - 1
    hi0 = _key_of(rmax) + 2
    c_pos = jnp.sum(mm_ref[2], axis=0, keepdims=True)
    pos = c_pos >= kf
    lo = jnp.where(pos, _KEY_MIN_NORMAL, jnp.minimum(lo0, _KEY_NEG_ZERO - 1))
    hi = jnp.where(pos, hi0, _KEY_MIN_NORMAL)
    clo = jnp.where(pos, c_pos, n_adm.astype(F32))
    chi = jnp.where(pos, 0.0, c_pos)
    exact0 = jnp.where(clo == kf, 1, 0)
    act0 = jnp.where(take_all, 0, jnp.where(exact0 > 0, 0, jnp.where(hi > lo + 1, 1, 0)))

    def bis_step(it, lo, hi, clo, chi, exact, act):
        kmid = (lo & hi) + ((lo ^ hi) >> 1)
        vmid = _key_of((_float_of(lo) + _float_of(hi)) * 0.5)
        by_value = jnp.where(it < VALUE_PIVOT_STEPS, jnp.where(vmid > lo, jnp.where(vmid < hi, 1, 0), 0), 0)
        mid = jnp.where(by_value > 0, vmid, kmid)
        at_zero = jnp.where(lo < _KEY_NEG_ZERO, jnp.where(hi == _KEY_MIN_NORMAL, 1, 0), 0)
        mid = jnp.where(at_zero > 0, _KEY_NEG_ZERO, mid)
        cnt = count_ge(mid)
        up = jnp.where(act > 0, jnp.where(cnt >= kf, 1, 0), 0)
        dn = act - up
        lo = jnp.where(up > 0, mid, lo)
        clo = jnp.where(up > 0, cnt, clo)
        hi = jnp.where(dn > 0, mid, hi)
        chi = jnp.where(dn > 0, cnt, chi)
        hit = jnp.where(up > 0, jnp.where(cnt == kf, 1, 0), 0)
        exact = exact + hit
        act = jnp.where(act > 0, jnp.where(hit > 0, 0, jnp.where(hi > lo + 1, 1, 0)), 0)
        return it + 1, lo, hi, clo, chi, exact, act

    def any_active(act):
        return jnp.max(act.astype(F32))

    def bis_body(st):
        st = bis_step(*bis_step(*st[1:]))
        return (any_active(st[-1]),) + st

    warm = jnp.where(n_t > 2, UNTESTED_STEPS // 2, 0)
    st0 = lax.fori_loop(0, warm, lambda _, st: bis_step(*bis_step(*st)),
                        (jnp.int32(0), lo, hi, clo, chi, exact0, act0))
    _, _, lo, _, clo, chi, exact, _ = lax.while_loop(bis_cond, bis_body, (any_active(st0[-1]),) + st0)
    thr = jnp.where(take_all, _INADMISSIBLE + 1, lo)
    thr_up = jnp.where(take_all, _INADMISSIBLE + 1, lo + 1)

    need = kf - chi
    tied = jnp.where(take_all, 0, jnp.where(exact > 0, 0, jnp.where(clo > kf, 1, 0)))

    def count_tied_below(cut):
        def body(t, cnt):
            for r0 in range(0, ts, COUNT_ROWS):
                kidx = t * ts + r0 + lax.broadcasted_iota(I32, (COUNT_ROWS, tq), 0)
                cnt = cnt + jnp.where(sc_ref[t, r0:r0 + COUNT_ROWS, :] == thr,
                                      jnp.where(kidx < cut, 1.0, 0.0), 0.0)
            return cnt
        return jnp.sum(lax.fori_loop(0, n_t, body, jnp.zeros((COUNT_ROWS, tq), F32)), axis=0, keepdims=True)

    def cut_body(st):
        _, lo_c, hi_c, act = st
        mid = (lo_c + hi_c) >> 1
        cnt = count_tied_below(mid)
        on = act > 0
        le = cnt <= need
        lo_c = jnp.where(on, jnp.where(le, mid, lo_c), lo_c)
        hi_c = jnp.where(on, jnp.where(le, hi_c, mid), hi_c)
        act = jnp.where(on, jnp.where(hi_c - lo_c > 1, 1, 0), 0)
        return any_active(act), lo_c, hi_c, act

    _, cut_lo, _, _ = lax.while_loop(
        bis_cond, cut_body,
        (any_active(tied), jnp.zeros((1, tq), I32), jnp.full((1, tq), n_tiles_total * ts + 1, I32), tied))
    cut = jnp.where(tied > 0, cut_lo, n_tiles_total * ts + 1)

    acc_ref[...] = jnp.zeros(acc_ref.shape, F32)
    rc = min(tq, ROW_CHUNK)
    ones = jnp.ones((ts, LANES), BF16)
    last = n_t - 1

    def logits(t, slot):
        for g in range(N_KV_HEADS):
            qg = q_ref[0, g * GQA_GROUP:(g + 1) * GQA_GROUP].reshape(GQA_GROUP * tq, HEAD_DIM)
            s_ref[slot, g * GQA_GROUP * tq:(g + 1) * GQA_GROUP * tq] = jnp.dot(
                qg, kt_ref[0, g, t], preferred_element_type=F32)

    def selection_bias(t, c, live, masked_value):
        kidx = t * ts + c * LANES + lax.broadcasted_iota(I32, (LANES, tq), 0)
        bound = jnp.where(live, jnp.where(kidx < cut, thr, thr_up), _NO_KEY)
        return jnp.where(sc_ref[t, c * LANES:(c + 1) * LANES, :] >= bound, 0.0, masked_value).T

    def value_matmul(t, slot, g):
        v_aug = jnp.concatenate([v_ref[0, g, t], ones], axis=-1)
        return jnp.dot(p_ref[slot, g], v_aug, preferred_element_type=F32)

    def softmax_pv(t, slot, live):
        for c in range(nc):
            bias_ref[slot, :, c * LANES:(c + 1) * LANES] = selection_bias(t, c, live, _NEG_INF)
        for g in range(N_KV_HEADS):
            for j in range(GQA_GROUP):
                hd = g * GQA_GROUP + j
                for r0 in range(0, tq, rc):
                    r1 = j * tq + r0
                    r2 = hd * tq + r0
                    s = [s_ref[slot, r2:r2 + rc, c * LANES:(c + 1) * LANES]
                         + bias_ref[slot, r0:r0 + rc, c * LANES:(c + 1) * LANES] for c in range(nc)]
                    mx = s[0]
                    for c in range(1, nc):
                        mx = jnp.maximum(mx, s[c])
                    m_old = m_ref[hd, r0:r0 + rc]
                    m_new = jnp.maximum(m_old, jnp.broadcast_to(jnp.max(mx, axis=-1, keepdims=True), (rc, LANES)))
                    alpha_ref[slot, g, r1:r1 + rc] = jnp.exp2(m_old - m_new)
                    m_ref[hd, r0:r0 + rc] = m_new
                    for c in range(nc):
                        p_ref[slot, g, r1:r1 + rc, c * LANES:(c + 1) * LANES] = jnp.exp2(s[c] - m_new).astype(BF16)
            pv = value_matmul(t, slot, g)
            for half in range(2):
                cols = slice(half * LANES, (half + 1) * LANES)
                acc_ref[g, :, cols] = alpha_ref[slot, g] * acc_ref[g, :, cols] + pv[:, cols]

    def attend_running():
        def attn_pair(u, carry):
            t0 = 2 * u
            t1 = jnp.minimum(t0 + 1, last)
            logits(t1, 1)
            softmax_pv(t0, 0, True)
            logits(jnp.minimum(t0 + 2, last), 0)
            softmax_pv(t1, 1, t0 + 1 < n_t)
            return carry

        lax.fori_loop(0, (n_t + 1) // 2, attn_pair, 0)

    def softmax_fixed(t, slot):
        for c in range(nc):
            bias_ref[slot, :, c * LANES:(c + 1) * LANES] = selection_bias(t, c, True, _NEG_INF)
        for g in range(N_KV_HEADS):
            for j in range(GQA_GROUP):
                hd = g * GQA_GROUP + j
                for r0 in range(0, tq, rc):
                    ref_pt = m_ref[hd, r0:r0 + rc]
                    for c in range(nc):
                        cs = slice(c * LANES, (c + 1) * LANES)
                        s = s_ref[slot, hd * tq + r0:hd * tq + r0 + rc, cs] + bias_ref[slot, r0:r0 + rc, cs]
                        p_ref[slot, g, j * tq + r0:j * tq + r0 + rc, cs] = jnp.exp2(s - ref_pt).astype(BF16)

    def attend_fixed():
        def accumulate(g, pv):
            for half in range(2):
                cols = slice(half * LANES, (half + 1) * LANES)
                acc_ref[g, :, cols] = acc_ref[g, :, cols] + pv[:, cols]

        def attn_pair(u, carry):
            t0 = 2 * u
            logits(t0 + 1, 1)
            softmax_fixed(t0, 0)
            logits(jnp.minimum(t0 + 2, last), 0)
            softmax_fixed(t0 + 1, 1)
            for g in range(N_KV_HEADS):
                accumulate(g, value_matmul(t0, 0, g) + value_matmul(t0 + 1, 1, g))
            return carry

        lax.fori_loop(0, n_t // 2, attn_pair, 0)

        @pl.when(n_t % 2 == 1)
        def _():
            softmax_fixed(last, 0)
            for g in range(N_KV_HEADS):
                accumulate(g, value_matmul(last, 0, g))

    @pl.when(qi == 0)
    def _():
        for g in range(N_KV_HEADS):
            def sq_norm_max(t, best):
                kk = kt_ref[0, g, t].astype(F32)
                return jnp.maximum(best, jnp.sum(kk * kk, axis=0, keepdims=True))
            best = lax.fori_loop(0, n_tiles_total, sq_norm_max, jnp.zeros((1, ts), F32))
            kmax_ref[g] = jnp.broadcast_to(jnp.max(best, axis=-1, keepdims=True), (8, LANES))

    logits(0, 0)
    top = jnp.zeros((tq, LANES), F32)
    for hd in range(N_HEADS):
        qf = q_ref[0, hd].astype(F32)
        q_sq = _row_sum(qf * qf)
        ref_pt = jnp.sqrt(q_sq * kmax_ref[hd // GQA_GROUP, 0:1, :]) * 1.001 + 1e-3
        m_ref[hd] = ref_pt
        top = jnp.maximum(top, ref_pt)
    fits = jnp.max(top) <= MAX_FIXED_REFERENCE

    @pl.when(fits)
    def _():
        attend_fixed()

    @pl.when(jnp.logical_not(fits))
    def _():
        m_ref[...] = jnp.full(m_ref.shape, _M_INIT, F32)
        attend_running()

    for hd in range(N_HEADS):
        g, j = divmod(hd, GQA_GROUP)
        o_ref[0, :, hd * HEAD_DIM:(hd + 1) * HEAD_DIM] = (
            acc_ref[g, j * tq:(j + 1) * tq, 0:HEAD_DIM] / acc_ref[g, j * tq:(j + 1) * tq, HEAD_DIM:2 * HEAD_DIM])


def _attend_t(qst, wrow, q_att, ki4, kt, v, q0, n_keys, tq, ts):
    B, _, T, _ = q_att.shape
    n_tiles = ki4.shape[1]
    k_sel = min(TOPK_MAX, n_keys // 4)
    kern = functools.partial(_attn_t_kernel, tq=tq, ts=ts, q0=q0, n_keys=n_keys, k_sel=k_sel)
    return pl.pallas_call(
        kern, grid=(B, T // tq),
        in_specs=[
            pl.BlockSpec((1, 1, 4 * IDX_DIM, N_IDX_HEADS * tq), lambda b, i: (b, i, 0, 0)),
            pl.BlockSpec((1, 1, N_IDX_HEADS, tq), lambda b, i: (b, i, 0, 0)),
            pl.BlockSpec((1, N_HEADS, tq, HEAD_DIM), lambda b, i: (b, 0, i, 0)),
            _resident((1, n_tiles, ts, 4 * IDX_DIM), lambda b, i: (b, 0, 0, 0)),
            _resident((1, N_KV_HEADS, n_tiles, HEAD_DIM, ts), lambda b, i: (b, 0, 0, 0, 0)),
            _resident((1, N_KV_HEADS, n_tiles, ts, HEAD_DIM), lambda b, i: (b, 0, 0, 0, 0)),
        ],
        out_specs=pl.BlockSpec((1, tq, N_HEADS * HEAD_DIM), lambda b, i: (b, i, 0)),
        out_shape=jax.ShapeDtypeStruct((B, T, N_HEADS * HEAD_DIM), F32),
        scratch_shapes=[
            pltpu.VMEM((n_tiles, ts, tq), I32),
            pltpu.VMEM((3, KEY_ROWS, tq), F32),
            pltpu.VMEM((2, ts, N_IDX_HEADS * tq), F32),
            pltpu.VMEM((2, N_HEADS * tq, ts), F32),
            pltpu.VMEM((N_HEADS, tq, LANES), F32),
            pltpu.VMEM((N_KV_HEADS, GQA_GROUP * tq, 2 * HEAD_DIM), F32),
            pltpu.VMEM((2, tq, ts), F32),
            pltpu.VMEM((2, N_KV_HEADS, GQA_GROUP * tq, ts), BF16),
            pltpu.VMEM((2, N_KV_HEADS, GQA_GROUP * tq, LANES), F32),
            pltpu.VMEM((N_KV_HEADS, 8, LANES), F32),
        ],
        name="attend_t",
        compiler_params=pltpu.CompilerParams(dimension_semantics=("arbitrary", "arbitrary"),
                                             vmem_limit_bytes=VMEM_LIMIT),
    )(qst, wrow, q_att, ki4, kt, v)


def _attend(qs, kiw, q_att, ki4, kt, v, q0, n_keys, tq, ts):
    B, _, T, _ = qs.shape
    lp = ki4.shape[1]
    n_tiles = lp // ts
    k_sel = min(TOPK_MAX, n_keys // 4)
    kern = functools.partial(_attn_kernel, tq=tq, ts=ts, q0=q0, n_keys=n_keys, k_sel=k_sel)
    keys = pl.BlockSpec((1, lp, 2 * LANES), lambda b, i: (b, 0, 0))
    return pl.pallas_call(
        kern, grid=(B, T // tq),
        in_specs=[
            pl.BlockSpec((1, N_IDX_HEADS, tq, 4 * IDX_DIM), lambda b, i: (b, 0, i, 0)),
            pl.BlockSpec((1, tq, LANES), lambda b, i: (b, i, 0)),
            pl.BlockSpec((1, N_HEADS, tq, HEAD_DIM), lambda b, i: (b, 0, i, 0)),
            keys, keys, keys,
        ],
        out_specs=pl.BlockSpec((1, tq, N_HEADS * HEAD_DIM), lambda b, i: (b, i, 0)),
        out_shape=jax.ShapeDtypeStruct((B, T, N_HEADS * HEAD_DIM), F32),
        scratch_shapes=[
            pltpu.VMEM((n_tiles, tq, ts), I32),
            pltpu.VMEM((2, tq, LANES), F32),
            pltpu.VMEM((N_IDX_HEADS, tq, LANES), F32),
            pltpu.VMEM((2, N_HEADS * tq, ts), F32),
            pltpu.VMEM((N_HEADS, tq, LANES), F32),
            pltpu.VMEM((N_KV_HEADS, GQA_GROUP * tq, 2 * HEAD_DIM), F32),
            pltpu.VMEM((2, tq, ts), F32),
            pltpu.VMEM((2, N_KV_HEADS, GQA_GROUP * tq, ts), BF16),
            pltpu.VMEM((2, N_KV_HEADS, GQA_GROUP * tq, LANES), F32),
        ],
        name="attend",
        compiler_params=pltpu.CompilerParams(dimension_semantics=("arbitrary", "arbitrary"),
                                             vmem_limit_bytes=VMEM_LIMIT),
    )(qs, kiw, q_att, ki4, kt, v)


def _rms(x, g):
    return x * lax.rsqrt(jnp.mean(x * x, axis=-1, keepdims=True) + RMS_EPS) * g


def _out_kernel(x_ref, pa_ref, sgb_ref, b_ref, wo_ref, g1_ref, g2_ref, wgu_ref, wd_ref, g3_ref, y_ref, *, d_ff):
    merged = pa_ref[...] + sgb_ref[...] * b_ref[...]
    mix = jnp.dot(merged.astype(BF16), wo_ref[...], preferred_element_type=F32)
    x1 = x_ref[...] + _rms(mix, g1_ref[...])
    h2 = _rms(x1, g2_ref[...]).astype(BF16)
    gu = jnp.dot(h2, wgu_ref[...], preferred_element_type=F32)
    gate, up = gu[:, :d_ff], gu[:, d_ff:]
    act = (gate * _sigmoid(gate) * up).astype(BF16)
    f = jnp.dot(act, wd_ref[...], preferred_element_type=F32)
    y_ref[...] = x1 + _rms(f, g3_ref[...])


def _finish(x, pa, sgb, b, w_out, g_post, g_ffn_pre, w_gate_up, w_down, g_ffn_post, tm):
    N, D = x.shape
    d_ff = w_down.shape[0]
    row = pl.BlockSpec((tm, D), lambda i: (i, 0))
    const = lambda i: (0, 0)
    return pl.pallas_call(
        functools.partial(_out_kernel, d_ff=d_ff), grid=(N // tm,),
        in_specs=[row, row, row, row,
                  _resident((D, D), const), _resident((1, D), const), _resident((1, D), const),
                  _resident((D, 2 * d_ff), const), _resident((d_ff, D), const), _resident((1, D), const)],
        out_specs=row, out_shape=jax.ShapeDtypeStruct((N, D), F32), name="finish",
        compiler_params=pltpu.CompilerParams(dimension_semantics=("arbitrary",), vmem_limit_bytes=VMEM_LIMIT),
    )(x, pa, sgb, b, w_out, g_post, g_ffn_pre, w_gate_up, w_down, g_ffn_post)


def _tile_keys(n_keys, ts):
    return -(-n_keys // ts) * ts


def _layer(x, pos0, hist, k_cache, v_cache, ki_cache, wts, tm, tq, ts, tm_out):
    (w_in_r, w_pool, pool_scale, w_out, w_gate_up, w_down, g_pre, g_post, g_ffn_pre, g_ffn_post) = wts
    B, T, D = x.shape
    hist16 = jnp.concatenate([jnp.zeros((B, HALO // 2 - POOL_HIST, POOL_WIDTH), F32), hist], axis=1)
    own_keys_only = k_cache is None and T % ts == 0 and ts % tm == 0 and tq == LANES and tm % LANES == 0
    outs = _project(x, hist16, pos0, w_in_r, g_pre, w_pool, pool_scale, tm, ts if own_keys_only else None)
    u, q_att, k, v, qs, kiw, pa, sgb = outs[:8]
    ki = kiw[:, :, :IDX_DIM]
    if own_keys_only:
        kt, vt, ki4, wrow = outs[8:]
        b = _attend_t(qs, wrow, q_att, ki4, kt, vt, pos0, T, tq, ts)
    else:
        if k_cache is None:
            k_all, v_all, ki_all = k, v, ki
        else:
            k_all = jnp.concatenate([k_cache.reshape(B, -1, N_KV_HEADS * HEAD_DIM), k], axis=1)
            v_all = jnp.concatenate([v_cache.reshape(B, -1, N_KV_HEADS * HEAD_DIM), v], axis=1)
            ki_all = jnp.concatenate([ki_cache, ki], axis=1)
        n_keys = k_all.shape[1]
        lp = _tile_keys(n_keys, ts)
        pad = ((0, 0), (0, lp - n_keys), (0, 0))
        kt = jnp.pad(k_all.astype(BF16), pad)
        vt = jnp.pad(v_all.astype(BF16), pad)
        ki4 = _split_keys(jnp.pad(ki_all, pad), ts)
        b = _attend(qs, kiw, q_att, ki4, kt, vt, pos0, n_keys, tq, ts)
    y = _finish(x.reshape(B * T, D), pa.reshape(B * T, D), sgb.reshape(B * T, D), b.reshape(B * T, D),
                w_out, g_post, g_ffn_pre, w_gate_up, w_down, g_ffn_post, tm_out).reshape(B, T, D)
    new_pool = jnp.concatenate([hist, u], axis=1)[:, T:]
    return (y, k.reshape(B, T, N_KV_HEADS, HEAD_DIM), v.reshape(B, T, N_KV_HEADS, HEAD_DIM), ki, new_pool)


def _relayout_w_in(w_in):
    d = w_in.shape[0]
    o_kiw = POOL_WIDTH + N_HEADS * HEAD_DIM + 2 * N_KV_HEADS * HEAD_DIM + N_IDX_HEADS * IDX_DIM
    narrow = IDX_DIM + N_IDX_HEADS
    padded = jnp.concatenate([w_in[:, :o_kiw + narrow], jnp.zeros((d, LANES - narrow), w_in.dtype),
                              w_in[:, o_kiw + narrow:]], axis=1)
    return padded.astype(BF16)


def kernel(x_prompt, x_sample, cache_k, cache_v, cache_k_idx, state_pool, w_in, w_pool, pool_scale, w_out,
           w_gate_up, w_down, norm_mix_pre, norm_mix_post, norm_ffn_pre, norm_ffn_post):
    depth = w_in.shape[0]
    past = cache_k.shape[2]
    t_p, t_s = x_prompt.shape[1], x_sample.shape[1]
    hist_p = jnp.zeros((x_prompt.shape[0], POOL_HIST, POOL_WIDTH), x_prompt.dtype)
    xp, xs = x_prompt, x_sample
    outs = [[] for _ in range(8)]
    for l in range(depth):
        wts = (_relayout_w_in(w_in[l]), w_pool[l].astype(BF16), pool_scale[l][None, :], w_out[l].astype(BF16),
               w_gate_up[l].astype(BF16), w_down[l].astype(BF16), norm_mix_pre[l][None, :],
               norm_mix_post[l][None, :], norm_ffn_pre[l][None, :], norm_ffn_post[l][None, :])
        tm_p = min(256, t_p)
        tq_p = min(128, t_p)
        xp, k1, v1, ki1, p1 = _layer(xp, 0, hist_p, None, None, None, wts, tm_p, tq_p, 512, tm_p)
        n_s = xs.shape[0] * t_s
        xs, k2, v2, ki2, p2 = _layer(xs, past, state_pool[l], cache_k[l], cache_v[l], cache_k_idx[l], wts,
                                     t_s, t_s, 512, min(256, n_s))
        for lst, val in zip(outs, (k1, v1, ki1, p1, k2, v2, ki2, p2)):
            lst.append(val)
    return (xp, xs) + tuple(jnp.stack(o) for o in outs)
```

```python
import functools

import jax
import jax.numpy as jnp
from jax import lax
from jax.experimental import pallas as pl
from jax.experimental.pallas import tpu as pltpu

F32 = jnp.float32
BF16 = jnp.bfloat16
I32 = jnp.int32

LANES = 128
CHUNK = 64
POOL_WINDOWS = (2, 4, 8, 16)
N_POOL_GROUPS = 4
POOL_GROUP_WIDTH = 128
POOL_WIDTH = N_POOL_GROUPS * POOL_GROUP_WIDTH
POOL_HIST = 15
N_HEADS = 8
N_KV_HEADS = 2
HEAD_DIM = 128
GQA_GROUP = N_HEADS // N_KV_HEADS
ROPE_THETA = 500000.0
N_IDX_HEADS = 8
IDX_DIM = 64
TOPK_MAX = 256
RMS_EPS = 1e-6
ATTN_SCALE = HEAD_DIM ** -0.5
IDX_SCALE = (N_IDX_HEADS ** -0.5) * (IDX_DIM ** -0.5)

LOG2_E = 1.4426950408889634
VALUE_PIVOT_STEPS = 28
MAX_FIXED_REFERENCE = 40.0
SPLIT_ROWS = 1536
UNTESTED_STEPS = 16
KEY_ROWS = 16
COUNT_ROWS = 32
ROW_CHUNK = 32
HALO = 32
VMEM_LIMIT = 56 * 1024 * 1024

_NEG_INF = float("-inf")
_POS_INF = float("inf")
_INADMISSIBLE = -(2 ** 31)
_NO_KEY = 2 ** 31 - 1
_KEY_MIN_NORMAL = 0x00800000
_KEY_NEG_ZERO = -1
_M_INIT = -1e30


def _resident(block_shape, index_map):
    return pl.BlockSpec(block_shape, index_map, pipeline_mode=pl.Buffered(1))


def _rope(xs, cos, sin, half, period):
    lane = lax.broadcasted_iota(I32, xs.shape, 1)
    ahead = pltpu.roll(xs, LANES - half, 1)
    behind = pltpu.roll(xs, half, 1)
    partner = jnp.where((lane & (period - 1)) < half, ahead, behind)
    return xs * cos + partner * sin


def _sigmoid(x):
    return 1.0 / (1.0 + jnp.exp(-x))


def _proj_kernel(x_ref, g_ref, w_ref, hist_ref, cq_ref, sq_ref, ci_ref, si_ref, wpool_ref, pscale_ref,
                 u_ref, qatt_ref, k_ref, v_ref, qs_ref, kiw_ref, pa_ref, sgb_ref, *rest, tm, pos0, d_model, key_tiles):
    if key_tiles:
        kt_ref, vt_ref, ki4_ref, wrow_ref, e_ref, s2_ref, s4_ref, s8_ref = rest
    else:
        e_ref, s2_ref, s4_ref, s8_ref = rest
    i = pl.program_id(1)
    x = x_ref[0]
    h = x * lax.rsqrt(jnp.mean(x * x, axis=-1, keepdims=True) + RMS_EPS) * g_ref[...]
    o_q = POOL_WIDTH
    o_k = o_q + N_HEADS * HEAD_DIM
    o_v = o_k + N_KV_HEADS * HEAD_DIM
    o_qi = o_v + N_KV_HEADS * HEAD_DIM
    o_kiw = o_qi + N_IDX_HEADS * IDX_DIM
    o_ga = o_kiw + LANES
    o_gb = o_ga + d_model

    proj = jnp.dot(h.astype(BF16), w_ref[...], preferred_element_type=F32)

    cq, sq, ci, si = cq_ref[...], sq_ref[...], ci_ref[...], si_ref[...]
    half_q = HEAD_DIM // 8
    half_i = IDX_DIM // 8

    for hd in range(N_HEADS):
        qh = _rope(proj[:, o_q + hd * HEAD_DIM:o_q + (hd + 1) * HEAD_DIM], cq, sq, half_q, HEAD_DIM)
        qatt_ref[0, hd] = (qh * (ATTN_SCALE * LOG2_E)).astype(BF16)
    for kh in range(N_KV_HEADS):
        k_h = _rope(proj[:, o_k + kh * HEAD_DIM:o_k + (kh + 1) * HEAD_DIM], cq, sq, half_q, HEAD_DIM)
        k_ref[0, :, kh * HEAD_DIM:(kh + 1) * HEAD_DIM] = k_h
        if key_tiles:
            kt_ref[0, kh, 0] = k_h.T.astype(BF16)
            vt_ref[0, kh, 0] = proj[:, o_v + kh * HEAD_DIM:o_v + (kh + 1) * HEAD_DIM].astype(BF16)
    v_ref[0] = proj[:, o_v:o_qi]

    for pr in range(N_IDX_HEADS // 2):
        qi2 = _rope(proj[:, o_qi + pr * LANES:o_qi + (pr + 1) * LANES], ci, si, half_i, IDX_DIM)
        hi = qi2.astype(BF16).astype(F32)
        lo = qi2 - hi
        lane2 = lax.broadcasted_iota(I32, qi2.shape, 1)
        first = jnp.where(lane2 < IDX_DIM, hi, pltpu.roll(lo, IDX_DIM, 1))
        second = jnp.where(lane2 < IDX_DIM, pltpu.roll(hi, IDX_DIM, 1), lo)
        for sub, slab in enumerate((first, second)):
            hd = 2 * pr + sub
            if key_tiles:
                for sb in range(tm // LANES):
                    slab_t = slab[sb * LANES:(sb + 1) * LANES].T.astype(BF16)
                    qs_ref[0, sb, 0:LANES, hd * LANES:(hd + 1) * LANES] = slab_t
                    qs_ref[0, sb, LANES:2 * LANES, hd * LANES:(hd + 1) * LANES] = slab_t
            else:
                qs_ref[0, hd, :, 0:LANES] = slab.astype(BF16)
                qs_ref[0, hd, :, LANES:2 * LANES] = slab.astype(BF16)

    kiw = _rope(proj[:, o_kiw:o_kiw + LANES], ci, si, half_i, IDX_DIM)
    lane = lax.broadcasted_iota(I32, kiw.shape, 1)
    kiw_full = jnp.where(lane < IDX_DIM, kiw, proj[:, o_kiw:o_kiw + LANES] * IDX_SCALE)
    kiw_ref[0] = kiw_full
    if key_tiles:
        ki_hi = kiw.astype(BF16).astype(F32)
        ki_lo = kiw - ki_hi
        ki4_ref[0, 0, :, 0:LANES] = jnp.where(lane < IDX_DIM, ki_hi, pltpu.roll(ki_hi, IDX_DIM, 1)).astype(BF16)
        ki4_ref[0, 0, :, LANES:2 * LANES] = jnp.where(lane < IDX_DIM, ki_lo, pltpu.roll(ki_lo, IDX_DIM, 1)).astype(BF16)
        for sb in range(tm // LANES):
            wrow_ref[0, sb] = kiw_full[sb * LANES:(sb + 1) * LANES].T[IDX_DIM:IDX_DIM + N_IDX_HEADS]

    u = proj[:, 0:POOL_WIDTH]
    u_ref[0] = u

    @pl.when(i == 0)
    def _():
        e_ref[0:HALO // 2, :] = jnp.zeros((HALO // 2, POOL_WIDTH), F32)
        e_ref[HALO // 2:HALO, :] = hist_ref[0]

    e_ref[HALO:HALO + tm, :] = u
    n2, n4, n8 = tm + 24, tm + 16, tm + 8
    s2_ref[8:8 + n2, :] = e_ref[8:8 + n2, :] + e_ref[7:7 + n2, :]
    s4_ref[16:16 + n4, :] = s2_ref[16:16 + n4, :] + s2_ref[14:14 + n4, :]
    s8_ref[24:24 + n8, :] = s4_ref[24:24 + n8, :] + s4_ref[20:20 + n8, :]
    s16 = s8_ref[HALO:HALO + tm, :] + s8_ref[HALO - 8:HALO - 8 + tm, :]
    wins = (s2_ref[HALO:HALO + tm, :], s4_ref[HALO:HALO + tm, :], s8_ref[HALO:HALO + tm, :], s16)
    e_ref[HALO // 2:HALO, :] = e_ref[HALO // 2 + tm:HALO + tm, :]

    pos = pos0 + i * tm + lax.broadcasted_iota(I32, (tm, POOL_GROUP_WIDTH), 0)
    a_parts = []
    for g, w in enumerate(POOL_WINDOWS):
        lo_l, hi_l = g * POOL_GROUP_WIDTH, (g + 1) * POOL_GROUP_WIDTH
        cnt = jnp.minimum(pos + 1, w).astype(F32)
        pooled = wins[g][:, lo_l:hi_l] / cnt - u[:, lo_l:hi_l]
        a_parts.append(jnp.dot(pooled.astype(BF16), wpool_ref[g], preferred_element_type=F32))
    a = jnp.concatenate(a_parts, axis=-1) * pscale_ref[...]
    pa_ref[0] = _sigmoid(proj[:, o_ga:o_gb]) * a
    sgb_ref[0] = _sigmoid(proj[:, o_gb:o_gb + d_model])


def _project(x, hist16, pos0, w_in_r, g_pre, w_pool, pool_scale, tm, key_tile=None):
    B, T, D = x.shape
    W = w_in_r.shape[1]
    pos = pos0 + jnp.arange(T, dtype=I32)
    cq, sq = _rope_tables(pos, HEAD_DIM)
    ci, si = _rope_tables(pos, IDX_DIM)
    row = lambda b, i: (b, i, 0)
    tab = lambda b, i: (i, 0)
    const2 = lambda b, i: (0, 0)
    out_shape = (
        jax.ShapeDtypeStruct((B, T, POOL_WIDTH), F32),
        jax.ShapeDtypeStruct((B, N_HEADS, T, HEAD_DIM), BF16),
        jax.ShapeDtypeStruct((B, T, N_KV_HEADS * HEAD_DIM), F32),
        jax.ShapeDtypeStruct((B, T, N_KV_HEADS * HEAD_DIM), F32),
        jax.ShapeDtypeStruct((B, N_IDX_HEADS, T, 4 * IDX_DIM), BF16),
        jax.ShapeDtypeStruct((B, T, LANES), F32),
        jax.ShapeDtypeStruct((B, T, D), F32),
        jax.ShapeDtypeStruct((B, T, D), F32),
    )
    out_specs = (
        pl.BlockSpec((1, tm, POOL_WIDTH), row),
        pl.BlockSpec((1, N_HEADS, tm, HEAD_DIM), lambda b, i: (b, 0, i, 0)),
        pl.BlockSpec((1, tm, N_KV_HEADS * HEAD_DIM), row),
        pl.BlockSpec((1, tm, N_KV_HEADS * HEAD_DIM), row),
        pl.BlockSpec((1, N_IDX_HEADS, tm, 4 * IDX_DIM), lambda b, i: (b, 0, i, 0)),
        pl.BlockSpec((1, tm, LANES), row),
        pl.BlockSpec((1, tm, D), row),
        pl.BlockSpec((1, tm, D), row),
    )
    in_specs = [
        pl.BlockSpec((1, tm, D), row),
        _resident((1, D), const2),
        _resident((D, W), const2),
        pl.BlockSpec((1, HALO // 2, POOL_WIDTH), lambda b, i: (b, 0, 0)),
        pl.BlockSpec((tm, LANES), tab), pl.BlockSpec((tm, LANES), tab),
        pl.BlockSpec((tm, LANES), tab), pl.BlockSpec((tm, LANES), tab),
        _resident((N_POOL_GROUPS, POOL_GROUP_WIDTH, D // N_POOL_GROUPS), lambda b, i: (0, 0, 0)),
        _resident((1, D), const2),
    ]
    if key_tile is not None:
        assert key_tile % tm == 0 and T % key_tile == 0 and tm % LANES == 0
        per = key_tile // tm
        n_tiles = T // key_tile
        sub = tm // LANES
        out_shape = out_shape[:4] + (
            jax.ShapeDtypeStruct((B, T // LANES, 4 * IDX_DIM, N_IDX_HEADS * LANES), BF16),
        ) + out_shape[5:] + (
            jax.ShapeDtypeStruct((B, N_KV_HEADS, n_tiles, HEAD_DIM, key_tile), BF16),
            jax.ShapeDtypeStruct((B, N_KV_HEADS, n_tiles, key_tile, HEAD_DIM), BF16),
            jax.ShapeDtypeStruct((B, n_tiles, key_tile, 4 * IDX_DIM), BF16),
            jax.ShapeDtypeStruct((B, T // LANES, N_IDX_HEADS, LANES), F32),
        )
        out_specs = out_specs[:4] + (
            pl.BlockSpec((1, sub, 4 * IDX_DIM, N_IDX_HEADS * LANES), lambda b, i: (b, i, 0, 0)),
        ) + out_specs[5:] + (
            pl.BlockSpec((1, N_KV_HEADS, 1, HEAD_DIM, tm), lambda b, i: (b, 0, i // per, 0, i % per)),
            pl.BlockSpec((1, N_KV_HEADS, 1, tm, HEAD_DIM), lambda b, i: (b, 0, i // per, i % per, 0)),
            pl.BlockSpec((1, 1, tm, 4 * IDX_DIM), lambda b, i: (b, i // per, i % per, 0)),
            pl.BlockSpec((1, sub, N_IDX_HEADS, LANES), lambda b, i: (b, i, 0, 0)),
        )
    scratch = [pltpu.VMEM((HALO + tm, POOL_WIDTH), F32) for _ in range(4)]
    return pl.pallas_call(
        functools.partial(_proj_kernel, tm=tm, pos0=pos0, d_model=D, key_tiles=key_tile is not None),
        grid=(B, T // tm), in_specs=in_specs, out_specs=out_specs, out_shape=out_shape,
        scratch_shapes=scratch, name="proj",
        compiler_params=pltpu.CompilerParams(dimension_semantics=("arbitrary", "arbitrary"),
                                             vmem_limit_bytes=VMEM_LIMIT),
    )(x, g_pre, w_in_r, hist16, cq, sq, ci, si, w_pool, pool_scale)


def _rope_tables(pos, dim):
    rot = dim // 4
    half = rot // 2
    inv = ROPE_THETA ** (-jnp.arange(half, dtype=F32) / half)
    ang = pos.astype(F32)[:, None] * inv[None, :]
    cos, sin = jnp.cos(ang), jnp.sin(ang)
    rest = dim - rot
    n = pos.shape[0]
    c = jnp.concatenate([cos, cos, jnp.ones((n, rest), F32)], axis=-1)
    s = jnp.concatenate([-sin, sin, jnp.zeros((n, rest), F32)], axis=-1)
    return jnp.tile(c, (1, LANES // dim)), jnp.tile(s, (1, LANES // dim))


def _split_kernel(k_ref, out_ref):
    k = k_ref[0]
    hi = k.astype(BF16).astype(F32)
    lo = k - hi
    out_ref[0, :, 0:LANES] = jnp.concatenate([hi, hi], axis=-1).astype(BF16)
    out_ref[0, :, LANES:2 * LANES] = jnp.concatenate([lo, lo], axis=-1).astype(BF16)


def _split_keys(ki, ts):
    B, Lp, _ = ki.shape
    rows = ts * max(1, SPLIT_ROWS // ts)
    rows = rows if Lp % rows == 0 else ts
    return pl.pallas_call(
        _split_kernel, grid=(B, Lp // rows),
        in_specs=[pl.BlockSpec((1, rows, IDX_DIM), lambda b, t: (b, t, 0))],
        out_specs=pl.BlockSpec((1, rows, 4 * IDX_DIM), lambda b, t: (b, t, 0)),
        out_shape=jax.ShapeDtypeStruct((B, Lp, 4 * IDX_DIM), BF16), name="split_keys",
        compiler_params=pltpu.CompilerParams(dimension_semantics=("arbitrary", "arbitrary")),
    )(ki)


def _key_of(x):
    b = pltpu.bitcast(x, I32)
    return b ^ ((b >> 31) & 0x7FFFFFFF)


def _float_of(k):
    return pltpu.bitcast(k ^ ((k >> 31) & 0x7FFFFFFF), F32)


def _row_sum(x):
    return jnp.broadcast_to(jnp.sum(x, axis=-1, keepdims=True), x.shape)


def _attn_kernel(qs_ref, kiw_ref, q_ref, ki_ref, kt_ref, v_ref, o_ref,
                 sc_ref, mm_ref, wb_ref, s_ref, m_ref, acc_ref, bias_ref, p_ref, alpha_ref,
                 *, tq, ts, q0, n_keys, k_sel):
    qi = pl.program_id(1)
    n_tiles_total = ki_ref.shape[1] // ts
    nc = ts // LANES
    nt_dims = (((1,), (1,)), ((), ()))

    def key_rows(t):
        return pl.ds(pl.multiple_of(t * ts, ts), ts)

    row = lax.broadcasted_iota(I32, (tq, LANES), 0)
    lane = lax.broadcasted_iota(I32, (tq, LANES), 1)
    qpos = q0 + qi * tq + row
    n_adm = jnp.minimum((qpos // CHUNK + 1) * CHUNK, n_keys)
    last_adm = jnp.minimum(((q0 + (qi + 1) * tq - 1) // CHUNK + 1) * CHUNK, n_keys)
    n_t = jnp.minimum((last_adm + ts - 1) // ts, n_tiles_total)

    kiw = kiw_ref[0]
    for hd in range(N_IDX_HEADS):
        wb_ref[hd] = jnp.broadcast_to(kiw[:, IDX_DIM + hd:IDX_DIM + hd + 1], (tq, LANES))
    qs2 = qs_ref[0].reshape(N_IDX_HEADS * tq, 4 * IDX_DIM)

    def idx_logits(t, slot):
        s_ref[slot] = lax.dot_general(qs2, ki_ref[0, key_rows(t), :], nt_dims,
                                      preferred_element_type=F32)

    rc = min(tq, ROW_CHUNK)
    mm_ref[0] = jnp.full((tq, LANES), _POS_INF, F32)
    mm_ref[1] = jnp.full((tq, LANES), _NEG_INF, F32)

    def score_tile(t, slot, masked):
        for r0 in range(0, tq, rc):
            rows = slice(r0, r0 + rc)
            mn, mx = mm_ref[0, rows], mm_ref[1, rows]
            for c in range(nc):
                cs = slice(c * LANES, (c + 1) * LANES)
                acc = None
                for hd in range(N_IDX_HEADS):
                    r = jnp.maximum(s_ref[slot, hd * tq + r0:hd * tq + r0 + rc, cs], 0.0) * wb_ref[hd, rows]
                    acc = r if acc is None else acc + r
                key = _key_of(acc)
                if masked:
                    qpos_c = q0 + qi * tq + r0 + lax.broadcasted_iota(I32, (rc, LANES), 0)
                    n_adm_c = jnp.minimum((qpos_c // CHUNK + 1) * CHUNK, n_keys)
                    adm = (t * ts + c * LANES + lax.broadcasted_iota(I32, (rc, LANES), 1)) < n_adm_c
                    key = jnp.where(adm, key, _INADMISSIBLE)
                    mx = jnp.maximum(mx, jnp.where(adm, acc, _NEG_INF))
                    mn = jnp.minimum(mn, jnp.where(adm, acc, _POS_INF))
                else:
                    mx = jnp.maximum(mx, acc)
                    mn = jnp.minimum(mn, acc)
                sc_ref[t, rows, cs] = key
            mm_ref[0, rows] = mn
            mm_ref[1, rows] = mx

    n_full = jnp.minimum(jnp.minimum(((q0 + qi * tq) // CHUNK + 1) * CHUNK, n_keys) // ts, n_t)
    last_full = jnp.maximum(n_full - 1, 0)
    idx_logits(0, 0)

    def score_pair(u, carry):
        t0 = 2 * u
        t1 = jnp.minimum(t0 + 1, last_full)
        idx_logits(t1, 1)
        score_tile(t0, 0, False)
        idx_logits(jnp.minimum(t0 + 2, last_full), 0)
        score_tile(t1, 1, False)
        return carry

    lax.fori_loop(0, (n_full + 1) // 2, score_pair, 0)

    def score_tail(t, carry):
        idx_logits(t, 0)
        score_tile(t, 0, True)
        return carry

    lax.fori_loop(n_full, n_t, score_tail, 0)
    rmin = jnp.broadcast_to(jnp.min(mm_ref[0], axis=-1, keepdims=True), (tq, LANES))
    rmax = jnp.broadcast_to(jnp.max(mm_ref[1], axis=-1, keepdims=True), (tq, LANES))

    def count_ge(thr):
        def body(t, cnt):
            for c in range(nc):
                cnt = cnt + jnp.where(sc_ref[t, :, c * LANES:(c + 1) * LANES] >= thr, 1.0, 0.0)
            return cnt
        return _row_sum(lax.fori_loop(0, n_t, body, jnp.zeros((tq, LANES), F32)))

    def bis_cond(st):
        return st[0] > 0

    kf = float(k_sel)
    take_all = n_adm <= k_sel
    lo0 = _key_of(rmin) - 1
    hi0 = _key_of(rmax) + 2
    c_pos = count_ge(jnp.full((tq, LANES), _KEY_MIN_NORMAL, I32))
    c_nn = count_ge(jnp.full((tq, LANES), _KEY_NEG_ZERO, I32))
    pos = c_pos >= kf
    neg = c_nn < kf
    lo = jnp.where(pos, _KEY_MIN_NORMAL, jnp.where(neg, lo0, _KEY_NEG_ZERO))
    hi = jnp.where(pos, hi0, jnp.where(neg, _KEY_NEG_ZERO, _KEY_MIN_NORMAL))
    clo = jnp.where(pos, c_pos, jnp.where(neg, n_adm.astype(F32), c_nn))
    chi = jnp.where(pos, 0.0, jnp.where(neg, c_nn, c_pos))
    exact0 = jnp.where(clo == kf, 1, 0)
    act0 = jnp.where(take_all, 0, jnp.where(exact0 > 0, 0, jnp.where(hi > lo + 1, 1, 0)))

    def bis_step(it, lo, hi, clo, chi, exact, act):
        kmid = (lo & hi) + ((lo ^ hi) >> 1)
        vmid = _key_of((_float_of(lo) + _float_of(hi)) * 0.5)
        by_value = jnp.where(it < VALUE_PIVOT_STEPS, jnp.where(vmid > lo, jnp.where(vmid < hi, 1, 0), 0), 0)
        mid = jnp.where(by_value > 0, vmid, kmid)
        cnt = count_ge(mid)
        up = jnp.where(act > 0, jnp.where(cnt >= kf, 1, 0), 0)
        dn = act - up
        lo = jnp.where(up > 0, mid, lo)
        clo = jnp.where(up > 0, cnt, clo)
        hi = jnp.where(dn > 0, mid, hi)
        chi = jnp.where(dn > 0, cnt, chi)
        hit = jnp.where(up > 0, jnp.where(cnt == kf, 1, 0), 0)
        exact = exact + hit
        act = jnp.where(act > 0, jnp.where(hit > 0, 0, jnp.where(hi > lo + 1, 1, 0)), 0)
        return it + 1, lo, hi, clo, chi, exact, act

    def any_active(act):
        return jnp.max(act.astype(F32))

    def bis_body(st):
        st = bis_step(*bis_step(*st[1:]))
        return (any_active(st[-1]),) + st

    _, _, lo, _, clo, chi, exact, _ = lax.while_loop(
        bis_cond, bis_body, (any_active(act0), jnp.int32(0), lo, hi, clo, chi, exact0, act0))
    thr = jnp.where(take_all, _INADMISSIBLE + 1, lo)
    thr_up = jnp.where(take_all, _INADMISSIBLE + 1, lo + 1)

    need = kf - chi
    tied = jnp.where(take_all, 0, jnp.where(exact > 0, 0, jnp.where(clo > kf, 1, 0)))

    def count_tied_below(cut):
        def body(t, cnt):
            for c in range(nc):
                s = sc_ref[t, :, c * LANES:(c + 1) * LANES]
                col = t * ts + c * LANES + lane
                cnt = cnt + jnp.where(s == thr, jnp.where(col < cut, 1.0, 0.0), 0.0)
            return cnt
        return _row_sum(lax.fori_loop(0, n_t, body, jnp.zeros((tq, LANES), F32)))

    def cut_body(st):
        _, lo_c, hi_c, act = st
        mid = (lo_c + hi_c) >> 1
        cnt = count_tied_below(mid)
        on = act > 0
        le = cnt <= need
        lo_c = jnp.where(on, jnp.where(le, mid, lo_c), lo_c)
        hi_c = jnp.where(on, jnp.where(le, hi_c, mid), hi_c)
        act = jnp.where(on, jnp.where(hi_c - lo_c > 1, 1, 0), 0)
        return jnp.max(act), lo_c, hi_c, act

    _, cut_lo, _, _ = lax.while_loop(
        bis_cond, cut_body,
        (jnp.max(tied), jnp.zeros((tq, LANES), I32), jnp.full((tq, LANES), n_tiles_total * ts + 1, I32), tied))
    cut = jnp.where(tied > 0, cut_lo, n_tiles_total * ts + 1)

    m_ref[...] = jnp.full(m_ref.shape, _M_INIT, F32)
    acc_ref[...] = jnp.zeros(acc_ref.shape, F32)
    rc = min(tq, ROW_CHUNK)
    ones = jnp.ones((ts, LANES), BF16)
    last = n_t - 1

    def logits(t, slot):
        for g in range(N_KV_HEADS):
            qg = q_ref[0, g * GQA_GROUP:(g + 1) * GQA_GROUP].reshape(GQA_GROUP * tq, HEAD_DIM)
            s_ref[slot, g * GQA_GROUP * tq:(g + 1) * GQA_GROUP * tq] = lax.dot_general(
                qg, kt_ref[0, key_rows(t), g * HEAD_DIM:(g + 1) * HEAD_DIM], nt_dims,
                preferred_element_type=F32)

    def softmax_pv(t, slot, live):
        for c in range(nc):
            col = t * ts + c * LANES + lane
            bound = jnp.where(live, jnp.where(col < cut, thr, thr_up), _NO_KEY)
            bias_ref[slot, :, c * LANES:(c + 1) * LANES] = jnp.where(
                sc_ref[t, :, c * LANES:(c + 1) * LANES] >= bound, 0.0, _NEG_INF)
        for g in range(N_KV_HEADS):
            for j in range(GQA_GROUP):
                hd = g * GQA_GROUP + j
                for r0 in range(0, tq, rc):
                    r1 = j * tq + r0
                    r2 = hd * tq + r0
                    s = [s_ref[slot, r2:r2 + rc, c * LANES:(c + 1) * LANES]
                         + bias_ref[slot, r0:r0 + rc, c * LANES:(c + 1) * LANES] for c in range(nc)]
                    mx = s[0]
                    for c in range(1, nc):
                        mx = jnp.maximum(mx, s[c])
                    m_old = m_ref[hd, r0:r0 + rc]
                    m_new = jnp.maximum(m_old, jnp.broadcast_to(jnp.max(mx, axis=-1, keepdims=True), (rc, LANES)))
                    alpha_ref[slot, g, r1:r1 + rc] = jnp.exp2(m_old - m_new)
                    for c in range(nc):
                        p_ref[slot, g, r1:r1 + rc, c * LANES:(c + 1) * LANES] = jnp.exp2(s[c] - m_new).astype(BF16)
                    m_ref[hd, r0:r0 + rc] = m_new
            v_aug = jnp.concatenate([v_ref[0, key_rows(t), g * HEAD_DIM:(g + 1) * HEAD_DIM], ones], axis=-1)
            pv = jnp.dot(p_ref[slot, g], v_aug, preferred_element_type=F32)
            alpha = alpha_ref[slot, g]
            for half in range(2):
                cols = slice(half * LANES, (half + 1) * LANES)
                acc_ref[g, :, cols] = alpha * acc_ref[g, :, cols] + pv[:, cols]

    logits(0, 0)

    def attn_pair(u, carry):
        t0 = 2 * u
        t1 = jnp.minimum(t0 + 1, last)
        logits(t1, 1)
        softmax_pv(t0, 0, True)
        logits(jnp.minimum(t0 + 2, last), 0)
        softmax_pv(t1, 1, t0 + 1 < n_t)
        return carry

    lax.fori_loop(0, (n_t + 1) // 2, attn_pair, 0)
    for hd in range(N_HEADS):
        g, j = divmod(hd, GQA_GROUP)
        o_ref[0, :, hd * HEAD_DIM:(hd + 1) * HEAD_DIM] = (
            acc_ref[g, j * tq:(j + 1) * tq, 0:HEAD_DIM] / acc_ref[g, j * tq:(j + 1) * tq, HEAD_DIM:2 * HEAD_DIM])


def _attn_t_kernel(qst_ref, wrow_ref, q_ref, ki_ref, kt_ref, v_ref, o_ref,
                   sc_ref, mm_ref, st_ref, s_ref, m_ref, acc_ref, bias_ref, p_ref, alpha_ref, kmax_ref,
                   *, tq, ts, q0, n_keys, k_sel):
    qi = pl.program_id(1)
    n_tiles_total = ki_ref.shape[1]
    nc = ts // LANES
    rk = KEY_ROWS
    qpos = q0 + qi * tq + lax.broadcasted_iota(I32, (1, tq), 1)
    n_adm = jnp.minimum((qpos // CHUNK + 1) * CHUNK, n_keys)
    last_adm = jnp.minimum(((q0 + (qi + 1) * tq - 1) // CHUNK + 1) * CHUNK, n_keys)
    n_t = jnp.minimum((last_adm + ts - 1) // ts, n_tiles_total)
    n_full = jnp.minimum(jnp.minimum(((q0 + qi * tq) // CHUNK + 1) * CHUNK, n_keys) // ts, n_t)
    last_full = jnp.maximum(n_full - 1, 0)
    qst = qst_ref[0, 0]
    wr = wrow_ref[0, 0]

    mm_ref[0] = jnp.full((rk, tq), _POS_INF, F32)
    mm_ref[1] = jnp.full((rk, tq), _NEG_INF, F32)
    mm_ref[2] = jnp.zeros((rk, tq), F32)

    def idx_logits(t, slot):
        st_ref[slot] = jnp.dot(ki_ref[0, t], qst, preferred_element_type=F32)

    def score_tile(t, slot, masked):
        mn, mx, n_pos = mm_ref[0], mm_ref[1], mm_ref[2]
        for r0 in range(0, ts, rk):
            acc = None
            for hd in range(N_IDX_HEADS):
                r = jnp.maximum(st_ref[slot, r0:r0 + rk, hd * tq:(hd + 1) * tq], 0.0) * wr[hd:hd + 1, :]
                acc = r if acc is None else acc + r
            key = _key_of(acc)
            if masked:
                adm = (t * ts + r0 + lax.broadcasted_iota(I32, (rk, tq), 0)) < n_adm
                key = jnp.where(adm, key, _INADMISSIBLE)
                mx = jnp.maximum(mx, jnp.where(adm, acc, _NEG_INF))
                mn = jnp.minimum(mn, jnp.where(adm, acc, _POS_INF))
            else:
                mx = jnp.maximum(mx, acc)
                mn = jnp.minimum(mn, acc)
            n_pos = n_pos + jnp.where(key >= _KEY_MIN_NORMAL, 1.0, 0.0)
            sc_ref[t, r0:r0 + rk, :] = key
        mm_ref[0] = mn
        mm_ref[1] = mx
        mm_ref[2] = n_pos

    idx_logits(0, 0)

    def score_pair(u, carry):
        t0 = 2 * u
        idx_logits(t0 + 1, 1)
        score_tile(t0, 0, False)
        idx_logits(jnp.minimum(t0 + 2, last_full), 0)
        score_tile(t0 + 1, 1, False)
        return carry

    lax.fori_loop(0, n_full // 2, score_pair, 0)

    @pl.when(n_full % 2 == 1)
    def _():
        score_tile(last_full, 0, False)

    def score_tail(t, carry):
        idx_logits(t, 0)
        score_tile(t, 0, True)
        return carry

    lax.fori_loop(n_full, n_t, score_tail, 0)
    rmin = jnp.min(mm_ref[0], axis=0, keepdims=True)
    rmax = jnp.max(mm_ref[1], axis=0, keepdims=True)

    def count_ge(thr):
        def body(t, cnt):
            for r0 in range(0, ts, COUNT_ROWS):
                cnt = cnt + jnp.where(sc_ref[t, r0:r0 + COUNT_ROWS, :] >= thr, 1.0, 0.0)
            return cnt
        return jnp.sum(lax.fori_loop(0, n_t, body, jnp.zeros((COUNT_ROWS, tq), F32)), axis=0, keepdims=True)

    def bis_cond(st):
        return st[0] > 0

    kf = float(k_sel)
    take_all = n_adm <= k_sel
    lo0 = _key_of(rmin) - 1
    hi0 = _key_of(rmax) + 2
    c_pos = jnp.sum(mm_ref[2], axis=0, keepdims=True)
    pos = c_pos >= kf
    lo = jnp.where(pos, _KEY_MIN_NORMAL, jnp.minimum(lo0, _KEY_NEG_ZERO - 1))
    hi = jnp.where(pos, hi0, _KEY_MIN_NORMAL)
    clo = jnp.where(pos, c_pos, n_adm.astype(F32))
    chi = jnp.where(pos, 0.0, c_pos)
    exact0 = jnp.where(clo == kf, 1, 0)
    act0 = jnp.where(take_all, 0, jnp.where(exact0 > 0, 0, jnp.where(hi > lo + 1, 1, 0)))

    def bis_step(it, lo, hi, clo, chi, exact, act):
        kmid = (lo & hi) + ((lo ^ hi) >> 1)
        vmid = _key_of((_float_of(lo) + _float_of(hi)) * 0.5)
        by_value = jnp.where(it < VALUE_PIVOT_STEPS, jnp.where(vmid > lo, jnp.where(vmid < hi, 1, 0), 0), 0)
        mid = jnp.where(by_value > 0, vmid, kmid)
        at_zero = jnp.where(lo < _KEY_NEG_ZERO, jnp.where(hi == _KEY_MIN_NORMAL, 1, 0), 0)
        mid = jnp.where(at_zero > 0, _KEY_NEG_ZERO, mid)
        cnt = count_ge(mid)
        up = jnp.where(act > 0, jnp.where(cnt >= kf, 1, 0), 0)
        dn = act - up
        lo = jnp.where(up > 0, mid, lo)
        clo = jnp.where(up > 0, cnt, clo)
        hi = jnp.where(dn > 0, mid, hi)
        chi = jnp.where(dn > 0, cnt, chi)
        hit = jnp.where(up > 0, jnp.where(cnt == kf, 1, 0), 0)
        exact = exact + hit
        act = jnp.where(act > 0, jnp.where(hit > 0, 0, jnp.where(hi > lo + 1, 1, 0)), 0)
        return it + 1, lo, hi, clo, chi, exact, act

    def any_active(act):
        return jnp.max(act.astype(F32))

    def bis_body(st):
        st = bis_step(*bis_step(*st[1:]))
        return (any_active(st[-1]),) + st

    warm = jnp.where(n_t > 2, UNTESTED_STEPS // 2, 0)
    st0 = lax.fori_loop(0, warm, lambda _, st: bis_step(*bis_step(*st)),
                        (jnp.int32(0), lo, hi, clo, chi, exact0, act0))
    _, _, lo, _, clo, chi, exact, _ = lax.while_loop(bis_cond, bis_body, (any_active(st0[-1]),) + st0)
    thr = jnp.where(take_all, _INADMISSIBLE + 1, lo)
    thr_up = jnp.where(take_all, _INADMISSIBLE + 1, lo + 1)

    need = kf - chi
    tied = jnp.where(take_all, 0, jnp.where(exact > 0, 0, jnp.where(clo > kf, 1, 0)))

    def count_tied_below(cut):
        def body(t, cnt):
            for r0 in range(0, ts, COUNT_ROWS):
                kidx = t * ts + r0 + lax.broadcasted_iota(I32, (COUNT_ROWS, tq), 0)
                cnt = cnt + jnp.where(sc_ref[t, r0:r0 + COUNT_ROWS, :] == thr,
                                      jnp.where(kidx < cut, 1.0, 0.0), 0.0)
            return cnt
        return jnp.sum(lax.fori_loop(0, n_t, body, jnp.zeros((COUNT_ROWS, tq), F32)), axis=0, keepdims=True)

    def cut_body(st):
        _, lo_c, hi_c, act = st
        mid = (lo_c + hi_c) >> 1
        cnt = count_tied_below(mid)
        on = act > 0
        le = cnt <= need
        lo_c = jnp.where(on, jnp.where(le, mid, lo_c), lo_c)
        hi_c = jnp.where(on, jnp.where(le, hi_c, mid), hi_c)
        act = jnp.where(on, jnp.where(hi_c - lo_c > 1, 1, 0), 0)
        return any_active(act), lo_c, hi_c, act

    _, cut_lo, _, _ = lax.while_loop(
        bis_cond, cut_body,
        (any_active(tied), jnp.zeros((1, tq), I32), jnp.full((1, tq), n_tiles_total * ts + 1, I32), tied))
    cut = jnp.where(tied > 0, cut_lo, n_tiles_total * ts + 1)

    acc_ref[...] = jnp.zeros(acc_ref.shape, F32)
    rc = min(tq, ROW_CHUNK)
    ones = jnp.ones((ts, LANES), BF16)
    last = n_t - 1

    def logits(t, slot):
        for g in range(N_KV_HEADS):
            qg = q_ref[0, g * GQA_GROUP:(g + 1) * GQA_GROUP].reshape(GQA_GROUP * tq, HEAD_DIM)
            s_ref[slot, g * GQA_GROUP * tq:(g + 1) * GQA_GROUP * tq] = jnp.dot(
                qg, kt_ref[0, g, t], preferred_element_type=F32)

    def selection_bias(t, c, live, masked_value):
        kidx = t * ts + c * LANES + lax.broadcasted_iota(I32, (LANES, tq), 0)
        bound = jnp.where(live, jnp.where(kidx < cut, thr, thr_up), _NO_KEY)
        return jnp.where(sc_ref[t, c * LANES:(c + 1) * LANES, :] >= bound, 0.0, masked_value).T

    def value_matmul(t, slot, g):
        v_aug = jnp.concatenate([v_ref[0, g, t], ones], axis=-1)
        return jnp.dot(p_ref[slot, g], v_aug, preferred_element_type=F32)

    def softmax_pv(t, slot, live):
        for c in range(nc):
            bias_ref[slot, :, c * LANES:(c + 1) * LANES] = selection_bias(t, c, live, _NEG_INF)
        for g in range(N_KV_HEADS):
            for j in range(GQA_GROUP):
                hd = g * GQA_GROUP + j
                for r0 in range(0, tq, rc):
                    r1 = j * tq + r0
                    r2 = hd * tq + r0
                    s = [s_ref[slot, r2:r2 + rc, c * LANES:(c + 1) * LANES]
                         + bias_ref[slot, r0:r0 + rc, c * LANES:(c + 1) * LANES] for c in range(nc)]
                    mx = s[0]
                    for c in range(1, nc):
                        mx = jnp.maximum(mx, s[c])
                    m_old = m_ref[hd, r0:r0 + rc]
                    m_new = jnp.maximum(m_old, jnp.broadcast_to(jnp.max(mx, axis=-1, keepdims=True), (rc, LANES)))
                    alpha_ref[slot, g, r1:r1 + rc] = jnp.exp2(m_old - m_new)
                    m_ref[hd, r0:r0 + rc] = m_new
                    for c in range(nc):
                        p_ref[slot, g, r1:r1 + rc, c * LANES:(c + 1) * LANES] = jnp.exp2(s[c] - m_new).astype(BF16)
            pv = value_matmul(t, slot, g)
            for half in range(2):
                cols = slice(half * LANES, (half + 1) * LANES)
                acc_ref[g, :, cols] = alpha_ref[slot, g] * acc_ref[g, :, cols] + pv[:, cols]

    def attend_running():
        def attn_pair(u, carry):
            t0 = 2 * u
            t1 = jnp.minimum(t0 + 1, last)
            logits(t1, 1)
            softmax_pv(t0, 0, True)
            logits(jnp.minimum(t0 + 2, last), 0)
            softmax_pv(t1, 1, t0 + 1 < n_t)
            return carry

        lax.fori_loop(0, (n_t + 1) // 2, attn_pair, 0)

    def softmax_fixed(t, slot):
        for c in range(nc):
            bias_ref[slot, :, c * LANES:(c + 1) * LANES] = selection_bias(t, c, True, _NEG_INF)
        for g in range(N_KV_HEADS):
            for j in range(GQA_GROUP):
                hd = g * GQA_GROUP + j
                for r0 in range(0, tq, rc):
                    ref_pt = m_ref[hd, r0:r0 + rc]
                    for c in range(nc):
                        cs = slice(c * LANES, (c + 1) * LANES)
                        s = s_ref[slot, hd * tq + r0:hd * tq + r0 + rc, cs] + bias_ref[slot, r0:r0 + rc, cs]
                        p_ref[slot, g, j * tq + r0:j * tq + r0 + rc, cs] = jnp.exp2(s - ref_pt).astype(BF16)

    def attend_fixed():
        def accumulate(g, pv):
            for half in range(2):
                cols = slice(half * LANES, (half + 1) * LANES)
                acc_ref[g, :, cols] = acc_ref[g, :, cols] + pv[:, cols]

        def attn_pair(u, carry):
            t0 = 2 * u
            logits(t0 + 1, 1)
            softmax_fixed(t0, 0)
            logits(jnp.minimum(t0 + 2, last), 0)
            softmax_fixed(t0 + 1, 1)
            for g in range(N_KV_HEADS):
                accumulate(g, value_matmul(t0, 0, g) + value_matmul(t0 + 1, 1, g))
            return carry

        lax.fori_loop(0, n_t // 2, attn_pair, 0)

        @pl.when(n_t % 2 == 1)
        def _():
            softmax_fixed(last, 0)
            for g in range(N_KV_HEADS):
                accumulate(g, value_matmul(last, 0, g))

    @pl.when(qi == 0)
    def _():
        for g in range(N_KV_HEADS):
            def sq_norm_max(t, best):
                kk = kt_ref[0, g, t].astype(F32)
                return jnp.maximum(best, jnp.sum(kk * kk, axis=0, keepdims=True))
            best = lax.fori_loop(0, n_tiles_total, sq_norm_max, jnp.zeros((1, ts), F32))
            kmax_ref[g] = jnp.broadcast_to(jnp.max(best, axis=-1, keepdims=True), (8, LANES))

    logits(0, 0)
    top = jnp.zeros((tq, LANES), F32)
    for hd in range(N_HEADS):
        qf = q_ref[0, hd].astype(F32)
        q_sq = _row_sum(qf * qf)
        ref_pt = jnp.sqrt(q_sq * kmax_ref[hd // GQA_GROUP, 0:1, :]) * 1.001 + 1e-3
        m_ref[hd] = ref_pt
        top = jnp.maximum(top, ref_pt)
    fits = jnp.max(top) <= MAX_FIXED_REFERENCE

    @pl.when(fits)
    def _():
        attend_fixed()

    @pl.when(jnp.logical_not(fits))
    def _():
        m_ref[...] = jnp.full(m_ref.shape, _M_INIT, F32)
        attend_running()

    for hd in range(N_HEADS):
        g, j = divmod(hd, GQA_GROUP)
        o_ref[0, :, hd * HEAD_DIM:(hd + 1) * HEAD_DIM] = (
            acc_ref[g, j * tq:(j + 1) * tq, 0:HEAD_DIM] / acc_ref[g, j * tq:(j + 1) * tq, HEAD_DIM:2 * HEAD_DIM])


def _attend_t(qst, wrow, q_att, ki4, kt, v, q0, n_keys, tq, ts):
    B, _, T, _ = q_att.shape
    n_tiles = ki4.shape[1]
    k_sel = min(TOPK_MAX, n_keys // 4)
    kern = functools.partial(_attn_t_kernel, tq=tq, ts=ts, q0=q0, n_keys=n_keys, k_sel=k_sel)
    return pl.pallas_call(
        kern, grid=(B, T // tq),
        in_specs=[
            pl.BlockSpec((1, 1, 4 * IDX_DIM, N_IDX_HEADS * tq), lambda b, i: (b, i, 0, 0)),
            pl.BlockSpec((1, 1, N_IDX_HEADS, tq), lambda b, i: (b, i, 0, 0)),
            pl.BlockSpec((1, N_HEADS, tq, HEAD_DIM), lambda b, i: (b, 0, i, 0)),
            _resident((1, n_tiles, ts, 4 * IDX_DIM), lambda b, i: (b, 0, 0, 0)),
            _resident((1, N_KV_HEADS, n_tiles, HEAD_DIM, ts), lambda b, i: (b, 0, 0, 0, 0)),
            _resident((1, N_KV_HEADS, n_tiles, ts, HEAD_DIM), lambda b, i: (b, 0, 0, 0, 0)),
        ],
        out_specs=pl.BlockSpec((1, tq, N_HEADS * HEAD_DIM), lambda b, i: (b, i, 0)),
        out_shape=jax.ShapeDtypeStruct((B, T, N_HEADS * HEAD_DIM), F32),
        scratch_shapes=[
            pltpu.VMEM((n_tiles, ts, tq), I32),
            pltpu.VMEM((3, KEY_ROWS, tq), F32),
            pltpu.VMEM((2, ts, N_IDX_HEADS * tq), F32),
            pltpu.VMEM((2, N_HEADS * tq, ts), F32),
            pltpu.VMEM((N_HEADS, tq, LANES), F32),
            pltpu.VMEM((N_KV_HEADS, GQA_GROUP * tq, 2 * HEAD_DIM), F32),
            pltpu.VMEM((2, tq, ts), F32),
            pltpu.VMEM((2, N_KV_HEADS, GQA_GROUP * tq, ts), BF16),
            pltpu.VMEM((2, N_KV_HEADS, GQA_GROUP * tq, LANES), F32),
            pltpu.VMEM((N_KV_HEADS, 8, LANES), F32),
        ],
        name="attend_t",
        compiler_params=pltpu.CompilerParams(dimension_semantics=("arbitrary", "arbitrary"),
                                             vmem_limit_bytes=VMEM_LIMIT),
    )(qst, wrow, q_att, ki4, kt, v)


def _attend(qs, kiw, q_att, ki4, kt, v, q0, n_keys, tq, ts):
    B, _, T, _ = qs.shape
    lp = ki4.shape[1]
    n_tiles = lp // ts
    k_sel = min(TOPK_MAX, n_keys // 4)
    kern = functools.partial(_attn_kernel, tq=tq, ts=ts, q0=q0, n_keys=n_keys, k_sel=k_sel)
    keys = pl.BlockSpec((1, lp, 2 * LANES), lambda b, i: (b, 0, 0))
    return pl.pallas_call(
        kern, grid=(B, T // tq),
        in_specs=[
            pl.BlockSpec((1, N_IDX_HEADS, tq, 4 * IDX_DIM), lambda b, i: (b, 0, i, 0)),
            pl.BlockSpec((1, tq, LANES), lambda b, i: (b, i, 0)),
            pl.BlockSpec((1, N_HEADS, tq, HEAD_DIM), lambda b, i: (b, 0, i, 0)),
            keys, keys, keys,
        ],
        out_specs=pl.BlockSpec((1, tq, N_HEADS * HEAD_DIM), lambda b, i: (b, i, 0)),
        out_shape=jax.ShapeDtypeStruct((B, T, N_HEADS * HEAD_DIM), F32),
        scratch_shapes=[
            pltpu.VMEM((n_tiles, tq, ts), I32),
            pltpu.VMEM((2, tq, LANES), F32),
            pltpu.VMEM((N_IDX_HEADS, tq, LANES), F32),
            pltpu.VMEM((2, N_HEADS * tq, ts), F32),
            pltpu.VMEM((N_HEADS, tq, LANES), F32),
            pltpu.VMEM((N_KV_HEADS, GQA_GROUP * tq, 2 * HEAD_DIM), F32),
            pltpu.VMEM((2, tq, ts), F32),
            pltpu.VMEM((2, N_KV_HEADS, GQA_GROUP * tq, ts), BF16),
            pltpu.VMEM((2, N_KV_HEADS, GQA_GROUP * tq, LANES), F32),
        ],
        name="attend",
        compiler_params=pltpu.CompilerParams(dimension_semantics=("arbitrary", "arbitrary"),
                                             vmem_limit_bytes=VMEM_LIMIT),
    )(qs, kiw, q_att, ki4, kt, v)


def _rms(x, g):
    return x * lax.rsqrt(jnp.mean(x * x, axis=-1, keepdims=True) + RMS_EPS) * g


def _out_kernel(x_ref, pa_ref, sgb_ref, b_ref, wo_ref, g1_ref, g2_ref, wgu_ref, wd_ref, g3_ref, y_ref, *, d_ff):
    merged = pa_ref[...] + sgb_ref[...] * b_ref[...]
    mix = jnp.dot(merged.astype(BF16), wo_ref[...], preferred_element_type=F32)
    x1 = x_ref[...] + _rms(mix, g1_ref[...])
    h2 = _rms(x1, g2_ref[...]).astype(BF16)
    gu = jnp.dot(h2, wgu_ref[...], preferred_element_type=F32)
    gate, up = gu[:, :d_ff], gu[:, d_ff:]
    act = (gate * _sigmoid(gate) * up).astype(BF16)
    f = jnp.dot(act, wd_ref[...], preferred_element_type=F32)
    y_ref[...] = x1 + _rms(f, g3_ref[...])


def _finish(x, pa, sgb, b, w_out, g_post, g_ffn_pre, w_gate_up, w_down, g_ffn_post, tm):
    N, D = x.shape
    d_ff = w_down.shape[0]
    row = pl.BlockSpec((tm, D), lambda i: (i, 0))
    const = lambda i: (0, 0)
    return pl.pallas_call(
        functools.partial(_out_kernel, d_ff=d_ff), grid=(N // tm,),
        in_specs=[row, row, row, row,
                  _resident((D, D), const), _resident((1, D), const), _resident((1, D), const),
                  _resident((D, 2 * d_ff), const), _resident((d_ff, D), const), _resident((1, D), const)],
        out_specs=row, out_shape=jax.ShapeDtypeStruct((N, D), F32), name="finish",
        compiler_params=pltpu.CompilerParams(dimension_semantics=("arbitrary",), vmem_limit_bytes=VMEM_LIMIT),
    )(x, pa, sgb, b, w_out, g_post, g_ffn_pre, w_gate_up, w_down, g_ffn_post)


def _tile_keys(n_keys, ts):
    return -(-n_keys // ts) * ts


def _layer(x, pos0, hist, k_cache, v_cache, ki_cache, wts, tm, tq, ts, tm_out):
    (w_in_r, w_pool, pool_scale, w_out, w_gate_up, w_down, g_pre, g_post, g_ffn_pre, g_ffn_post) = wts
    B, T, D = x.shape
    hist16 = jnp.concatenate([jnp.zeros((B, HALO // 2 - POOL_HIST, POOL_WIDTH), F32), hist], axis=1)
    own_keys_only = k_cache is None and T % ts == 0 and ts % tm == 0 and tq == LANES and tm % LANES == 0
    outs = _project(x, hist16, pos0, w_in_r, g_pre, w_pool, pool_scale, tm, ts if own_keys_only else None)
    u, q_att, k, v, qs, kiw, pa, sgb = outs[:8]
    ki = kiw[:, :, :IDX_DIM]
    if own_keys_only:
        kt, vt, ki4, wrow = outs[8:]
        b = _attend_t(qs, wrow, q_att, ki4, kt, vt, pos0, T, tq, ts)
    else:
        if k_cache is None:
            k_all, v_all, ki_all = k, v, ki
        else:
            k_all = jnp.concatenate([k_cache.reshape(B, -1, N_KV_HEADS * HEAD_DIM), k], axis=1)
            v_all = jnp.concatenate([v_cache.reshape(B, -1, N_KV_HEADS * HEAD_DIM), v], axis=1)
            ki_all = jnp.concatenate([ki_cache, ki], axis=1)
        n_keys = k_all.shape[1]
        lp = _tile_keys(n_keys, ts)
        pad = ((0, 0), (0, lp - n_keys), (0, 0))
        kt = jnp.pad(k_all.astype(BF16), pad)
        vt = jnp.pad(v_all.astype(BF16), pad)
        ki4 = _split_keys(jnp.pad(ki_all, pad), ts)
        b = _attend(qs, kiw, q_att, ki4, kt, vt, pos0, n_keys, tq, ts)
    y = _finish(x.reshape(B * T, D), pa.reshape(B * T, D), sgb.reshape(B * T, D), b.reshape(B * T, D),
                w_out, g_post, g_ffn_pre, w_gate_up, w_down, g_ffn_post, tm_out).reshape(B, T, D)
    new_pool = jnp.concatenate([hist, u], axis=1)[:, T:]
    return (y, k.reshape(B, T, N_KV_HEADS, HEAD_DIM), v.reshape(B, T, N_KV_HEADS, HEAD_DIM), ki, new_pool)


def _relayout_w_in(w_in):
    d = w_in.shape[0]
    o_kiw = POOL_WIDTH + N_HEADS * HEAD_DIM + 2 * N_KV_HEADS * HEAD_DIM + N_IDX_HEADS * IDX_DIM
    narrow = IDX_DIM + N_IDX_HEADS
    padded = jnp.concatenate([w_in[:, :o_kiw + narrow], jnp.zeros((d, LANES - narrow), w_in.dtype),
                              w_in[:, o_kiw + narrow:]], axis=1)
    return padded.astype(BF16)


def kernel(x_prompt, x_sample, cache_k, cache_v, cache_k_idx, state_pool, w_in, w_pool, pool_scale, w_out,
           w_gate_up, w_down, norm_mix_pre, norm_mix_post, norm_ffn_pre, norm_ffn_post):
    depth = w_in.shape[0]
    past = cache_k.shape[2]
    t_p, t_s = x_prompt.shape[1], x_sample.shape[1]
    hist_p = jnp.zeros((x_prompt.shape[0], POOL_HIST, POOL_WIDTH), x_prompt.dtype)
    xp, xs = x_prompt, x_sample
    outs = [[] for _ in range(8)]
    for l in range(depth):
        wts = (_relayout_w_in(w_in[l]), w_pool[l].astype(BF16), pool_scale[l][None, :], w_out[l].astype(BF16),
               w_gate_up[l].astype(BF16), w_down[l].astype(BF16), norm_mix_pre[l][None, :],
               norm_mix_post[l][None, :], norm_ffn_pre[l][None, :], norm_ffn_post[l][None, :])
        tm_p = min(256, t_p)
        tq_p = min(128, t_p)
        xp, k1, v1, ki1, p1 = _layer(xp, 0, hist_p, None, None, None, wts, tm_p, tq_p, 512, tm_p)
        n_s = xs.shape[0] * t_s
        xs, k2, v2, ki2, p2 = _layer(xs, past, state_pool[l], cache_k[l], cache_v[l], cache_k_idx[l], wts,
                                     t_s, t_s, 512, min(256, n_s))
        for lst, val in zip(outs, (k1, v1, ki1, p1, k2, v2, ki2, p2)):
            lst.append(val)
    return (xp, xs) + tuple(jnp.stack(o) for o in outs)
```

```python
import functools

import jax
import jax.numpy as jnp
from jax import lax
from jax.experimental import pallas as pl
from jax.experimental.pallas import tpu as pltpu

F32 = jnp.float32
BF16 = jnp.bfloat16
I32 = jnp.int32

LANES = 128
CHUNK = 64
POOL_WINDOWS = (2, 4, 8, 16)
N_POOL_GROUPS = 4
POOL_GROUP_WIDTH = 128
POOL_WIDTH = N_POOL_GROUPS * POOL_GROUP_WIDTH
POOL_HIST = 15
N_HEADS = 8
N_KV_HEADS = 2
HEAD_DIM = 128
GQA_GROUP = N_HEADS // N_KV_HEADS
ROPE_THETA = 500000.0
N_IDX_HEADS = 8
IDX_DIM = 64
TOPK_MAX = 256
RMS_EPS = 1e-6
ATTN_SCALE = HEAD_DIM ** -0.5
IDX_SCALE = (N_IDX_HEADS ** -0.5) * (IDX_DIM ** -0.5)

LOG2_E = 1.4426950408889634
VALUE_PIVOT_STEPS = 28
MAX_FIXED_REFERENCE = 40.0
SPLIT_ROWS = 1536
UNTESTED_STEPS = 16
KEY_ROWS = 16
COUNT_ROWS = 32
ROW_CHUNK = 32
HALO = 32
VMEM_LIMIT = 60 * 1024 * 1024

_NEG_INF = float("-inf")
_POS_INF = float("inf")
_INADMISSIBLE = -(2 ** 31)
_NO_KEY = 2 ** 31 - 1
_KEY_MIN_NORMAL = 0x00800000
_KEY_NEG_ZERO = -1
_M_INIT = -1e30


def _resident(block_shape, index_map):
    return pl.BlockSpec(block_shape, index_map, pipeline_mode=pl.Buffered(1))


def _rope(xs, cos, sin, half, period):
    lane = lax.broadcasted_iota(I32, xs.shape, 1)
    ahead = pltpu.roll(xs, LANES - half, 1)
    behind = pltpu.roll(xs, half, 1)
    partner = jnp.where((lane & (period - 1)) < half, ahead, behind)
    return xs * cos + partner * sin


def _sigmoid(x):
    return 1.0 / (1.0 + jnp.exp(-x))


def _proj_kernel(x_ref, g_ref, w_ref, hist_ref, cq_ref, sq_ref, ci_ref, si_ref, wpool_ref, pscale_ref,
                 u_ref, qatt_ref, k_ref, v_ref, qs_ref, kiw_ref, pa_ref, sgb_ref, *rest, tm, pos0, d_model, key_tiles):
    if key_tiles:
        kt_ref, vt_ref, ki4_ref, wrow_ref, e_ref, s2_ref, s4_ref, s8_ref = rest
    else:
        e_ref, s2_ref, s4_ref, s8_ref = rest
    i = pl.program_id(1)
    x = x_ref[0]
    h = x * lax.rsqrt(jnp.mean(x * x, axis=-1, keepdims=True) + RMS_EPS) * g_ref[...]
    o_q = POOL_WIDTH
    o_k = o_q + N_HEADS * HEAD_DIM
    o_v = o_k + N_KV_HEADS * HEAD_DIM
    o_qi = o_v + N_KV_HEADS * HEAD_DIM
    o_kiw = o_qi + N_IDX_HEADS * IDX_DIM
    o_ga = o_kiw + LANES
    o_gb = o_ga + d_model

    proj = jnp.dot(h.astype(BF16), w_ref[...], preferred_element_type=F32)

    cq, sq, ci, si = cq_ref[...], sq_ref[...], ci_ref[...], si_ref[...]
    half_q = HEAD_DIM // 8
    half_i = IDX_DIM // 8

    for hd in range(N_HEADS):
        qh = _rope(proj[:, o_q + hd * HEAD_DIM:o_q + (hd + 1) * HEAD_DIM], cq, sq, half_q, HEAD_DIM)
        qatt_ref[0, hd] = (qh * (ATTN_SCALE * LOG2_E)).astype(BF16)
    for kh in range(N_KV_HEADS):
        k_h = _rope(proj[:, o_k + kh * HEAD_DIM:o_k + (kh + 1) * HEAD_DIM], cq, sq, half_q, HEAD_DIM)
        k_ref[0, :, kh * HEAD_DIM:(kh + 1) * HEAD_DIM] = k_h
        if key_tiles:
            kt_ref[0, kh, 0] = k_h.T.astype(BF16)
            vt_ref[0, kh, 0] = proj[:, o_v + kh * HEAD_DIM:o_v + (kh + 1) * HEAD_DIM].astype(BF16)
    v_ref[0] = proj[:, o_v:o_qi]

    for pr in range(N_IDX_HEADS // 2):
        qi2 = _rope(proj[:, o_qi + pr * LANES:o_qi + (pr + 1) * LANES], ci, si, half_i, IDX_DIM)
        hi = qi2.astype(BF16).astype(F32)
        lo = qi2 - hi
        lane2 = lax.broadcasted_iota(I32, qi2.shape, 1)
        first = jnp.where(lane2 < IDX_DIM, hi, pltpu.roll(lo, IDX_DIM, 1))
        second = jnp.where(lane2 < IDX_DIM, pltpu.roll(hi, IDX_DIM, 1), lo)
        for sub, slab in enumerate((first, second)):
            hd = 2 * pr + sub
            if key_tiles:
                for sb in range(tm // LANES):
                    slab_t = slab[sb * LANES:(sb + 1) * LANES].T.astype(BF16)
                    qs_ref[0, sb, 0:LANES, hd * LANES:(hd + 1) * LANES] = slab_t
                    qs_ref[0, sb, LANES:2 * LANES, hd * LANES:(hd + 1) * LANES] = slab_t
            else:
                qs_ref[0, hd, :, 0:LANES] = slab.astype(BF16)
                qs_ref[0, hd, :, LANES:2 * LANES] = slab.astype(BF16)

    kiw = _rope(proj[:, o_kiw:o_kiw + LANES], ci, si, half_i, IDX_DIM)
    lane = lax.broadcasted_iota(I32, kiw.shape, 1)
    kiw_full = jnp.where(lane < IDX_DIM, kiw, proj[:, o_kiw:o_kiw + LANES] * IDX_SCALE)
    kiw_ref[0] = kiw_full
    if key_tiles:
        ki_hi = kiw.astype(BF16).astype(F32)
        ki_lo = kiw - ki_hi
        ki4_ref[0, 0, :, 0:LANES] = jnp.where(lane < IDX_DIM, ki_hi, pltpu.roll(ki_hi, IDX_DIM, 1)).astype(BF16)
        ki4_ref[0, 0, :, LANES:2 * LANES] = jnp.where(lane < IDX_DIM, ki_lo, pltpu.roll(ki_lo, IDX_DIM, 1)).astype(BF16)
        for sb in range(tm // LANES):
            wrow_ref[0, sb] = kiw_full[sb * LANES:(sb + 1) * LANES].T[IDX_DIM:IDX_DIM + N_IDX_HEADS]

    u = proj[:, 0:POOL_WIDTH]
    u_ref[0] = u

    @pl.when(i == 0)
    def _():
        e_ref[0:HALO // 2, :] = jnp.zeros((HALO // 2, POOL_WIDTH), F32)
        e_ref[HALO // 2:HALO, :] = hist_ref[0]

    e_ref[HALO:HALO + tm, :] = u
    n2, n4, n8 = tm + 24, tm + 16, tm + 8
    s2_ref[8:8 + n2, :] = e_ref[8:8 + n2, :] + e_ref[7:7 + n2, :]
    s4_ref[16:16 + n4, :] = s2_ref[16:16 + n4, :] + s2_ref[14:14 + n4, :]
    s8_ref[24:24 + n8, :] = s4_ref[24:24 + n8, :] + s4_ref[20:20 + n8, :]
    s16 = s8_ref[HALO:HALO + tm, :] + s8_ref[HALO - 8:HALO - 8 + tm, :]
    wins = (s2_ref[HALO:HALO + tm, :], s4_ref[HALO:HALO + tm, :], s8_ref[HALO:HALO + tm, :], s16)
    e_ref[HALO // 2:HALO, :] = e_ref[HALO // 2 + tm:HALO + tm, :]

    pos = pos0 + i * tm + lax.broadcasted_iota(I32, (tm, POOL_GROUP_WIDTH), 0)
    a_parts = []
    for g, w in enumerate(POOL_WINDOWS):
        lo_l, hi_l = g * POOL_GROUP_WIDTH, (g + 1) * POOL_GROUP_WIDTH
        cnt = jnp.minimum(pos + 1, w).astype(F32)
        pooled = wins[g][:, lo_l:hi_l] / cnt - u[:, lo_l:hi_l]
        a_parts.append(jnp.dot(pooled.astype(BF16), wpool_ref[g], preferred_element_type=F32))
    a = jnp.concatenate(a_parts, axis=-1) * pscale_ref[...]
    pa_ref[0] = _sigmoid(proj[:, o_ga:o_gb]) * a
    sgb_ref[0] = _sigmoid(proj[:, o_gb:o_gb + d_model])


def _project(x, hist16, pos0, w_in_r, g_pre, w_pool, pool_scale, tm, key_tile=None):
    B, T, D = x.shape
    W = w_in_r.shape[1]
    pos = pos0 + jnp.arange(T, dtype=I32)
    cq, sq = _rope_tables(pos, HEAD_DIM)
    ci, si = _rope_tables(pos, IDX_DIM)
    row = lambda b, i: (b, i, 0)
    tab = lambda b, i: (i, 0)
    const2 = lambda b, i: (0, 0)
    out_shape = (
        jax.ShapeDtypeStruct((B, T, POOL_WIDTH), F32),
        jax.ShapeDtypeStruct((B, N_HEADS, T, HEAD_DIM), BF16),
        jax.ShapeDtypeStruct((B, T, N_KV_HEADS * HEAD_DIM), F32),
        jax.ShapeDtypeStruct((B, T, N_KV_HEADS * HEAD_DIM), F32),
        jax.ShapeDtypeStruct((B, N_IDX_HEADS, T, 4 * IDX_DIM), BF16),
        jax.ShapeDtypeStruct((B, T, LANES), F32),
        jax.ShapeDtypeStruct((B, T, D), F32),
        jax.ShapeDtypeStruct((B, T, D), F32),
    )
    out_specs = (
        pl.BlockSpec((1, tm, POOL_WIDTH), row),
        pl.BlockSpec((1, N_HEADS, tm, HEAD_DIM), lambda b, i: (b, 0, i, 0)),
        pl.BlockSpec((1, tm, N_KV_HEADS * HEAD_DIM), row),
        pl.BlockSpec((1, tm, N_KV_HEADS * HEAD_DIM), row),
        pl.BlockSpec((1, N_IDX_HEADS, tm, 4 * IDX_DIM), lambda b, i: (b, 0, i, 0)),
        pl.BlockSpec((1, tm, LANES), row),
        pl.BlockSpec((1, tm, D), row),
        pl.BlockSpec((1, tm, D), row),
    )
    in_specs = [
        pl.BlockSpec((1, tm, D), row),
        _resident((1, D), const2),
        _resident((D, W), const2),
        pl.BlockSpec((1, HALO // 2, POOL_WIDTH), lambda b, i: (b, 0, 0)),
        pl.BlockSpec((tm, LANES), tab), pl.BlockSpec((tm, LANES), tab),
        pl.BlockSpec((tm, LANES), tab), pl.BlockSpec((tm, LANES), tab),
        _resident((N_POOL_GROUPS, POOL_GROUP_WIDTH, D // N_POOL_GROUPS), lambda b, i: (0, 0, 0)),
        _resident((1, D), const2),
    ]
    if key_tile is not None:
        assert key_tile % tm == 0 and T % key_tile == 0 and tm % LANES == 0
        per = key_tile // tm
        n_tiles = T // key_tile
        sub = tm // LANES
        out_shape = out_shape[:4] + (
            jax.ShapeDtypeStruct((B, T // LANES, 4 * IDX_DIM, N_IDX_HEADS * LANES), BF16),
        ) + out_shape[5:] + (
            jax.ShapeDtypeStruct((B, N_KV_HEADS, n_tiles, HEAD_DIM, key_tile), BF16),
            jax.ShapeDtypeStruct((B, N_KV_HEADS, n_tiles, key_tile, HEAD_DIM), BF16),
            jax.ShapeDtypeStruct((B, n_tiles, key_tile, 4 * IDX_DIM), BF16),
            jax.ShapeDtypeStruct((B, T // LANES, N_IDX_HEADS, LANES), F32),
        )
        out_specs = out_specs[:4] + (
            pl.BlockSpec((1, sub, 4 * IDX_DIM, N_IDX_HEADS * LANES), lambda b, i: (b, i, 0, 0)),
        ) + out_specs[5:] + (
            pl.BlockSpec((1, N_KV_HEADS, 1, HEAD_DIM, tm), lambda b, i: (b, 0, i // per, 0, i % per)),
            pl.BlockSpec((1, N_KV_HEADS, 1, tm, HEAD_DIM), lambda b, i: (b, 0, i // per, i % per, 0)),
            pl.BlockSpec((1, 1, tm, 4 * IDX_DIM), lambda b, i: (b, i // per, i % per, 0)),
            pl.BlockSpec((1, sub, N_IDX_HEADS, LANES), lambda b, i: (b, i, 0, 0)),
        )
    scratch = [pltpu.VMEM((HALO + tm, POOL_WIDTH), F32) for _ in range(4)]
    return pl.pallas_call(
        functools.partial(_proj_kernel, tm=tm, pos0=pos0, d_model=D, key_tiles=key_tile is not None),
        grid=(B, T // tm), in_specs=in_specs, out_specs=out_specs, out_shape=out_shape,
        scratch_shapes=scratch, name="proj",
        compiler_params=pltpu.CompilerParams(dimension_semantics=("arbitrary", "arbitrary"),
                                             vmem_limit_bytes=VMEM_LIMIT),
    )(x, g_pre, w_in_r, hist16, cq, sq, ci, si, w_pool, pool_scale)


def _rope_tables(pos, dim):
    rot = dim // 4
    half = rot // 2
    inv = ROPE_THETA ** (-jnp.arange(half, dtype=F32) / half)
    ang = pos.astype(F32)[:, None] * inv[None, :]
    cos, sin = jnp.cos(ang), jnp.sin(ang)
    rest = dim - rot
    n = pos.shape[0]
    c = jnp.concatenate([cos, cos, jnp.ones((n, rest), F32)], axis=-1)
    s = jnp.concatenate([-sin, sin, jnp.zeros((n, rest), F32)], axis=-1)
    return jnp.tile(c, (1, LANES // dim)), jnp.tile(s, (1, LANES // dim))


def _split_kernel(k_ref, out_ref):
    k = k_ref[0]
    hi = k.astype(BF16).astype(F32)
    lo = k - hi
    out_ref[0, :, 0:LANES] = jnp.concatenate([hi, hi], axis=-1).astype(BF16)
    out_ref[0, :, LANES:2 * LANES] = jnp.concatenate([lo, lo], axis=-1).astype(BF16)


def _split_keys(ki, ts):
    B, Lp, _ = ki.shape
    rows = ts * max(1, SPLIT_ROWS // ts)
    rows = rows if Lp % rows == 0 else ts
    return pl.pallas_call(
        _split_kernel, grid=(B, Lp // rows),
        in_specs=[pl.BlockSpec((1, rows, IDX_DIM), lambda b, t: (b, t, 0))],
        out_specs=pl.BlockSpec((1, rows, 4 * IDX_DIM), lambda b, t: (b, t, 0)),
        out_shape=jax.ShapeDtypeStruct((B, Lp, 4 * IDX_DIM), BF16), name="split_keys",
        compiler_params=pltpu.CompilerParams(dimension_semantics=("arbitrary", "arbitrary")),
    )(ki)


def _key_of(x):
    b = pltpu.bitcast(x, I32)
    return b ^ ((b >> 31) & 0x7FFFFFFF)


def _float_of(k):
    return pltpu.bitcast(k ^ ((k >> 31) & 0x7FFFFFFF), F32)


def _row_sum(x):
    return jnp.broadcast_to(jnp.sum(x, axis=-1, keepdims=True), x.shape)


def _attn_kernel(qs_ref, kiw_ref, q_ref, ki_ref, kt_ref, v_ref, o_ref,
                 sc_ref, mm_ref, wb_ref, s_ref, m_ref, acc_ref, bias_ref, p_ref, alpha_ref,
                 *, tq, ts, q0, n_keys, k_sel):
    qi = pl.program_id(1)
    n_tiles_total = ki_ref.shape[1] // ts
    nc = ts // LANES
    nt_dims = (((1,), (1,)), ((), ()))

    def key_rows(t):
        return pl.ds(pl.multiple_of(t * ts, ts), ts)

    row = lax.broadcasted_iota(I32, (tq, LANES), 0)
    lane = lax.broadcasted_iota(I32, (tq, LANES), 1)
    qpos = q0 + qi * tq + row
    n_adm = jnp.minimum((qpos // CHUNK + 1) * CHUNK, n_keys)
    last_adm = jnp.minimum(((q0 + (qi + 1) * tq - 1) // CHUNK + 1) * CHUNK, n_keys)
    n_t = jnp.minimum((last_adm + ts - 1) // ts, n_tiles_total)

    kiw = kiw_ref[0]
    for hd in range(N_IDX_HEADS):
        wb_ref[hd] = jnp.broadcast_to(kiw[:, IDX_DIM + hd:IDX_DIM + hd + 1], (tq, LANES))
    qs2 = qs_ref[0].reshape(N_IDX_HEADS * tq, 4 * IDX_DIM)

    def idx_logits(t, slot):
        s_ref[slot] = lax.dot_general(qs2, ki_ref[0, key_rows(t), :], nt_dims,
                                      preferred_element_type=F32)

    rc = min(tq, ROW_CHUNK)
    mm_ref[0] = jnp.full((tq, LANES), _POS_INF, F32)
    mm_ref[1] = jnp.full((tq, LANES), _NEG_INF, F32)

    def score_tile(t, slot, masked):
        for r0 in range(0, tq, rc):
            rows = slice(r0, r0 + rc)
            mn, mx = mm_ref[0, rows], mm_ref[1, rows]
            for c in range(nc):
                cs = slice(c * LANES, (c + 1) * LANES)
                acc = None
                for hd in range(N_IDX_HEADS):
                    r = jnp.maximum(s_ref[slot, hd * tq + r0:hd * tq + r0 + rc, cs], 0.0) * wb_ref[hd, rows]
                    acc = r if acc is None else acc + r
                key = _key_of(acc)
                if masked:
                    qpos_c = q0 + qi * tq + r0 + lax.broadcasted_iota(I32, (rc, LANES), 0)
                    n_adm_c = jnp.minimum((qpos_c // CHUNK + 1) * CHUNK, n_keys)
                    adm = (t * ts + c * LANES + lax.broadcasted_iota(I32, (rc, LANES), 1)) < n_adm_c
                    key = jnp.where(adm, key, _INADMISSIBLE)
                    mx = jnp.maximum(mx, jnp.where(adm, acc, _NEG_INF))
                    mn = jnp.minimum(mn, jnp.where(adm, acc, _POS_INF))
                else:
                    mx = jnp.maximum(mx, acc)
                    mn = jnp.minimum(mn, acc)
                sc_ref[t, rows, cs] = key
            mm_ref[0, rows] = mn
            mm_ref[1, rows] = mx

    n_full = jnp.minimum(jnp.minimum(((q0 + qi * tq) // CHUNK + 1) * CHUNK, n_keys) // ts, n_t)
    last_full = jnp.maximum(n_full - 1, 0)
    idx_logits(0, 0)

    def score_pair(u, carry):
        t0 = 2 * u
        t1 = jnp.minimum(t0 + 1, last_full)
        idx_logits(t1, 1)
        score_tile(t0, 0, False)
        idx_logits(jnp.minimum(t0 + 2, last_full), 0)
        score_tile(t1, 1, False)
        return carry

    lax.fori_loop(0, (n_full + 1) // 2, score_pair, 0)

    def score_tail(t, carry):
        idx_logits(t, 0)
        score_tile(t, 0, True)
        return carry

    lax.fori_loop(n_full, n_t, score_tail, 0)
    rmin = jnp.broadcast_to(jnp.min(mm_ref[0], axis=-1, keepdims=True), (tq, LANES))
    rmax = jnp.broadcast_to(jnp.max(mm_ref[1], axis=-1, keepdims=True), (tq, LANES))

    def count_ge(thr):
        def body(t, cnt):
            for c in range(nc):
                cnt = cnt + jnp.where(sc_ref[t, :, c * LANES:(c + 1) * LANES] >= thr, 1.0, 0.0)
            return cnt
        return _row_sum(lax.fori_loop(0, n_t, body, jnp.zeros((tq, LANES), F32)))

    def bis_cond(st):
        return st[0] > 0

    kf = float(k_sel)
    take_all = n_adm <= k_sel
    lo0 = _key_of(rmin) - 1
    hi0 = _key_of(rmax) + 2
    c_pos = count_ge(jnp.full((tq, LANES), _KEY_MIN_NORMAL, I32))
    c_nn = count_ge(jnp.full((tq, LANES), _KEY_NEG_ZERO, I32))
    pos = c_pos >= kf
    neg = c_nn < kf
    lo = jnp.where(pos, _KEY_MIN_NORMAL, jnp.where(neg, lo0, _KEY_NEG_ZERO))
    hi = jnp.where(pos, hi0, jnp.where(neg, _KEY_NEG_ZERO, _KEY_MIN_NORMAL))
    clo = jnp.where(pos, c_pos, jnp.where(neg, n_adm.astype(F32), c_nn))
    chi = jnp.where(pos, 0.0, jnp.where(neg, c_nn, c_pos))
    exact0 = jnp.where(clo == kf, 1, 0)
    act0 = jnp.where(take_all, 0, jnp.where(exact0 > 0, 0, jnp.where(hi > lo + 1, 1, 0)))

    def bis_step(it, lo, hi, clo, chi, exact, act):
        kmid = (lo & hi) + ((lo ^ hi) >> 1)
        vmid = _key_of((_float_of(lo) + _float_of(hi)) * 0.5)
        by_value = jnp.where(it < VALUE_PIVOT_STEPS, jnp.where(vmid > lo, jnp.where(vmid < hi, 1, 0), 0), 0)
        mid = jnp.where(by_value > 0, vmid, kmid)
        cnt = count_ge(mid)
        up = jnp.where(act > 0, jnp.where(cnt >= kf, 1, 0), 0)
        dn = act - up
        lo = jnp.where(up > 0, mid, lo)
        clo = jnp.where(up > 0, cnt, clo)
        hi = jnp.where(dn > 0, mid, hi)
        chi = jnp.where(dn > 0, cnt, chi)
        hit = jnp.where(up > 0, jnp.where(cnt == kf, 1, 0), 0)
        exact = exact + hit
        act = jnp.where(act > 0, jnp.where(hit > 0, 0, jnp.where(hi > lo + 1, 1, 0)), 0)
        return it + 1, lo, hi, clo, chi, exact, act

    def any_active(act):
        return jnp.max(act.astype(F32))

    def bis_body(st):
        st = bis_step(*bis_step(*st[1:]))
        return (any_active(st[-1]),) + st

    _, _, lo, _, clo, chi, exact, _ = lax.while_loop(
        bis_cond, bis_body, (any_active(act0), jnp.int32(0), lo, hi, clo, chi, exact0, act0))
    thr = jnp.where(take_all, _INADMISSIBLE + 1, lo)
    thr_up = jnp.where(take_all, _INADMISSIBLE + 1, lo + 1)

    need = kf - chi
    tied = jnp.where(take_all, 0, jnp.where(exact > 0, 0, jnp.where(clo > kf, 1, 0)))

    def count_tied_below(cut):
        def body(t, cnt):
            for c in range(nc):
                s = sc_ref[t, :, c * LANES:(c + 1) * LANES]
                col = t * ts + c * LANES + lane
                cnt = cnt + jnp.where(s == thr, jnp.where(col < cut, 1.0, 0.0), 0.0)
            return cnt
        return _row_sum(lax.fori_loop(0, n_t, body, jnp.zeros((tq, LANES), F32)))

    def cut_body(st):
        _, lo_c, hi_c, act = st
        mid = (lo_c + hi_c) >> 1
        cnt = count_tied_below(mid)
        on = act > 0
        le = cnt <= need
        lo_c = jnp.where(on, jnp.where(le, mid, lo_c), lo_c)
        hi_c = jnp.where(on, jnp.where(le, hi_c, mid), hi_c)
        act = jnp.where(on, jnp.where(hi_c - lo_c > 1, 1, 0), 0)
        return jnp.max(act), lo_c, hi_c, act

    _, cut_lo, _, _ = lax.while_loop(
        bis_cond, cut_body,
        (jnp.max(tied), jnp.zeros((tq, LANES), I32), jnp.full((tq, LANES), n_tiles_total * ts + 1, I32), tied))
    cut = jnp.where(tied > 0, cut_lo, n_tiles_total * ts + 1)

    m_ref[...] = jnp.full(m_ref.shape, _M_INIT, F32)
    acc_ref[...] = jnp.zeros(acc_ref.shape, F32)
    rc = min(tq, ROW_CHUNK)
    ones = jnp.ones((ts, LANES), BF16)
    last = n_t - 1

    def logits(t, slot):
        for g in range(N_KV_HEADS):
            qg = q_ref[0, g * GQA_GROUP:(g + 1) * GQA_GROUP].reshape(GQA_GROUP * tq, HEAD_DIM)
            s_ref[slot, g * GQA_GROUP * tq:(g + 1) * GQA_GROUP * tq] = lax.dot_general(
                qg, kt_ref[0, key_rows(t), g * HEAD_DIM:(g + 1) * HEAD_DIM], nt_dims,
                preferred_element_type=F32)

    def softmax_pv(t, slot, live):
        for c in range(nc):
            col = t * ts + c * LANES + lane
            bound = jnp.where(live, jnp.where(col < cut, thr, thr_up), _NO_KEY)
            bias_ref[slot, :, c * LANES:(c + 1) * LANES] = jnp.where(
                sc_ref[t, :, c * LANES:(c + 1) * LANES] >= bound, 0.0, _NEG_INF)
        for g in range(N_KV_HEADS):
            for j in range(GQA_GROUP):
                hd = g * GQA_GROUP + j
                for r0 in range(0, tq, rc):
                    r1 = j * tq + r0
                    r2 = hd * tq + r0
                    s = [s_ref[slot, r2:r2 + rc, c * LANES:(c + 1) * LANES]
                         + bias_ref[slot, r0:r0 + rc, c * LANES:(c + 1) * LANES] for c in range(nc)]
                    mx = s[0]
                    for c in range(1, nc):
                        mx = jnp.maximum(mx, s[c])
                    m_old = m_ref[hd, r0:r0 + rc]
                    m_new = jnp.maximum(m_old, jnp.broadcast_to(jnp.max(mx, axis=-1, keepdims=True), (rc, LANES)))
                    alpha_ref[slot, g, r1:r1 + rc] = jnp.exp2(m_old - m_new)
                    for c in range(nc):
                        p_ref[slot, g, r1:r1 + rc, c * LANES:(c + 1) * LANES] = jnp.exp2(s[c] - m_new).astype(BF16)
                    m_ref[hd, r0:r0 + rc] = m_new
            v_aug = jnp.concatenate([v_ref[0, key_rows(t), g * HEAD_DIM:(g + 1) * HEAD_DIM], ones], axis=-1)
            pv = jnp.dot(p_ref[slot, g], v_aug, preferred_element_type=F32)
            alpha = alpha_ref[slot, g]
            for half in range(2):
                cols = slice(half * LANES, (half + 1) * LANES)
                acc_ref[g, :, cols] = alpha * acc_ref[g, :, cols] + pv[:, cols]

    logits(0, 0)

    def attn_pair(u, carry):
        t0 = 2 * u
        t1 = jnp.minimum(t0 + 1, last)
        logits(t1, 1)
        softmax_pv(t0, 0, True)
        logits(jnp.minimum(t0 + 2, last), 0)
        softmax_pv(t1, 1, t0 + 1 < n_t)
        return carry

    lax.fori_loop(0, (n_t + 1) // 2, attn_pair, 0)
    for hd in range(N_HEADS):
        g, j = divmod(hd, GQA_GROUP)
        o_ref[0, :, hd * HEAD_DIM:(hd + 1) * HEAD_DIM] = (
            acc_ref[g, j * tq:(j + 1) * tq, 0:HEAD_DIM] / acc_ref[g, j * tq:(j + 1) * tq, HEAD_DIM:2 * HEAD_DIM])


def _attn_t_kernel(qst_ref, wrow_ref, q_ref, ki_ref, kt_ref, v_ref, o_ref,
                   sc_ref, mm_ref, *scratch, tq, ts, q0, n_keys, k_sel):
    if ts == N_IDX_HEADS * tq:
        s_ref, m_ref, acc_ref, bias_ref, p_ref, alpha_ref, kmax_ref = scratch
        st_ref = s_ref
    else:
        st_ref, s_ref, m_ref, acc_ref, bias_ref, p_ref, alpha_ref, kmax_ref = scratch
    qi = pl.program_id(1)
    n_tiles_total = ki_ref.shape[1]
    nc = ts // LANES
    rk = KEY_ROWS
    qpos = q0 + qi * tq + lax.broadcasted_iota(I32, (1, tq), 1)
    n_adm = jnp.minimum((qpos // CHUNK + 1) * CHUNK, n_keys)
    last_adm = jnp.minimum(((q0 + (qi + 1) * tq - 1) // CHUNK + 1) * CHUNK, n_keys)
    n_t = jnp.minimum((last_adm + ts - 1) // ts, n_tiles_total)
    n_full = jnp.minimum(jnp.minimum(((q0 + qi * tq) // CHUNK + 1) * CHUNK, n_keys) // ts, n_t)
    last_full = jnp.maximum(n_full - 1, 0)
    qst = qst_ref[0, 0]
    wr = wrow_ref[0, 0]

    mm_ref[0] = jnp.full((rk, tq), _POS_INF, F32)
    mm_ref[1] = jnp.full((rk, tq), _NEG_INF, F32)
    mm_ref[2] = jnp.zeros((rk, tq), F32)

    def idx_logits(t, slot):
        st_ref[slot] = jnp.dot(ki_ref[0, t], qst, preferred_element_type=F32)

    def score_tile(t, slot, masked):
        mn, mx, n_pos = mm_ref[0], mm_ref[1], mm_ref[2]
        for r0 in range(0, ts, rk):
            acc = None
            for hd in range(N_IDX_HEADS):
                r = jnp.maximum(st_ref[slot, r0:r0 + rk, hd * tq:(hd + 1) * tq], 0.0) * wr[hd:hd + 1, :]
                acc = r if acc is None else acc + r
            key = _key_of(acc)
            if masked:
                adm = (t * ts + r0 + lax.broadcasted_iota(I32, (rk, tq), 0)) < n_adm
                key = jnp.where(adm, key, _INADMISSIBLE)
                mx = jnp.maximum(mx, jnp.where(adm, acc, _NEG_INF))
                mn = jnp.minimum(mn, jnp.where(adm, acc, _POS_INF))
            else:
                mx = jnp.maximum(mx, acc)
                mn = jnp.minimum(mn, acc)
            n_pos = n_pos + jnp.where(key >= _KEY_MIN_NORMAL, 1.0, 0.0)
            sc_ref[t, r0:r0 + rk, :] = key
        mm_ref[0] = mn
        mm_ref[1] = mx
        mm_ref[2] = n_pos

    idx_logits(0, 0)

    def score_pair(u, carry):
        t0 = 2 * u
        idx_logits(t0 + 1, 1)
        score_tile(t0, 0, False)
        idx_logits(jnp.minimum(t0 + 2, last_full), 0)
        score_tile(t0 + 1, 1, False)
        return carry

    lax.fori_loop(0, n_full // 2, score_pair, 0)

    @pl.when(n_full % 2 == 1)
    def _():
        score_tile(last_full, 0, False)

    def score_tail(t, carry):
        idx_logits(t, 0)
        score_tile(t, 0, True)
        return carry

    lax.fori_loop(n_full, n_t, score_tail, 0)
    rmin = jnp.min(mm_ref[0], axis=0, keepdims=True)
    rmax = jnp.max(mm_ref[1], axis=0, keepdims=True)

    def count_ge(thr):
        def body(t, cnt):
            for r0 in range(0, ts, COUNT_ROWS):
                cnt = cnt + jnp.where(sc_ref[t, r0:r0 + COUNT_ROWS, :] >= thr, 1.0, 0.0)
            return cnt
        return jnp.sum(lax.fori_loop(0, n_t, body, jnp.zeros((COUNT_ROWS, tq), F32)), axis=0, keepdims=True)

    def bis_cond(st):
        return st[0] > 0

    kf = float(k_sel)
    take_all = n_adm <= k_sel
    lo0 = _key_of(rmin) - 1
    hi0 = _key_of(rmax) + 2
    c_pos = jnp.sum(mm_ref[2], axis=0, keepdims=True)
    pos = c_pos >= kf
    lo = jnp.where(pos, _KEY_MIN_NORMAL, jnp.minimum(lo0, _KEY_NEG_ZERO - 1))
    hi = jnp.where(pos, hi0, _KEY_MIN_NORMAL)
    clo = jnp.where(pos, c_pos, n_adm.astype(F32))
    chi = jnp.where(pos, 0.0, c_pos)
    exact0 = jnp.where(clo == kf, 1, 0)
    act0 = jnp.where(take_all, 0, jnp.where(exact0 > 0, 0, jnp.where(hi > lo + 1, 1, 0)))

    def bis_step(it, lo, hi, clo, chi, exact, act):
        kmid = (lo & hi) + ((lo ^ hi) >> 1)
        vmid = _key_of((_float_of(lo) + _float_of(hi)) * 0.5)
        by_value = jnp.where(it < VALUE_PIVOT_STEPS, jnp.where(vmid > lo, jnp.where(vmid < hi, 1, 0), 0), 0)
        mid = jnp.where(by_value > 0, vmid, kmid)
        at_zero = jnp.where(lo < _KEY_NEG_ZERO, jnp.where(hi == _KEY_MIN_NORMAL, 1, 0), 0)
        mid = jnp.where(at_zero > 0, _KEY_NEG_ZERO, mid)
        cnt = count_ge(mid)
        up = jnp.where(act > 0, jnp.where(cnt >= kf, 1, 0), 0)
        dn = act - up
        lo = jnp.where(up > 0, mid, lo)
        clo = jnp.where(up > 0, cnt, clo)
        hi = jnp.where(dn > 0, mid, hi)
        chi = jnp.where(dn > 0, cnt, chi)
        hit = jnp.where(up > 0, jnp.where(cnt == kf, 1, 0), 0)
        exact = exact + hit
        act = jnp.where(act > 0, jnp.where(hit > 0, 0, jnp.where(hi > lo + 1, 1, 0)), 0)
        return it + 1, lo, hi, clo, chi, exact, act

    def any_active(act):
        return jnp.max(act.astype(F32))

    def bis_body(st):
        st = bis_step(*bis_step(*st[1:]))
        return (any_active(st[-1]),) + st

    warm = jnp.where(n_t > 2, UNTESTED_STEPS // 2, 0)
    st0 = lax.fori_loop(0, warm, lambda _, st: bis_step(*bis_step(*st)),
                        (jnp.int32(0), lo, hi, clo, chi, exact0, act0))
    _, _, lo, _, clo, chi, exact, _ = lax.while_loop(bis_cond, bis_body, (any_active(st0[-1]),) + st0)
    thr = jnp.where(take_all, _INADMISSIBLE + 1, lo)
    thr_up = jnp.where(take_all, _INADMISSIBLE + 1, lo + 1)

    need = kf - chi
    tied = jnp.where(take_all, 0, jnp.where(exact > 0, 0, jnp.where(clo > kf, 1, 0)))

    def count_tied_below(cut):
        def body(t, cnt):
            for r0 in range(0, ts, COUNT_ROWS):
                kidx = t * ts + r0 + lax.broadcasted_iota(I32, (COUNT_ROWS, tq), 0)
                cnt = cnt + jnp.where(sc_ref[t, r0:r0 + COUNT_ROWS, :] == thr,
                                      jnp.where(kidx < cut, 1.0, 0.0), 0.0)
            return cnt
        return jnp.sum(lax.fori_loop(0, n_t, body, jnp.zeros((COUNT_ROWS, tq), F32)), axis=0, keepdims=True)

    def cut_body(st):
        _, lo_c, hi_c, act = st
        mid = (lo_c + hi_c) >> 1
        cnt = count_tied_below(mid)
        on = act > 0
        le = cnt <= need
        lo_c = jnp.where(on, jnp.where(le, mid, lo_c), lo_c)
        hi_c = jnp.where(on, jnp.where(le, hi_c, mid), hi_c)
        act = jnp.where(on, jnp.where(hi_c - lo_c > 1, 1, 0), 0)
        return any_active(act), lo_c, hi_c, act

    _, cut_lo, _, _ = lax.while_loop(
        bis_cond, cut_body,
        (any_active(tied), jnp.zeros((1, tq), I32), jnp.full((1, tq), n_tiles_total * ts + 1, I32), tied))
    cut = jnp.where(tied > 0, cut_lo, n_tiles_total * ts + 1)

    acc_ref[...] = jnp.zeros(acc_ref.shape, F32)
    rc = min(tq, ROW_CHUNK)
    ones = jnp.ones((ts, LANES), BF16)
    last = n_t - 1

    def logits(t, slot):
        for g in range(N_KV_HEADS):
            qg = q_ref[0, g * GQA_GROUP:(g + 1) * GQA_GROUP].reshape(GQA_GROUP * tq, HEAD_DIM)
            s_ref[slot, g * GQA_GROUP * tq:(g + 1) * GQA_GROUP * tq] = jnp.dot(
                qg, kt_ref[0, g, t], preferred_element_type=F32)

    def selection_bias(t, c, live, masked_value):
        kidx = t * ts + c * LANES + lax.broadcasted_iota(I32, (LANES, tq), 0)
        bound = jnp.where(live, jnp.where(kidx < cut, thr, thr_up), _NO_KEY)
        return jnp.where(sc_ref[t, c * LANES:(c + 1) * LANES, :] >= bound, 0.0, masked_value).T

    def value_matmul(t, slot, g):
        v_aug = jnp.concatenate([v_ref[0, g, t], ones], axis=-1)
        return jnp.dot(p_ref[slot, g], v_aug, preferred_element_type=F32)

    def softmax_pv(t, slot, live):
        for c in range(nc):
            bias_ref[slot, :, c * LANES:(c + 1) * LANES] = selection_bias(t, c, live, _NEG_INF)
        for g in range(N_KV_HEADS):
            for j in range(GQA_GROUP):
                hd = g * GQA_GROUP + j
                for r0 in range(0, tq, rc):
                    r1 = j * tq + r0
                    r2 = hd * tq + r0
                    s = [s_ref[slot, r2:r2 + rc, c * LANES:(c + 1) * LANES]
                         + bias_ref[slot, r0:r0 + rc, c * LANES:(c + 1) * LANES] for c in range(nc)]
                    mx = s[0]
                    for c in range(1, nc):
                        mx = jnp.maximum(mx, s[c])
                    m_old = m_ref[hd, r0:r0 + rc]
                    m_new = jnp.maximum(m_old, jnp.broadcast_to(jnp.max(mx, axis=-1, keepdims=True), (rc, LANES)))
                    alpha_ref[slot, g, r1:r1 + rc] = jnp.exp2(m_old - m_new)
                    m_ref[hd, r0:r0 + rc] = m_new
                    for c in range(nc):
                        p_ref[slot, g, r1:r1 + rc, c * LANES:(c + 1) * LANES] = jnp.exp2(s[c] - m_new).astype(BF16)
            pv = value_matmul(t, slot, g)
            for half in range(2):
                cols = slice(half * LANES, (half + 1) * LANES)
                acc_ref[g, :, cols] = alpha_ref[slot, g] * acc_ref[g, :, cols] + pv[:, cols]

    def attend_running():
        def attn_pair(u, carry):
            t0 = 2 * u
            t1 = jnp.minimum(t0 + 1, last)
            logits(t1, 1)
            softmax_pv(t0, 0, True)
            logits(jnp.minimum(t0 + 2, last), 0)
            softmax_pv(t1, 1, t0 + 1 < n_t)
            return carry

        lax.fori_loop(0, (n_t + 1) // 2, attn_pair, 0)

    def softmax_fixed(t, slot):
        for c in range(nc):
            bias_ref[slot, :, c * LANES:(c + 1) * LANES] = selection_bias(t, c, True, _NEG_INF)
        for g in range(N_KV_HEADS):
            for j in range(GQA_GROUP):
                hd = g * GQA_GROUP + j
                for r0 in range(0, tq, rc):
                    ref_pt = m_ref[hd, r0:r0 + rc]
                    for c in range(nc):
                        cs = slice(c * LANES, (c + 1) * LANES)
                        s = s_ref[slot, hd * tq + r0:hd * tq + r0 + rc, cs] + bias_ref[slot, r0:r0 + rc, cs]
                        p_ref[slot, g, j * tq + r0:j * tq + r0 + rc, cs] = jnp.exp2(s - ref_pt).astype(BF16)

    def attend_fixed():
        def accumulate(g, pv):
            for half in range(2):
                cols = slice(half * LANES, (half + 1) * LANES)
                acc_ref[g, :, cols] = acc_ref[g, :, cols] + pv[:, cols]

        def attn_pair(u, carry):
            t0 = 2 * u
            logits(t0 + 1, 1)
            softmax_fixed(t0, 0)
            logits(jnp.minimum(t0 + 2, last), 0)
            softmax_fixed(t0 + 1, 1)
            for g in range(N_KV_HEADS):
                accumulate(g, value_matmul(t0, 0, g) + value_matmul(t0 + 1, 1, g))
            return carry

        lax.fori_loop(0, n_t // 2, attn_pair, 0)

        @pl.when(n_t % 2 == 1)
        def _():
            softmax_fixed(last, 0)
            for g in range(N_KV_HEADS):
                accumulate(g, value_matmul(last, 0, g))

    @pl.when(qi == 0)
    def _():
        for g in range(N_KV_HEADS):
            def sq_norm_max(t, best):
                kk = kt_ref[0, g, t].astype(F32)
                return jnp.maximum(best, jnp.sum(kk * kk, axis=0, keepdims=True))
            best = lax.fori_loop(0, n_tiles_total, sq_norm_max, jnp.zeros((1, ts), F32))
            kmax_ref[g] = jnp.broadcast_to(jnp.max(best, axis=-1, keepdims=True), (8, LANES))

    logits(0, 0)
    top = jnp.zeros((tq, LANES), F32)
    for hd in range(N_HEADS):
        qf = q_ref[0, hd].astype(F32)
        q_sq = _row_sum(qf * qf)
        ref_pt = jnp.sqrt(q_sq * kmax_ref[hd // GQA_GROUP, 0:1, :]) * 1.001 + 1e-3
        m_ref[hd] = ref_pt
        top = jnp.maximum(top, ref_pt)
    fits = jnp.max(top) <= MAX_FIXED_REFERENCE

    @pl.when(fits)
    def _():
        attend_fixed()

    @pl.when(jnp.logical_not(fits))
    def _():
        m_ref[...] = jnp.full(m_ref.shape, _M_INIT, F32)
        attend_running()

    for hd in range(N_HEADS):
        g, j = divmod(hd, GQA_GROUP)
        o_ref[0, :, hd * HEAD_DIM:(hd + 1) * HEAD_DIM] = (
            acc_ref[g, j * tq:(j + 1) * tq, 0:HEAD_DIM] / acc_ref[g, j * tq:(j + 1) * tq, HEAD_DIM:2 * HEAD_DIM])


def _attend_t(qst, wrow, q_att, ki4, kt, v, q0, n_keys, tq, ts):
    B, _, T, _ = q_att.shape
    n_tiles = ki4.shape[1]
    k_sel = min(TOPK_MAX, n_keys // 4)
    kern = functools.partial(_attn_t_kernel, tq=tq, ts=ts, q0=q0, n_keys=n_keys, k_sel=k_sel)
    return pl.pallas_call(
        kern, grid=(B, T // tq),
        in_specs=[
            pl.BlockSpec((1, 1, 4 * IDX_DIM, N_IDX_HEADS * tq), lambda b, i: (b, i, 0, 0)),
            pl.BlockSpec((1, 1, N_IDX_HEADS, tq), lambda b, i: (b, i, 0, 0)),
            pl.BlockSpec((1, N_HEADS, tq, HEAD_DIM), lambda b, i: (b, 0, i, 0)),
            _resident((1, n_tiles, ts, 4 * IDX_DIM), lambda b, i: (b, 0, 0, 0)),
            _resident((1, N_KV_HEADS, n_tiles, HEAD_DIM, ts), lambda b, i: (b, 0, 0, 0, 0)),
            _resident((1, N_KV_HEADS, n_tiles, ts, HEAD_DIM), lambda b, i: (b, 0, 0, 0, 0)),
        ],
        out_specs=pl.BlockSpec((1, tq, N_HEADS * HEAD_DIM), lambda b, i: (b, i, 0)),
        out_shape=jax.ShapeDtypeStruct((B, T, N_HEADS * HEAD_DIM), F32),
        scratch_shapes=[
            pltpu.VMEM((n_tiles, ts, tq), I32),
            pltpu.VMEM((3, KEY_ROWS, tq), F32),
        ] + ([] if ts == N_IDX_HEADS * tq else [
            pltpu.VMEM((2, ts, N_IDX_HEADS * tq), F32),
        ]) + [
            pltpu.VMEM((2, N_HEADS * tq, ts), F32),
            pltpu.VMEM((N_HEADS, tq, LANES), F32),
            pltpu.VMEM((N_KV_HEADS, GQA_GROUP * tq, 2 * HEAD_DIM), F32),
            pltpu.VMEM((2, tq, ts), F32),
            pltpu.VMEM((2, N_KV_HEADS, GQA_GROUP * tq, ts), BF16),
            pltpu.VMEM((2, N_KV_HEADS, GQA_GROUP * tq, LANES), F32),
            pltpu.VMEM((N_KV_HEADS, 8, LANES), F32),
        ],
        name="attend_t",
        compiler_params=pltpu.CompilerParams(dimension_semantics=("arbitrary", "arbitrary"),
                                             vmem_limit_bytes=VMEM_LIMIT),
    )(qst, wrow, q_att, ki4, kt, v)


def _attend(qs, kiw, q_att, ki4, kt, v, q0, n_keys, tq, ts):
    B, _, T, _ = qs.shape
    lp = ki4.shape[1]
    n_tiles = lp // ts
    k_sel = min(TOPK_MAX, n_keys // 4)
    kern = functools.partial(_attn_kernel, tq=tq, ts=ts, q0=q0, n_keys=n_keys, k_sel=k_sel)
    keys = pl.BlockSpec((1, lp, 2 * LANES), lambda b, i: (b, 0, 0))
    return pl.pallas_call(
        kern, grid=(B, T // tq),
        in_specs=[
            pl.BlockSpec((1, N_IDX_HEADS, tq, 4 * IDX_DIM), lambda b, i: (b, 0, i, 0)),
            pl.BlockSpec((1, tq, LANES), lambda b, i: (b, i, 0)),
            pl.BlockSpec((1, N_HEADS, tq, HEAD_DIM), lambda b, i: (b, 0, i, 0)),
            keys, keys, keys,
        ],
        out_specs=pl.BlockSpec((1, tq, N_HEADS * HEAD_DIM), lambda b, i: (b, i, 0)),
        out_shape=jax.ShapeDtypeStruct((B, T, N_HEADS * HEAD_DIM), F32),
        scratch_shapes=[
            pltpu.VMEM((n_tiles, tq, ts), I32),
            pltpu.VMEM((2, tq, LANES), F32),
            pltpu.VMEM((N_IDX_HEADS, tq, LANES), F32),
            pltpu.VMEM((2, N_HEADS * tq, ts), F32),
            pltpu.VMEM((N_HEADS, tq, LANES), F32),
            pltpu.VMEM((N_KV_HEADS, GQA_GROUP * tq, 2 * HEAD_DIM), F32),
            pltpu.VMEM((2, tq, ts), F32),
            pltpu.VMEM((2, N_KV_HEADS, GQA_GROUP * tq, ts), BF16),
            pltpu.VMEM((2, N_KV_HEADS, GQA_GROUP * tq, LANES), F32),
        ],
        name="attend",
        compiler_params=pltpu.CompilerParams(dimension_semantics=("arbitrary", "arbitrary"),
                                             vmem_limit_bytes=VMEM_LIMIT),
    )(qs, kiw, q_att, ki4, kt, v)


def _rms(x, g):
    return x * lax.rsqrt(jnp.mean(x * x, axis=-1, keepdims=True) + RMS_EPS) * g


def _out_kernel(x_ref, pa_ref, sgb_ref, b_ref, wo_ref, g1_ref, g2_ref, wgu_ref, wd_ref, g3_ref, y_ref, *, d_ff):
    merged = pa_ref[...] + sgb_ref[...] * b_ref[...]
    mix = jnp.dot(merged.astype(BF16), wo_ref[...], preferred_element_type=F32)
    x1 = x_ref[...] + _rms(mix, g1_ref[...])
    h2 = _rms(x1, g2_ref[...]).astype(BF16)
    gu = jnp.dot(h2, wgu_ref[...], preferred_element_type=F32)
    gate, up = gu[:, :d_ff], gu[:, d_ff:]
    act = (gate * _sigmoid(gate) * up).astype(BF16)
    f = jnp.dot(act, wd_ref[...], preferred_element_type=F32)
    y_ref[...] = x1 + _rms(f, g3_ref[...])


def _finish(x, pa, sgb, b, w_out, g_post, g_ffn_pre, w_gate_up, w_down, g_ffn_post, tm):
    N, D = x.shape
    d_ff = w_down.shape[0]
    row = pl.BlockSpec((tm, D), lambda i: (i, 0))
    const = lambda i: (0, 0)
    return pl.pallas_call(
        functools.partial(_out_kernel, d_ff=d_ff), grid=(N // tm,),
        in_specs=[row, row, row, row,
                  _resident((D, D), const), _resident((1, D), const), _resident((1, D), const),
                  _resident((D, 2 * d_ff), const), _resident((d_ff, D), const), _resident((1, D), const)],
        out_specs=row, out_shape=jax.ShapeDtypeStruct((N, D), F32), name="finish",
        compiler_params=pltpu.CompilerParams(dimension_semantics=("arbitrary",), vmem_limit_bytes=VMEM_LIMIT),
    )(x, pa, sgb, b, w_out, g_post, g_ffn_pre, w_gate_up, w_down, g_ffn_post)


def _tile_keys(n_keys, ts):
    return -(-n_keys // ts) * ts


def _layer(x, pos0, hist, k_cache, v_cache, ki_cache, wts, tm, tq, ts, tm_out):
    (w_in_r, w_pool, pool_scale, w_out, w_gate_up, w_down, g_pre, g_post, g_ffn_pre, g_ffn_post) = wts
    B, T, D = x.shape
    hist16 = jnp.concatenate([jnp.zeros((B, HALO // 2 - POOL_HIST, POOL_WIDTH), F32), hist], axis=1)
    own_keys_only = k_cache is None and T % ts == 0 and ts % tm == 0 and tq == LANES and tm % LANES == 0
    outs = _project(x, hist16, pos0, w_in_r, g_pre, w_pool, pool_scale, tm, ts if own_keys_only else None)
    u, q_att, k, v, qs, kiw, pa, sgb = outs[:8]
    ki = kiw[:, :, :IDX_DIM]
    if own_keys_only:
        kt, vt, ki4, wrow = outs[8:]
        b = _attend_t(qs, wrow, q_att, ki4, kt, vt, pos0, T, tq, ts)
    else:
        if k_cache is None:
            k_all, v_all, ki_all = k, v, ki
        else:
            k_all = jnp.concatenate([k_cache.reshape(B, -1, N_KV_HEADS * HEAD_DIM), k], axis=1)
            v_all = jnp.concatenate([v_cache.reshape(B, -1, N_KV_HEADS * HEAD_DIM), v], axis=1)
            ki_all = jnp.concatenate([ki_cache, ki], axis=1)
        n_keys = k_all.shape[1]
        lp = _tile_keys(n_keys, ts)
        pad = ((0, 0), (0, lp - n_keys), (0, 0))
        kt = jnp.pad(k_all.astype(BF16), pad)
        vt = jnp.pad(v_all.astype(BF16), pad)
        ki4 = _split_keys(jnp.pad(ki_all, pad), ts)
        b = _attend(qs, kiw, q_att, ki4, kt, vt, pos0, n_keys, tq, ts)
    y = _finish(x.reshape(B * T, D), pa.reshape(B * T, D), sgb.reshape(B * T, D), b.reshape(B * T, D),
                w_out, g_post, g_ffn_pre, w_gate_up, w_down, g_ffn_post, tm_out).reshape(B, T, D)
    new_pool = jnp.concatenate([hist, u], axis=1)[:, T:]
    return (y, k.reshape(B, T, N_KV_HEADS, HEAD_DIM), v.reshape(B, T, N_KV_HEADS, HEAD_DIM), ki, new_pool)


def _relayout_w_in(w_in):
    d = w_in.shape[0]
    o_kiw = POOL_WIDTH + N_HEADS * HEAD_DIM + 2 * N_KV_HEADS * HEAD_DIM + N_IDX_HEADS * IDX_DIM
    narrow = IDX_DIM + N_IDX_HEADS
    padded = jnp.concatenate([w_in[:, :o_kiw + narrow], jnp.zeros((d, LANES - narrow), w_in.dtype),
                              w_in[:, o_kiw + narrow:]], axis=1)
    return padded.astype(BF16)


def kernel(x_prompt, x_sample, cache_k, cache_v, cache_k_idx, state_pool, w_in, w_pool, pool_scale, w_out,
           w_gate_up, w_down, norm_mix_pre, norm_mix_post, norm_ffn_pre, norm_ffn_post):
    depth = w_in.shape[0]
    past = cache_k.shape[2]
    t_p, t_s = x_prompt.shape[1], x_sample.shape[1]
    hist_p = jnp.zeros((x_prompt.shape[0], POOL_HIST, POOL_WIDTH), x_prompt.dtype)
    xp, xs = x_prompt, x_sample
    outs = [[] for _ in range(8)]
    for l in range(depth):
        wts = (_relayout_w_in(w_in[l]), w_pool[l].astype(BF16), pool_scale[l][None, :], w_out[l].astype(BF16),
               w_gate_up[l].astype(BF16), w_down[l].astype(BF16), norm_mix_pre[l][None, :],
               norm_mix_post[l][None, :], norm_ffn_pre[l][None, :], norm_ffn_post[l][None, :])
        tm_p = min(256, t_p)
        tq_p = min(128, t_p)
        ts_p = 1024 if t_p % 1024 == 0 else 512
        xp, k1, v1, ki1, p1 = _layer(xp, 0, hist_p, None, None, None, wts, tm_p, tq_p, ts_p, tm_p)
        n_s = xs.shape[0] * t_s
        xs, k2, v2, ki2, p2 = _layer(xs, past, state_pool[l], cache_k[l], cache_v[l], cache_k_idx[l], wts,
                                     t_s, t_s, 512, min(256, n_s))
        for lst, val in zip(outs, (k1, v1, ki1, p1, k2, v2, ki2, p2)):
            lst.append(val)
    return (xp, xs) + tuple(jnp.stack(o) for o in outs)
```
